```python
import jax, jax.numpy as jnp
from jax import lax
import numpy as np

D_MODEL = 1024
BATCH = 32
SEQ = 256
DEPTH = 2
DEC_BATCH = 4
DEC_SEQ = 1024
PAST_LEN = 256

GRID_W = 64
CHUNK = 16
N_EVEN = (DEPTH + 1) // 2
N_ODD = DEPTH // 2
EPS = 1e-6
H_A = 4
DK_A = 128
DV_A = 128
H_B = 4
DK_B = 64
DV_B = 128
GLA_RANK = 16
GLA_GATE_NORM = 16.0
H_C = 8
DK_C = 128
DV_C = 128
ROPE_BASE = 10000.0

W_A = H_A * DV_A
W_B = H_B * DV_B
W_EVEN = W_A + W_B
W_ODD = H_C * DV_C
EVEN_SPLITS = [H_A * DK_A, W_A, H_A * DK_A, H_A * DK_A, W_A, H_B * DK_B, H_B * DK_B, W_B, W_B, GLA_RANK, GLA_RANK]
EVEN_SPLIT_IDX = np.cumsum(EVEN_SPLITS)[:-1].tolist()
D_IN_EVEN = int(sum(EVEN_SPLITS))
ODD_SPLIT_IDX = [H_C * DK_C, 2 * H_C * DK_C, 2 * H_C * DK_C + W_ODD]
D_IN_ODD = 2 * H_C * DK_C + 2 * W_ODD

kernel_name = "hybrid_hgrn2_gla_retention_diffusion_step"


def rms_norm(x, w):
    xf = x.astype(jnp.float32)
    y = xf * lax.rsqrt(jnp.mean(xf * xf, axis=-1, keepdims=True) + EPS)
    return (y * w.astype(jnp.float32)).astype(x.dtype)


def group_rms_norm(o, w):
    b, t, h, v = o.shape
    of = o.astype(jnp.float32)
    y = of * lax.rsqrt(jnp.mean(of * of, axis=-1, keepdims=True) + EPS)
    return y.reshape(b, t, h * v) * w.astype(jnp.float32)


def grid_angles(T):
    rows = T // GRID_W
    t_row = jnp.repeat(jnp.arange(rows), GRID_W).astype(jnp.float32)
    t_col = jnp.tile(jnp.arange(GRID_W), rows).astype(jnp.float32)
    half = DK_C // 2
    inv = ROPE_BASE ** (-jnp.arange(0, half, 2, dtype=jnp.float32) / half)
    ang_r = t_row[:, None] * inv
    ang_c = t_col[:, None] * inv
    ang = jnp.concatenate([ang_r, ang_r, ang_c, ang_c], axis=-1)
    return jnp.cos(ang), jnp.sin(ang)


def apply_grid_rope(x, cos, sin):
    half = DK_C // 2
    qd = half // 2
    def rot(a):
        return jnp.concatenate([-a[..., qd:], a[..., :qd]], axis=-1)
    xr = jnp.concatenate([rot(x[..., :half]), rot(x[..., half:])], axis=-1)
    return x * cos[None, :, None, :] + xr * sin[None, :, None, :]


def chunk_gated_linear(q, k, v, log_g, s0):
    f32 = jnp.float32
    q, k, v, log_g, s0 = (a.astype(f32) for a in (q, k, v, log_g, s0))
    b_, t, h, kd = q.shape
    n = t // CHUNK
    rs = lambda a: a.reshape(b_, n, CHUNK, h, a.shape[-1])
    qc, kc, vc, gc = rs(q), rs(k), rs(v), rs(log_g)
    bcum = jnp.cumsum(gc, axis=2)
    b_last = bcum[:, :, -1]
    causal = jnp.tril(jnp.ones((CHUNK, CHUNK), dtype=bool))
    diff = bcum[:, :, :, None] - bcum[:, :, None, :]
    decay = jnp.exp(jnp.where(causal[None, None, :, :, None, None], diff, -jnp.inf))
    if log_g.shape[-1] == 1:
        scores = jnp.einsum('bnthk,bnshk->bntsh', qc, kc) * decay[..., 0]
    else:
        scores = jnp.einsum('bnthk,bnshk,bntshk->bntsh', qc, kc, decay)
    o_intra = jnp.einsum('bntsh,bnshv->bnthv', scores, vc)
    q_dec = qc * jnp.exp(bcum)
    k_dec = kc * jnp.exp(b_last[:, :, None] - bcum)
    chunk_kv = jnp.einsum('bnshk,bnshv->bnhkv', k_dec, vc)
    g_chunk = jnp.exp(b_last)

    def step(S, inp):
        g_n, kv_n = inp
        return g_n[..., None] * S + kv_n, S

    s_final, s_prev = lax.scan(step, s0, (jnp.moveaxis(g_chunk, 1, 0), jnp.moveaxis(chunk_kv, 1, 0)))
    s_prev = jnp.moveaxis(s_prev, 0, 1)
    o_inter = jnp.einsum('bnthk,bnhkv->bnthv', q_dec, s_prev)
    o = (o_intra + o_inter).reshape(b_, t, h, v.shape[-1])
    return o, s_final


def bidir_scan(q, k_pair, v, lg_pair, s0):
    flip = lambda a: a[:, ::-1]
    o_f, s_f = chunk_gated_linear(q, k_pair[0], v, lg_pair[0], s0[:, 0])
    o_b, s_b = chunk_gated_linear(flip(q), flip(k_pair[1]), flip(v), flip(lg_pair[1]), s0[:, 1])
    return o_f + flip(o_b), jnp.stack([s_f, s_b], axis=1)


def even_mixer(h, s0_a, s0_b, lb, w_in, gk_w, gk_b, gn_w, w_out):
    b_, t, _ = h.shape
    p = h @ w_in
    aq, ai, af_f, af_b, ag, bq, bk, bv, bg, bl_f, bl_b = jnp.split(p, EVEN_SPLIT_IDX, axis=-1)
    hd = lambda a, nh: a.reshape(b_, t, nh, -1)
    lbf = lb.astype(jnp.float32)
    def log_forget(a):
        return jnp.logaddexp(jnp.log(lbf), jnp.log1p(-lbf) + jax.nn.log_sigmoid(a.astype(jnp.float32)))
    lf_f = hd(log_forget(af_f), H_A)
    lf_b = hd(log_forget(af_b), H_A)
    o_a, st_a = bidir_scan(hd(jax.nn.silu(aq), H_A), (-jnp.expm1(lf_f), -jnp.expm1(lf_b)),
                           hd(ai, H_A), (lf_f, lf_b), s0_a)
    def gla_log_gate(low, d):
        return jax.nn.log_sigmoid((low @ gk_w[d] + gk_b[d]).astype(jnp.float32)) / GLA_GATE_NORM
    kb = hd(bk, H_B)
    o_b, st_b = bidir_scan(hd(bq, H_B) * (DK_B ** -0.5), (kb, kb), hd(bv, H_B),
                           (hd(gla_log_gate(bl_f, 0), H_B), hd(gla_log_gate(bl_b, 1), H_B)), s0_b)
    o = jnp.concatenate([o_a, o_b], axis=2)
    o = group_rms_norm(o, gn_w).astype(h.dtype) * jax.nn.silu(jnp.concatenate([ag, bg], axis=-1))
    return o @ w_out, st_a, st_b


def odd_mixer(h, s0, decay_logit, w_in, gn_w, w_out, rope):
    b_, t, _ = h.shape
    p = h @ w_in
    q, k, v, g = jnp.split(p, ODD_SPLIT_IDX, axis=-1)
    q = q.reshape(b_, t, H_C, DK_C)
    k = k.reshape(b_, t, H_C, DK_C) * (DK_C ** -0.5)
    if rope is not None:
        q = apply_grid_rope(q, rope[0], rope[1])
        k = apply_grid_rope(k, rope[0], rope[1])
    lg = jax.nn.log_sigmoid(decay_logit.astype(jnp.float32))
    lg_f = jnp.broadcast_to(lg[0][:, None], (b_, t, H_C, 1))
    lg_b = jnp.broadcast_to(lg[1][:, None], (b_, t, H_C, 1))
    o, st = bidir_scan(q, (k, k), v.reshape(b_, t, H_C, DV_C), (lg_f, lg_b), s0)
    o = group_rms_norm(o, gn_w).astype(h.dtype) * jax.nn.silu(g)
    return o @ w_out, st


def setup_inputs(seed: int = 0) -> dict:
    key = jax.random.key(seed)
    ks = jax.random.split(key, 24)
    f32 = jnp.float32
    nrm = lambda k, shape, s: jax.random.normal(k, shape, f32) * s
    ret_init = jnp.asarray(np.log(2.0 ** (5.0 + np.arange(H_C)) - 1.0), dtype=f32)
    return {
        "x_prompt": nrm(ks[0], (BATCH, SEQ, D_MODEL), 1.0),
        "x_sample": nrm(ks[1], (DEC_BATCH, DEC_SEQ, D_MODEL), 1.0),
        "state_hgrn": nrm(ks[2], (DEC_BATCH, N_EVEN, 2, H_A, DK_A, DV_A), 0.5),
        "state_gla": nrm(ks[3], (DEC_BATCH, N_EVEN, 2, H_B, DK_B, DV_B), 0.5),
        "state_ret": nrm(ks[4], (DEC_BATCH, N_ODD, 2, H_C, DK_C, DV_C), 1.0),
        "c": nrm(ks[5], (DEC_BATCH, D_MODEL), 1.0),
        "c_ctx": nrm(ks[6], (D_MODEL,), 1.0),
        "norm_w": 1.0 + nrm(ks[7], (DEPTH, D_MODEL), 0.02),
        "ada_w": nrm(ks[8], (DEPTH, D_MODEL, 3 * D_MODEL), 0.5 * D_MODEL ** -0.5),
        "ada_b": nrm(ks[9], (DEPTH, 3 * D_MODEL), 0.02),
        "w_in_even": nrm(ks[10], (N_EVEN, D_MODEL, D_IN_EVEN), D_MODEL ** -0.5),
        "hgrn_lb": nrm(ks[11], (N_EVEN + 1, H_A * DK_A), 0.5),
        "gla_gk_w": nrm(ks[12], (N_EVEN, 2, GLA_RANK, H_B * DK_B), GLA_RANK ** -0.5),
        "gla_gk_b": nrm(ks[13], (N_EVEN, 2, H_B * DK_B), 0.02),
        "gn_even": 1.0 + nrm(ks[14], (N_EVEN, W_EVEN), 0.02),
        "w_out_even": nrm(ks[15], (N_EVEN, W_EVEN, D_MODEL), W_EVEN ** -0.5),
        "w_in_odd": nrm(ks[16], (N_ODD, D_MODEL, D_IN_ODD), D_MODEL ** -0.5),
        "ret_decay": ret_init[None, None, :] + nrm(ks[17], (N_ODD, 2, H_C), 0.1),
        "gn_odd": 1.0 + nrm(ks[18], (N_ODD, W_ODD), 0.02),
        "w_out_odd": nrm(ks[19], (N_ODD, W_ODD, D_MODEL), W_ODD ** -0.5),
        "final_norm_w": 1.0 + nrm(ks[20], (D_MODEL,), 0.02),
    }


def reference(x_prompt, x_sample, state_hgrn, state_gla, state_ret, c, c_ctx, norm_w, ada_w, ada_b,
              w_in_even, hgrn_lb, gla_gk_w, gla_gk_b, gn_even, w_out_even, w_in_odd, ret_decay, gn_odd,
              w_out_odd, final_norm_w):
    f32 = jnp.float32
    b_ctx = x_prompt.shape[0]
    lbs = jnp.cumsum(jax.nn.softmax(hgrn_lb.astype(f32), axis=0), axis=0)
    cond_ctx = jax.nn.silu(c_ctx)[None, None, :]
    cond_lat = jax.nn.silu(c)[:, None, :]
    rope = grid_angles(x_sample.shape[1])
    x_c, x_l = x_prompt, x_sample
    new_hgrn, new_gla, new_ret = [], [], []
    for l in range(DEPTH):
        i = l // 2
        sh_c, sc_c, g_c = jnp.split(cond_ctx @ ada_w[l] + ada_b[l], 3, axis=-1)
        sh_l, sc_l, g_l = jnp.split(cond_lat @ ada_w[l] + ada_b[l], 3, axis=-1)
        h_c = rms_norm(x_c, norm_w[l]) * (1.0 + sc_c) + sh_c
        h_l = rms_norm(x_l, norm_w[l]) * (1.0 + sc_l) + sh_l
        if l % 2 == 0:
            z_a = jnp.zeros((b_ctx, 2, H_A, DK_A, DV_A), f32)
            z_b = jnp.zeros((b_ctx, 2, H_B, DK_B, DV_B), f32)
            out_c, st_a, st_b = even_mixer(h_c, z_a, z_b, lbs[i], w_in_even[i], gla_gk_w[i], gla_gk_b[i],
                                           gn_even[i], w_out_even[i])
            out_l, _, _ = even_mixer(h_l, state_hgrn[:, i], state_gla[:, i], lbs[i], w_in_even[i],
                                     gla_gk_w[i], gla_gk_b[i], gn_even[i], w_out_even[i])
            new_hgrn.append(st_a)
            new_gla.append(st_b)
        else:
            z_c = jnp.zeros((b_ctx, 2, H_C, DK_C, DV_C), f32)
            out_c, st_c = odd_mixer(h_c, z_c, ret_decay[i], w_in_odd[i], gn_odd[i], w_out_odd[i], None)
            out_l, _ = odd_mixer(h_l, state_ret[:, i], ret_decay[i], w_in_odd[i], gn_odd[i], w_out_odd[i], rope)
            new_ret.append(st_c)
        x_c = x_c + g_c * out_c
        x_l = x_l + g_l * out_l
    y_prompt = rms_norm(x_c, final_norm_w)
    y_sample = rms_norm(x_l, final_norm_w)
    new_state_hgrn = jnp.stack(new_hgrn, axis=1)
    new_state_gla = jnp.stack(new_gla, axis=1)
    new_state_ret = jnp.stack(new_ret, axis=1)
    return (y_prompt, y_sample, new_state_hgrn, new_state_gla, new_state_ret)
```

```python
import functools

import numpy as np
import jax
import jax.numpy as jnp
from jax import lax
from jax.experimental import pallas as pl
from jax.experimental.pallas import tpu as pltpu

f32 = jnp.float32
bf16 = jnp.bfloat16
HIGHEST = lax.Precision.HIGHEST

EPS = 1e-6
HEAD = 128
N_HGRN = 4
N_GLA = 4
GLA_DK = 64
N_RET = 8
N_HEADS = 8
GLA_RANK = 16
GLA_GATE_NORM = 16.0
GRID_W = 64
ROPE_BASE = 10000.0

CHUNK = 128
LEVELS = (1, 2, 4, 8, 16, 32, 64)
DIAG_LEVEL = len(LEVELS)
ROW_TILE = 256
MOD_COLS = 768
VMEM_LIMIT_BYTES = 56 * 1024 * 1024


def _dot(a, b, precision=None):
    return jnp.dot(a, b, precision=precision, preferred_element_type=f32)


def _dot_nt(a, b):
    return lax.dot_general(a, b, (((1,), (1,)), ((), ())), preferred_element_type=f32)


def _dot_tn(a, b):
    return lax.dot_general(a, b, (((0,), (0,)), ((), ())), preferred_element_type=f32)


def _silu(x):
    return x * jax.nn.sigmoid(x)


def _softplus(x):
    return jnp.maximum(x, 0.0) + jnp.log1p(jnp.exp(-jnp.abs(x)))


def _row_tile(i):
    return pl.ds(pl.multiple_of(i * ROW_TILE, ROW_TILE), ROW_TILE)


def _chunk_rows(c):
    return pl.ds(pl.multiple_of(c * CHUNK, CHUNK), CHUNK)


def _mod_kernel(cond_ref, w_ref, b_ref, o_ref):
    o_ref[0] = _dot(_silu(cond_ref[...]), w_ref[0], HIGHEST) + b_ref[0]


def _modulation(cond, ada_w, ada_b):
    depth, d, d3 = ada_w.shape
    rows = cond.shape[0]
    return pl.pallas_call(
        _mod_kernel,
        grid=(depth, d3 // MOD_COLS),
        in_specs=[
            pl.BlockSpec((rows, d), lambda l, j: (0, 0)),
            pl.BlockSpec((1, d, MOD_COLS), lambda l, j: (l, 0, j)),
            pl.BlockSpec((1, 1, MOD_COLS), lambda l, j: (l, 0, j)),
        ],
        out_specs=pl.BlockSpec((1, rows, MOD_COLS), lambda l, j: (l, 0, j)),
        out_shape=jax.ShapeDtypeStruct((depth, rows, d3), f32),
        compiler_params=pltpu.CompilerParams(dimension_semantics=("arbitrary", "arbitrary")),
        name="modulation",
    )(cond, ada_w, ada_b.reshape(depth, 1, d3))


def _modulated_norm(x_ref, mod_ref, nw_ref, h_scr, seq):
    def body(i, carry):
        rows = _row_tile(i)
        x = x_ref[0, rows, :]
        y = x * lax.rsqrt(jnp.mean(x * x, axis=-1, keepdims=True) + EPS) * nw_ref[...]
        h_scr[rows, :] = (y * (1.0 + mod_ref[0, 1:2, :]) + mod_ref[0, 0:1, :]).astype(bf16)
        return carry
    lax.fori_loop(0, seq // ROW_TILE, body, 0)


def _gated_output(x_ref, mod_ref, h_scr, o_scr, gnw_ref, wg_ref, wo_ref, fnw_ref, out_ref, seq, final):
    def body(i, carry):
        rows = _row_tile(i)
        gate = _dot(h_scr[rows, :], wg_ref[...])
        parts = []
        for hh in range(N_HEADS):
            o = o_scr[hh, rows, :]
            parts.append(o * lax.rsqrt(jnp.mean(o * o, axis=-1, keepdims=True) + EPS))
        y = jnp.concatenate(parts, axis=-1) * gnw_ref[...]
        z = (y * _silu(gate)).astype(bf16)
        xn = x_ref[0, rows, :] + mod_ref[0, 2:3, :] * _dot(z, wo_ref[...])
        if final:
            xn = xn * lax.rsqrt(jnp.mean(xn * xn, axis=-1, keepdims=True) + EPS) * fnw_ref[...]
        out_ref[0, rows, :] = xn
        return carry
    lax.fori_loop(0, seq // ROW_TILE, body, 0)


def _pair_tables(rev):
    t = lax.broadcasted_iota(jnp.int32, (CHUNK, CHUNK), 0)
    s = lax.broadcasted_iota(jnp.int32, (CHUNK, CHUNK), 1)
    x = t ^ s
    lvl = jnp.zeros((CHUNK, CHUNK), jnp.int32)
    for j in range(1, len(LEVELS)):
        lvl = lvl + (x >= (1 << j)).astype(jnp.int32)
    seen = (s > t) if rev else (t > s)
    lvl = jnp.where(seen, lvl, jnp.where(t == s, DIAG_LEVEL, -1))
    tri = ((s >= t) if rev else (s <= t)).astype(f32)
    return tri, lvl


def _block_mid(b, m, rev):
    blk = 2 * m
    r = m if rev else m - 1
    x = b.reshape(CHUNK // blk, blk, b.shape[-1])
    return jnp.broadcast_to(x[:, r:r + 1, :], x.shape).reshape(b.shape)


def _gated_chunk(q, k, v, g, state, tri, lvl, rev):
    b = _dot(tri, g, HIGHEST)
    tot = b[0:1, :] if rev else b[CHUNK - 1:CHUNK, :]
    vb = v.astype(bf16)
    o = _dot((q * jnp.exp(b)).astype(bf16), state.astype(bf16))
    kv = _dot_tn((k * jnp.exp(tot - b)).astype(bf16), vb)
    tot_col = jnp.broadcast_to(tot, (HEAD, HEAD)).T
    new_state = jnp.exp(tot_col) * state + kv
    sc = jnp.where(lvl == DIAG_LEVEL, _dot_nt(q.astype(bf16), k.astype(bf16)), 0.0)
    for i, m in enumerate(LEVELS):
        e = jnp.exp(-jnp.abs(b - _block_mid(b, m, rev)))
        sc = jnp.where(lvl == i, _dot_nt((q * e).astype(bf16), (k * e).astype(bf16)), sc)
    return o + _dot(sc.astype(bf16), vb), new_state


def _gated_scan(q_s, k_s, v_s, g_s, s_scr, tab_scr, lvl_scr, o_scr, head, seq, rev):
    n = seq // CHUNK
    d = 1 if rev else 0

    def body(i, carry):
        rows = _chunk_rows(n - 1 - i if rev else i)
        o, new_state = _gated_chunk(q_s[rows, :], k_s[rows, :], v_s[rows, :], g_s[rows, :],
                                    s_scr[...], tab_scr[d], lvl_scr[d], rev)
        o_scr[head, rows, :] += o
        s_scr[...] = new_state
        return carry
    lax.fori_loop(0, n, body, 0)


def _even_kernel(*refs, seq, has_state, emit_state, final):
    it = iter(refs)
    x_ref, mod_ref, nw_ref = next(it), next(it), next(it)
    wa_ref, wb_ref, wlow_ref, wg_ref, wo_ref = next(it), next(it), next(it), next(it), next(it)
    gkw_ref, gkb_ref, lb_ref, gnw_ref, fnw_ref = next(it), next(it), next(it), next(it), next(it)
    s0a_ref, s0b_ref = (next(it), next(it)) if has_state else (None, None)
    out_ref = next(it)
    sta_ref, stb_ref = (next(it), next(it)) if emit_state else (None, None)
    h_scr, low_scr, o_scr, q_s, v_s, kf_s, kb_s, gf_s, gb_s, s_scr, tab_scr, lvl_scr = it

    for d, rev in enumerate((False, True)):
        tri, lvl = _pair_tables(rev)
        tab_scr[d] = tri
        lvl_scr[d] = lvl
    _modulated_norm(x_ref, mod_ref, nw_ref, h_scr, seq)
    o_scr[...] = jnp.zeros(o_scr.shape, f32)
    n_tiles = seq // ROW_TILE

    def low_body(i, carry):
        rows = _row_tile(i)
        low_scr[rows, :] = _dot(h_scr[rows, :], wlow_ref[...]).astype(bf16)
        return carry
    lax.fori_loop(0, n_tiles, low_body, 0)

    def run_both(head, state_in, state_out, key_rows, k_fwd, k_bwd):
        for d, (rev, k_s, g_s) in enumerate(((False, k_fwd, gf_s), (True, k_bwd, gb_s))):
            if state_in is None:
                s_scr[...] = jnp.zeros((HEAD, HEAD), f32)
            else:
                if key_rows < HEAD:
                    s_scr[pl.ds(key_rows, HEAD - key_rows), :] = jnp.zeros((HEAD - key_rows, HEAD), f32)
                s_scr[pl.ds(0, key_rows), :] = state_in(d)
            _gated_scan(q_s, k_s, v_s, g_s, s_scr, tab_scr, lvl_scr, o_scr, head, seq, rev)
            if state_out is not None:
                state_out(d, s_scr[pl.ds(0, key_rows), :])

    def hgrn_head(hh, carry):
        lb = lb_ref[hh]
        log_lb = jnp.log(lb)

        def proj(i, c):
            rows = _row_tile(i)
            p = _dot(h_scr[rows, :], wa_ref[hh])
            q_s[rows, :] = _silu(p[:, 0:HEAD])
            v_s[rows, :] = p[:, HEAD:2 * HEAD]
            for a, k_s, g_s in ((p[:, 2 * HEAD:3 * HEAD], kf_s, gf_s), (p[:, 3 * HEAD:4 * HEAD], kb_s, gb_s)):
                g_s[rows, :] = _softplus(log_lb - a) - _softplus(-a)
                k_s[rows, :] = (1.0 - lb) * jax.nn.sigmoid(-a)
            return c
        lax.fori_loop(0, n_tiles, proj, 0)
        state_in = (lambda d: s0a_ref[0, d, hh]) if has_state else None

        def state_out(d, s):
            sta_ref[0, d, hh] = s
        run_both(hh, state_in, state_out if emit_state else None, HEAD, kf_s, kb_s)
        return carry
    lax.fori_loop(0, N_HGRN, hgrn_head, 0)

    def gla_head(hh, carry):
        def proj(i, c):
            rows = _row_tile(i)
            p = _dot(h_scr[rows, :], wb_ref[hh])
            q_s[rows, :] = p[:, 0:HEAD] * (GLA_DK ** -0.5)
            kf_s[rows, :] = p[:, HEAD:2 * HEAD]
            v_s[rows, :] = p[:, 2 * HEAD:3 * HEAD]
            low = low_scr[rows, :]
            for d, g_s in enumerate((gf_s, gb_s)):
                logits = _dot(low, gkw_ref[d, hh]) + gkb_ref[d, hh]
                g_s[rows, :] = -_softplus(-logits) * (1.0 / GLA_GATE_NORM)
            return c
        lax.fori_loop(0, n_tiles, proj, 0)
        state_in = (lambda d: s0b_ref[0, d, hh]) if has_state else None

        def state_out(d, s):
            stb_ref[0, d, hh] = s
        run_both(N_HGRN + hh, state_in, state_out if emit_state else None, GLA_DK, kf_s, kf_s)
        return carry
    lax.fori_loop(0, N_GLA, gla_head, 0)

    _gated_output(x_ref, mod_ref, h_scr, o_scr, gnw_ref, wg_ref, wo_ref, fnw_ref, out_ref, seq, final)


def _odd_kernel(*refs, seq, has_state, emit_state, use_rope, final):
    it = iter(refs)
    lg_ref = next(it)
    x_ref, mod_ref, nw_ref = next(it), next(it), next(it)
    wr_ref, wg_ref, wo_ref, gnw_ref, fnw_ref = next(it), next(it), next(it), next(it), next(it)
    cos_ref, sin_ref = (next(it), next(it)) if use_rope else (None, None)
    s0_ref = next(it) if has_state else None
    out_ref = next(it)
    st_ref = next(it) if emit_state else None
    h_scr, o_scr, q_s, k_s, v_s, sf_scr, sb_scr = it

    _modulated_norm(x_ref, mod_ref, nw_ref, h_scr, seq)
    n_tiles = seq // ROW_TILE
    n_chunks = seq // CHUNK
    t_idx = lax.broadcasted_iota(jnp.int32, (CHUNK, CHUNK), 0)
    s_idx = lax.broadcasted_iota(jnp.int32, (CHUNK, CHUNK), 1)
    row_f = lax.broadcasted_iota(jnp.int32, (CHUNK, HEAD), 0).astype(f32)
    chunk_len = jnp.full((1, HEAD), CHUNK, f32)
    if use_rope:
        lane = lax.broadcasted_iota(jnp.int32, (ROW_TILE, HEAD), 1)
        first_quarter = (lane // (HEAD // 4)) % 2 == 0

    def rope(x, cos, sin_signed):
        xr = jnp.where(first_quarter, pltpu.roll(x, HEAD - HEAD // 4, axis=1), pltpu.roll(x, HEAD // 4, axis=1))
        return x * cos + xr * sin_signed

    def head_body(hh, carry):
        lg_f = lg_ref[0, hh]
        lg_b = lg_ref[1, hh]

        def proj(i, c):
            rows = _row_tile(i)
            p = _dot(h_scr[rows, :], wr_ref[hh])
            q = p[:, 0:HEAD]
            k = p[:, HEAD:2 * HEAD] * (HEAD ** -0.5)
            if use_rope:
                cos, sin_signed = cos_ref[rows, :], sin_ref[rows, :]
                q, k = rope(q, cos, sin_signed), rope(k, cos, sin_signed)
            q_s[rows, :] = q.astype(bf16)
            k_s[rows, :] = k
            v_s[rows, :] = p[:, 2 * HEAD:3 * HEAD].astype(bf16)
            return c
        lax.fori_loop(0, n_tiles, proj, 0)

        if has_state:
            sf_scr[...] = s0_ref[0, 0, hh]
            sb_scr[...] = s0_ref[0, 1, hh]
        else:
            sf_scr[...] = jnp.zeros((HEAD, HEAD), f32)
            sb_scr[...] = jnp.zeros((HEAD, HEAD), f32)

        out_b = jnp.exp(lg_b * (CHUNK - row_f))
        key_b = jnp.exp(lg_b * row_f)
        all_b = jnp.exp(lg_b * chunk_len)

        def bwd(i, c):
            rows = _chunk_rows(n_chunks - 1 - i)
            o_scr[hh, rows, :] = out_b * _dot(q_s[rows, :], sb_scr[...].astype(bf16))
            sb_scr[...] = all_b * sb_scr[...] + _dot_tn((k_s[rows, :] * key_b).astype(bf16), v_s[rows, :])
            return c
        lax.fori_loop(0, n_chunks, bwd, 0)

        dist = (t_idx - s_idx).astype(f32)
        decay = (jnp.where(t_idx >= s_idx, jnp.exp(lg_f * jnp.maximum(dist, 0.0)), 0.0)
                 + jnp.where(s_idx >= t_idx, jnp.exp(lg_b * jnp.maximum(-dist, 0.0)), 0.0))
        out_f = jnp.exp(lg_f * (row_f + 1.0))
        key_f = jnp.exp(lg_f * (CHUNK - 1.0 - row_f))
        all_f = jnp.exp(lg_f * chunk_len)

        def fwd(i, c):
            rows = _chunk_rows(i)
            q, k, v = q_s[rows, :], k_s[rows, :], v_s[rows, :]
            sc = _dot_nt(q, k.astype(bf16)) * decay
            o_scr[hh, rows, :] += _dot(sc.astype(bf16), v) + out_f * _dot(q, sf_scr[...].astype(bf16))
            sf_scr[...] = all_f * sf_scr[...] + _dot_tn((k * key_f).astype(bf16), v)
            return c
        lax.fori_loop(0, n_chunks, fwd, 0)
        if emit_state:
            st_ref[0, 0, hh] = sf_scr[...]
            st_ref[0, 1, hh] = sb_scr[...]
        return carry
    lax.fori_loop(0, N_RET, head_body, 0)

    _gated_output(x_ref, mod_ref, h_scr, o_scr, gnw_ref, wg_ref, wo_ref, fnw_ref, out_ref, seq, final)


def _const_spec(shape):
    zeros = (0,) * len(shape)
    return pl.BlockSpec(shape, lambda i: zeros, pipeline_mode=pl.Buffered(1))


def _seq_spec(shape):
    zeros = (0,) * (len(shape) - 1)
    return pl.BlockSpec((1,) + tuple(shape[1:]), lambda i: (i,) + zeros)


def _mod_spec(d, per_sequence):
    if per_sequence:
        return pl.BlockSpec((1, 3, d), lambda i: (i + 1, 0, 0))
    return pl.BlockSpec((1, 3, d), lambda i: (0, 0, 0))


def _even_layer(x, mod, norm_w, w, final_w, states, emit_state, per_sequence_mod, final):
    n_seq, seq, d = x.shape
    has_state = states is not None
    consts = [norm_w, w["wa"], w["wb"], w["wlow"], w["wgate"], w["wout"], w["gkw"], w["gkb"], w["lb"], w["gnw"], final_w]
    inputs = [x, mod] + consts
    in_specs = [_seq_spec(x.shape), _mod_spec(d, per_sequence_mod)] + [_const_spec(a.shape) for a in consts]
    if has_state:
        inputs += list(states)
        in_specs += [_seq_spec(s.shape) for s in states]
    out_shape = [jax.ShapeDtypeStruct(x.shape, f32)]
    if emit_state:
        out_shape += [jax.ShapeDtypeStruct((n_seq, 2, N_HGRN, HEAD, HEAD), f32),
                      jax.ShapeDtypeStruct((n_seq, 2, N_GLA, GLA_DK, HEAD), f32)]
    out_specs = [_seq_spec(s.shape) for s in out_shape]
    scratch = [
        pltpu.VMEM((seq, d), bf16),
        pltpu.VMEM((seq, HEAD), bf16),
        pltpu.VMEM((N_HEADS, seq, HEAD), f32),
    ] + [pltpu.VMEM((seq, HEAD), f32)] * 6 + [
        pltpu.VMEM((HEAD, HEAD), f32),
        pltpu.VMEM((2, CHUNK, CHUNK), f32),
        pltpu.VMEM((2, CHUNK, CHUNK), jnp.int32),
    ]
    outs = pl.pallas_call(
        functools.partial(_even_kernel, seq=seq, has_state=has_state, emit_state=emit_state, final=final),
        grid=(n_seq,),
        in_specs=in_specs,
        out_specs=out_specs,
        out_shape=out_shape,
        scratch_shapes=scratch,
        compiler_params=pltpu.CompilerParams(dimension_semantics=("arbitrary",), vmem_limit_bytes=VMEM_LIMIT_BYTES),
        name="even_layer_seq%d" % seq,
    )(*inputs)
    return outs


def _odd_layer(x, mod, norm_w, w, final_w, log_decay, rope, state, emit_state, per_sequence_mod, final):
    n_seq, seq, d = x.shape
    has_state = state is not None
    use_rope = rope is not None
    consts = [norm_w, w["wr"], w["wgate"], w["wout"], w["gnw"], final_w]
    if use_rope:
        consts += list(rope)
    inputs = [log_decay, x, mod] + consts
    in_specs = [pl.BlockSpec(memory_space=pltpu.SMEM), _seq_spec(x.shape), _mod_spec(d, per_sequence_mod)]
    in_specs += [_const_spec(a.shape) for a in consts]
    if has_state:
        inputs.append(state)
        in_specs.append(_seq_spec(state.shape))
    out_shape = [jax.ShapeDtypeStruct(x.shape, f32)]
    if emit_state:
        out_shape.append(jax.ShapeDtypeStruct((n_seq, 2, N_RET, HEAD, HEAD), f32))
    out_specs = [_seq_spec(s.shape) for s in out_shape]
    scratch = [
        pltpu.VMEM((seq, d), bf16),
        pltpu.VMEM((N_HEADS, seq, HEAD), f32),
        pltpu.VMEM((seq, HEAD), bf16),
        pltpu.VMEM((seq, HEAD), f32),
        pltpu.VMEM((seq, HEAD), bf16),
        pltpu.VMEM((HEAD, HEAD), f32),
        pltpu.VMEM((HEAD, HEAD), f32),
    ]
    outs = pl.pallas_call(
        functools.partial(_odd_kernel, seq=seq, has_state=has_state, emit_state=emit_state,
                          use_rope=use_rope, final=final),
        grid=(n_seq,),
        in_specs=in_specs,
        out_specs=out_specs,
        out_shape=out_shape,
        scratch_shapes=scratch,
        compiler_params=pltpu.CompilerParams(dimension_semantics=("arbitrary",), vmem_limit_bytes=VMEM_LIMIT_BYTES),
        name="odd_layer_seq%d" % seq,
    )(*inputs)
    return outs


def _per_head(w, n_heads):
    d = w.shape[0]
    return jnp.transpose(w.reshape(d, n_heads, -1), (1, 0, 2))


def _pad_last(a, width):
    return jnp.pad(a, [(0, 0)] * (a.ndim - 1) + [(0, width - a.shape[-1])])


def _even_weights(w_in, gk_w, gk_b, lb, gn_w, w_out):
    wa_w = N_HGRN * HEAD
    wb_k = N_GLA * GLA_DK
    wb_v = N_GLA * HEAD
    edges = np.cumsum([wa_w, wa_w, wa_w, wa_w, wa_w, wb_k, wb_k, wb_v, wb_v, GLA_RANK, GLA_RANK])[:-1].tolist()
    aq, ai, af_f, af_b, ag, bq, bk, bv, bg, bl_f, bl_b = jnp.split(w_in, edges, axis=-1)
    wa = jnp.concatenate([_per_head(a, N_HGRN) for a in (aq, ai, af_f, af_b)], axis=-1)
    wb = jnp.concatenate([_pad_last(_per_head(bq, N_GLA), HEAD), _pad_last(_per_head(bk, N_GLA), HEAD),
                          _per_head(bv, N_GLA)], axis=-1)
    wlow = _pad_last(jnp.concatenate([bl_f, bl_b], axis=-1), HEAD)
    gkw_heads = jnp.transpose(gk_w.reshape(2, GLA_RANK, N_GLA, GLA_DK), (0, 2, 1, 3))
    gkw = jnp.zeros((2, N_GLA, HEAD, HEAD), f32)
    for d in range(2):
        gkw = gkw.at[d, :, d * GLA_RANK:(d + 1) * GLA_RANK, :GLA_DK].set(gkw_heads[d])
    gkb = _pad_last(gk_b.reshape(2, N_GLA, 1, GLA_DK), HEAD)
    return {
        "wa": wa.astype(bf16), "wb": wb.astype(bf16), "wlow": wlow.astype(bf16),
        "wgate": jnp.concatenate([ag, bg], axis=-1).astype(bf16), "wout": w_out.astype(bf16),
        "gkw": gkw.astype(bf16), "gkb": gkb, "lb": lb.reshape(N_HGRN, 1, HEAD), "gnw": gn_w.reshape(1, -1),
    }


def _odd_weights(w_in, gn_w, w_out):
    wd = N_RET * HEAD
    q, k, v, g = jnp.split(w_in, [wd, 2 * wd, 3 * wd], axis=-1)
    wr = jnp.concatenate([_per_head(a, N_RET) for a in (q, k, v)], axis=-1)
    return {"wr": wr.astype(bf16), "wgate": g.astype(bf16), "wout": w_out.astype(bf16), "gnw": gn_w.reshape(1, -1)}


def _rope_tables(seq):
    rows = seq // GRID_W
    t_row = jnp.repeat(jnp.arange(rows), GRID_W).astype(f32)
    t_col = jnp.tile(jnp.arange(GRID_W), rows).astype(f32)
    half = HEAD // 2
    inv = ROPE_BASE ** (-jnp.arange(0, half, 2, dtype=f32) / half)
    ang_r = t_row[:, None] * inv
    ang_c = t_col[:, None] * inv
    ang = jnp.concatenate([ang_r, ang_r, ang_c, ang_c], axis=-1)
    sign = jnp.where((jnp.arange(HEAD) // (HEAD // 4)) % 2 == 0, -1.0, 1.0).astype(f32)
    return jnp.cos(ang), jnp.sin(ang) * sign


def kernel(x_prompt, x_sample, state_hgrn, state_gla, state_ret, c, c_ctx, norm_w, ada_w, ada_b, w_in_even, hgrn_lb, gla_gk_w, gla_gk_b, gn_even, w_out_even, w_in_odd, ret_decay, gn_odd, w_out_odd, final_norm_w):
    depth, d = norm_w.shape
    n_lat = x_sample.shape[0]
    n_cond = -(-(1 + n_lat) // 8) * 8
    cond = jnp.zeros((n_cond, d), f32).at[0].set(c_ctx).at[1:1 + n_lat].set(c)
    mod = _modulation(cond, ada_w, ada_b).reshape(depth, n_cond, 3, d)
    lbs = jnp.cumsum(jax.nn.softmax(hgrn_lb.astype(f32), axis=0), axis=0)
    final_w = final_norm_w.reshape(1, d)
    rope = _rope_tables(x_sample.shape[1])

    x_c, x_l = x_prompt, x_sample
    new_hgrn, new_gla, new_ret = [], [], []
    for l in range(depth):
        i = l // 2
        final = l == depth - 1
        nw = norm_w[l].reshape(1, d)
        if l % 2 == 0:
            w = _even_weights(w_in_even[i], gla_gk_w[i], gla_gk_b[i], lbs[i], gn_even[i], w_out_even[i])
            x_c, st_a, st_b = _even_layer(x_c, mod[l], nw, w, final_w, None, True, False, final)
            (x_l,) = _even_layer(x_l, mod[l], nw, w, final_w, (state_hgrn[:, i], state_gla[:, i]), False, True, final)
            new_hgrn.append(st_a)
            new_gla.append(st_b)
        else:
            w = _odd_weights(w_in_odd[i], gn_odd[i], w_out_odd[i])
            log_decay = jax.nn.log_sigmoid(ret_decay[i].astype(f32))
            x_c, st_c = _odd_layer(x_c, mod[l], nw, w, final_w, log_decay, None, None, True, False, final)
            (x_l,) = _odd_layer(x_l, mod[l], nw, w, final_w, log_decay, rope, state_ret[:, i], False, True, final)
            new_ret.append(st_c)
    return (x_c, x_l, jnp.stack(new_hgrn, axis=1), jnp.stack(new_gla, axis=1), jnp.stack(new_ret, axis=1))
```

```python
import functools

import numpy as np
import jax
import jax.numpy as jnp
from jax import lax
from jax.experimental import pallas as pl
from jax.experimental.pallas import tpu as pltpu

f32 = jnp.float32
bf16 = jnp.bfloat16
HIGHEST = lax.Precision.HIGHEST

EPS = 1e-6
LOG2E = 1.4426950408889634
HEAD = 128
N_HGRN = 4
N_GLA = 4
GLA_DK = 64
N_RET = 8
N_HEADS = 8
GLA_RANK = 16
GLA_GATE_NORM = 16.0
GRID_W = 64
ROPE_BASE = 10000.0

CHUNK = 128
LEVELS = (1, 2, 4, 8, 16, 32, 64)
DIAG_LEVEL = len(LEVELS)
N_FINE = 3
BLK_CUM = N_FINE
ROW_TOTAL = (N_FINE + 1) * CHUNK
PAIR = 2
STEP_TOKENS = 1024
ROW_TILE = 512
MOD_COLS = 768
VMEM_LIMIT_BYTES = 58 * 1024 * 1024


def _dot(a, b, precision=None):
    return jnp.dot(a, b, precision=precision, preferred_element_type=f32)


def _dot_nt(a, b):
    return lax.dot_general(a, b, (((1,), (1,)), ((), ())), preferred_element_type=f32)


def _dot_tn(a, b):
    return lax.dot_general(a, b, (((0,), (0,)), ((), ())), preferred_element_type=f32)


def _silu(x):
    return x * jax.nn.sigmoid(x)


def _softplus(x):
    return jnp.maximum(x, 0.0) + jnp.log1p(jnp.exp(-jnp.abs(x)))


def _row_tile(i):
    return pl.ds(pl.multiple_of(i * ROW_TILE, ROW_TILE), ROW_TILE)


def _chunk_rows(seq_start, c):
    return pl.ds(pl.multiple_of(seq_start + c * CHUNK, CHUNK), CHUNK)


def _mod_kernel(cond_ref, w_ref, b_ref, o_ref):
    o_ref[0] = _dot(_silu(cond_ref[...]), w_ref[0], HIGHEST) + b_ref[0]


def _modulation(cond, ada_w, ada_b):
    depth, d, d3 = ada_w.shape
    rows = cond.shape[0]
    return pl.pallas_call(
        _mod_kernel,
        grid=(depth, d3 // MOD_COLS),
        in_specs=[
            pl.BlockSpec((rows, d), lambda l, j: (0, 0)),
            pl.BlockSpec((1, d, MOD_COLS), lambda l, j: (l, 0, j)),
            pl.BlockSpec((1, 1, MOD_COLS), lambda l, j: (l, 0, j)),
        ],
        out_specs=pl.BlockSpec((1, rows, MOD_COLS), lambda l, j: (l, 0, j)),
        out_shape=jax.ShapeDtypeStruct((depth, rows, d3), f32),
        compiler_params=pltpu.CompilerParams(dimension_semantics=("arbitrary", "arbitrary")),
        name="modulation",
    )(cond, ada_w, ada_b.reshape(depth, 1, d3))


def _modulated_norm(x_ref, mod_ref, nw_ref, h_scr):
    def body(i, carry):
        rows = _row_tile(i)
        x = x_ref[0, rows, :]
        y = x * lax.rsqrt(jnp.mean(x * x, axis=-1, keepdims=True) + EPS) * nw_ref[...]
        h_scr[rows, :] = (y * (1.0 + mod_ref[0, 1:2, :]) + mod_ref[0, 0:1, :]).astype(bf16)
        return carry
    lax.fori_loop(0, STEP_TOKENS // ROW_TILE, body, 0)


def _gated_output(x_ref, mod_ref, h_scr, o_scr, gnw_ref, wg_ref, wo_ref, fnw_ref, out_ref, final):
    def body(i, carry):
        rows = _row_tile(i)
        gate = _dot(h_scr[rows, :], wg_ref[...])
        parts = []
        for hh in range(N_HEADS):
            o = o_scr[0, hh, rows, :] + o_scr[1, hh, rows, :]
            parts.append(o * lax.rsqrt(jnp.mean(o * o, axis=-1, keepdims=True) + EPS))
        y = jnp.concatenate(parts, axis=-1) * gnw_ref[...]
        z = (y * _silu(gate)).astype(bf16)
        xn = x_ref[0, rows, :] + mod_ref[0, 2:3, :] * _dot(z, wo_ref[...])
        if final:
            xn = xn * lax.rsqrt(jnp.mean(xn * xn, axis=-1, keepdims=True) + EPS) * fnw_ref[...]
        out_ref[0, rows, :] = xn
        return carry
    lax.fori_loop(0, STEP_TOKENS // ROW_TILE, body, 0)


def _scan_tables():
    t = np.arange(CHUNK)[:, None]
    j = np.arange(CHUNK)[None, :]
    fwd = []
    for m in LEVELS[:N_FINE]:
        mid = (t // (2 * m)) * (2 * m) + m
        right = t >= mid
        fwd.append(np.where(right, (j >= mid) & (j <= t), (j > t) & (j < mid)))
    fwd.append(j <= t)
    fwd.append(np.ones((8, CHUNK), bool))
    fwd = np.concatenate(fwd, axis=0).astype(np.float32)
    bwd = fwd.copy()
    n_sym = ROW_TOTAL // CHUNK
    bwd[:ROW_TOTAL] = fwd[:ROW_TOTAL].reshape(n_sym, CHUNK, CHUNK)[:, ::-1, ::-1].reshape(ROW_TOTAL, CHUNK)
    table = np.stack([np.tile(fwd, (1, 2)), np.tile(bwd, (1, 2))])
    x = t ^ j
    lvl = np.zeros((CHUNK, CHUNK), np.int32)
    for b in range(1, len(LEVELS)):
        lvl += (x >= (1 << b)).astype(np.int32)
    lvl_f = np.where(t > j, lvl, np.where(t == j, DIAG_LEVEL, -1)).astype(np.int32)
    return jnp.asarray(table, bf16), jnp.asarray(np.stack([lvl_f, lvl_f.T]))


def _split2(x):
    hi = x.astype(bf16)
    lo = (x - hi.astype(f32)).astype(bf16)
    return jnp.concatenate([hi, lo], axis=0)


def _neg_abs(x):
    return -jnp.abs(x)


def _block_mid(b, m, rev):
    blk = 2 * m
    r = m if rev else m - 1
    x = b.reshape(CHUNK // blk, blk, b.shape[-1])
    return jnp.broadcast_to(x[:, r:r + 1, :], x.shape).reshape(b.shape)


def _gated_chunk(q, k, v, g, states, table, lvl, rev):
    sums = _dot(table, _split2(g * LOG2E))
    cum = sums[BLK_CUM * CHUNK:(BLK_CUM + 1) * CHUNK, :]
    total = sums[ROW_TOTAL:ROW_TOTAL + 1, :]
    level_decay = [jnp.exp2(sums[i * CHUNK:(i + 1) * CHUNK, :]) for i in range(N_FINE)]
    level_decay += [jnp.exp2(_neg_abs(cum - _block_mid(cum, m, rev))) for m in LEVELS[N_FINE:]]
    decay_in = jnp.exp2(cum)
    decay_out = jnp.exp2(total - cum)
    outs, new_states = [], []
    for h in range(PAIR):
        lanes = slice(h * HEAD, (h + 1) * HEAD)
        qh, kh, vb = q[:, lanes], k[:, lanes], v[:, lanes].astype(bf16)
        sc = jnp.where(lvl == DIAG_LEVEL, _dot_nt(qh.astype(bf16), kh.astype(bf16)), 0.0)
        for i, e in enumerate(level_decay):
            eh = e[:, lanes]
            sc = jnp.where(lvl == i, _dot_nt((qh * eh).astype(bf16), (kh * eh).astype(bf16)), sc)
        o = _dot((qh * decay_in[:, lanes]).astype(bf16), states[h].astype(bf16)) + _dot(sc.astype(bf16), vb)
        kv = _dot_tn((kh * decay_out[:, lanes]).astype(bf16), vb)
        whole = jnp.broadcast_to(jnp.exp2(total[:, lanes]), (HEAD, HEAD)).T
        outs.append(o)
        new_states.append(whole * states[h] + kv)
    return outs, new_states


def _gated_scans(q_s, k_refs, v_s, g_refs, s_scr, table_ref, lvl_ref, o_scr, head0, seq):
    n = seq // CHUNK

    def seq_body(j, carry):
        def chunk_body(i, c):
            for d in range(2):
                rows = _chunk_rows(j * seq, n - 1 - i if d else i)
                outs, new_states = _gated_chunk(
                    q_s[rows, :], k_refs[d][rows, :], v_s[rows, :], g_refs[d][rows, :],
                    [s_scr[j, d, h] for h in range(PAIR)], table_ref[d], lvl_ref[d], bool(d))
                for h in range(PAIR):
                    o_scr[d, head0 + h, rows, :] = outs[h]
                    s_scr[j, d, h] = new_states[h]
            return c
        lax.fori_loop(0, n, chunk_body, 0)
        return carry
    lax.fori_loop(0, STEP_TOKENS // seq, seq_body, 0)


def _even_kernel(*refs, seq, has_state, emit_state, final):
    it = iter(refs)
    x_ref, mod_ref, nw_ref = next(it), next(it), next(it)
    wa_ref, wb_ref, wlow_ref, wg_ref, wo_ref = next(it), next(it), next(it), next(it), next(it)
    gkw_ref, gkb_ref, lb_ref, gnw_ref, fnw_ref = next(it), next(it), next(it), next(it), next(it)
    table_ref, lvl_ref = next(it), next(it)
    s0a_ref, s0b_ref = (next(it), next(it)) if has_state else (None, None)
    out_ref = next(it)
    sta_ref, stb_ref = (next(it), next(it)) if emit_state else (None, None)
    h_scr, low_scr, o_scr, q_s, v_s, kf_s, kb_s, gf_s, gb_s, s_scr = it
    n_seq = STEP_TOKENS // seq
    n_tiles = STEP_TOKENS // ROW_TILE

    _modulated_norm(x_ref, mod_ref, nw_ref, h_scr)

    def low_body(i, carry):
        rows = _row_tile(i)
        low_scr[rows, :] = _dot(h_scr[rows, :], wlow_ref[...]).astype(bf16)
        return carry
    lax.fori_loop(0, n_tiles, low_body, 0)

    width = PAIR * HEAD

    def run_scans(head0, s0_ref, st_ref, h0, key_rows, k_refs):
        for j in range(n_seq):
            for d in range(2):
                for h in range(PAIR):
                    if s0_ref is None:
                        s_scr[j, d, h] = jnp.zeros((HEAD, HEAD), f32)
                    else:
                        if key_rows < HEAD:
                            s_scr[j, d, h, pl.ds(key_rows, HEAD - key_rows), :] = jnp.zeros(
                                (HEAD - key_rows, HEAD), f32)
                        s_scr[j, d, h, pl.ds(0, key_rows), :] = s0_ref[j, d, h0 + h]
        _gated_scans(q_s, k_refs, v_s, (gf_s, gb_s), s_scr, table_ref, lvl_ref, o_scr, head0, seq)
        if st_ref is not None:
            for j in range(n_seq):
                for d in range(2):
                    for h in range(PAIR):
                        st_ref[j, d, h0 + h] = s_scr[j, d, h, pl.ds(0, key_rows), :]

    def hgrn_pair(pp, carry):
        lb = lb_ref[pp]
        log_lb = jnp.log(lb)

        def proj(i, c):
            rows = _row_tile(i)
            p = _dot(h_scr[rows, :], wa_ref[pp])
            q_s[rows, :] = _silu(p[:, 0:width])
            v_s[rows, :] = p[:, width:2 * width]
            for a, k_s, g_s in ((p[:, 2 * width:3 * width], kf_s, gf_s), (p[:, 3 * width:4 * width], kb_s, gb_s)):
                g_s[rows, :] = _softplus(log_lb - a) - _softplus(-a)
                k_s[rows, :] = (1.0 - lb) * jax.nn.sigmoid(-a)
            return c
        lax.fori_loop(0, n_tiles, proj, 0)
        run_scans(PAIR * pp, s0a_ref, sta_ref, PAIR * pp, HEAD, (kf_s, kb_s))
        return carry
    lax.fori_loop(0, N_HGRN // PAIR, hgrn_pair, 0)

    def gla_pair(pp, carry):
        def proj(i, c):
            rows = _row_tile(i)
            p = _dot(h_scr[rows, :], wb_ref[pp])
            q_s[rows, :] = p[:, 0:width] * (GLA_DK ** -0.5)
            kf_s[rows, :] = p[:, width:2 * width]
            v_s[rows, :] = p[:, 2 * width:3 * width]
            low = low_scr[rows, :]
            for d, g_s in enumerate((gf_s, gb_s)):
                logits = _dot(low, gkw_ref[d, pp]) + gkb_ref[d, pp]
                g_s[rows, :] = -_softplus(-logits) * (1.0 / GLA_GATE_NORM)
            return c
        lax.fori_loop(0, n_tiles, proj, 0)
        run_scans(N_HGRN + PAIR * pp, s0b_ref, stb_ref, PAIR * pp, GLA_DK, (kf_s, kf_s))
        return carry
    lax.fori_loop(0, N_GLA // PAIR, gla_pair, 0)

    _gated_output(x_ref, mod_ref, h_scr, o_scr, gnw_ref, wg_ref, wo_ref, fnw_ref, out_ref, final)


def _odd_kernel(*refs, seq, has_state, emit_state, use_rope, final):
    it = iter(refs)
    lg_ref = next(it)
    x_ref, mod_ref, nw_ref = next(it), next(it), next(it)
    wr_ref, wg_ref, wo_ref, gnw_ref, fnw_ref = next(it), next(it), next(it), next(it), next(it)
    cos_ref, sin_ref = (next(it), next(it)) if use_rope else (None, None)
    s0_ref = next(it) if has_state else None
    out_ref = next(it)
    st_ref = next(it) if emit_state else None
    h_scr, o_scr, q_s, k_s, v_s, s_scr = it
    n_seq = STEP_TOKENS // seq
    n_tiles = STEP_TOKENS // ROW_TILE
    n_chunks = seq // CHUNK

    _modulated_norm(x_ref, mod_ref, nw_ref, h_scr)
    t_idx = lax.broadcasted_iota(jnp.int32, (CHUNK, CHUNK), 0)
    s_idx = lax.broadcasted_iota(jnp.int32, (CHUNK, CHUNK), 1)
    row_f = lax.broadcasted_iota(jnp.int32, (CHUNK, HEAD), 0).astype(f32)
    chunk_len = jnp.full((1, HEAD), CHUNK, f32)
    if use_rope:
        lane = lax.broadcasted_iota(jnp.int32, (ROW_TILE, HEAD), 1)
        first_quarter = (lane // (HEAD // 4)) % 2 == 0

    def rope(x, cos, sin_signed):
        xr = jnp.where(first_quarter, pltpu.roll(x, HEAD - HEAD // 4, axis=1), pltpu.roll(x, HEAD // 4, axis=1))
        return x * cos + xr * sin_signed

    def head_body(hh, carry):
        lg_f = lg_ref[0, hh]
        lg_b = lg_ref[1, hh]

        def proj(i, c):
            rows = _row_tile(i)
            p = _dot(h_scr[rows, :], wr_ref[hh])
            q = p[:, 0:HEAD]
            k = p[:, HEAD:2 * HEAD] * (HEAD ** -0.5)
            if use_rope:
                cos, sin_signed = cos_ref[rows, :], sin_ref[rows, :]
                q, k = rope(q, cos, sin_signed), rope(k, cos, sin_signed)
            q_s[rows, :] = q.astype(bf16)
            k_s[rows, :] = k
            v_s[rows, :] = p[:, 2 * HEAD:3 * HEAD].astype(bf16)
            return c
        lax.fori_loop(0, n_tiles, proj, 0)

        for j in range(n_seq):
            for d in range(2):
                s_scr[j, d] = s0_ref[j, d, hh] if has_state else jnp.zeros((HEAD, HEAD), f32)

        in_f = jnp.exp(lg_f * (row_f + 1.0))
        out_f = jnp.exp(lg_f * (CHUNK - 1.0 - row_f))
        all_f = jnp.exp(lg_f * chunk_len)
        in_b = jnp.exp(lg_b * (CHUNK - row_f))
        out_b = jnp.exp(lg_b * row_f)
        all_b = jnp.exp(lg_b * chunk_len)
        dist = (t_idx - s_idx).astype(f32)
        pair = (jnp.where(t_idx >= s_idx, jnp.exp(lg_f * jnp.maximum(dist, 0.0)), 0.0)
                + jnp.where(s_idx >= t_idx, jnp.exp(lg_b * jnp.maximum(-dist, 0.0)), 0.0))

        def body(i, c):
            for j in range(n_seq):
                rows = _chunk_rows(j * seq, n_chunks - 1 - i)
                q, k, v = q_s[rows, :], k_s[rows, :], v_s[rows, :]
                o_scr[1, hh, rows, :] = in_b * _dot(q, s_scr[j, 1].astype(bf16))
                s_scr[j, 1] = all_b * s_scr[j, 1] + _dot_tn((k * out_b).astype(bf16), v)
                rows = _chunk_rows(j * seq, i)
                q, k, v = q_s[rows, :], k_s[rows, :], v_s[rows, :]
                sc = _dot_nt(q, k.astype(bf16)) * pair
                o_scr[0, hh, rows, :] = _dot(sc.astype(bf16), v) + in_f * _dot(q, s_scr[j, 0].astype(bf16))
                s_scr[j, 0] = all_f * s_scr[j, 0] + _dot_tn((k * out_f).astype(bf16), v)
            return c
        lax.fori_loop(0, n_chunks, body, 0)
        if emit_state:
            for j in range(n_seq):
                for d in range(2):
                    st_ref[j, d, hh] = s_scr[j, d]
        return carry
    lax.fori_loop(0, N_RET, head_body, 0)

    _gated_output(x_ref, mod_ref, h_scr, o_scr, gnw_ref, wg_ref, wo_ref, fnw_ref, out_ref, final)


def _const_spec(shape):
    zeros = (0,) * len(shape)
    return pl.BlockSpec(shape, lambda i: zeros, pipeline_mode=pl.Buffered(1))


def _step_spec(shape, per_step):
    zeros = (0,) * (len(shape) - 1)
    return pl.BlockSpec((per_step,) + tuple(shape[1:]), lambda i: (i,) + zeros)


def _mod_spec(d, per_sequence):
    if per_sequence:
        return pl.BlockSpec((1, 3, d), lambda i: (i + 1, 0, 0))
    return pl.BlockSpec((1, 3, d), lambda i: (0, 0, 0))


def _layer_call(body, x, mod, consts, states, state_shapes, per_sequence_mod, scratch, name, smem_inputs=()):
    n_seq, seq, d = x.shape
    per_step = STEP_TOKENS // seq
    assert per_step * seq == STEP_TOKENS and n_seq % per_step == 0 and seq % CHUNK == 0
    assert not per_sequence_mod or per_step == 1
    n_steps = n_seq // per_step
    xs = x.reshape(n_steps, STEP_TOKENS, d)
    inputs = list(smem_inputs) + [xs, mod] + list(consts) + list(states)
    in_specs = [pl.BlockSpec(memory_space=pltpu.SMEM)] * len(smem_inputs)
    in_specs += [_step_spec(xs.shape, 1), _mod_spec(d, per_sequence_mod)]
    in_specs += [_const_spec(a.shape) for a in consts]
    in_specs += [_step_spec(s.shape, per_step) for s in states]
    out_shape = [jax.ShapeDtypeStruct(xs.shape, f32)] + [jax.ShapeDtypeStruct(s, f32) for s in state_shapes]
    out_specs = [_step_spec(xs.shape, 1)] + [_step_spec(s, per_step) for s in state_shapes]
    outs = pl.pallas_call(
        body,
        grid=(n_steps,),
        in_specs=in_specs,
        out_specs=out_specs,
        out_shape=out_shape,
        scratch_shapes=scratch,
        compiler_params=pltpu.CompilerParams(dimension_semantics=("arbitrary",), vmem_limit_bytes=VMEM_LIMIT_BYTES),
        name=name,
    )(*inputs)
    return [outs[0].reshape(x.shape)] + list(outs[1:])


def _even_layer(x, mod, norm_w, w, final_w, tables, states, emit_state, per_sequence_mod, final):
    n_seq, seq, d = x.shape
    consts = [norm_w, w["wa"], w["wb"], w["wlow"], w["wgate"], w["wout"], w["gkw"], w["gkb"], w["lb"], w["gnw"],
              final_w] + list(tables)
    state_shapes = [(n_seq, 2, N_HGRN, HEAD, HEAD), (n_seq, 2, N_GLA, GLA_DK, HEAD)] if emit_state else []
    scratch = [
        pltpu.VMEM((STEP_TOKENS, d), bf16),
        pltpu.VMEM((STEP_TOKENS, HEAD), bf16),
        pltpu.VMEM((2, N_HEADS, STEP_TOKENS, HEAD), f32),
    ] + [pltpu.VMEM((STEP_TOKENS, PAIR * HEAD), f32)] * 6 + [
        pltpu.VMEM((STEP_TOKENS // seq, 2, PAIR, HEAD, HEAD), f32),
    ]
    body = functools.partial(_even_kernel, seq=seq, has_state=states is not None, emit_state=emit_state, final=final)
    return _layer_call(body, x, mod, consts, states or (), state_shapes, per_sequence_mod, scratch,
                       "even_layer_seq%d" % seq)


def _odd_layer(x, mod, norm_w, w, final_w, log_decay, rope, state, emit_state, per_sequence_mod, final):
    n_seq, seq, d = x.shape
    consts = [norm_w, w["wr"], w["wgate"], w["wout"], w["gnw"], final_w] + list(rope or ())
    state_shapes = [(n_seq, 2, N_RET, HEAD, HEAD)] if emit_state else []
    scratch = [
        pltpu.VMEM((STEP_TOKENS, d), bf16),
        pltpu.VMEM((2, N_HEADS, STEP_TOKENS, HEAD), f32),
        pltpu.VMEM((STEP_TOKENS, HEAD), bf16),
        pltpu.VMEM((STEP_TOKENS, HEAD), f32),
        pltpu.VMEM((STEP_TOKENS, HEAD), bf16),
        pltpu.VMEM((STEP_TOKENS // seq, 2, HEAD, HEAD), f32),
    ]
    body = functools.partial(_odd_kernel, seq=seq, has_state=state is not None, emit_state=emit_state,
                             use_rope=rope is not None, final=final)
    return _layer_call(body, x, mod, consts, () if state is None else (state,), state_shapes, per_sequence_mod,
                       scratch, "odd_layer_seq%d" % seq, smem_inputs=(log_decay,))


def _per_head(w, n_heads):
    d = w.shape[0]
    return jnp.transpose(w.reshape(d, n_heads, -1), (1, 0, 2))


def _per_pair(w, n_heads, head_width=HEAD):
    d = w.shape[0]
    w = _pad_last(w.reshape(d, n_heads, -1), head_width)
    return jnp.transpose(w.reshape(d, n_heads // PAIR, PAIR * head_width), (1, 0, 2))


def _pad_last(a, width):
    return jnp.pad(a, [(0, 0)] * (a.ndim - 1) + [(0, width - a.shape[-1])])


def _even_weights(w_in, gk_w, gk_b, lb, gn_w, w_out):
    wa_w = N_HGRN * HEAD
    wb_k = N_GLA * GLA_DK
    wb_v = N_GLA * HEAD
    edges = np.cumsum([wa_w, wa_w, wa_w, wa_w, wa_w, wb_k, wb_k, wb_v, wb_v, GLA_RANK, GLA_RANK])[:-1].tolist()
    aq, ai, af_f, af_b, ag, bq, bk, bv, bg, bl_f, bl_b = jnp.split(w_in, edges, axis=-1)
    wa = jnp.concatenate([_per_pair(a, N_HGRN) for a in (aq, ai, af_f, af_b)], axis=-1)
    wb = jnp.concatenate([_per_pair(a, N_GLA) for a in (bq, bk, bv)], axis=-1)
    wlow = _pad_last(jnp.concatenate([bl_f, bl_b], axis=-1), HEAD)
    gkw = jnp.stack([
        jnp.pad(_per_pair(gk_w[d], N_GLA), ((0, 0), (d * GLA_RANK, HEAD - (d + 1) * GLA_RANK), (0, 0)))
        for d in range(2)])
    gkb = _pad_last(gk_b.reshape(2, N_GLA, GLA_DK), HEAD).reshape(2, N_GLA // PAIR, 1, PAIR * HEAD)
    return {
        "wa": wa.astype(bf16), "wb": wb.astype(bf16), "wlow": wlow.astype(bf16),
        "wgate": jnp.concatenate([ag, bg], axis=-1).astype(bf16), "wout": w_out.astype(bf16),
        "gkw": gkw.astype(bf16), "gkb": gkb, "lb": lb.reshape(N_HGRN // PAIR, 1, PAIR * HEAD),
        "gnw": gn_w.reshape(1, -1),
    }


def _odd_weights(w_in, gn_w, w_out):
    wd = N_RET * HEAD
    q, k, v, g = jnp.split(w_in, [wd, 2 * wd, 3 * wd], axis=-1)
    wr = jnp.concatenate([_per_head(a, N_RET) for a in (q, k, v)], axis=-1)
    return {"wr": wr.astype(bf16), "wgate": g.astype(bf16), "wout": w_out.astype(bf16), "gnw": gn_w.reshape(1, -1)}


def _rope_tables(seq):
    rows = seq // GRID_W
    t_row = jnp.repeat(jnp.arange(rows), GRID_W).astype(f32)
    t_col = jnp.tile(jnp.arange(GRID_W), rows).astype(f32)
    half = HEAD // 2
    inv = ROPE_BASE ** (-jnp.arange(0, half, 2, dtype=f32) / half)
    ang_r = t_row[:, None] * inv
    ang_c = t_col[:, None] * inv
    ang = jnp.concatenate([ang_r, ang_r, ang_c, ang_c], axis=-1)
    sign = jnp.where((jnp.arange(HEAD) // (HEAD // 4)) % 2 == 0, -1.0, 1.0).astype(f32)
    return jnp.cos(ang), jnp.sin(ang) * sign


def kernel(x_prompt, x_sample, state_hgrn, state_gla, state_ret, c, c_ctx, norm_w, ada_w, ada_b, w_in_even, hgrn_lb, gla_gk_w, gla_gk_b, gn_even, w_out_even, w_in_odd, ret_decay, gn_odd, w_out_odd, final_norm_w):
    depth, d = norm_w.shape
    n_lat = x_sample.shape[0]
    n_cond = -(-(1 + n_lat) // 8) * 8
    cond = jnp.zeros((n_cond, d), f32).at[0].set(c_ctx).at[1:1 + n_lat].set(c)
    mod = _modulation(cond, ada_w, ada_b).reshape(depth, n_cond, 3, d)
    lbs = jnp.cumsum(jax.nn.softmax(hgrn_lb.astype(f32), axis=0), axis=0)
    final_w = final_norm_w.reshape(1, d)
    rope = _rope_tables(x_sample.shape[1])
    tables = _scan_tables()

    x_c, x_l = x_prompt, x_sample
    new_hgrn, new_gla, new_ret = [], [], []
    for l in range(depth):
        i = l // 2
        final = l == depth - 1
        nw = norm_w[l].reshape(1, d)
        if l % 2 == 0:
            w = _even_weights(w_in_even[i], gla_gk_w[i], gla_gk_b[i], lbs[i], gn_even[i], w_out_even[i])
            x_c, st_a, st_b = _even_layer(x_c, mod[l], nw, w, final_w, tables, None, True, False, final)
            (x_l,) = _even_layer(x_l, mod[l], nw, w, final_w, tables, (state_hgrn[:, i], state_gla[:, i]),
                                 False, True, final)
            new_hgrn.append(st_a)
            new_gla.append(st_b)
        else:
            w = _odd_weights(w_in_odd[i], gn_odd[i], w_out_odd[i])
            log_decay = jax.nn.log_sigmoid(ret_decay[i].astype(f32))
            x_c, st_c = _odd_layer(x_c, mod[l], nw, w, final_w, log_decay, None, None, True, False, final)
            (x_l,) = _odd_layer(x_l, mod[l], nw, w, final_w, log_decay, rope, state_ret[:, i], False, True, final)
            new_ret.append(st_c)
    return (x_c, x_l, jnp.stack(new_hgrn, axis=1), jnp.stack(new_gla, axis=1), jnp.stack(new_ret, axis=1))
```

```python
import functools

import numpy as np
import jax
import jax.numpy as jnp
from jax import lax
from jax.experimental import pallas as pl
from jax.experimental.pallas import tpu as pltpu

f32 = jnp.float32
bf16 = jnp.bfloat16
HIGHEST = lax.Precision.HIGHEST

EPS = 1e-6
LOG2E = 1.4426950408889634
HEAD = 128
N_HGRN = 4
N_GLA = 4
GLA_DK = 64
N_RET = 8
N_HEADS = 8
GLA_RANK = 16
GLA_GATE_NORM = 16.0
GRID_W = 64
ROPE_BASE = 10000.0

CHUNK = 128
LEVELS = (1, 2, 4, 8, 16, 32, 64)
DIAG_LEVEL = len(LEVELS)
N_FINE = 3
BLK_CUM = N_FINE
ROW_TOTAL = (N_FINE + 1) * CHUNK
PAIR = 2
CHUNKS_PER_ITER = 2
STEP_TOKENS = 1024
ROW_TILE = 512
MOD_COLS = 768
VMEM_LIMIT_BYTES = 58 * 1024 * 1024


def _dot(a, b, precision=None):
    return jnp.dot(a, b, precision=precision, preferred_element_type=f32)


def _dot_nt(a, b):
    return jnp.dot(a, b.T, preferred_element_type=f32)


def _dot_tn(a, b):
    return lax.dot_general(a, b, (((0,), (0,)), ((), ())), preferred_element_type=f32)


def _silu(x):
    return x * jax.nn.sigmoid(x)


def _softplus(x):
    return jnp.maximum(x, 0.0) + jnp.log1p(jnp.exp(-jnp.abs(x)))


def _row_tile(i):
    return pl.ds(pl.multiple_of(i * ROW_TILE, ROW_TILE), ROW_TILE)


def _chunk_rows(seq_start, c):
    return pl.ds(pl.multiple_of(seq_start + c * CHUNK, CHUNK), CHUNK)


def _mod_kernel(cond_ref, w_ref, b_ref, o_ref):
    o_ref[0] = _dot(_silu(cond_ref[...]), w_ref[0], HIGHEST) + b_ref[0]


def _modulation(cond, ada_w, ada_b):
    depth, d, d3 = ada_w.shape
    rows = cond.shape[0]
    return pl.pallas_call(
        _mod_kernel,
        grid=(depth, d3 // MOD_COLS),
        in_specs=[
            pl.BlockSpec((rows, d), lambda l, j: (0, 0)),
            pl.BlockSpec((1, d, MOD_COLS), lambda l, j: (l, 0, j)),
            pl.BlockSpec((1, 1, MOD_COLS), lambda l, j: (l, 0, j)),
        ],
        out_specs=pl.BlockSpec((1, rows, MOD_COLS), lambda l, j: (l, 0, j)),
        out_shape=jax.ShapeDtypeStruct((depth, rows, d3), f32),
        compiler_params=pltpu.CompilerParams(dimension_semantics=("arbitrary", "arbitrary")),
        name="modulation",
    )(cond, ada_w, ada_b.reshape(depth, 1, d3))


def _modulated_norm(x_ref, mod_ref, nw_ref, h_scr):
    def body(i, carry):
        rows = _row_tile(i)
        x = x_ref[0, rows, :]
        y = x * lax.rsqrt(jnp.mean(x * x, axis=-1, keepdims=True) + EPS) * nw_ref[...]
        h_scr[rows, :] = (y * (1.0 + mod_ref[0, 1:2, :]) + mod_ref[0, 0:1, :]).astype(bf16)
        return carry
    lax.fori_loop(0, STEP_TOKENS // ROW_TILE, body, 0)


def _gated_output(x_ref, mod_ref, h_scr, o_scr, gnw_ref, wg_ref, wo_ref, fnw_ref, out_ref, final):
    def body(i, carry):
        rows = _row_tile(i)
        gate = _dot(h_scr[rows, :], wg_ref[...])
        parts = []
        for hh in range(N_HEADS):
            o = o_scr[0, hh, rows, :] + o_scr[1, hh, rows, :]
            parts.append(o * lax.rsqrt(jnp.mean(o * o, axis=-1, keepdims=True) + EPS))
        y = jnp.concatenate(parts, axis=-1) * gnw_ref[...]
        z = (y * _silu(gate)).astype(bf16)
        xn = x_ref[0, rows, :] + mod_ref[0, 2:3, :] * _dot(z, wo_ref[...])
        if final:
            xn = xn * lax.rsqrt(jnp.mean(xn * xn, axis=-1, keepdims=True) + EPS) * fnw_ref[...]
        out_ref[0, rows, :] = xn
        return carry
    lax.fori_loop(0, STEP_TOKENS // ROW_TILE, body, 0)


def _scan_tables():
    t = np.arange(CHUNK)[:, None]
    j = np.arange(CHUNK)[None, :]
    fwd = []
    for m in LEVELS[:N_FINE]:
        mid = (t // (2 * m)) * (2 * m) + m
        right = t >= mid
        fwd.append(np.where(right, (j >= mid) & (j <= t), (j > t) & (j < mid)))
    fwd.append(j <= t)
    fwd.append(np.ones((8, CHUNK), bool))
    fwd = np.concatenate(fwd, axis=0).astype(np.float32)
    bwd = fwd.copy()
    n_sym = ROW_TOTAL // CHUNK
    bwd[:ROW_TOTAL] = fwd[:ROW_TOTAL].reshape(n_sym, CHUNK, CHUNK)[:, ::-1, ::-1].reshape(ROW_TOTAL, CHUNK)
    table = np.stack([np.tile(fwd, (1, 2)), np.tile(bwd, (1, 2))])
    x = t ^ j
    lvl = np.zeros((CHUNK, CHUNK), np.int32)
    for b in range(1, len(LEVELS)):
        lvl += (x >= (1 << b)).astype(np.int32)
    lvl_f = np.where(t > j, lvl, np.where(t == j, DIAG_LEVEL, -1)).astype(np.int32)
    return jnp.asarray(table, bf16), jnp.asarray(np.stack([lvl_f, lvl_f.T]))


def _split2(x):
    hi = x.astype(bf16)
    lo = (x - hi.astype(f32)).astype(bf16)
    return jnp.concatenate([hi, lo], axis=0)


def _neg_abs(x):
    return -jnp.abs(x)


def _block_mid(b, m, rev):
    blk = 2 * m
    r = m if rev else m - 1
    x = b.reshape(CHUNK // blk, blk, b.shape[-1])
    return jnp.broadcast_to(x[:, r:r + 1, :], x.shape).reshape(b.shape)


def _in_chunk_scores(q, k, cum, fine, lvl, ones, rev):
    def key_major(x):
        return x.T * ones

    sc = jnp.where(lvl == DIAG_LEVEL, _dot(q.astype(bf16), key_major(k.astype(bf16))), 0.0)
    for i in range(N_FINE):
        e = jnp.exp2(fine[i])
        sc = jnp.where(lvl == i, _dot((q * e).astype(bf16), key_major((k * e).astype(bf16))), sc)
    for i in range(N_FINE, len(LEVELS)):
        m = LEVELS[i]
        blocks = []
        for p0 in range(0, CHUNK, 2 * m):
            left, right = slice(p0, p0 + m), slice(p0 + m, p0 + 2 * m)
            q_side, k_side = (left, right) if rev else (right, left)
            mid_row = p0 + m if rev else p0 + m - 1
            blocks.append((q_side, k_side, cum[mid_row:mid_row + 1, :]))
        q_rows = [(q[qs] * jnp.exp2(cum[qs] - mid)).astype(bf16) for qs, _, mid in blocks]
        k_rows = []
        for _, ks, mid in blocks:
            scaled = (k[ks] * jnp.exp2(mid - cum[ks])).astype(bf16)
            zero = jnp.zeros((m, HEAD), bf16)
            k_rows += [zero, scaled] if rev else [scaled, zero]
        s = _dot(jnp.concatenate(q_rows, axis=0), key_major(jnp.concatenate(k_rows, axis=0)))
        rows = []
        for b, (qs, ks, _) in enumerate(blocks):
            updated = jnp.where(lvl[qs, :] == i, s[b * m:(b + 1) * m, :], sc[qs, :])
            rows += [updated, sc[ks, :]] if rev else [sc[ks, :], updated]
        sc = jnp.concatenate(rows, axis=0)
    return sc.astype(bf16)


def _gated_chunks(chains):
    heads = [slice(h * HEAD, (h + 1) * HEAD) for h in range(PAIR)]
    flat = [(c, r) for c, (_, units) in enumerate(chains) for r in range(len(units))]
    unit = {(c, r): chains[c][1][r] for c, r in flat}
    sums = {key: _dot(unit[key][4], _split2(unit[key][3] * LOG2E)) for key in flat}
    cum = {key: sums[key][BLK_CUM * CHUNK:(BLK_CUM + 1) * CHUNK, :] for key in flat}
    total = {key: sums[key][ROW_TOTAL:ROW_TOTAL + 1, :] for key in flat}
    vb = {key: [unit[key][2][:, lanes].astype(bf16) for lanes in heads] for key in flat}
    kv, whole = {}, {}
    for key in flat:
        k = unit[key][1]
        decay_out = jnp.exp2(total[key] - cum[key])
        kv[key] = [_dot_tn((k[:, lanes] * decay_out[:, lanes]).astype(bf16), vb[key][h])
                   for h, lanes in enumerate(heads)]
        whole[key] = [jnp.broadcast_to(jnp.exp2(total[key][:, lanes]), (HEAD, HEAD)).T for lanes in heads]
    state = [list(states) for states, _ in chains]
    outs, scores = {}, {}
    for r in range(max(len(units) for _, units in chains)):
        live = [key for key in flat if key[1] == r]
        for key in live:
            q = unit[key][0]
            decay_in = jnp.exp2(cum[key])
            outs[key] = [_dot((q[:, lanes] * decay_in[:, lanes]).astype(bf16), state[key[0]][h].astype(bf16))
                         for h, lanes in enumerate(heads)]
            state[key[0]] = [whole[key][h] * state[key[0]][h] + kv[key][h] for h in range(PAIR)]
        for key in live:
            q, k, _, _, _, lvl, rev = unit[key]
            scores[key] = []
            ones = (lvl[0:1, :] >= -1).astype(bf16)
            for h, lanes in enumerate(heads):
                fine = [sums[key][i * CHUNK:(i + 1) * CHUNK, lanes] for i in range(N_FINE)]
                scores[key].append(_in_chunk_scores(q[:, lanes], k[:, lanes], cum[key][:, lanes], fine, lvl, ones, rev))
    for key in flat:
        outs[key] = [outs[key][h] + _dot(scores[key][h], vb[key][h]) for h in range(PAIR)]
    return [[outs[(c, r)] for r in range(len(units))] for c, (_, units) in enumerate(chains)], state


def _gated_scans(q_s, k_refs, v_s, g_refs, s_scr, table_ref, lvl_ref, o_scr, head0, seq):
    n = seq // CHUNK
    assert n % CHUNKS_PER_ITER == 0

    def seq_body(j, carry):
        def chunk_body(i, c):
            chains, rows = [], []
            for d in range(2):
                steps = [i * CHUNKS_PER_ITER + r for r in range(CHUNKS_PER_ITER)]
                rows.append([_chunk_rows(j * seq, n - 1 - t if d else t) for t in steps])
                units = [(q_s[rw, :], k_refs[d][rw, :], v_s[rw, :], g_refs[d][rw, :], table_ref[d], lvl_ref[d], bool(d))
                         for rw in rows[d]]
                chains.append(([s_scr[j, d, h] for h in range(PAIR)], units))
            outs, new_states = _gated_chunks(chains)
            for d in range(2):
                for h in range(PAIR):
                    for r, rw in enumerate(rows[d]):
                        o_scr[d, head0 + h, rw, :] = outs[d][r][h]
                    s_scr[j, d, h] = new_states[d][h]
            return c
        lax.fori_loop(0, n // CHUNKS_PER_ITER, chunk_body, 0)
        return carry
    lax.fori_loop(0, STEP_TOKENS // seq, seq_body, 0)


def _even_kernel(*refs, seq, has_state, emit_state, final):
    it = iter(refs)
    x_ref, mod_ref, nw_ref = next(it), next(it), next(it)
    wa_ref, wb_ref, wlow_ref, wg_ref, wo_ref = next(it), next(it), next(it), next(it), next(it)
    gkw_ref, gkb_ref, lb_ref, gnw_ref, fnw_ref = next(it), next(it), next(it), next(it), next(it)
    table_ref, lvl_ref = next(it), next(it)
    s0a_ref, s0b_ref = (next(it), next(it)) if has_state else (None, None)
    out_ref = next(it)
    sta_ref, stb_ref = (next(it), next(it)) if emit_state else (None, None)
    h_scr, low_scr, o_scr, q_s, v_s, kf_s, kb_s, gf_s, gb_s, s_scr = it
    n_seq = STEP_TOKENS // seq
    n_tiles = STEP_TOKENS // ROW_TILE

    _modulated_norm(x_ref, mod_ref, nw_ref, h_scr)

    def low_body(i, carry):
        rows = _row_tile(i)
        low_scr[rows, :] = _dot(h_scr[rows, :], wlow_ref[...]).astype(bf16)
        return carry
    lax.fori_loop(0, n_tiles, low_body, 0)

    width = PAIR * HEAD

    def run_scans(head0, s0_ref, st_ref, h0, key_rows, k_refs):
        for j in range(n_seq):
            for d in range(2):
                for h in range(PAIR):
                    if s0_ref is None:
                        s_scr[j, d, h] = jnp.zeros((HEAD, HEAD), f32)
                    else:
                        if key_rows < HEAD:
                            s_scr[j, d, h, pl.ds(key_rows, HEAD - key_rows), :] = jnp.zeros(
                                (HEAD - key_rows, HEAD), f32)
                        s_scr[j, d, h, pl.ds(0, key_rows), :] = s0_ref[j, d, h0 + h]
        _gated_scans(q_s, k_refs, v_s, (gf_s, gb_s), s_scr, table_ref, lvl_ref, o_scr, head0, seq)
        if st_ref is not None:
            for j in range(n_seq):
                for d in range(2):
                    for h in range(PAIR):
                        st_ref[j, d, h0 + h] = s_scr[j, d, h, pl.ds(0, key_rows), :]

    def hgrn_pair(pp, carry):
        lb = lb_ref[pp]
        log_lb = jnp.log(lb)

        def proj(i, c):
            rows = _row_tile(i)
            p = _dot(h_scr[rows, :], wa_ref[pp])
            q_s[rows, :] = _silu(p[:, 0:width])
            v_s[rows, :] = p[:, width:2 * width]
            for a, k_s, g_s in ((p[:, 2 * width:3 * width], kf_s, gf_s), (p[:, 3 * width:4 * width], kb_s, gb_s)):
                g_s[rows, :] = _softplus(log_lb - a) - _softplus(-a)
                k_s[rows, :] = (1.0 - lb) * jax.nn.sigmoid(-a)
            return c
        lax.fori_loop(0, n_tiles, proj, 0)
        run_scans(PAIR * pp, s0a_ref, sta_ref, PAIR * pp, HEAD, (kf_s, kb_s))
        return carry
    lax.fori_loop(0, N_HGRN // PAIR, hgrn_pair, 0)

    def gla_pair(pp, carry):
        def proj(i, c):
            rows = _row_tile(i)
            p = _dot(h_scr[rows, :], wb_ref[pp])
            q_s[rows, :] = p[:, 0:width] * (GLA_DK ** -0.5)
            kf_s[rows, :] = p[:, width:2 * width]
            v_s[rows, :] = p[:, 2 * width:3 * width]
            low = low_scr[rows, :]
            for d, g_s in enumerate((gf_s, gb_s)):
                logits = _dot(low, gkw_ref[d, pp]) + gkb_ref[d, pp]
                g_s[rows, :] = -_softplus(-logits) * (1.0 / GLA_GATE_NORM)
            return c
        lax.fori_loop(0, n_tiles, proj, 0)
        run_scans(N_HGRN + PAIR * pp, s0b_ref, stb_ref, PAIR * pp, GLA_DK, (kf_s, kf_s))
        return carry
    lax.fori_loop(0, N_GLA // PAIR, gla_pair, 0)

    _gated_output(x_ref, mod_ref, h_scr, o_scr, gnw_ref, wg_ref, wo_ref, fnw_ref, out_ref, final)


def _odd_kernel(*refs, seq, has_state, emit_state, use_rope, final):
    it = iter(refs)
    lg_ref = next(it)
    x_ref, mod_ref, nw_ref = next(it), next(it), next(it)
    wr_ref, wg_ref, wo_ref, gnw_ref, fnw_ref = next(it), next(it), next(it), next(it), next(it)
    cos_ref, sin_ref = (next(it), next(it)) if use_rope else (None, None)
    s0_ref = next(it) if has_state else None
    out_ref = next(it)
    st_ref = next(it) if emit_state else None
    h_scr, o_scr, q_s, k_s, v_s, s_scr = it
    n_seq = STEP_TOKENS // seq
    n_tiles = STEP_TOKENS // ROW_TILE
    n_chunks = seq // CHUNK

    _modulated_norm(x_ref, mod_ref, nw_ref, h_scr)
    t_idx = lax.broadcasted_iota(jnp.int32, (CHUNK, CHUNK), 0)
    s_idx = lax.broadcasted_iota(jnp.int32, (CHUNK, CHUNK), 1)
    row_f = lax.broadcasted_iota(jnp.int32, (CHUNK, HEAD), 0).astype(f32)
    chunk_len = jnp.full((1, HEAD), CHUNK, f32)
    if use_rope:
        lane = lax.broadcasted_iota(jnp.int32, (ROW_TILE, HEAD), 1)
        first_quarter = (lane // (HEAD // 4)) % 2 == 0

    def rope(x, cos, sin_signed):
        xr = jnp.where(first_quarter, pltpu.roll(x, HEAD - HEAD // 4, axis=1), pltpu.roll(x, HEAD // 4, axis=1))
        return x * cos + xr * sin_signed

    def head_body(hh, carry):
        lg_f = lg_ref[0, hh]
        lg_b = lg_ref[1, hh]

        def proj(i, c):
            rows = _row_tile(i)
            p = _dot(h_scr[rows, :], wr_ref[hh])
            q = p[:, 0:HEAD]
            k = p[:, HEAD:2 * HEAD] * (HEAD ** -0.5)
            if use_rope:
                cos, sin_signed = cos_ref[rows, :], sin_ref[rows, :]
                q, k = rope(q, cos, sin_signed), rope(k, cos, sin_signed)
            q_s[rows, :] = q.astype(bf16)
            k_s[rows, :] = k
            v_s[rows, :] = p[:, 2 * HEAD:3 * HEAD].astype(bf16)
            return c
        lax.fori_loop(0, n_tiles, proj, 0)

        for j in range(n_seq):
            for d in range(2):
                s_scr[j, d] = s0_ref[j, d, hh] if has_state else jnp.zeros((HEAD, HEAD), f32)

        in_f = jnp.exp(lg_f * (row_f + 1.0))
        out_f = jnp.exp(lg_f * (CHUNK - 1.0 - row_f))
        all_f = jnp.exp(lg_f * chunk_len)
        in_b = jnp.exp(lg_b * (CHUNK - row_f))
        out_b = jnp.exp(lg_b * row_f)
        all_b = jnp.exp(lg_b * chunk_len)
        dist = (t_idx - s_idx).astype(f32)
        pair = (jnp.where(t_idx >= s_idx, jnp.exp(lg_f * jnp.maximum(dist, 0.0)), 0.0)
                + jnp.where(s_idx >= t_idx, jnp.exp(lg_b * jnp.maximum(-dist, 0.0)), 0.0))

        def body(i, c):
            for j in range(n_seq):
                rows = _chunk_rows(j * seq, n_chunks - 1 - i)
                q, k, v = q_s[rows, :], k_s[rows, :], v_s[rows, :]
                o_scr[1, hh, rows, :] = in_b * _dot(q, s_scr[j, 1].astype(bf16))
                s_scr[j, 1] = all_b * s_scr[j, 1] + _dot_tn((k * out_b).astype(bf16), v)
                rows = _chunk_rows(j * seq, i)
                q, k, v = q_s[rows, :], k_s[rows, :], v_s[rows, :]
                sc = _dot_nt(q, k.astype(bf16)) * pair
                o_scr[0, hh, rows, :] = _dot(sc.astype(bf16), v) + in_f * _dot(q, s_scr[j, 0].astype(bf16))
                s_scr[j, 0] = all_f * s_scr[j, 0] + _dot_tn((k * out_f).astype(bf16), v)
            return c
        lax.fori_loop(0, n_chunks, body, 0)
        if emit_state:
            for j in range(n_seq):
                for d in range(2):
                    st_ref[j, d, hh] = s_scr[j, d]
        return carry
    lax.fori_loop(0, N_RET, head_body, 0)

    _gated_output(x_ref, mod_ref, h_scr, o_scr, gnw_ref, wg_ref, wo_ref, fnw_ref, out_ref, final)


def _const_spec(shape):
    zeros = (0,) * len(shape)
    return pl.BlockSpec(shape, lambda i: zeros, pipeline_mode=pl.Buffered(1))


def _step_spec(shape, per_step):
    zeros = (0,) * (len(shape) - 1)
    return pl.BlockSpec((per_step,) + tuple(shape[1:]), lambda i: (i,) + zeros)


def _mod_spec(d, per_sequence):
    if per_sequence:
        return pl.BlockSpec((1, 3, d), lambda i: (i + 1, 0, 0))
    return pl.BlockSpec((1, 3, d), lambda i: (0, 0, 0))


def _layer_call(body, x, mod, consts, states, state_shapes, per_sequence_mod, scratch, name, smem_inputs=()):
    n_seq, seq, d = x.shape
    per_step = STEP_TOKENS // seq
    assert per_step * seq == STEP_TOKENS and n_seq % per_step == 0 and seq % CHUNK == 0
    assert not per_sequence_mod or per_step == 1
    n_steps = n_seq // per_step
    xs = x.reshape(n_steps, STEP_TOKENS, d)
    inputs = list(smem_inputs) + [xs, mod] + list(consts) + list(states)
    in_specs = [pl.BlockSpec(memory_space=pltpu.SMEM)] * len(smem_inputs)
    in_specs += [_step_spec(xs.shape, 1), _mod_spec(d, per_sequence_mod)]
    in_specs += [_const_spec(a.shape) for a in consts]
    in_specs += [_step_spec(s.shape, per_step) for s in states]
    out_shape = [jax.ShapeDtypeStruct(xs.shape, f32)] + [jax.ShapeDtypeStruct(s, f32) for s in state_shapes]
    out_specs = [_step_spec(xs.shape, 1)] + [_step_spec(s, per_step) for s in state_shapes]
    outs = pl.pallas_call(
        body,
        grid=(n_steps,),
        in_specs=in_specs,
        out_specs=out_specs,
        out_shape=out_shape,
        scratch_shapes=scratch,
        compiler_params=pltpu.CompilerParams(dimension_semantics=("arbitrary",), vmem_limit_bytes=VMEM_LIMIT_BYTES),
        name=name,
    )(*inputs)
    return [outs[0].reshape(x.shape)] + list(outs[1:])


def _even_layer(x, mod, norm_w, w, final_w, tables, states, emit_state, per_sequence_mod, final):
    n_seq, seq, d = x.shape
    consts = [norm_w, w["wa"], w["wb"], w["wlow"], w["wgate"], w["wout"], w["gkw"], w["gkb"], w["lb"], w["gnw"],
              final_w] + list(tables)
    state_shapes = [(n_seq, 2, N_HGRN, HEAD, HEAD), (n_seq, 2, N_GLA, GLA_DK, HEAD)] if emit_state else []
    scratch = [
        pltpu.VMEM((STEP_TOKENS, d), bf16),
        pltpu.VMEM((STEP_TOKENS, HEAD), bf16),
        pltpu.VMEM((2, N_HEADS, STEP_TOKENS, HEAD), f32),
    ] + [pltpu.VMEM((STEP_TOKENS, PAIR * HEAD), f32)] * 6 + [
        pltpu.VMEM((STEP_TOKENS // seq, 2, PAIR, HEAD, HEAD), f32),
    ]
    body = functools.partial(_even_kernel, seq=seq, has_state=states is not None, emit_state=emit_state, final=final)
    return _layer_call(body, x, mod, consts, states or (), state_shapes, per_sequence_mod, scratch,
                       "even_layer_seq%d" % seq)


def _odd_layer(x, mod, norm_w, w, final_w, log_decay, rope, state, emit_state, per_sequence_mod, final):
    n_seq, seq, d = x.shape
    consts = [norm_w, w["wr"], w["wgate"], w["wout"], w["gnw"], final_w] + list(rope or ())
    state_shapes = [(n_seq, 2, N_RET, HEAD, HEAD)] if emit_state else []
    scratch = [
        pltpu.VMEM((STEP_TOKENS, d), bf16),
        pltpu.VMEM((2, N_HEADS, STEP_TOKENS, HEAD), f32),
        pltpu.VMEM((STEP_TOKENS, HEAD), bf16),
        pltpu.VMEM((STEP_TOKENS, HEAD), f32),
        pltpu.VMEM((STEP_TOKENS, HEAD), bf16),
        pltpu.VMEM((STEP_TOKENS // seq, 2, HEAD, HEAD), f32),
    ]
    body = functools.partial(_odd_kernel, seq=seq, has_state=state is not None, emit_state=emit_state,
                             use_rope=rope is not None, final=final)
    return _layer_call(body, x, mod, consts, () if state is None else (state,), state_shapes, per_sequence_mod,
                       scratch, "odd_layer_seq%d" % seq, smem_inputs=(log_decay,))


def _per_head(w, n_heads):
    d = w.shape[0]
    return jnp.transpose(w.reshape(d, n_heads, -1), (1, 0, 2))


def _per_pair(w, n_heads, head_width=HEAD):
    d = w.shape[0]
    w = _pad_last(w.reshape(d, n_heads, -1), head_width)
    return jnp.transpose(w.reshape(d, n_heads // PAIR, PAIR * head_width), (1, 0, 2))


def _pad_last(a, width):
    return jnp.pad(a, [(0, 0)] * (a.ndim - 1) + [(0, width - a.shape[-1])])


def _even_weights(w_in, gk_w, gk_b, lb, gn_w, w_out):
    wa_w = N_HGRN * HEAD
    wb_k = N_GLA * GLA_DK
    wb_v = N_GLA * HEAD
    edges = np.cumsum([wa_w, wa_w, wa_w, wa_w, wa_w, wb_k, wb_k, wb_v, wb_v, GLA_RANK, GLA_RANK])[:-1].tolist()
    aq, ai, af_f, af_b, ag, bq, bk, bv, bg, bl_f, bl_b = jnp.split(w_in, edges, axis=-1)
    wa = jnp.concatenate([_per_pair(a, N_HGRN) for a in (aq, ai, af_f, af_b)], axis=-1)
    wb = jnp.concatenate([_per_pair(a, N_GLA) for a in (bq, bk, bv)], axis=-1)
    wlow = _pad_last(jnp.concatenate([bl_f, bl_b], axis=-1), HEAD)
    gkw = jnp.stack([
        jnp.pad(_per_pair(gk_w[d], N_GLA), ((0, 0), (d * GLA_RANK, HEAD - (d + 1) * GLA_RANK), (0, 0)))
        for d in range(2)])
    gkb = _pad_last(gk_b.reshape(2, N_GLA, GLA_DK), HEAD).reshape(2, N_GLA // PAIR, 1, PAIR * HEAD)
    return {
        "wa": wa.astype(bf16), "wb": wb.astype(bf16), "wlow": wlow.astype(bf16),
        "wgate": jnp.concatenate([ag, bg], axis=-1).astype(bf16), "wout": w_out.astype(bf16),
        "gkw": gkw.astype(bf16), "gkb": gkb, "lb": lb.reshape(N_HGRN // PAIR, 1, PAIR * HEAD),
        "gnw": gn_w.reshape(1, -1),
    }


def _odd_weights(w_in, gn_w, w_out):
    wd = N_RET * HEAD
    q, k, v, g = jnp.split(w_in, [wd, 2 * wd, 3 * wd], axis=-1)
    wr = jnp.concatenate([_per_head(a, N_RET) for a in (q, k, v)], axis=-1)
    return {"wr": wr.astype(bf16), "wgate": g.astype(bf16), "wout": w_out.astype(bf16), "gnw": gn_w.reshape(1, -1)}


def _rope_tables(seq):
    rows = seq // GRID_W
    t_row = jnp.repeat(jnp.arange(rows), GRID_W).astype(f32)
    t_col = jnp.tile(jnp.arange(GRID_W), rows).astype(f32)
    half = HEAD // 2
    inv = ROPE_BASE ** (-jnp.arange(0, half, 2, dtype=f32) / half)
    ang_r = t_row[:, None] * inv
    ang_c = t_col[:, None] * inv
    ang = jnp.concatenate([ang_r, ang_r, ang_c, ang_c], axis=-1)
    sign = jnp.where((jnp.arange(HEAD) // (HEAD // 4)) % 2 == 0, -1.0, 1.0).astype(f32)
    return jnp.cos(ang), jnp.sin(ang) * sign


def kernel(x_prompt, x_sample, state_hgrn, state_gla, state_ret, c, c_ctx, norm_w, ada_w, ada_b, w_in_even, hgrn_lb, gla_gk_w, gla_gk_b, gn_even, w_out_even, w_in_odd, ret_decay, gn_odd, w_out_odd, final_norm_w):
    depth, d = norm_w.shape
    n_lat = x_sample.shape[0]
    n_cond = -(-(1 + n_lat) // 8) * 8
    cond = jnp.zeros((n_cond, d), f32).at[0].set(c_ctx).at[1:1 + n_lat].set(c)
    mod = _modulation(cond, ada_w, ada_b).reshape(depth, n_cond, 3, d)
    lbs = jnp.cumsum(jax.nn.softmax(hgrn_lb.astype(f32), axis=0), axis=0)
    final_w = final_norm_w.reshape(1, d)
    rope = _rope_tables(x_sample.shape[1])
    tables = _scan_tables()

    x_c, x_l = x_prompt, x_sample
    new_hgrn, new_gla, new_ret = [], [], []
    for l in range(depth):
        i = l // 2
        final = l == depth - 1
        nw = norm_w[l].reshape(1, d)
        if l % 2 == 0:
            w = _even_weights(w_in_even[i], gla_gk_w[i], gla_gk_b[i], lbs[i], gn_even[i], w_out_even[i])
            x_c, st_a, st_b = _even_layer(x_c, mod[l], nw, w, final_w, tables, None, True, False, final)
            (x_l,) = _even_layer(x_l, mod[l], nw, w, final_w, tables, (state_hgrn[:, i], state_gla[:, i]),
                                 False, True, final)
            new_hgrn.append(st_a)
            new_gla.append(st_b)
        else:
            w = _odd_weights(w_in_odd[i], gn_odd[i], w_out_odd[i])
            log_decay = jax.nn.log_sigmoid(ret_decay[i].astype(f32))
            x_c, st_c = _odd_layer(x_c, mod[l], nw, w, final_w, log_decay, None, None, True, False, final)
            (x_l,) = _odd_layer(x_l, mod[l], nw, w, final_w, log_decay, rope, state_ret[:, i], False, True, final)
            new_ret.append(st_c)
    return (x_c, x_l, jnp.stack(new_hgrn, axis=1), jnp.stack(new_gla, axis=1), jnp.stack(new_ret, axis=1))
```

```python
import functools

import numpy as np
import jax
import jax.numpy as jnp
from jax import lax
from jax.experimental import pallas as pl
from jax.experimental.pallas import tpu as pltpu

f32 = jnp.float32
bf16 = jnp.bfloat16
HIGHEST = lax.Precision.HIGHEST

EPS = 1e-6
LOG2E = 1.4426950408889634
HEAD = 128
N_HGRN = 4
N_GLA = 4
GLA_DK = 64
N_RET = 8
N_HEADS = 8
GLA_RANK = 16
GLA_GATE_NORM = 16.0
GRID_W = 64
ROPE_BASE = 10000.0

CHUNK = 128
LEVELS = (1, 2, 4, 8, 16, 32, 64)
DIAG_LEVEL = len(LEVELS)
N_FINE = 3
BLK_CUM = N_FINE
ROW_TOTAL = (N_FINE + 1) * CHUNK
PAIR = 2
CHUNKS_PER_ITER = 2
RET_UNITS_PER_ITER = 8
STEP_TOKENS = 1024
ROW_TILE = 512
MOD_COLS = 768
VMEM_LIMIT_BYTES = 58 * 1024 * 1024


def _dot(a, b, precision=None):
    return jnp.dot(a, b, precision=precision, preferred_element_type=f32)


def _dot_tn(a, b):
    return lax.dot_general(a, b, (((0,), (0,)), ((), ())), preferred_element_type=f32)


def _silu(x):
    return x * jax.nn.sigmoid(x)


def _row_tile(i):
    return pl.ds(pl.multiple_of(i * ROW_TILE, ROW_TILE), ROW_TILE)


def _chunk_rows(seq_start, c):
    return pl.ds(pl.multiple_of(seq_start + c * CHUNK, CHUNK), CHUNK)


def _mod_kernel(cond_ref, w_ref, b_ref, o_ref):
    o_ref[0] = _dot(_silu(cond_ref[...]), w_ref[0], HIGHEST) + b_ref[0]


def _modulation(cond, ada_w, ada_b):
    depth, d, d3 = ada_w.shape
    rows = cond.shape[0]
    return pl.pallas_call(
        _mod_kernel,
        grid=(depth, d3 // MOD_COLS),
        in_specs=[
            pl.BlockSpec((rows, d), lambda l, j: (0, 0)),
            pl.BlockSpec((1, d, MOD_COLS), lambda l, j: (l, 0, j)),
            pl.BlockSpec((1, 1, MOD_COLS), lambda l, j: (l, 0, j)),
        ],
        out_specs=pl.BlockSpec((1, rows, MOD_COLS), lambda l, j: (l, 0, j)),
        out_shape=jax.ShapeDtypeStruct((depth, rows, d3), f32),
        compiler_params=pltpu.CompilerParams(dimension_semantics=("arbitrary", "arbitrary")),
        name="modulation",
    )(cond, ada_w, ada_b.reshape(depth, 1, d3))


def _modulated_norm(x_ref, mod_ref, nw_ref, h_scr):
    def body(i, carry):
        rows = _row_tile(i)
        x = x_ref[0, rows, :]
        y = x * lax.rsqrt(jnp.mean(x * x, axis=-1, keepdims=True) + EPS) * nw_ref[...]
        h_scr[rows, :] = (y * (1.0 + mod_ref[0, 1:2, :]) + mod_ref[0, 0:1, :]).astype(bf16)
        return carry
    lax.fori_loop(0, STEP_TOKENS // ROW_TILE, body, 0)


def _gated_output(x_ref, mod_ref, h_scr, o_scr, gnw_ref, wg_ref, wo_ref, fnw_ref, out_ref, final):
    def body(i, carry):
        rows = _row_tile(i)
        gate = _dot(h_scr[rows, :], wg_ref[...])
        parts = []
        for hh in range(N_HEADS):
            o = o_scr[0, hh, rows, :] + o_scr[1, hh, rows, :]
            parts.append(o * lax.rsqrt(jnp.mean(o * o, axis=-1, keepdims=True) + EPS))
        y = jnp.concatenate(parts, axis=-1) * gnw_ref[...]
        z = (y * _silu(gate)).astype(bf16)
        xn = x_ref[0, rows, :] + mod_ref[0, 2:3, :] * _dot(z, wo_ref[...])
        if final:
            xn = xn * lax.rsqrt(jnp.mean(xn * xn, axis=-1, keepdims=True) + EPS) * fnw_ref[...]
        out_ref[0, rows, :] = xn
        return carry
    lax.fori_loop(0, STEP_TOKENS // ROW_TILE, body, 0)


def _scan_tables():
    t = np.arange(CHUNK)[:, None]
    j = np.arange(CHUNK)[None, :]
    fwd = []
    for m in LEVELS[:N_FINE]:
        mid = (t // (2 * m)) * (2 * m) + m
        right = t >= mid
        fwd.append(np.where(right, (j >= mid) & (j <= t), (j > t) & (j < mid)))
    fwd.append(j <= t)
    fwd.append(np.ones((8, CHUNK), bool))
    fwd = np.concatenate(fwd, axis=0).astype(np.float32)
    bwd = fwd.copy()
    n_sym = ROW_TOTAL // CHUNK
    bwd[:ROW_TOTAL] = fwd[:ROW_TOTAL].reshape(n_sym, CHUNK, CHUNK)[:, ::-1, ::-1].reshape(ROW_TOTAL, CHUNK)
    table = np.stack([np.tile(fwd, (1, 2)), np.tile(bwd, (1, 2))])
    x = t ^ j
    lvl = np.zeros((CHUNK, CHUNK), np.int32)
    for b in range(1, len(LEVELS)):
        lvl += (x >= (1 << b)).astype(np.int32)
    lvl_f = np.where(t > j, lvl, np.where(t == j, DIAG_LEVEL, -1)).astype(np.int32)
    return jnp.asarray(table, bf16), jnp.asarray(np.stack([lvl_f, lvl_f.T]))


def _split2(x):
    hi = x.astype(bf16)
    lo = (x - hi.astype(f32)).astype(bf16)
    return jnp.concatenate([hi, lo], axis=0)


def _in_chunk_scores(q, k, cum, fine, lvl, ones, rev):
    def key_major(x):
        return x.T * ones

    sc = jnp.where(lvl == DIAG_LEVEL, _dot(q.astype(bf16), key_major(k.astype(bf16))), 0.0)
    for i in range(N_FINE):
        e = jnp.exp2(fine[i])
        sc = jnp.where(lvl == i, _dot((q * e).astype(bf16), key_major((k * e).astype(bf16))), sc)
    for i in range(N_FINE, len(LEVELS)):
        m = LEVELS[i]
        blocks = []
        for p0 in range(0, CHUNK, 2 * m):
            left, right = slice(p0, p0 + m), slice(p0 + m, p0 + 2 * m)
            q_side, k_side = (left, right) if rev else (right, left)
            mid_row = p0 + m if rev else p0 + m - 1
            blocks.append((q_side, k_side, cum[mid_row:mid_row + 1, :]))
        q_rows = [(q[qs] * jnp.exp2(cum[qs] - mid)).astype(bf16) for qs, _, mid in blocks]
        k_rows = []
        for _, ks, mid in blocks:
            scaled = (k[ks] * jnp.exp2(mid - cum[ks])).astype(bf16)
            zero = jnp.zeros((m, HEAD), bf16)
            k_rows += [zero, scaled] if rev else [scaled, zero]
        s = _dot(jnp.concatenate(q_rows, axis=0), key_major(jnp.concatenate(k_rows, axis=0)))
        rows = []
        for b, (qs, ks, _) in enumerate(blocks):
            updated = jnp.where(lvl[qs, :] == i, s[b * m:(b + 1) * m, :], sc[qs, :])
            rows += [updated, sc[ks, :]] if rev else [sc[ks, :], updated]
        sc = jnp.concatenate(rows, axis=0)
    return sc.astype(bf16)


def _gated_chunks(chains):
    heads = [slice(h * HEAD, (h + 1) * HEAD) for h in range(PAIR)]
    flat = [(c, r) for c, (_, units) in enumerate(chains) for r in range(len(units))]
    unit = {(c, r): chains[c][1][r] for c, r in flat}
    sums = {key: _dot(unit[key][4], _split2(unit[key][3] * LOG2E)) for key in flat}
    cum = {key: sums[key][BLK_CUM * CHUNK:(BLK_CUM + 1) * CHUNK, :] for key in flat}
    total = {key: sums[key][ROW_TOTAL:ROW_TOTAL + 1, :] for key in flat}
    vb = {key: [unit[key][2][:, lanes].astype(bf16) for lanes in heads] for key in flat}
    kv, whole = {}, {}
    for key in flat:
        k = unit[key][1]
        decay_out = jnp.exp2(total[key] - cum[key])
        kv[key] = [_dot_tn((k[:, lanes] * decay_out[:, lanes]).astype(bf16), vb[key][h])
                   for h, lanes in enumerate(heads)]
        whole[key] = [jnp.broadcast_to(jnp.exp2(total[key][:, lanes]), (HEAD, HEAD)).T for lanes in heads]
    state = [list(states) for states, _ in chains]
    outs, scores = {}, {}
    for r in range(max(len(units) for _, units in chains)):
        live = [key for key in flat if key[1] == r]
        for key in live:
            q = unit[key][0]
            decay_in = jnp.exp2(cum[key])
            outs[key] = [_dot((q[:, lanes] * decay_in[:, lanes]).astype(bf16), state[key[0]][h].astype(bf16))
                         for h, lanes in enumerate(heads)]
            state[key[0]] = [whole[key][h] * state[key[0]][h] + kv[key][h] for h in range(PAIR)]
        for key in live:
            q, k, _, _, _, lvl, rev = unit[key]
            scores[key] = []
            ones = (lvl[0:1, :] >= -1).astype(bf16)
            for h, lanes in enumerate(heads):
                fine = [sums[key][i * CHUNK:(i + 1) * CHUNK, lanes] for i in range(N_FINE)]
                scores[key].append(_in_chunk_scores(q[:, lanes], k[:, lanes], cum[key][:, lanes], fine, lvl, ones, rev))
    for key in flat:
        outs[key] = [outs[key][h] + _dot(scores[key][h], vb[key][h]) for h in range(PAIR)]
    return [[outs[(c, r)] for r in range(len(units))] for c, (_, units) in enumerate(chains)], state


def _gated_scans(q_s, k_refs, v_s, g_refs, s_scr, table_ref, lvl_ref, o_scr, head0, seq):
    n = seq // CHUNK
    assert n % CHUNKS_PER_ITER == 0

    def seq_body(j, carry):
        def chunk_body(i, c):
            chains, rows = [], []
            for d in range(2):
                steps = [i * CHUNKS_PER_ITER + r for r in range(CHUNKS_PER_ITER)]
                rows.append([_chunk_rows(j * seq, n - 1 - t if d else t) for t in steps])
                units = [(q_s[rw, :], k_refs[d][rw, :], v_s[rw, :], g_refs[d][rw, :], table_ref[d], lvl_ref[d], bool(d))
                         for rw in rows[d]]
                chains.append(([s_scr[j, d, h] for h in range(PAIR)], units))
            outs, new_states = _gated_chunks(chains)
            for d in range(2):
                for h in range(PAIR):
                    for r, rw in enumerate(rows[d]):
                        o_scr[d, head0 + h, rw, :] = outs[d][r][h]
                    s_scr[j, d, h] = new_states[d][h]
            return c
        lax.fori_loop(0, n // CHUNKS_PER_ITER, chunk_body, 0)
        return carry
    lax.fori_loop(0, STEP_TOKENS // seq, seq_body, 0)


def _even_kernel(*refs, seq, has_state, emit_state, final):
    it = iter(refs)
    x_ref, mod_ref, nw_ref = next(it), next(it), next(it)
    wa_ref, wb_ref, wlow_ref, wg_ref, wo_ref = next(it), next(it), next(it), next(it), next(it)
    gkw_ref, gkb_ref, lb_ref, gnw_ref, fnw_ref = next(it), next(it), next(it), next(it), next(it)
    table_ref, lvl_ref = next(it), next(it)
    s0a_ref, s0b_ref = (next(it), next(it)) if has_state else (None, None)
    out_ref = next(it)
    sta_ref, stb_ref = (next(it), next(it)) if emit_state else (None, None)
    h_scr, low_scr, o_scr, q_s, v_s, kf_s, kb_s, gf_s, gb_s, s_scr = it
    n_seq = STEP_TOKENS // seq
    n_tiles = STEP_TOKENS // ROW_TILE

    _modulated_norm(x_ref, mod_ref, nw_ref, h_scr)

    def low_body(i, carry):
        rows = _row_tile(i)
        low_scr[rows, :] = _dot(h_scr[rows, :], wlow_ref[...]).astype(bf16)
        return carry
    lax.fori_loop(0, n_tiles, low_body, 0)

    width = PAIR * HEAD

    def run_scans(head0, s0_ref, st_ref, h0, key_rows, k_refs):
        for j in range(n_seq):
            for d in range(2):
                for h in range(PAIR):
                    if s0_ref is None:
                        s_scr[j, d, h] = jnp.zeros((HEAD, HEAD), f32)
                    else:
                        if key_rows < HEAD:
                            s_scr[j, d, h, pl.ds(key_rows, HEAD - key_rows), :] = jnp.zeros(
                                (HEAD - key_rows, HEAD), f32)
                        s_scr[j, d, h, pl.ds(0, key_rows), :] = s0_ref[j, d, h0 + h]
        _gated_scans(q_s, k_refs, v_s, (gf_s, gb_s), s_scr, table_ref, lvl_ref, o_scr, head0, seq)
        if st_ref is not None:
            for j in range(n_seq):
                for d in range(2):
                    for h in range(PAIR):
                        st_ref[j, d, h0 + h] = s_scr[j, d, h, pl.ds(0, key_rows), :]

    def hgrn_pair(pp, carry):
        lb = lb_ref[pp]
        log_lb = jnp.log(lb)

        def proj(i, c):
            rows = _row_tile(i)
            p = _dot(h_scr[rows, :], wa_ref[pp])
            q_s[rows, :] = _silu(p[:, 0:width])
            v_s[rows, :] = p[:, width:2 * width]
            for a, k_s, g_s in ((p[:, 2 * width:3 * width], kf_s, gf_s), (p[:, 3 * width:4 * width], kb_s, gb_s)):
                z = log_lb - a
                u = jnp.exp(-jnp.abs(a))
                w = jnp.exp(-jnp.abs(z))
                r = 1.0 / (1.0 + u)
                g_s[rows, :] = jnp.maximum(z, 0.0) + jnp.minimum(a, 0.0) + jnp.log((1.0 + w) * r)
                k_s[rows, :] = (1.0 - lb) * jnp.where(a >= 0.0, u * r, r)
            return c
        lax.fori_loop(0, n_tiles, proj, 0)
        run_scans(PAIR * pp, s0a_ref, sta_ref, PAIR * pp, HEAD, (kf_s, kb_s))
        return carry
    lax.fori_loop(0, N_HGRN // PAIR, hgrn_pair, 0)

    def gla_pair(pp, carry):
        def proj(i, c):
            rows = _row_tile(i)
            p = _dot(h_scr[rows, :], wb_ref[pp])
            q_s[rows, :] = p[:, 0:width] * (GLA_DK ** -0.5)
            kf_s[rows, :] = p[:, width:2 * width]
            v_s[rows, :] = p[:, 2 * width:3 * width]
            low = low_scr[rows, :]
            for d, g_s in enumerate((gf_s, gb_s)):
                logits = _dot(low, gkw_ref[d, pp]) + gkb_ref[d, pp]
                log_gate = jnp.minimum(logits, 0.0) - jnp.log(1.0 + jnp.exp(-jnp.abs(logits)))
                g_s[rows, :] = log_gate * (1.0 / GLA_GATE_NORM)
            return c
        lax.fori_loop(0, n_tiles, proj, 0)
        run_scans(N_HGRN + PAIR * pp, s0b_ref, stb_ref, PAIR * pp, GLA_DK, (kf_s, kf_s))
        return carry
    lax.fori_loop(0, N_GLA // PAIR, gla_pair, 0)

    _gated_output(x_ref, mod_ref, h_scr, o_scr, gnw_ref, wg_ref, wo_ref, fnw_ref, out_ref, final)


def _odd_kernel(*refs, seq, has_state, emit_state, use_rope, final):
    it = iter(refs)
    lg_ref = next(it)
    x_ref, mod_ref, nw_ref = next(it), next(it), next(it)
    wr_ref, wg_ref, wo_ref, gnw_ref, fnw_ref = next(it), next(it), next(it), next(it), next(it)
    cos_ref, sin_ref = (next(it), next(it)) if use_rope else (None, None)
    s0_ref = next(it) if has_state else None
    out_ref = next(it)
    st_ref = next(it) if emit_state else None
    h_scr, o_scr, q_s, kt_s, v_s, s_scr, dec_scr = it
    n_seq = STEP_TOKENS // seq
    n_tiles = STEP_TOKENS // ROW_TILE
    n_chunks = seq // CHUNK
    per_iter = min(n_chunks, RET_UNITS_PER_ITER // n_seq)
    assert n_chunks % per_iter == 0

    _modulated_norm(x_ref, mod_ref, nw_ref, h_scr)
    t_idx = lax.broadcasted_iota(jnp.int32, (CHUNK, CHUNK), 0)
    s_idx = lax.broadcasted_iota(jnp.int32, (CHUNK, CHUNK), 1)
    row_f = lax.broadcasted_iota(jnp.int32, (CHUNK, HEAD), 0).astype(f32)
    col_f = lax.broadcasted_iota(jnp.int32, (8, CHUNK), 1).astype(f32)
    chunk_len = jnp.full((8, HEAD), CHUNK, f32)
    if use_rope:
        lane = lax.broadcasted_iota(jnp.int32, (ROW_TILE, HEAD), 1)
        first_quarter = (lane // (HEAD // 4)) % 2 == 0

    def rope(x, cos, sin_signed):
        xr = jnp.where(first_quarter, pltpu.roll(x, HEAD - HEAD // 4, axis=1), pltpu.roll(x, HEAD // 4, axis=1))
        return x * cos + xr * sin_signed

    def head_body(hh, carry):
        lg_f = lg_ref[0, hh]
        lg_b = lg_ref[1, hh]

        def proj(i, c):
            rows = _row_tile(i)
            p = _dot(h_scr[rows, :], wr_ref[hh])
            q = p[:, 0:HEAD]
            k = p[:, HEAD:2 * HEAD] * (HEAD ** -0.5)
            if use_rope:
                cos, sin_signed = cos_ref[rows, :], sin_ref[rows, :]
                q, k = rope(q, cos, sin_signed), rope(k, cos, sin_signed)
            q_s[rows, :] = q.astype(bf16)
            kt_s[:, rows] = k.T
            v_s[rows, :] = p[:, 2 * HEAD:3 * HEAD].astype(bf16)
            return c
        lax.fori_loop(0, n_tiles, proj, 0)

        for j in range(n_seq):
            for d in range(2):
                s_scr[j, d] = s0_ref[j, d, hh] if has_state else jnp.zeros((HEAD, HEAD), f32)

        dist = (t_idx - s_idx).astype(f32)
        dec_scr[0] = jnp.exp(lg_f * (row_f + 1.0))
        dec_scr[1] = jnp.exp(lg_b * (CHUNK - row_f))
        dec_scr[2] = (jnp.where(t_idx >= s_idx, jnp.exp(lg_f * jnp.maximum(dist, 0.0)), 0.0)
                      + jnp.where(s_idx >= t_idx, jnp.exp(lg_b * jnp.maximum(-dist, 0.0)), 0.0))
        dec_scr[3, 0:8, :] = jnp.exp(lg_f * (CHUNK - 1.0 - col_f))
        dec_scr[3, 8:16, :] = jnp.exp(lg_b * col_f)
        dec_scr[3, 16:24, :] = jnp.exp(lg_f * chunk_len)
        dec_scr[3, 24:32, :] = jnp.exp(lg_b * chunk_len)

        def body(i, c):
            units = [(j, r) for j in range(n_seq) for r in range(per_iter)]
            rows = {(j, r, d): _chunk_rows(j * seq, n_chunks - 1 - (i * per_iter + r) if d else i * per_iter + r)
                    for j, r in units for d in range(2)}
            scores = {u: _dot(q_s[rows[u + (0,)], :], kt_s[:, rows[u + (0,)]].astype(bf16)) for u in units}
            kv = {}
            for j, r in units:
                for d in range(2):
                    rw = rows[(j, r, d)]
                    keys = (kt_s[:, rw] * dec_scr[3, 8 * d:8 * d + 1, :]).astype(bf16)
                    kv[(j, r, d)] = _dot(keys, v_s[rw, :])
            state = {(j, d): s_scr[j, d] for j in range(n_seq) for d in range(2)}
            carried = {}
            for r in range(per_iter):
                for j in range(n_seq):
                    for d in range(2):
                        carried[(j, r, d)] = _dot(q_s[rows[(j, r, d)], :], state[(j, d)].astype(bf16))
                        state[(j, d)] = dec_scr[3, 16 + 8 * d:17 + 8 * d, :] * state[(j, d)] + kv[(j, r, d)]
            for j, r in units:
                rw = rows[(j, r, 0)]
                inside = _dot((scores[(j, r)] * dec_scr[2]).astype(bf16), v_s[rw, :])
                o_scr[0, hh, rw, :] = inside + dec_scr[0] * carried[(j, r, 0)]
                o_scr[1, hh, rows[(j, r, 1)], :] = dec_scr[1] * carried[(j, r, 1)]
            for (j, d), s in state.items():
                s_scr[j, d] = s
            return c
        lax.fori_loop(0, n_chunks // per_iter, body, 0)
        if emit_state:
            for j in range(n_seq):
                for d in range(2):
                    st_ref[j, d, hh] = s_scr[j, d]
        return carry
    lax.fori_loop(0, N_RET, head_body, 0)

    _gated_output(x_ref, mod_ref, h_scr, o_scr, gnw_ref, wg_ref, wo_ref, fnw_ref, out_ref, final)


def _const_spec(shape):
    zeros = (0,) * len(shape)
    return pl.BlockSpec(shape, lambda i: zeros, pipeline_mode=pl.Buffered(1))


def _step_spec(shape, per_step):
    zeros = (0,) * (len(shape) - 1)
    return pl.BlockSpec((per_step,) + tuple(shape[1:]), lambda i: (i,) + zeros)


def _mod_spec(d, per_sequence):
    if per_sequence:
        return pl.BlockSpec((1, 3, d), lambda i: (i + 1, 0, 0))
    return pl.BlockSpec((1, 3, d), lambda i: (0, 0, 0))


def _layer_call(body, x, mod, consts, states, state_shapes, per_sequence_mod, scratch, name, smem_inputs=()):
    n_seq, seq, d = x.shape
    per_step = STEP_TOKENS // seq
    assert per_step * seq == STEP_TOKENS and n_seq % per_step == 0 and seq % CHUNK == 0
    assert not per_sequence_mod or per_step == 1
    n_steps = n_seq // per_step
    xs = x.reshape(n_steps, STEP_TOKENS, d)
    inputs = list(smem_inputs) + [xs, mod] + list(consts) + list(states)
    in_specs = [pl.BlockSpec(memory_space=pltpu.SMEM)] * len(smem_inputs)
    in_specs += [_step_spec(xs.shape, 1), _mod_spec(d, per_sequence_mod)]
    in_specs += [_const_spec(a.shape) for a in consts]
    in_specs += [_step_spec(s.shape, per_step) for s in states]
    out_shape = [jax.ShapeDtypeStruct(xs.shape, f32)] + [jax.ShapeDtypeStruct(s, f32) for s in state_shapes]
    out_specs = [_step_spec(xs.shape, 1)] + [_step_spec(s, per_step) for s in state_shapes]
    outs = pl.pallas_call(
        body,
        grid=(n_steps,),
        in_specs=in_specs,
        out_specs=out_specs,
        out_shape=out_shape,
        scratch_shapes=scratch,
        compiler_params=pltpu.CompilerParams(dimension_semantics=("arbitrary",), vmem_limit_bytes=VMEM_LIMIT_BYTES),
        name=name,
    )(*inputs)
    return [outs[0].reshape(x.shape)] + list(outs[1:])


def _even_layer(x, mod, norm_w, w, final_w, tables, states, emit_state, per_sequence_mod, final):
    n_seq, seq, d = x.shape
    consts = [norm_w, w["wa"], w["wb"], w["wlow"], w["wgate"], w["wout"], w["gkw"], w["gkb"], w["lb"], w["gnw"],
              final_w] + list(tables)
    state_shapes = [(n_seq, 2, N_HGRN, HEAD, HEAD), (n_seq, 2, N_GLA, GLA_DK, HEAD)] if emit_state else []
    scratch = [
        pltpu.VMEM((STEP_TOKENS, d), bf16),
        pltpu.VMEM((STEP_TOKENS, HEAD), bf16),
        pltpu.VMEM((2, N_HEADS, STEP_TOKENS, HEAD), f32),
    ] + [pltpu.VMEM((STEP_TOKENS, PAIR * HEAD), f32)] * 6 + [
        pltpu.VMEM((STEP_TOKENS // seq, 2, PAIR, HEAD, HEAD), f32),
    ]
    body = functools.partial(_even_kernel, seq=seq, has_state=states is not None, emit_state=emit_state, final=final)
    return _layer_call(body, x, mod, consts, states or (), state_shapes, per_sequence_mod, scratch,
                       "even_layer_seq%d" % seq)


def _odd_layer(x, mod, norm_w, w, final_w, log_decay, rope, state, emit_state, per_sequence_mod, final):
    n_seq, seq, d = x.shape
    consts = [norm_w, w["wr"], w["wgate"], w["wout"], w["gnw"], final_w] + list(rope or ())
    state_shapes = [(n_seq, 2, N_RET, HEAD, HEAD)] if emit_state else []
    scratch = [
        pltpu.VMEM((STEP_TOKENS, d), bf16),
        pltpu.VMEM((2, N_HEADS, STEP_TOKENS, HEAD), f32),
        pltpu.VMEM((STEP_TOKENS, HEAD), bf16),
        pltpu.VMEM((HEAD, STEP_TOKENS), f32),
        pltpu.VMEM((STEP_TOKENS, HEAD), bf16),
        pltpu.VMEM((STEP_TOKENS // seq, 2, HEAD, HEAD), f32),
        pltpu.VMEM((4, CHUNK, HEAD), f32),
    ]
    body = functools.partial(_odd_kernel, seq=seq, has_state=state is not None, emit_state=emit_state,
                             use_rope=rope is not None, final=final)
    return _layer_call(body, x, mod, consts, () if state is None else (state,), state_shapes, per_sequence_mod,
                       scratch, "odd_layer_seq%d" % seq, smem_inputs=(log_decay,))


def _per_head(w, n_heads):
    d = w.shape[0]
    return jnp.transpose(w.reshape(d, n_heads, -1), (1, 0, 2))


def _per_pair(w, n_heads, head_width=HEAD):
    d = w.shape[0]
    w = _pad_last(w.reshape(d, n_heads, -1), head_width)
    return jnp.transpose(w.reshape(d, n_heads // PAIR, PAIR * head_width), (1, 0, 2))


def _pad_last(a, width):
    return jnp.pad(a, [(0, 0)] * (a.ndim - 1) + [(0, width - a.shape[-1])])


def _even_weights(w_in, gk_w, gk_b, lb, gn_w, w_out):
    wa_w = N_HGRN * HEAD
    wb_k = N_GLA * GLA_DK
    wb_v = N_GLA * HEAD
    edges = np.cumsum([wa_w, wa_w, wa_w, wa_w, wa_w, wb_k, wb_k, wb_v, wb_v, GLA_RANK, GLA_RANK])[:-1].tolist()
    aq, ai, af_f, af_b, ag, bq, bk, bv, bg, bl_f, bl_b = jnp.split(w_in, edges, axis=-1)
    wa = jnp.concatenate([_per_pair(a, N_HGRN) for a in (aq, ai, af_f, af_b)], axis=-1)
    wb = jnp.concatenate([_per_pair(a, N_GLA) for a in (bq, bk, bv)], axis=-1)
    wlow = _pad_last(jnp.concatenate([bl_f, bl_b], axis=-1), HEAD)
    gkw = jnp.stack([
        jnp.pad(_per_pair(gk_w[d], N_GLA), ((0, 0), (d * GLA_RANK, HEAD - (d + 1) * GLA_RANK), (0, 0)))
        for d in range(2)])
    gkb = _pad_last(gk_b.reshape(2, N_GLA, GLA_DK), HEAD).reshape(2, N_GLA // PAIR, 1, PAIR * HEAD)
    return {
        "wa": wa.astype(bf16), "wb": wb.astype(bf16), "wlow": wlow.astype(bf16),
        "wgate": jnp.concatenate([ag, bg], axis=-1).astype(bf16), "wout": w_out.astype(bf16),
        "gkw": gkw.astype(bf16), "gkb": gkb, "lb": lb.reshape(N_HGRN // PAIR, 1, PAIR * HEAD),
        "gnw": gn_w.reshape(1, -1),
    }


def _odd_weights(w_in, gn_w, w_out):
    wd = N_RET * HEAD
    q, k, v, g = jnp.split(w_in, [wd, 2 * wd, 3 * wd], axis=-1)
    wr = jnp.concatenate([_per_head(a, N_RET) for a in (q, k, v)], axis=-1)
    return {"wr": wr.astype(bf16), "wgate": g.astype(bf16), "wout": w_out.astype(bf16), "gnw": gn_w.reshape(1, -1)}


def _rope_tables(seq):
    rows = seq // GRID_W
    t_row = jnp.repeat(jnp.arange(rows), GRID_W).astype(f32)
    t_col = jnp.tile(jnp.arange(GRID_W), rows).astype(f32)
    half = HEAD // 2
    inv = ROPE_BASE ** (-jnp.arange(0, half, 2, dtype=f32) / half)
    ang_r = t_row[:, None] * inv
    ang_c = t_col[:, None] * inv
    ang = jnp.concatenate([ang_r, ang_r, ang_c, ang_c], axis=-1)
    sign = jnp.where((jnp.arange(HEAD) // (HEAD // 4)) % 2 == 0, -1.0, 1.0).astype(f32)
    return jnp.cos(ang), jnp.sin(ang) * sign


def kernel(x_prompt, x_sample, state_hgrn, state_gla, state_ret, c, c_ctx, norm_w, ada_w, ada_b, w_in_even, hgrn_lb, gla_gk_w, gla_gk_b, gn_even, w_out_even, w_in_odd, ret_decay, gn_odd, w_out_odd, final_norm_w):
    depth, d = norm_w.shape
    n_lat = x_sample.shape[0]
    n_cond = -(-(1 + n_lat) // 8) * 8
    cond = jnp.zeros((n_cond, d), f32).at[0].set(c_ctx).at[1:1 + n_lat].set(c)
    mod = _modulation(cond, ada_w, ada_b).reshape(depth, n_cond, 3, d)
    lbs = jnp.cumsum(jax.nn.softmax(hgrn_lb.astype(f32), axis=0), axis=0)
    final_w = final_norm_w.reshape(1, d)
    rope = _rope_tables(x_sample.shape[1])
    tables = _scan_tables()

    x_c, x_l = x_prompt, x_sample
    new_hgrn, new_gla, new_ret = [], [], []
    for l in range(depth):
        i = l // 2
        final = l == depth - 1
        nw = norm_w[l].reshape(1, d)
        if l % 2 == 0:
            w = _even_weights(w_in_even[i], gla_gk_w[i], gla_gk_b[i], lbs[i], gn_even[i], w_out_even[i])
            x_c, st_a, st_b = _even_layer(x_c, mod[l], nw, w, final_w, tables, None, True, False, final)
            (x_l,) = _even_layer(x_l, mod[l], nw, w, final_w, tables, (state_hgrn[:, i], state_gla[:, i]),
                                 False, True, final)
            new_hgrn.append(st_a)
            new_gla.append(st_b)
        else:
            w = _odd_weights(w_in_odd[i], gn_odd[i], w_out_odd[i])
            log_decay = jax.nn.log_sigmoid(ret_decay[i].astype(f32))
            x_c, st_c = _odd_layer(x_c, mod[l], nw, w, final_w, log_decay, None, None, True, False, final)
            (x_l,) = _odd_layer(x_l, mod[l], nw, w, final_w, log_decay, rope, state_ret[:, i], False, True, final)
            new_ret.append(st_c)
    return (x_c, x_l, jnp.stack(new_hgrn, axis=1), jnp.stack(new_gla, axis=1), jnp.stack(new_ret, axis=1))
```

```python
import functools

import numpy as np
import jax
import jax.numpy as jnp
from jax import lax
from jax.experimental import pallas as pl
from jax.experimental.pallas import tpu as pltpu

f32 = jnp.float32
bf16 = jnp.bfloat16
HIGHEST = lax.Precision.HIGHEST

EPS = 1e-6
LOG2E = 1.4426950408889634
HEAD = 128
N_HGRN = 4
N_GLA = 4
GLA_DK = 64
N_RET = 8
N_HEADS = 8
GLA_RANK = 16
GLA_GATE_NORM = 16.0
GRID_W = 64
ROPE_BASE = 10000.0

CHUNK = 128
LEVELS = (1, 2, 4, 8, 16, 32, 64)
DIAG_LEVEL = len(LEVELS)
N_FINE = 3
BLK_CUM = N_FINE
ROW_TOTAL = (N_FINE + 1) * CHUNK
PAIR = 2
GATED_UNITS_PER_ITER = 4
RET_UNITS_PER_ITER = 8
STEP_TOKENS = 1024
ROW_TILE = 512
MOD_COLS = 768
VMEM_LIMIT_BYTES = 58 * 1024 * 1024


def _dot(a, b, precision=None):
    return jnp.dot(a, b, precision=precision, preferred_element_type=f32)


def _dot_tn(a, b):
    return lax.dot_general(a, b, (((0,), (0,)), ((), ())), preferred_element_type=f32)


def _silu(x):
    return x * jax.nn.sigmoid(x)


def _row_tile(i):
    return pl.ds(pl.multiple_of(i * ROW_TILE, ROW_TILE), ROW_TILE)


def _chunk_rows(seq_start, c):
    return pl.ds(pl.multiple_of(seq_start + c * CHUNK, CHUNK), CHUNK)


def _mod_kernel(cond_ref, w_ref, b_ref, o_ref):
    o_ref[0] = _dot(_silu(cond_ref[...]), w_ref[0], HIGHEST) + b_ref[0]


def _modulation(cond, ada_w, ada_b):
    depth, d, d3 = ada_w.shape
    rows = cond.shape[0]
    return pl.pallas_call(
        _mod_kernel,
        grid=(depth, d3 // MOD_COLS),
        in_specs=[
            pl.BlockSpec((rows, d), lambda l, j: (0, 0)),
            pl.BlockSpec((1, d, MOD_COLS), lambda l, j: (l, 0, j)),
            pl.BlockSpec((1, 1, MOD_COLS), lambda l, j: (l, 0, j)),
        ],
        out_specs=pl.BlockSpec((1, rows, MOD_COLS), lambda l, j: (l, 0, j)),
        out_shape=jax.ShapeDtypeStruct((depth, rows, d3), f32),
        compiler_params=pltpu.CompilerParams(dimension_semantics=("arbitrary", "arbitrary")),
        name="modulation",
    )(cond, ada_w, ada_b.reshape(depth, 1, d3))


def _modulated_norm(x_ref, mod_ref, nw_ref, h_scr):
    def body(i, carry):
        rows = _row_tile(i)
        x = x_ref[0, rows, :]
        y = x * lax.rsqrt(jnp.mean(x * x, axis=-1, keepdims=True) + EPS) * nw_ref[...]
        h_scr[rows, :] = (y * (1.0 + mod_ref[0, 1:2, :]) + mod_ref[0, 0:1, :]).astype(bf16)
        return carry
    lax.fori_loop(0, STEP_TOKENS // ROW_TILE, body, 0)


def _clear_output(out_ref):
    def body(i, carry):
        out_ref[0, _row_tile(i), :] = jnp.zeros((ROW_TILE, out_ref.shape[-1]), f32)
        return carry
    lax.fori_loop(0, STEP_TOKENS // ROW_TILE, body, 0)


def _add_pair_output(pair, gate_s, o_scr, gnw_ref, wo_ref, out_ref):
    tiles = []
    for i in range(STEP_TOKENS // ROW_TILE):
        rows = pl.ds(i * ROW_TILE, ROW_TILE)
        parts = []
        for h in range(PAIR):
            o = o_scr[0, h, rows, :] + o_scr[1, h, rows, :]
            parts.append(o * lax.rsqrt(jnp.mean(o * o, axis=-1, keepdims=True) + EPS))
        y = jnp.concatenate(parts, axis=-1) * gnw_ref[pair]
        tiles.append((rows, (y * gate_s[rows, :].astype(f32)).astype(bf16)))
    products = [(rows, _dot(z, wo_ref[pair])) for rows, z in tiles]
    for rows, product in products:
        out_ref[0, rows, :] += product


def _finish_output(x_ref, mod_ref, fnw_ref, out_ref, final):
    def body(i, carry):
        rows = _row_tile(i)
        xn = x_ref[0, rows, :] + mod_ref[0, 2:3, :] * out_ref[0, rows, :]
        if final:
            xn = xn * lax.rsqrt(jnp.mean(xn * xn, axis=-1, keepdims=True) + EPS) * fnw_ref[...]
        out_ref[0, rows, :] = xn
        return carry
    lax.fori_loop(0, STEP_TOKENS // ROW_TILE, body, 0)


def _scan_tables():
    t = np.arange(CHUNK)[:, None]
    j = np.arange(CHUNK)[None, :]
    fwd = []
    for m in LEVELS[:N_FINE]:
        mid = (t // (2 * m)) * (2 * m) + m
        right = t >= mid
        fwd.append(np.where(right, (j >= mid) & (j <= t), (j > t) & (j < mid)))
    fwd.append(j <= t)
    fwd.append(np.ones((8, CHUNK), bool))
    fwd = np.concatenate(fwd, axis=0).astype(np.float32)
    bwd = fwd.copy()
    n_sym = ROW_TOTAL // CHUNK
    bwd[:ROW_TOTAL] = fwd[:ROW_TOTAL].reshape(n_sym, CHUNK, CHUNK)[:, ::-1, ::-1].reshape(ROW_TOTAL, CHUNK)
    table = np.stack([np.tile(fwd, (1, 2)), np.tile(bwd, (1, 2))])
    x = t ^ j
    lvl = np.zeros((CHUNK, CHUNK), np.int32)
    for b in range(1, len(LEVELS)):
        lvl += (x >= (1 << b)).astype(np.int32)
    lvl_f = np.where(t > j, lvl, np.where(t == j, DIAG_LEVEL, -1)).astype(np.int32)
    return jnp.asarray(table, bf16), jnp.asarray(np.stack([lvl_f, lvl_f.T]))


def _split2(x):
    hi = x.astype(bf16)
    lo = (x - hi.astype(f32)).astype(bf16)
    return jnp.concatenate([hi, lo], axis=0)


def _in_chunk_scores(q, k, cum, fine, lvl, ones, rev):
    qb = q.astype(bf16)
    kt = k.astype(bf16).T
    sc = jnp.where(lvl == DIAG_LEVEL, _dot(qb, kt * ones), 0.0)
    for i in range(N_FINE):
        e = jnp.exp2(fine[i]).astype(bf16)
        sc = jnp.where(lvl == i, _dot(qb * e, kt * e.T), sc)
    for i in range(N_FINE, len(LEVELS)):
        m = LEVELS[i]
        blocks = []
        for p0 in range(0, CHUNK, 2 * m):
            left, right = slice(p0, p0 + m), slice(p0 + m, p0 + 2 * m)
            q_side, k_side = (left, right) if rev else (right, left)
            mid_row = p0 + m if rev else p0 + m - 1
            blocks.append((q_side, k_side, cum[mid_row:mid_row + 1, :]))
        q_rows = [qb[qs] * jnp.exp2(cum[qs] - mid).astype(bf16) for qs, _, mid in blocks]
        k_decay = []
        for _, ks, mid in blocks:
            decay = jnp.exp2(mid - cum[ks]).astype(bf16)
            zero = jnp.zeros((m, HEAD), bf16)
            k_decay += [zero, decay] if rev else [decay, zero]
        s = _dot(jnp.concatenate(q_rows, axis=0), kt * jnp.concatenate(k_decay, axis=0).T)
        rows = []
        for b, (qs, ks, _) in enumerate(blocks):
            updated = jnp.where(lvl[qs, :] == i, s[b * m:(b + 1) * m, :], sc[qs, :])
            rows += [updated, sc[ks, :]] if rev else [sc[ks, :], updated]
        sc = jnp.concatenate(rows, axis=0)
    return sc.astype(bf16)


def _store_log2_split(g_ref, rows, g):
    width = g.shape[-1]
    x = g * LOG2E
    hi = x.astype(bf16)
    g_ref[rows, 0:width] = hi
    g_ref[rows, width:2 * width] = (x - hi.astype(f32)).astype(bf16)


def _decay_sums(g_split, table):
    width = g_split.shape[-1] // 2
    return _dot(table, jnp.concatenate([g_split[:, :width], g_split[:, width:]], axis=0))


def _gated_chunks(chains, sums_ref, upcoming, upcoming_ref):
    heads = [slice(h * HEAD, (h + 1) * HEAD) for h in range(PAIR)]
    flat = [(c, r) for c, (_, units) in enumerate(chains) for r in range(len(units))]
    unit = {(c, r): chains[c][1][r] for c, r in flat}
    index = {key: u for u, key in enumerate(flat)}
    cum = {key: sums_ref[index[key], BLK_CUM * CHUNK:(BLK_CUM + 1) * CHUNK, :] for key in flat}
    total = {key: sums_ref[index[key], ROW_TOTAL:ROW_TOTAL + 1, :] for key in flat}
    vb = {key: [unit[key][2][:, lanes] for lanes in heads] for key in flat}
    upcoming = list(enumerate(upcoming))
    kv, whole = {}, {}
    for key in flat:
        k = unit[key][1]
        decay_out = jnp.exp2(total[key] - cum[key])
        kv[key] = [_dot_tn((k[:, lanes] * decay_out[:, lanes]).astype(bf16), vb[key][h])
                   for h, lanes in enumerate(heads)]
        whole[key] = [jnp.broadcast_to(jnp.exp2(total[key][:, lanes]), (HEAD, HEAD)).T for lanes in heads]
    state = [list(states) for states, _ in chains]
    outs, pending = {}, None
    for r in range(max(len(units) for _, units in chains)):
        live = [key for key in flat if key[1] == r]
        for key in live:
            q = unit[key][0]
            decay_in = jnp.exp2(cum[key])
            outs[key] = [_dot((q[:, lanes] * decay_in[:, lanes]).astype(bf16), state[key[0]][h].astype(bf16))
                         for h, lanes in enumerate(heads)]
            state[key[0]] = [whole[key][h] * state[key[0]][h] + kv[key][h] for h in range(PAIR)]
        for key in live:
            q, k, _, lvl, rev = unit[key]
            ones = (lvl[0:1, :] >= -1).astype(bf16)
            for h, lanes in enumerate(heads):
                fine = [sums_ref[index[key], i * CHUNK:(i + 1) * CHUNK, lanes] for i in range(N_FINE)]
                sc = _in_chunk_scores(q[:, lanes], k[:, lanes], cum[key][:, lanes], fine, lvl, ones, rev)
                if pending is not None:
                    pkey, ph, psc = pending
                    outs[pkey][ph] = outs[pkey][ph] + _dot(psc, vb[pkey][ph])
                pending = (key, h, sc)
            if upcoming:
                u, (g_split, table) = upcoming.pop(0)
                upcoming_ref[u] = _decay_sums(g_split, table)
    for u, (g_split, table) in upcoming:
        upcoming_ref[u] = _decay_sums(g_split, table)
    pkey, ph, psc = pending
    outs[pkey][ph] = outs[pkey][ph] + _dot(psc, vb[pkey][ph])
    return [[outs[(c, r)] for r in range(len(units))] for c, (_, units) in enumerate(chains)], state


def _gated_scans(q_s, k_refs, v_s, g_refs, s_scr, sums_scr, table_ref, lvl_ref, o_scr, seq):
    n = seq // CHUNK
    n_seq = STEP_TOKENS // seq
    per_iter = min(n, GATED_UNITS_PER_ITER // 2)
    seqs_per_iter = min(n_seq, GATED_UNITS_PER_ITER // (2 * per_iter))
    assert n % per_iter == 0 and n_seq % seqs_per_iter == 0
    iters_per_seq = n // per_iter
    n_iters = (n_seq // seqs_per_iter) * iters_per_seq

    def layout(it):
        jj, i = it // iters_per_seq, it % iters_per_seq
        chains = []
        for js in range(seqs_per_iter):
            j = jj * seqs_per_iter + js
            for d in range(2):
                steps = [i * per_iter + r for r in range(per_iter)]
                chains.append((j, d, [_chunk_rows(j * seq, n - 1 - t if d else t) for t in steps]))
        return chains

    def sums_inputs(it):
        return [(g_refs[d][rw, :], table_ref[d]) for _, d, rws in layout(it) for rw in rws]

    for u, (g_split, table) in enumerate(sums_inputs(0)):
        sums_scr[0, u] = _decay_sums(g_split, table)

    def iteration(it, slot):
        chains = layout(it)
        args = [([s_scr[j, d, h] for h in range(PAIR)],
                 [(q_s[rw, :], k_refs[d][rw, :], v_s[rw, :], lvl_ref[d], bool(d)) for rw in rws])
                for j, d, rws in chains]
        upcoming = sums_inputs(jnp.minimum(it + 1, n_iters - 1))
        outs, new_states = _gated_chunks(args, sums_scr.at[slot], upcoming, sums_scr.at[1 - slot])
        for c, (j, d, rws) in enumerate(chains):
            for h in range(PAIR):
                for r, rw in enumerate(rws):
                    o_scr[d, h, rw, :] = outs[c][r][h]
                s_scr[j, d, h] = new_states[c][h]

    assert n_iters % 2 == 0

    def body(it2, carry):
        iteration(2 * it2, 0)
        iteration(2 * it2 + 1, 1)
        return carry
    lax.fori_loop(0, n_iters // 2, body, 0)


def _even_kernel(*refs, seq, has_state, emit_state, final):
    it = iter(refs)
    x_ref, mod_ref, nw_ref = next(it), next(it), next(it)
    wa_ref, wb_ref, wlow_ref, wo_ref = next(it), next(it), next(it), next(it)
    gkw_ref, gkb_ref, lb_ref, gnw_ref, fnw_ref = next(it), next(it), next(it), next(it), next(it)
    table_ref, lvl_ref = next(it), next(it)
    s0a_ref, s0b_ref = (next(it), next(it)) if has_state else (None, None)
    out_ref = next(it)
    sta_ref, stb_ref = (next(it), next(it)) if emit_state else (None, None)
    h_scr, low_scr, o_scr, gate_s, q_s, v_s, kf_s, kb_s, gf_s, gb_s, s_scr, sums_scr = it
    n_seq = STEP_TOKENS // seq
    n_tiles = STEP_TOKENS // ROW_TILE

    _modulated_norm(x_ref, mod_ref, nw_ref, h_scr)
    _clear_output(out_ref)

    def low_body(i, carry):
        rows = _row_tile(i)
        low_scr[rows, :] = _dot(h_scr[rows, :], wlow_ref[...]).astype(bf16)
        return carry
    lax.fori_loop(0, n_tiles, low_body, 0)

    width = PAIR * HEAD

    def run_scans(pair, s0_ref, st_ref, h0, key_rows, k_refs):
        for j in range(n_seq):
            for d in range(2):
                for h in range(PAIR):
                    if s0_ref is None:
                        s_scr[j, d, h] = jnp.zeros((HEAD, HEAD), f32)
                    else:
                        if key_rows < HEAD:
                            s_scr[j, d, h, pl.ds(key_rows, HEAD - key_rows), :] = jnp.zeros(
                                (HEAD - key_rows, HEAD), f32)
                        s_scr[j, d, h, pl.ds(0, key_rows), :] = s0_ref[j, d, h0 + h]
        _gated_scans(q_s, k_refs, v_s, (gf_s, gb_s), s_scr, sums_scr, table_ref, lvl_ref, o_scr, seq)
        if st_ref is not None:
            for j in range(n_seq):
                for d in range(2):
                    for h in range(PAIR):
                        st_ref[j, d, h0 + h] = s_scr[j, d, h, pl.ds(0, key_rows), :]
        _add_pair_output(pair, gate_s, o_scr, gnw_ref, wo_ref, out_ref)

    def hgrn_pair(pp, carry):
        lb = lb_ref[pp]
        log_lb = jnp.log(lb)

        def proj(i, c):
            rows = _row_tile(i)
            p = _dot(h_scr[rows, :], wa_ref[pp])
            gate_s[rows, :] = _silu(p[:, 4 * width:5 * width]).astype(bf16)
            q_s[rows, :] = _silu(p[:, 0:width])
            v_s[rows, :] = p[:, width:2 * width].astype(bf16)
            for a, k_s, g_s in ((p[:, 2 * width:3 * width], kf_s, gf_s), (p[:, 3 * width:4 * width], kb_s, gb_s)):
                z = log_lb - a
                u = jnp.exp(-jnp.abs(a))
                w = jnp.exp(-jnp.abs(z))
                r = 1.0 / (1.0 + u)
                _store_log2_split(g_s, rows, jnp.maximum(z, 0.0) + jnp.minimum(a, 0.0) + jnp.log((1.0 + w) * r))
                k_s[rows, :] = (1.0 - lb) * jnp.where(a >= 0.0, u * r, r)
            return c
        lax.fori_loop(0, n_tiles, proj, 0)
        run_scans(pp, s0a_ref, sta_ref, PAIR * pp, HEAD, (kf_s, kb_s))
        return carry
    lax.fori_loop(0, N_HGRN // PAIR, hgrn_pair, 0)

    def gla_pair(pp, carry):
        def proj(i, c):
            rows = _row_tile(i)
            p = _dot(h_scr[rows, :], wb_ref[pp])
            gate_s[rows, :] = _silu(p[:, 3 * width:4 * width]).astype(bf16)
            q_s[rows, :] = p[:, 0:width] * (GLA_DK ** -0.5)
            kf_s[rows, :] = p[:, width:2 * width]
            v_s[rows, :] = p[:, 2 * width:3 * width].astype(bf16)
            low = low_scr[rows, :]
            for d, g_s in enumerate((gf_s, gb_s)):
                logits = _dot(low, gkw_ref[d, pp]) + gkb_ref[d, pp]
                log_gate = jnp.minimum(logits, 0.0) - jnp.log(1.0 + jnp.exp(-jnp.abs(logits)))
                _store_log2_split(g_s, rows, log_gate * (1.0 / GLA_GATE_NORM))
            return c
        lax.fori_loop(0, n_tiles, proj, 0)
        run_scans(N_HGRN // PAIR + pp, s0b_ref, stb_ref, PAIR * pp, GLA_DK, (kf_s, kf_s))
        return carry
    lax.fori_loop(0, N_GLA // PAIR, gla_pair, 0)

    _finish_output(x_ref, mod_ref, fnw_ref, out_ref, final)


def _odd_kernel(*refs, seq, has_state, emit_state, use_rope, final):
    it = iter(refs)
    lg_ref = next(it)
    x_ref, mod_ref, nw_ref = next(it), next(it), next(it)
    wr_ref, wo_ref, gnw_ref, fnw_ref = next(it), next(it), next(it), next(it)
    cos_ref, sin_ref = (next(it), next(it)) if use_rope else (None, None)
    s0_ref = next(it) if has_state else None
    out_ref = next(it)
    st_ref = next(it) if emit_state else None
    h_scr, o_scr, gate_s, q_s, kt_s, v_s, s_scr, dec_scr = it
    n_seq = STEP_TOKENS // seq
    n_tiles = STEP_TOKENS // ROW_TILE
    n_chunks = seq // CHUNK
    per_iter = min(n_chunks, RET_UNITS_PER_ITER // n_seq)
    assert n_chunks % per_iter == 0

    _modulated_norm(x_ref, mod_ref, nw_ref, h_scr)
    _clear_output(out_ref)
    t_idx = lax.broadcasted_iota(jnp.int32, (CHUNK, CHUNK), 0)
    s_idx = lax.broadcasted_iota(jnp.int32, (CHUNK, CHUNK), 1)
    row_f = lax.broadcasted_iota(jnp.int32, (CHUNK, HEAD), 0).astype(f32)
    col_f = lax.broadcasted_iota(jnp.int32, (8, CHUNK), 1).astype(f32)
    chunk_len = jnp.full((8, HEAD), CHUNK, f32)
    if use_rope:
        lane = lax.broadcasted_iota(jnp.int32, (ROW_TILE, HEAD), 1)
        first_quarter = (lane // (HEAD // 4)) % 2 == 0

    def rope(x, cos, sin_signed):
        xr = jnp.where(first_quarter, pltpu.roll(x, HEAD - HEAD // 4, axis=1), pltpu.roll(x, HEAD // 4, axis=1))
        return x * cos + xr * sin_signed

    width = PAIR * HEAD

    def pair_body(pp, carry):
        def proj(i, c):
            rows = _row_tile(i)
            p = _dot(h_scr[rows, :], wr_ref[pp])
            gate_s[rows, :] = _silu(p[:, 3 * width:4 * width]).astype(bf16)
            q = p[:, 0:width]
            k = p[:, width:2 * width] * (HEAD ** -0.5)
            if use_rope:
                cos, sin_signed = cos_ref[rows, :], sin_ref[rows, :]
                heads = [slice(h * HEAD, (h + 1) * HEAD) for h in range(PAIR)]
                q = jnp.concatenate([rope(q[:, lanes], cos, sin_signed) for lanes in heads], axis=-1)
                k = jnp.concatenate([rope(k[:, lanes], cos, sin_signed) for lanes in heads], axis=-1)
            q_s[rows, :] = q.astype(bf16)
            kt_s[:, rows] = k.T
            v_s[rows, :] = p[:, 2 * width:3 * width].astype(bf16)
            return c
        lax.fori_loop(0, n_tiles, proj, 0)
        for h in range(PAIR):
            scan_head(PAIR * pp + h, h)
        _add_pair_output(pp, gate_s, o_scr, gnw_ref, wo_ref, out_ref)
        return carry

    def scan_head(hh, h):
        lanes = slice(h * HEAD, (h + 1) * HEAD)
        lg_f = lg_ref[0, hh]
        lg_b = lg_ref[1, hh]
        for j in range(n_seq):
            for d in range(2):
                s_scr[j, d] = s0_ref[j, d, hh] if has_state else jnp.zeros((HEAD, HEAD), f32)

        dist = (t_idx - s_idx).astype(f32)
        dec_scr[0] = jnp.exp(lg_f * (row_f + 1.0))
        dec_scr[1] = jnp.exp(lg_b * (CHUNK - row_f))
        dec_scr[2] = (jnp.where(t_idx >= s_idx, jnp.exp(lg_f * jnp.maximum(dist, 0.0)), 0.0)
                      + jnp.where(s_idx >= t_idx, jnp.exp(lg_b * jnp.maximum(-dist, 0.0)), 0.0))
        dec_scr[3, 0:8, :] = jnp.exp(lg_f * (CHUNK - 1.0 - col_f))
        dec_scr[3, 8:16, :] = jnp.exp(lg_b * col_f)
        dec_scr[3, 16:24, :] = jnp.exp(lg_f * chunk_len)
        dec_scr[3, 24:32, :] = jnp.exp(lg_b * chunk_len)

        def body(i, c):
            units = [(j, r) for j in range(n_seq) for r in range(per_iter)]
            rows = {(j, r, d): _chunk_rows(j * seq, n_chunks - 1 - (i * per_iter + r) if d else i * per_iter + r)
                    for j, r in units for d in range(2)}
            scores = {u: _dot(q_s[rows[u + (0,)], lanes], kt_s[lanes, rows[u + (0,)]].astype(bf16)) for u in units}
            kv = {}
            for j, r in units:
                for d in range(2):
                    rw = rows[(j, r, d)]
                    keys = (kt_s[lanes, rw] * dec_scr[3, 8 * d:8 * d + 1, :]).astype(bf16)
                    kv[(j, r, d)] = _dot(keys, v_s[rw, lanes])
            state = {(j, d): s_scr[j, d] for j in range(n_seq) for d in range(2)}
            carried = {}
            for r in range(per_iter):
                for j in range(n_seq):
                    for d in range(2):
                        carried[(j, r, d)] = _dot(q_s[rows[(j, r, d)], lanes], state[(j, d)].astype(bf16))
                        state[(j, d)] = dec_scr[3, 16 + 8 * d:17 + 8 * d, :] * state[(j, d)] + kv[(j, r, d)]
            for j, r in units:
                rw = rows[(j, r, 0)]
                inside = _dot((scores[(j, r)] * dec_scr[2]).astype(bf16), v_s[rw, lanes])
                o_scr[0, h, rw, :] = inside + dec_scr[0] * carried[(j, r, 0)]
                o_scr[1, h, rows[(j, r, 1)], :] = dec_scr[1] * carried[(j, r, 1)]
            for (j, d), s in state.items():
                s_scr[j, d] = s
            return c
        lax.fori_loop(0, n_chunks // per_iter, body, 0)
        if emit_state:
            for j in range(n_seq):
                for d in range(2):
                    st_ref[j, d, hh] = s_scr[j, d]

    lax.fori_loop(0, N_RET // PAIR, pair_body, 0)

    _finish_output(x_ref, mod_ref, fnw_ref, out_ref, final)


def _const_spec(shape):
    zeros = (0,) * len(shape)
    return pl.BlockSpec(shape, lambda i: zeros, pipeline_mode=pl.Buffered(1))


def _step_spec(shape, per_step):
    zeros = (0,) * (len(shape) - 1)
    return pl.BlockSpec((per_step,) + tuple(shape[1:]), lambda i: (i,) + zeros)


def _mod_spec(d, per_sequence):
    if per_sequence:
        return pl.BlockSpec((1, 3, d), lambda i: (i + 1, 0, 0))
    return pl.BlockSpec((1, 3, d), lambda i: (0, 0, 0))


def _layer_call(body, x, mod, consts, states, state_shapes, per_sequence_mod, scratch, name, smem_inputs=()):
    n_seq, seq, d = x.shape
    per_step = STEP_TOKENS // seq
    assert per_step * seq == STEP_TOKENS and n_seq % per_step == 0 and seq % CHUNK == 0
    assert not per_sequence_mod or per_step == 1
    n_steps = n_seq // per_step
    xs = x.reshape(n_steps, STEP_TOKENS, d)
    inputs = list(smem_inputs) + [xs, mod] + list(consts) + list(states)
    in_specs = [pl.BlockSpec(memory_space=pltpu.SMEM)] * len(smem_inputs)
    in_specs += [_step_spec(xs.shape, 1), _mod_spec(d, per_sequence_mod)]
    in_specs += [_const_spec(a.shape) for a in consts]
    in_specs += [_step_spec(s.shape, per_step) for s in states]
    out_shape = [jax.ShapeDtypeStruct(xs.shape, f32)] + [jax.ShapeDtypeStruct(s, f32) for s in state_shapes]
    out_specs = [_step_spec(xs.shape, 1)] + [_step_spec(s, per_step) for s in state_shapes]
    outs = pl.pallas_call(
        body,
        grid=(n_steps,),
        in_specs=in_specs,
        out_specs=out_specs,
        out_shape=out_shape,
        scratch_shapes=scratch,
        compiler_params=pltpu.CompilerParams(dimension_semantics=("arbitrary",), vmem_limit_bytes=VMEM_LIMIT_BYTES),
        name=name,
    )(*inputs)
    return [outs[0].reshape(x.shape)] + list(outs[1:])


def _even_layer(x, mod, norm_w, w, final_w, tables, states, emit_state, per_sequence_mod, final):
    n_seq, seq, d = x.shape
    consts = [norm_w, w["wa"], w["wb"], w["wlow"], w["wout"], w["gkw"], w["gkb"], w["lb"], w["gnw"],
              final_w] + list(tables)
    state_shapes = [(n_seq, 2, N_HGRN, HEAD, HEAD), (n_seq, 2, N_GLA, GLA_DK, HEAD)] if emit_state else []
    scratch = [
        pltpu.VMEM((STEP_TOKENS, d), bf16),
        pltpu.VMEM((STEP_TOKENS, HEAD), bf16),
        pltpu.VMEM((2, PAIR, STEP_TOKENS, HEAD), f32),
        pltpu.VMEM((STEP_TOKENS, PAIR * HEAD), bf16),
        pltpu.VMEM((STEP_TOKENS, PAIR * HEAD), f32),
        pltpu.VMEM((STEP_TOKENS, PAIR * HEAD), bf16),
        pltpu.VMEM((STEP_TOKENS, PAIR * HEAD), f32),
        pltpu.VMEM((STEP_TOKENS, PAIR * HEAD), f32),
        pltpu.VMEM((STEP_TOKENS, 2 * PAIR * HEAD), bf16),
        pltpu.VMEM((STEP_TOKENS, 2 * PAIR * HEAD), bf16),
        pltpu.VMEM((STEP_TOKENS // seq, 2, PAIR, HEAD, HEAD), f32),
        pltpu.VMEM((2, GATED_UNITS_PER_ITER, ROW_TOTAL + 8, PAIR * HEAD), f32),
    ]
    body = functools.partial(_even_kernel, seq=seq, has_state=states is not None, emit_state=emit_state, final=final)
    return _layer_call(body, x, mod, consts, states or (), state_shapes, per_sequence_mod, scratch,
                       "even_layer_seq%d" % seq)


def _odd_layer(x, mod, norm_w, w, final_w, log_decay, rope, state, emit_state, per_sequence_mod, final):
    n_seq, seq, d = x.shape
    consts = [norm_w, w["wr"], w["wout"], w["gnw"], final_w] + list(rope or ())
    state_shapes = [(n_seq, 2, N_RET, HEAD, HEAD)] if emit_state else []
    scratch = [
        pltpu.VMEM((STEP_TOKENS, d), bf16),
        pltpu.VMEM((2, PAIR, STEP_TOKENS, HEAD), f32),
        pltpu.VMEM((STEP_TOKENS, PAIR * HEAD), bf16),
        pltpu.VMEM((STEP_TOKENS, PAIR * HEAD), bf16),
        pltpu.VMEM((PAIR * HEAD, STEP_TOKENS), f32),
        pltpu.VMEM((STEP_TOKENS, PAIR * HEAD), bf16),
        pltpu.VMEM((STEP_TOKENS // seq, 2, HEAD, HEAD), f32),
        pltpu.VMEM((4, CHUNK, HEAD), f32),
    ]
    body = functools.partial(_odd_kernel, seq=seq, has_state=state is not None, emit_state=emit_state,
                             use_rope=rope is not None, final=final)
    return _layer_call(body, x, mod, consts, () if state is None else (state,), state_shapes, per_sequence_mod,
                       scratch, "odd_layer_seq%d" % seq, smem_inputs=(log_decay,))


def _per_pair(w, n_heads, head_width=HEAD):
    d = w.shape[0]
    w = _pad_last(w.reshape(d, n_heads, -1), head_width)
    return jnp.transpose(w.reshape(d, n_heads // PAIR, PAIR * head_width), (1, 0, 2))


def _pad_last(a, width):
    return jnp.pad(a, [(0, 0)] * (a.ndim - 1) + [(0, width - a.shape[-1])])


def _even_weights(w_in, gk_w, gk_b, lb, gn_w, w_out):
    wa_w = N_HGRN * HEAD
    wb_k = N_GLA * GLA_DK
    wb_v = N_GLA * HEAD
    edges = np.cumsum([wa_w, wa_w, wa_w, wa_w, wa_w, wb_k, wb_k, wb_v, wb_v, GLA_RANK, GLA_RANK])[:-1].tolist()
    aq, ai, af_f, af_b, ag, bq, bk, bv, bg, bl_f, bl_b = jnp.split(w_in, edges, axis=-1)
    wa = jnp.concatenate([_per_pair(a, N_HGRN) for a in (aq, ai, af_f, af_b, ag)], axis=-1)
    wb = jnp.concatenate([_per_pair(a, N_GLA) for a in (bq, bk, bv, bg)], axis=-1)
    wlow = _pad_last(jnp.concatenate([bl_f, bl_b], axis=-1), HEAD)
    gkw = jnp.stack([
        jnp.pad(_per_pair(gk_w[d], N_GLA), ((0, 0), (d * GLA_RANK, HEAD - (d + 1) * GLA_RANK), (0, 0)))
        for d in range(2)])
    gkb = _pad_last(gk_b.reshape(2, N_GLA, GLA_DK), HEAD).reshape(2, N_GLA // PAIR, 1, PAIR * HEAD)
    return {
        "wa": wa.astype(bf16), "wb": wb.astype(bf16), "wlow": wlow.astype(bf16),
        "gkw": gkw.astype(bf16), "gkb": gkb, "lb": lb.reshape(N_HGRN // PAIR, 1, PAIR * HEAD),
        **_output_weights(gn_w, w_out),
    }


def _output_weights(gn_w, w_out):
    n_pairs = N_HEADS // PAIR
    return {"gnw": gn_w.reshape(n_pairs, 1, PAIR * HEAD),
            "wout": w_out.reshape(n_pairs, PAIR * HEAD, w_out.shape[-1]).astype(bf16)}


def _odd_weights(w_in, gn_w, w_out):
    wd = N_RET * HEAD
    wr = jnp.concatenate([_per_pair(a, N_RET) for a in jnp.split(w_in, [wd, 2 * wd, 3 * wd], axis=-1)], axis=-1)
    return {"wr": wr.astype(bf16), **_output_weights(gn_w, w_out)}


def _rope_tables(seq):
    rows = seq // GRID_W
    t_row = jnp.repeat(jnp.arange(rows), GRID_W).astype(f32)
    t_col = jnp.tile(jnp.arange(GRID_W), rows).astype(f32)
    half = HEAD // 2
    inv = ROPE_BASE ** (-jnp.arange(0, half, 2, dtype=f32) / half)
    ang_r = t_row[:, None] * inv
    ang_c = t_col[:, None] * inv
    ang = jnp.concatenate([ang_r, ang_r, ang_c, ang_c], axis=-1)
    sign = jnp.where((jnp.arange(HEAD) // (HEAD // 4)) % 2 == 0, -1.0, 1.0).astype(f32)
    return jnp.cos(ang), jnp.sin(ang) * sign


def kernel(x_prompt, x_sample, state_hgrn, state_gla, state_ret, c, c_ctx, norm_w, ada_w, ada_b, w_in_even, hgrn_lb, gla_gk_w, gla_gk_b, gn_even, w_out_even, w_in_odd, ret_decay, gn_odd, w_out_odd, final_norm_w):
    depth, d = norm_w.shape
    n_lat = x_sample.shape[0]
    n_cond = -(-(1 + n_lat) // 8) * 8
    cond = jnp.zeros((n_cond, d), f32).at[0].set(c_ctx).at[1:1 + n_lat].set(c)
    mod = _modulation(cond, ada_w, ada_b).reshape(depth, n_cond, 3, d)
    lbs = jnp.cumsum(jax.nn.softmax(hgrn_lb.astype(f32), axis=0), axis=0)
    final_w = final_norm_w.reshape(1, d)
    rope = _rope_tables(x_sample.shape[1])
    tables = _scan_tables()

    x_c, x_l = x_prompt, x_sample
    new_hgrn, new_gla, new_ret = [], [], []
    for l in range(depth):
        i = l // 2
        final = l == depth - 1
        nw = norm_w[l].reshape(1, d)
        if l % 2 == 0:
            w = _even_weights(w_in_even[i], gla_gk_w[i], gla_gk_b[i], lbs[i], gn_even[i], w_out_even[i])
            x_c, st_a, st_b = _even_layer(x_c, mod[l], nw, w, final_w, tables, None, True, False, final)
            (x_l,) = _even_layer(x_l, mod[l], nw, w, final_w, tables, (state_hgrn[:, i], state_gla[:, i]),
                                 False, True, final)
            new_hgrn.append(st_a)
            new_gla.append(st_b)
        else:
            w = _odd_weights(w_in_odd[i], gn_odd[i], w_out_odd[i])
            log_decay = jax.nn.log_sigmoid(ret_decay[i].astype(f32))
            x_c, st_c = _odd_layer(x_c, mod[l], nw, w, final_w, log_decay, None, None, True, False, final)
            (x_l,) = _odd_layer(x_l, mod[l], nw, w, final_w, log_decay, rope, state_ret[:, i], False, True, final)
            new_ret.append(st_c)
    return (x_c, x_l, jnp.stack(new_hgrn, axis=1), jnp.stack(new_gla, axis=1), jnp.stack(new_ret, axis=1))
```

```python
import functools

import numpy as np
import jax
import jax.numpy as jnp
from jax import lax
from jax.experimental import pallas as pl
from jax.experimental.pallas import tpu as pltpu

f32 = jnp.float32
bf16 = jnp.bfloat16
HIGHEST = lax.Precision.HIGHEST

EPS = 1e-6
LOG2E = 1.4426950408889634
HEAD = 128
N_HGRN = 4
N_GLA = 4
GLA_DK = 64
N_RET = 8
N_HEADS = 8
GLA_RANK = 16
GLA_GATE_NORM = 16.0
GRID_W = 64
ROPE_BASE = 10000.0

EVEN_SPLITS = (N_HGRN * HEAD,) * 5 + (N_GLA * GLA_DK,) * 2 + (N_GLA * HEAD,) * 2 + (GLA_RANK,) * 2
EVEN_STARTS = tuple(int(c) for c in np.cumsum((0,) + EVEN_SPLITS[:-1]))
HGRN_COLS = EVEN_STARTS[0:5]
GLA_QK_COLS = EVEN_STARTS[5:7]
GLA_COLS = EVEN_STARTS[7:9]
GLA_LOW_COLS = EVEN_STARTS[9:11]
RET_COLS = tuple(i * N_RET * HEAD for i in range(4))

CHUNK = 128
LEVELS = (1, 2, 4, 8, 16, 32, 64)
DIAG_LEVEL = len(LEVELS)
N_FINE = 3
BLK_CUM = N_FINE
ROW_TOTAL = (N_FINE + 1) * CHUNK
PAIR = 2
GATED_UNITS_PER_ITER = 4
RET_UNITS_PER_ITER = 8
STEP_TOKENS = 1024
ROW_TILE = 512
MOD_COLS = 768
VMEM_LIMIT_BYTES = 58 * 1024 * 1024


def _dot(a, b, precision=None):
    return jnp.dot(a, b, precision=precision, preferred_element_type=f32)


def _dot_tn(a, b):
    return lax.dot_general(a, b, (((0,), (0,)), ((), ())), preferred_element_type=f32)


def _silu(x):
    return x * jax.nn.sigmoid(x)


def _row_tile(i):
    return pl.ds(pl.multiple_of(i * ROW_TILE, ROW_TILE), ROW_TILE)


def _chunk_rows(seq_start, c):
    return pl.ds(pl.multiple_of(seq_start + c * CHUNK, CHUNK), CHUNK)


def _mod_kernel(cond_ref, w_ref, b_ref, o_ref):
    o_ref[0] = _dot(_silu(cond_ref[...]), w_ref[0], HIGHEST) + b_ref[0]


def _modulation(cond, ada_w, ada_b):
    depth, d, d3 = ada_w.shape
    rows = cond.shape[0]
    return pl.pallas_call(
        _mod_kernel,
        grid=(depth, d3 // MOD_COLS),
        in_specs=[
            pl.BlockSpec((rows, d), lambda l, j: (0, 0)),
            pl.BlockSpec((1, d, MOD_COLS), lambda l, j: (l, 0, j)),
            pl.BlockSpec((1, 1, MOD_COLS), lambda l, j: (l, 0, j)),
        ],
        out_specs=pl.BlockSpec((1, rows, MOD_COLS), lambda l, j: (l, 0, j)),
        out_shape=jax.ShapeDtypeStruct((depth, rows, d3), f32),
        compiler_params=pltpu.CompilerParams(dimension_semantics=("arbitrary", "arbitrary")),
        name="modulation",
    )(cond, ada_w, ada_b.reshape(depth, 1, d3))


def _modulated_norm(x_ref, mod_ref, nw_ref, h_scr):
    def body(i, carry):
        rows = _row_tile(i)
        x = x_ref[0, rows, :]
        y = x * lax.rsqrt(jnp.mean(x * x, axis=-1, keepdims=True) + EPS) * nw_ref[...]
        h_scr[rows, :] = (y * (1.0 + mod_ref[0, 1:2, :]) + mod_ref[0, 0:1, :]).astype(bf16)
        return carry
    lax.fori_loop(0, STEP_TOKENS // ROW_TILE, body, 0)


def _clear_output(out_ref):
    def body(i, carry):
        out_ref[0, _row_tile(i), :] = jnp.zeros((ROW_TILE, out_ref.shape[-1]), f32)
        return carry
    lax.fori_loop(0, STEP_TOKENS // ROW_TILE, body, 0)


def _pair_lanes(pair):
    return pl.ds(pl.multiple_of(pair * (PAIR * HEAD), PAIR * HEAD), PAIR * HEAD)


def _project(h, w_ref, pair, first_cols):
    width = PAIR * HEAD
    return [_dot(h, w_ref[:, pl.ds(pl.multiple_of(c0 + pair * width, width), width)]) for c0 in first_cols]


def _add_pair_output(pair, gate_s, o_scr, gnw_ref, wo_ref, out_ref):
    tiles = []
    for i in range(STEP_TOKENS // ROW_TILE):
        rows = pl.ds(i * ROW_TILE, ROW_TILE)
        parts = []
        for h in range(PAIR):
            o = o_scr[0, h, rows, :] + o_scr[1, h, rows, :]
            parts.append(o * lax.rsqrt(jnp.mean(o * o, axis=-1, keepdims=True) + EPS))
        y = jnp.concatenate(parts, axis=-1) * gnw_ref[:, _pair_lanes(pair)]
        tiles.append((rows, (y * gate_s[rows, :].astype(f32)).astype(bf16)))
    products = [(rows, _dot(z, wo_ref[_pair_lanes(pair), :])) for rows, z in tiles]
    for rows, product in products:
        out_ref[0, rows, :] += product


def _finish_output(x_ref, mod_ref, fnw_ref, out_ref, final):
    def body(i, carry):
        rows = _row_tile(i)
        xn = x_ref[0, rows, :] + mod_ref[0, 2:3, :] * out_ref[0, rows, :]
        if final:
            xn = xn * lax.rsqrt(jnp.mean(xn * xn, axis=-1, keepdims=True) + EPS) * fnw_ref[...]
        out_ref[0, rows, :] = xn
        return carry
    lax.fori_loop(0, STEP_TOKENS // ROW_TILE, body, 0)


def _scan_tables():
    t = np.arange(CHUNK)[:, None]
    j = np.arange(CHUNK)[None, :]
    fwd = []
    for m in LEVELS[:N_FINE]:
        mid = (t // (2 * m)) * (2 * m) + m
        right = t >= mid
        fwd.append(np.where(right, (j >= mid) & (j <= t), (j > t) & (j < mid)))
    fwd.append(j <= t)
    fwd.append(np.ones((8, CHUNK), bool))
    fwd = np.concatenate(fwd, axis=0).astype(np.float32)
    bwd = fwd.copy()
    n_sym = ROW_TOTAL // CHUNK
    bwd[:ROW_TOTAL] = fwd[:ROW_TOTAL].reshape(n_sym, CHUNK, CHUNK)[:, ::-1, ::-1].reshape(ROW_TOTAL, CHUNK)
    table = np.stack([np.tile(fwd, (1, 2)), np.tile(bwd, (1, 2))])
    x = t ^ j
    lvl = np.zeros((CHUNK, CHUNK), np.int32)
    for b in range(1, len(LEVELS)):
        lvl += (x >= (1 << b)).astype(np.int32)
    lvl_f = np.where(t > j, lvl, np.where(t == j, DIAG_LEVEL, -1)).astype(np.int32)
    return jnp.asarray(table, bf16), jnp.asarray(np.stack([lvl_f, lvl_f.T]))


def _split2(x):
    hi = x.astype(bf16)
    lo = (x - hi.astype(f32)).astype(bf16)
    return jnp.concatenate([hi, lo], axis=0)


def _in_chunk_scores(q, k, cum, fine, lvl, ones, rev):
    qb = q.astype(bf16)
    kt = k.astype(bf16).T
    sc = jnp.where(lvl == DIAG_LEVEL, _dot(qb, kt * ones), 0.0)
    for i in range(N_FINE):
        e = jnp.exp2(fine[i]).astype(bf16)
        sc = jnp.where(lvl == i, _dot(qb * e, kt * e.T), sc)
    for i in range(N_FINE, len(LEVELS)):
        m = LEVELS[i]
        blocks = []
        for p0 in range(0, CHUNK, 2 * m):
            left, right = slice(p0, p0 + m), slice(p0 + m, p0 + 2 * m)
            q_side, k_side = (left, right) if rev else (right, left)
            mid_row = p0 + m if rev else p0 + m - 1
            blocks.append((q_side, k_side, cum[mid_row:mid_row + 1, :]))
        q_rows = [qb[qs] * jnp.exp2(cum[qs] - mid).astype(bf16) for qs, _, mid in blocks]
        k_decay = []
        for _, ks, mid in blocks:
            decay = jnp.exp2(mid - cum[ks]).astype(bf16)
            zero = jnp.zeros((m, HEAD), bf16)
            k_decay += [zero, decay] if rev else [decay, zero]
        s = _dot(jnp.concatenate(q_rows, axis=0), kt * jnp.concatenate(k_decay, axis=0).T)
        rows = []
        for b, (qs, ks, _) in enumerate(blocks):
            updated = jnp.where(lvl[qs, :] == i, s[b * m:(b + 1) * m, :], sc[qs, :])
            rows += [updated, sc[ks, :]] if rev else [sc[ks, :], updated]
        sc = jnp.concatenate(rows, axis=0)
    return sc.astype(bf16)


def _store_log2_split(g_ref, rows, g):
    width = g.shape[-1]
    x = g * LOG2E
    hi = x.astype(bf16)
    g_ref[rows, 0:width] = hi
    g_ref[rows, width:2 * width] = (x - hi.astype(f32)).astype(bf16)


def _decay_sums(g_split, table):
    width = g_split.shape[-1] // 2
    return _dot(table, jnp.concatenate([g_split[:, :width], g_split[:, width:]], axis=0))


def _gated_chunks(chains, sums_ref, upcoming, upcoming_ref):
    heads = [slice(h * HEAD, (h + 1) * HEAD) for h in range(PAIR)]
    flat = [(c, r) for c, (_, units) in enumerate(chains) for r in range(len(units))]
    unit = {(c, r): chains[c][1][r] for c, r in flat}
    index = {key: u for u, key in enumerate(flat)}
    cum = {key: sums_ref[index[key], BLK_CUM * CHUNK:(BLK_CUM + 1) * CHUNK, :] for key in flat}
    total = {key: sums_ref[index[key], ROW_TOTAL:ROW_TOTAL + 1, :] for key in flat}
    vb = {key: [unit[key][2][:, lanes] for lanes in heads] for key in flat}
    upcoming = list(enumerate(upcoming))
    kv, whole = {}, {}
    for key in flat:
        k = unit[key][1]
        decay_out = jnp.exp2(total[key] - cum[key])
        kv[key] = [_dot_tn((k[:, lanes] * decay_out[:, lanes]).astype(bf16), vb[key][h])
                   for h, lanes in enumerate(heads)]
        whole[key] = [jnp.broadcast_to(jnp.exp2(total[key][:, lanes]), (HEAD, HEAD)).T for lanes in heads]
    state = [list(states) for states, _ in chains]
    outs, pending = {}, None
    for r in range(max(len(units) for _, units in chains)):
        live = [key for key in flat if key[1] == r]
        for key in live:
            q = unit[key][0]
            decay_in = jnp.exp2(cum[key])
            outs[key] = [_dot((q[:, lanes] * decay_in[:, lanes]).astype(bf16), state[key[0]][h].astype(bf16))
                         for h, lanes in enumerate(heads)]
            state[key[0]] = [whole[key][h] * state[key[0]][h] + kv[key][h] for h in range(PAIR)]
        for key in live:
            q, k, _, lvl, rev = unit[key]
            ones = (lvl[0:1, :] >= -1).astype(bf16)
            for h, lanes in enumerate(heads):
                fine = [sums_ref[index[key], i * CHUNK:(i + 1) * CHUNK, lanes] for i in range(N_FINE)]
                sc = _in_chunk_scores(q[:, lanes], k[:, lanes], cum[key][:, lanes], fine, lvl, ones, rev)
                if pending is not None:
                    pkey, ph, psc = pending
                    outs[pkey][ph] = outs[pkey][ph] + _dot(psc, vb[pkey][ph])
                pending = (key, h, sc)
            if upcoming:
                u, (g_split, table) = upcoming.pop(0)
                upcoming_ref[u] = _decay_sums(g_split, table)
    for u, (g_split, table) in upcoming:
        upcoming_ref[u] = _decay_sums(g_split, table)
    pkey, ph, psc = pending
    outs[pkey][ph] = outs[pkey][ph] + _dot(psc, vb[pkey][ph])
    return [[outs[(c, r)] for r in range(len(units))] for c, (_, units) in enumerate(chains)], state


def _gated_scans(q_s, k_refs, v_s, g_refs, s_scr, sums_scr, table_ref, lvl_ref, o_scr, seq):
    n = seq // CHUNK
    n_seq = STEP_TOKENS // seq
    per_iter = min(n, GATED_UNITS_PER_ITER // 2)
    seqs_per_iter = min(n_seq, GATED_UNITS_PER_ITER // (2 * per_iter))
    assert n % per_iter == 0 and n_seq % seqs_per_iter == 0
    iters_per_seq = n // per_iter
    n_iters = (n_seq // seqs_per_iter) * iters_per_seq

    def layout(it):
        jj, i = it // iters_per_seq, it % iters_per_seq
        chains = []
        for js in range(seqs_per_iter):
            j = jj * seqs_per_iter + js
            for d in range(2):
                steps = [i * per_iter + r for r in range(per_iter)]
                chains.append((j, d, [_chunk_rows(j * seq, n - 1 - t if d else t) for t in steps]))
        return chains

    def sums_inputs(it):
        return [(g_refs[d][rw, :], table_ref[d]) for _, d, rws in layout(it) for rw in rws]

    for u, (g_split, table) in enumerate(sums_inputs(0)):
        sums_scr[0, u] = _decay_sums(g_split, table)

    def iteration(it, slot):
        chains = layout(it)
        args = [([s_scr[j, d, h] for h in range(PAIR)],
                 [(q_s[rw, :], k_refs[d][rw, :], v_s[rw, :], lvl_ref[d], bool(d)) for rw in rws])
                for j, d, rws in chains]
        upcoming = sums_inputs(jnp.minimum(it + 1, n_iters - 1))
        outs, new_states = _gated_chunks(args, sums_scr.at[slot], upcoming, sums_scr.at[1 - slot])
        for c, (j, d, rws) in enumerate(chains):
            for h in range(PAIR):
                for r, rw in enumerate(rws):
                    o_scr[d, h, rw, :] = outs[c][r][h]
                s_scr[j, d, h] = new_states[c][h]

    assert n_iters % 2 == 0

    def body(it2, carry):
        iteration(2 * it2, 0)
        iteration(2 * it2 + 1, 1)
        return carry
    lax.fori_loop(0, n_iters // 2, body, 0)


def _even_kernel(*refs, seq, has_state, emit_state, final):
    it = iter(refs)
    x_ref, mod_ref, nw_ref = next(it), next(it), next(it)
    win_ref, wqk_ref, wlow_ref, wo_ref = next(it), next(it), next(it), next(it)
    gkw_ref, gkb_ref, lb_ref, gnw_ref, fnw_ref = next(it), next(it), next(it), next(it), next(it)
    table_ref, lvl_ref = next(it), next(it)
    s0a_ref, s0b_ref = (next(it), next(it)) if has_state else (None, None)
    out_ref = next(it)
    sta_ref, stb_ref = (next(it), next(it)) if emit_state else (None, None)
    h_scr, low_scr, o_scr, gate_s, q_s, v_s, kf_s, kb_s, gf_s, gb_s, s_scr, sums_scr = it
    n_seq = STEP_TOKENS // seq
    n_tiles = STEP_TOKENS // ROW_TILE

    _modulated_norm(x_ref, mod_ref, nw_ref, h_scr)
    _clear_output(out_ref)

    def low_body(i, carry):
        rows = _row_tile(i)
        low_scr[rows, :] = _dot(h_scr[rows, :], wlow_ref[...]).astype(bf16)
        return carry
    lax.fori_loop(0, n_tiles, low_body, 0)

    width = PAIR * HEAD

    def run_scans(pair, s0_ref, st_ref, h0, key_rows, k_refs):
        for j in range(n_seq):
            for d in range(2):
                for h in range(PAIR):
                    if s0_ref is None:
                        s_scr[j, d, h] = jnp.zeros((HEAD, HEAD), f32)
                    else:
                        if key_rows < HEAD:
                            s_scr[j, d, h, pl.ds(key_rows, HEAD - key_rows), :] = jnp.zeros(
                                (HEAD - key_rows, HEAD), f32)
                        s_scr[j, d, h, pl.ds(0, key_rows), :] = s0_ref[j, d, h0 + h]
        _gated_scans(q_s, k_refs, v_s, (gf_s, gb_s), s_scr, sums_scr, table_ref, lvl_ref, o_scr, seq)
        if st_ref is not None:
            for j in range(n_seq):
                for d in range(2):
                    for h in range(PAIR):
                        st_ref[j, d, h0 + h] = s_scr[j, d, h, pl.ds(0, key_rows), :]
        _add_pair_output(pair, gate_s, o_scr, gnw_ref, wo_ref, out_ref)

    def hgrn_pair(pp, carry):
        lb = lb_ref[:, _pair_lanes(pp)]
        log_lb = jnp.log(lb)

        def proj(i, c):
            rows = _row_tile(i)
            query, value, forget_f, forget_b, gate = _project(h_scr[rows, :], win_ref, pp, HGRN_COLS)
            gate_s[rows, :] = _silu(gate).astype(bf16)
            q_s[rows, :] = _silu(query)
            v_s[rows, :] = value.astype(bf16)
            for a, k_s, g_s in ((forget_f, kf_s, gf_s), (forget_b, kb_s, gb_s)):
                z = log_lb - a
                u = jnp.exp(-jnp.abs(a))
                w = jnp.exp(-jnp.abs(z))
                r = 1.0 / (1.0 + u)
                _store_log2_split(g_s, rows, jnp.maximum(z, 0.0) + jnp.minimum(a, 0.0) + jnp.log((1.0 + w) * r))
                k_s[rows, :] = (1.0 - lb) * jnp.where(a >= 0.0, u * r, r)
            return c
        lax.fori_loop(0, n_tiles, proj, 0)
        run_scans(pp, s0a_ref, sta_ref, PAIR * pp, HEAD, (kf_s, kb_s))
        return carry
    lax.fori_loop(0, N_HGRN // PAIR, hgrn_pair, 0)

    def gla_pair(pp, carry):
        def proj(i, c):
            rows = _row_tile(i)
            h = h_scr[rows, :]
            qk = _dot(h, wqk_ref[pp])
            value, gate = _project(h, win_ref, pp, GLA_COLS)
            gate_s[rows, :] = _silu(gate).astype(bf16)
            q_s[rows, :] = qk[:, 0:width] * (GLA_DK ** -0.5)
            kf_s[rows, :] = qk[:, width:2 * width]
            v_s[rows, :] = value.astype(bf16)
            low = low_scr[rows, :]
            for d, g_s in enumerate((gf_s, gb_s)):
                logits = _dot(low, gkw_ref[d, pp]) + gkb_ref[d, pp]
                log_gate = jnp.minimum(logits, 0.0) - jnp.log(1.0 + jnp.exp(-jnp.abs(logits)))
                _store_log2_split(g_s, rows, log_gate * (1.0 / GLA_GATE_NORM))
            return c
        lax.fori_loop(0, n_tiles, proj, 0)
        run_scans(N_HGRN // PAIR + pp, s0b_ref, stb_ref, PAIR * pp, GLA_DK, (kf_s, kf_s))
        return carry
    lax.fori_loop(0, N_GLA // PAIR, gla_pair, 0)

    _finish_output(x_ref, mod_ref, fnw_ref, out_ref, final)


def _odd_kernel(*refs, seq, has_state, emit_state, use_rope, final):
    it = iter(refs)
    lg_ref = next(it)
    x_ref, mod_ref, nw_ref = next(it), next(it), next(it)
    wr_ref, wo_ref, gnw_ref, fnw_ref = next(it), next(it), next(it), next(it)
    cos_ref, sin_ref = (next(it), next(it)) if use_rope else (None, None)
    s0_ref = next(it) if has_state else None
    out_ref = next(it)
    st_ref = next(it) if emit_state else None
    h_scr, o_scr, gate_s, q_s, kt_s, v_s, s_scr, dec_scr = it
    n_seq = STEP_TOKENS // seq
    n_tiles = STEP_TOKENS // ROW_TILE
    n_chunks = seq // CHUNK
    per_iter = min(n_chunks, RET_UNITS_PER_ITER // n_seq)
    assert n_chunks % per_iter == 0

    _modulated_norm(x_ref, mod_ref, nw_ref, h_scr)
    _clear_output(out_ref)
    t_idx = lax.broadcasted_iota(jnp.int32, (CHUNK, CHUNK), 0)
    s_idx = lax.broadcasted_iota(jnp.int32, (CHUNK, CHUNK), 1)
    row_f = lax.broadcasted_iota(jnp.int32, (CHUNK, HEAD), 0).astype(f32)
    col_f = lax.broadcasted_iota(jnp.int32, (8, CHUNK), 1).astype(f32)
    chunk_len = jnp.full((8, HEAD), CHUNK, f32)
    if use_rope:
        lane = lax.broadcasted_iota(jnp.int32, (ROW_TILE, HEAD), 1)
        first_quarter = (lane // (HEAD // 4)) % 2 == 0

    def rope(x, cos, sin_signed):
        xr = jnp.where(first_quarter, pltpu.roll(x, HEAD - HEAD // 4, axis=1), pltpu.roll(x, HEAD // 4, axis=1))
        return x * cos + xr * sin_signed

    width = PAIR * HEAD

    def pair_body(pp, carry):
        def proj(i, c):
            rows = _row_tile(i)
            q, k, value, gate = _project(h_scr[rows, :], wr_ref, pp, RET_COLS)
            gate_s[rows, :] = _silu(gate).astype(bf16)
            k = k * (HEAD ** -0.5)
            if use_rope:
                cos, sin_signed = cos_ref[rows, :], sin_ref[rows, :]
                heads = [slice(h * HEAD, (h + 1) * HEAD) for h in range(PAIR)]
                q = jnp.concatenate([rope(q[:, lanes], cos, sin_signed) for lanes in heads], axis=-1)
                k = jnp.concatenate([rope(k[:, lanes], cos, sin_signed) for lanes in heads], axis=-1)
            q_s[rows, :] = q.astype(bf16)
            kt_s[:, rows] = k.T
            v_s[rows, :] = value.astype(bf16)
            return c
        lax.fori_loop(0, n_tiles, proj, 0)
        for h in range(PAIR):
            scan_head(PAIR * pp + h, h)
        _add_pair_output(pp, gate_s, o_scr, gnw_ref, wo_ref, out_ref)
        return carry

    def scan_head(hh, h):
        lanes = slice(h * HEAD, (h + 1) * HEAD)
        lg_f = lg_ref[0, hh]
        lg_b = lg_ref[1, hh]
        for j in range(n_seq):
            for d in range(2):
                s_scr[j, d] = s0_ref[j, d, hh] if has_state else jnp.zeros((HEAD, HEAD), f32)

        dist = (t_idx - s_idx).astype(f32)
        dec_scr[0] = jnp.exp(lg_f * (row_f + 1.0))
        dec_scr[1] = jnp.exp(lg_b * (CHUNK - row_f))
        dec_scr[2] = (jnp.where(t_idx >= s_idx, jnp.exp(lg_f * jnp.maximum(dist, 0.0)), 0.0)
                      + jnp.where(s_idx >= t_idx, jnp.exp(lg_b * jnp.maximum(-dist, 0.0)), 0.0))
        dec_scr[3, 0:8, :] = jnp.exp(lg_f * (CHUNK - 1.0 - col_f))
        dec_scr[3, 8:16, :] = jnp.exp(lg_b * col_f)
        dec_scr[3, 16:24, :] = jnp.exp(lg_f * chunk_len)
        dec_scr[3, 24:32, :] = jnp.exp(lg_b * chunk_len)

        def body(i, c):
            units = [(j, r) for j in range(n_seq) for r in range(per_iter)]
            rows = {(j, r, d): _chunk_rows(j * seq, n_chunks - 1 - (i * per_iter + r) if d else i * per_iter + r)
                    for j, r in units for d in range(2)}
            scores = {u: _dot(q_s[rows[u + (0,)], lanes], kt_s[lanes, rows[u + (0,)]].astype(bf16)) for u in units}
            kv = {}
            for j, r in units:
                for d in range(2):
                    rw = rows[(j, r, d)]
                    keys = (kt_s[lanes, rw] * dec_scr[3, 8 * d:8 * d + 1, :]).astype(bf16)
                    kv[(j, r, d)] = _dot(keys, v_s[rw, lanes])
            state = {(j, d): s_scr[j, d] for j in range(n_seq) for d in range(2)}
            carried = {}
            for r in range(per_iter):
                for j in range(n_seq):
                    for d in range(2):
                        carried[(j, r, d)] = _dot(q_s[rows[(j, r, d)], lanes], state[(j, d)].astype(bf16))
                        state[(j, d)] = dec_scr[3, 16 + 8 * d:17 + 8 * d, :] * state[(j, d)] + kv[(j, r, d)]
            for j, r in units:
                rw = rows[(j, r, 0)]
                inside = _dot((scores[(j, r)] * dec_scr[2]).astype(bf16), v_s[rw, lanes])
                o_scr[0, h, rw, :] = inside + dec_scr[0] * carried[(j, r, 0)]
                o_scr[1, h, rows[(j, r, 1)], :] = dec_scr[1] * carried[(j, r, 1)]
            for (j, d), s in state.items():
                s_scr[j, d] = s
            return c
        lax.fori_loop(0, n_chunks // per_iter, body, 0)
        if emit_state:
            for j in range(n_seq):
                for d in range(2):
                    st_ref[j, d, hh] = s_scr[j, d]

    lax.fori_loop(0, N_RET // PAIR, pair_body, 0)

    _finish_output(x_ref, mod_ref, fnw_ref, out_ref, final)


def _const_spec(shape):
    zeros = (0,) * len(shape)
    return pl.BlockSpec(shape, lambda i: zeros, pipeline_mode=pl.Buffered(1))


def _step_spec(shape, per_step):
    zeros = (0,) * (len(shape) - 1)
    return pl.BlockSpec((per_step,) + tuple(shape[1:]), lambda i: (i,) + zeros)


def _mod_spec(d, per_sequence):
    if per_sequence:
        return pl.BlockSpec((1, 3, d), lambda i: (i + 1, 0, 0))
    return pl.BlockSpec((1, 3, d), lambda i: (0, 0, 0))


def _layer_call(body, x, mod, consts, states, state_shapes, per_sequence_mod, scratch, name, smem_inputs=()):
    n_seq, seq, d = x.shape
    per_step = STEP_TOKENS // seq
    assert per_step * seq == STEP_TOKENS and n_seq % per_step == 0 and seq % CHUNK == 0
    assert not per_sequence_mod or per_step == 1
    n_steps = n_seq // per_step
    xs = x.reshape(n_steps, STEP_TOKENS, d)
    inputs = list(smem_inputs) + [xs, mod] + list(consts) + list(states)
    in_specs = [pl.BlockSpec(memory_space=pltpu.SMEM)] * len(smem_inputs)
    in_specs += [_step_spec(xs.shape, 1), _mod_spec(d, per_sequence_mod)]
    in_specs += [_const_spec(a.shape) for a in consts]
    in_specs += [_step_spec(s.shape, per_step) for s in states]
    out_shape = [jax.ShapeDtypeStruct(xs.shape, f32)] + [jax.ShapeDtypeStruct(s, f32) for s in state_shapes]
    out_specs = [_step_spec(xs.shape, 1)] + [_step_spec(s, per_step) for s in state_shapes]
    outs = pl.pallas_call(
        body,
        grid=(n_steps,),
        in_specs=in_specs,
        out_specs=out_specs,
        out_shape=out_shape,
        scratch_shapes=scratch,
        compiler_params=pltpu.CompilerParams(dimension_semantics=("arbitrary",), vmem_limit_bytes=VMEM_LIMIT_BYTES),
        name=name,
    )(*inputs)
    return [outs[0].reshape(x.shape)] + list(outs[1:])


def _even_layer(x, mod, norm_w, w, final_w, tables, states, emit_state, per_sequence_mod, final):
    n_seq, seq, d = x.shape
    consts = [norm_w, w["win"], w["wqk"], w["wlow"], w["wout"], w["gkw"], w["gkb"], w["lb"], w["gnw"],
              final_w] + list(tables)
    state_shapes = [(n_seq, 2, N_HGRN, HEAD, HEAD), (n_seq, 2, N_GLA, GLA_DK, HEAD)] if emit_state else []
    scratch = [
        pltpu.VMEM((STEP_TOKENS, d), bf16),
        pltpu.VMEM((STEP_TOKENS, HEAD), bf16),
        pltpu.VMEM((2, PAIR, STEP_TOKENS, HEAD), f32),
        pltpu.VMEM((STEP_TOKENS, PAIR * HEAD), bf16),
        pltpu.VMEM((STEP_TOKENS, PAIR * HEAD), f32),
        pltpu.VMEM((STEP_TOKENS, PAIR * HEAD), bf16),
        pltpu.VMEM((STEP_TOKENS, PAIR * HEAD), f32),
        pltpu.VMEM((STEP_TOKENS, PAIR * HEAD), f32),
        pltpu.VMEM((STEP_TOKENS, 2 * PAIR * HEAD), bf16),
        pltpu.VMEM((STEP_TOKENS, 2 * PAIR * HEAD), bf16),
        pltpu.VMEM((STEP_TOKENS // seq, 2, PAIR, HEAD, HEAD), f32),
        pltpu.VMEM((2, GATED_UNITS_PER_ITER, ROW_TOTAL + 8, PAIR * HEAD), f32),
    ]
    body = functools.partial(_even_kernel, seq=seq, has_state=states is not None, emit_state=emit_state, final=final)
    return _layer_call(body, x, mod, consts, states or (), state_shapes, per_sequence_mod, scratch,
                       "even_layer_seq%d" % seq)


def _odd_layer(x, mod, norm_w, w, final_w, log_decay, rope, state, emit_state, per_sequence_mod, final):
    n_seq, seq, d = x.shape
    consts = [norm_w, w["wr"], w["wout"], w["gnw"], final_w] + list(rope or ())
    state_shapes = [(n_seq, 2, N_RET, HEAD, HEAD)] if emit_state else []
    scratch = [
        pltpu.VMEM((STEP_TOKENS, d), bf16),
        pltpu.VMEM((2, PAIR, STEP_TOKENS, HEAD), f32),
        pltpu.VMEM((STEP_TOKENS, PAIR * HEAD), bf16),
        pltpu.VMEM((STEP_TOKENS, PAIR * HEAD), bf16),
        pltpu.VMEM((PAIR * HEAD, STEP_TOKENS), f32),
        pltpu.VMEM((STEP_TOKENS, PAIR * HEAD), bf16),
        pltpu.VMEM((STEP_TOKENS // seq, 2, HEAD, HEAD), f32),
        pltpu.VMEM((4, CHUNK, HEAD), f32),
    ]
    body = functools.partial(_odd_kernel, seq=seq, has_state=state is not None, emit_state=emit_state,
                             use_rope=rope is not None, final=final)
    return _layer_call(body, x, mod, consts, () if state is None else (state,), state_shapes, per_sequence_mod,
                       scratch, "odd_layer_seq%d" % seq, smem_inputs=(log_decay,))


def _per_pair(w, n_heads, head_width=HEAD):
    d = w.shape[0]
    w = _pad_last(w.reshape(d, n_heads, -1), head_width)
    return jnp.transpose(w.reshape(d, n_heads // PAIR, PAIR * head_width), (1, 0, 2))


def _pad_last(a, width):
    return jnp.pad(a, [(0, 0)] * (a.ndim - 1) + [(0, width - a.shape[-1])])


def _even_weights(w_in, gk_w, gk_b, lb, gn_w, w_out):
    cols = lambda c0, n: w_in[:, c0:c0 + n]
    wqk = jnp.concatenate([_per_pair(cols(c0, N_GLA * GLA_DK), N_GLA) for c0 in GLA_QK_COLS], axis=-1)
    wlow = _pad_last(cols(GLA_LOW_COLS[0], 2 * GLA_RANK), HEAD)
    gkw = jnp.stack([
        jnp.pad(_per_pair(gk_w[d], N_GLA), ((0, 0), (d * GLA_RANK, HEAD - (d + 1) * GLA_RANK), (0, 0)))
        for d in range(2)])
    gkb = _pad_last(gk_b.reshape(2, N_GLA, GLA_DK), HEAD).reshape(2, N_GLA // PAIR, 1, PAIR * HEAD)
    return {"win": w_in.astype(bf16), "wqk": wqk.astype(bf16), "wlow": wlow.astype(bf16), "gkw": gkw.astype(bf16),
            "gkb": gkb, "lb": lb.reshape(1, -1), "gnw": gn_w.reshape(1, -1), "wout": w_out.astype(bf16)}


def _odd_weights(w_in, gn_w, w_out):
    return {"wr": w_in.astype(bf16), "gnw": gn_w.reshape(1, -1), "wout": w_out.astype(bf16)}


def _rope_tables(seq):
    rows = seq // GRID_W
    t_row = jnp.repeat(jnp.arange(rows), GRID_W).astype(f32)
    t_col = jnp.tile(jnp.arange(GRID_W), rows).astype(f32)
    half = HEAD // 2
    inv = ROPE_BASE ** (-jnp.arange(0, half, 2, dtype=f32) / half)
    ang_r = t_row[:, None] * inv
    ang_c = t_col[:, None] * inv
    ang = jnp.concatenate([ang_r, ang_r, ang_c, ang_c], axis=-1)
    sign = jnp.where((jnp.arange(HEAD) // (HEAD // 4)) % 2 == 0, -1.0, 1.0).astype(f32)
    return jnp.cos(ang), jnp.sin(ang) * sign


def kernel(x_prompt, x_sample, state_hgrn, state_gla, state_ret, c, c_ctx, norm_w, ada_w, ada_b, w_in_even, hgrn_lb, gla_gk_w, gla_gk_b, gn_even, w_out_even, w_in_odd, ret_decay, gn_odd, w_out_odd, final_norm_w):
    depth, d = norm_w.shape
    n_lat = x_sample.shape[0]
    n_cond = -(-(1 + n_lat) // 8) * 8
    cond = jnp.zeros((n_cond, d), f32).at[0].set(c_ctx).at[1:1 + n_lat].set(c)
    mod = _modulation(cond, ada_w, ada_b).reshape(depth, n_cond, 3, d)
    lbs = jnp.cumsum(jax.nn.softmax(hgrn_lb.astype(f32), axis=0), axis=0)
    final_w = final_norm_w.reshape(1, d)
    rope = _rope_tables(x_sample.shape[1])
    tables = _scan_tables()

    x_c, x_l = x_prompt, x_sample
    new_hgrn, new_gla, new_ret = [], [], []
    for l in range(depth):
        i = l // 2
        final = l == depth - 1
        nw = norm_w[l].reshape(1, d)
        if l % 2 == 0:
            w = _even_weights(w_in_even[i], gla_gk_w[i], gla_gk_b[i], lbs[i], gn_even[i], w_out_even[i])
            x_c, st_a, st_b = _even_layer(x_c, mod[l], nw, w, final_w, tables, None, True, False, final)
            (x_l,) = _even_layer(x_l, mod[l], nw, w, final_w, tables, (state_hgrn[:, i], state_gla[:, i]),
                                 False, True, final)
            new_hgrn.append(st_a)
            new_gla.append(st_b)
        else:
            w = _odd_weights(w_in_odd[i], gn_odd[i], w_out_odd[i])
            log_decay = jax.nn.log_sigmoid(ret_decay[i].astype(f32))
            x_c, st_c = _odd_layer(x_c, mod[l], nw, w, final_w, log_decay, None, None, True, False, final)
            (x_l,) = _odd_layer(x_l, mod[l], nw, w, final_w, log_decay, rope, state_ret[:, i], False, True, final)
            new_ret.append(st_c)
    return (x_c, x_l, jnp.stack(new_hgrn, axis=1), jnp.stack(new_gla, axis=1), jnp.stack(new_ret, axis=1))
```

```python
import functools

import numpy as np
import jax
import jax.numpy as jnp
from jax import lax
from jax.experimental import pallas as pl
from jax.experimental.pallas import tpu as pltpu

f32 = jnp.float32
bf16 = jnp.bfloat16
HIGHEST = lax.Precision.HIGHEST

EPS = 1e-6
LOG2E = 1.4426950408889634
HEAD = 128
N_HGRN = 4
N_GLA = 4
GLA_DK = 64
N_RET = 8
N_HEADS = 8
GLA_RANK = 16
GLA_GATE_NORM = 16.0
GRID_W = 64
ROPE_BASE = 10000.0

EVEN_SPLITS = (N_HGRN * HEAD,) * 5 + (N_GLA * GLA_DK,) * 2 + (N_GLA * HEAD,) * 2 + (GLA_RANK,) * 2
EVEN_STARTS = tuple(int(c) for c in np.cumsum((0,) + EVEN_SPLITS[:-1]))
HGRN_COLS = EVEN_STARTS[0:5]
GLA_QK_COLS = EVEN_STARTS[5:7]
GLA_COLS = EVEN_STARTS[7:9]
GLA_LOW_COLS = EVEN_STARTS[9:11]
RET_COLS = tuple(i * N_RET * HEAD for i in range(4))

CHUNK = 128
LEVELS = (1, 2, 4, 8, 16, 32, 64)
DIAG_LEVEL = len(LEVELS)
N_FINE = 3
BLK_CUM = N_FINE
ROW_TOTAL = (N_FINE + 1) * CHUNK
PAIR = 2
GATED_HEADS_PER_ITER = 8
GATED_UNITS_MAX = 4
GLA_HEADS_PER_GROUP = HEAD // GLA_DK
RET_UNITS_PER_ITER = 8
STEP_TOKENS = 1024
ROW_TILE = 512
MOD_COLS = 768
VMEM_LIMIT_BYTES = 58 * 1024 * 1024


def _dot(a, b, precision=None):
    return jnp.dot(a, b, precision=precision, preferred_element_type=f32)


def _dot_tn(a, b):
    return lax.dot_general(a, b, (((0,), (0,)), ((), ())), preferred_element_type=f32)


def _silu(x):
    return x * jax.nn.sigmoid(x)


def _row_tile(i):
    return pl.ds(pl.multiple_of(i * ROW_TILE, ROW_TILE), ROW_TILE)


def _chunk_rows(seq_start, c):
    return pl.ds(pl.multiple_of(seq_start + c * CHUNK, CHUNK), CHUNK)


def _mod_kernel(cond_ref, w_ref, b_ref, o_ref):
    o_ref[0] = _dot(_silu(cond_ref[...]), w_ref[0], HIGHEST) + b_ref[0]


def _modulation(cond, ada_w, ada_b):
    depth, d, d3 = ada_w.shape
    rows = cond.shape[0]
    return pl.pallas_call(
        _mod_kernel,
        grid=(depth, d3 // MOD_COLS),
        in_specs=[
            pl.BlockSpec((rows, d), lambda l, j: (0, 0)),
            pl.BlockSpec((1, d, MOD_COLS), lambda l, j: (l, 0, j)),
            pl.BlockSpec((1, 1, MOD_COLS), lambda l, j: (l, 0, j)),
        ],
        out_specs=pl.BlockSpec((1, rows, MOD_COLS), lambda l, j: (l, 0, j)),
        out_shape=jax.ShapeDtypeStruct((depth, rows, d3), f32),
        compiler_params=pltpu.CompilerParams(dimension_semantics=("arbitrary", "arbitrary")),
        name="modulation",
    )(cond, ada_w, ada_b.reshape(depth, 1, d3))


def _modulated_norm(x_ref, mod_ref, nw_ref, h_scr):
    def body(i, carry):
        rows = _row_tile(i)
        x = x_ref[0, rows, :]
        y = x * lax.rsqrt(jnp.mean(x * x, axis=-1, keepdims=True) + EPS) * nw_ref[...]
        h_scr[rows, :] = (y * (1.0 + mod_ref[0, 1:2, :]) + mod_ref[0, 0:1, :]).astype(bf16)
        return carry
    lax.fori_loop(0, STEP_TOKENS // ROW_TILE, body, 0)


def _clear_output(out_ref):
    def body(i, carry):
        out_ref[0, _row_tile(i), :] = jnp.zeros((ROW_TILE, out_ref.shape[-1]), f32)
        return carry
    lax.fori_loop(0, STEP_TOKENS // ROW_TILE, body, 0)


def _pair_lanes(pair):
    return pl.ds(pl.multiple_of(pair * (PAIR * HEAD), PAIR * HEAD), PAIR * HEAD)


def _project(h, w_ref, pair, first_cols):
    width = PAIR * HEAD
    return [_dot(h, w_ref[:, pl.ds(pl.multiple_of(c0 + pair * width, width), width)]) for c0 in first_cols]


def _add_heads_output(first_head, n_heads, gate_s, o_scr, gnw_ref, wo_ref, out_ref):
    width = n_heads * HEAD
    lanes = pl.ds(pl.multiple_of(first_head * HEAD, width), width)
    tiles = []
    for i in range(STEP_TOKENS // ROW_TILE):
        rows = pl.ds(i * ROW_TILE, ROW_TILE)
        parts = []
        for h in range(n_heads):
            o = o_scr[0, h, rows, :] + o_scr[1, h, rows, :]
            parts.append(o * lax.rsqrt(jnp.mean(o * o, axis=-1, keepdims=True) + EPS))
        y = jnp.concatenate(parts, axis=-1) * gnw_ref[:, lanes]
        tiles.append((rows, (y * gate_s[rows, 0:width].astype(f32)).astype(bf16)))
    products = [(rows, _dot(z, wo_ref[lanes, :])) for rows, z in tiles]
    for rows, product in products:
        out_ref[0, rows, :] += product


def _finish_output(x_ref, mod_ref, fnw_ref, out_ref, final):
    def body(i, carry):
        rows = _row_tile(i)
        xn = x_ref[0, rows, :] + mod_ref[0, 2:3, :] * out_ref[0, rows, :]
        if final:
            xn = xn * lax.rsqrt(jnp.mean(xn * xn, axis=-1, keepdims=True) + EPS) * fnw_ref[...]
        out_ref[0, rows, :] = xn
        return carry
    lax.fori_loop(0, STEP_TOKENS // ROW_TILE, body, 0)


def _scan_tables():
    t = np.arange(CHUNK)[:, None]
    j = np.arange(CHUNK)[None, :]
    fwd = []
    for m in LEVELS[:N_FINE]:
        mid = (t // (2 * m)) * (2 * m) + m
        right = t >= mid
        fwd.append(np.where(right, (j >= mid) & (j <= t), (j > t) & (j < mid)))
    fwd.append(j <= t)
    fwd.append(np.ones((8, CHUNK), bool))
    fwd = np.concatenate(fwd, axis=0).astype(np.float32)
    bwd = fwd.copy()
    n_sym = ROW_TOTAL // CHUNK
    bwd[:ROW_TOTAL] = fwd[:ROW_TOTAL].reshape(n_sym, CHUNK, CHUNK)[:, ::-1, ::-1].reshape(ROW_TOTAL, CHUNK)
    table = np.stack([np.tile(fwd, (1, 2)), np.tile(bwd, (1, 2))])
    x = t ^ j
    lvl = np.zeros((CHUNK, CHUNK), np.int32)
    for b in range(1, len(LEVELS)):
        lvl += (x >= (1 << b)).astype(np.int32)
    lvl_f = np.where(t > j, lvl, np.where(t == j, DIAG_LEVEL, -1)).astype(np.int32)
    return jnp.asarray(table, bf16), jnp.asarray(np.stack([lvl_f, lvl_f.T]))


def _in_chunk_scores(q, k, cum, fine, lvl, ones, rev, key_masks):
    qb = q.astype(bf16)
    q_heads = [qb if mask is None else qb * mask for mask in key_masks]
    kt = k.astype(bf16).T
    diag_keys = kt * ones
    scores = [jnp.where(lvl == DIAG_LEVEL, _dot(qh, diag_keys), 0.0) for qh in q_heads]
    for i in range(N_FINE):
        e = jnp.exp2(fine[i]).astype(bf16)
        keys = kt * e.T
        scores = [jnp.where(lvl == i, _dot(qh * e, keys), sc) for qh, sc in zip(q_heads, scores)]
    for i in range(N_FINE, len(LEVELS)):
        m = LEVELS[i]
        blocks = []
        for p0 in range(0, CHUNK, 2 * m):
            left, right = slice(p0, p0 + m), slice(p0 + m, p0 + 2 * m)
            q_side, k_side = (left, right) if rev else (right, left)
            mid_row = p0 + m if rev else p0 + m - 1
            blocks.append((q_side, k_side, cum[mid_row:mid_row + 1, :]))
        q_decay = [jnp.exp2(cum[qs] - mid).astype(bf16) for qs, _, mid in blocks]
        k_decay = []
        for _, ks, mid in blocks:
            decay = jnp.exp2(mid - cum[ks]).astype(bf16)
            zero = jnp.zeros((m, HEAD), bf16)
            k_decay += [zero, decay] if rev else [decay, zero]
        keys = kt * jnp.concatenate(k_decay, axis=0).T
        for h, qh in enumerate(q_heads):
            s = _dot(jnp.concatenate([qh[qs] * e for (qs, _, _), e in zip(blocks, q_decay)], axis=0), keys)
            rows = []
            for b, (qs, ks, _) in enumerate(blocks):
                updated = jnp.where(lvl[qs, :] == i, s[b * m:(b + 1) * m, :], scores[h][qs, :])
                rows += [updated, scores[h][ks, :]] if rev else [scores[h][ks, :], updated]
            scores[h] = jnp.concatenate(rows, axis=0)
    return [sc.astype(bf16) for sc in scores]


def _store_log2_split(g_ref, rows, g):
    width = g.shape[-1]
    x = g * LOG2E
    hi = x.astype(bf16)
    g_ref[rows, 0:width] = hi
    g_ref[rows, width:2 * width] = (x - hi.astype(f32)).astype(bf16)


def _decay_sums(g_split, table):
    width = g_split.shape[-1] // 2
    return _dot(table, jnp.concatenate([g_split[:, :width], g_split[:, width:]], axis=0))


def _key_masks(heads_per_group):
    if heads_per_group == 1:
        return [None]
    lane = lax.broadcasted_iota(jnp.int32, (1, HEAD), 1)
    return [(lane // (HEAD // heads_per_group) == sub).astype(bf16) for sub in range(heads_per_group)]


def _gated_chunks(chains, sums_ref, upcoming, upcoming_ref, heads_per_group):
    masks = _key_masks(heads_per_group)
    groups = [slice(g * HEAD, (g + 1) * HEAD) for g in range(PAIR)]
    n_heads = PAIR * heads_per_group
    group_of = [h // heads_per_group for h in range(n_heads)]
    mask_of = [masks[h % heads_per_group] for h in range(n_heads)]
    flat = [(c, r) for c, (_, units) in enumerate(chains) for r in range(len(units))]
    unit = {(c, r): chains[c][1][r] for c, r in flat}
    index = {key: u for u, key in enumerate(flat)}
    cum = {key: sums_ref[index[key], BLK_CUM * CHUNK:(BLK_CUM + 1) * CHUNK, :] for key in flat}
    total = {key: sums_ref[index[key], ROW_TOTAL:ROW_TOTAL + 1, :] for key in flat}
    vb = {key: [unit[key][2][:, h * HEAD:(h + 1) * HEAD] for h in range(n_heads)] for key in flat}
    upcoming = list(enumerate(upcoming))
    kv, whole = {}, {}
    for key in flat:
        k = unit[key][1]
        keys = (k * jnp.exp2(total[key] - cum[key])).astype(bf16)
        kv[key] = []
        for h in range(n_heads):
            own = keys[:, groups[group_of[h]]]
            kv[key].append(_dot_tn(own if mask_of[h] is None else own * mask_of[h], vb[key][h]))
        whole[key] = [jnp.broadcast_to(jnp.exp2(total[key][:, lanes]), (HEAD, HEAD)).T for lanes in groups]
    state = [list(states) for states, _ in chains]
    outs, pending = {}, None
    for r in range(max(len(units) for _, units in chains)):
        live = [key for key in flat if key[1] == r]
        for key in live:
            queries = (unit[key][0] * jnp.exp2(cum[key])).astype(bf16)
            outs[key] = [_dot(queries[:, groups[group_of[h]]], state[key[0]][h].astype(bf16))
                         for h in range(n_heads)]
            state[key[0]] = [whole[key][group_of[h]] * state[key[0]][h] + kv[key][h] for h in range(n_heads)]
        for key in live:
            q, k, _, lvl, rev = unit[key]
            ones = (lvl[0:1, :] >= -1).astype(bf16)
            for g, lanes in enumerate(groups):
                fine = [sums_ref[index[key], i * CHUNK:(i + 1) * CHUNK, lanes] for i in range(N_FINE)]
                group_scores = _in_chunk_scores(q[:, lanes], k[:, lanes], cum[key][:, lanes], fine, lvl, ones, rev,
                                                masks)
                for sub, sc in enumerate(group_scores):
                    if pending is not None:
                        pkey, ph, psc = pending
                        outs[pkey][ph] = outs[pkey][ph] + _dot(psc, vb[pkey][ph])
                    pending = (key, g * heads_per_group + sub, sc)
            if upcoming:
                u, (g_split, table) = upcoming.pop(0)
                upcoming_ref[u] = _decay_sums(g_split, table)
    for u, (g_split, table) in upcoming:
        upcoming_ref[u] = _decay_sums(g_split, table)
    pkey, ph, psc = pending
    outs[pkey][ph] = outs[pkey][ph] + _dot(psc, vb[pkey][ph])
    return [[outs[(c, r)] for r in range(len(units))] for c, (_, units) in enumerate(chains)], state


def _gated_scans(q_s, k_refs, v_s, g_refs, s_scr, sums_scr, table_ref, lvl_ref, o_scr, seq, heads_per_group):
    n = seq // CHUNK
    n_seq = STEP_TOKENS // seq
    n_heads = PAIR * heads_per_group
    units = GATED_HEADS_PER_ITER // n_heads
    per_iter = min(n, units // 2)
    seqs_per_iter = min(n_seq, units // (2 * per_iter))
    assert n % per_iter == 0 and n_seq % seqs_per_iter == 0
    iters_per_seq = n // per_iter
    n_iters = (n_seq // seqs_per_iter) * iters_per_seq

    def layout(it):
        jj, i = it // iters_per_seq, it % iters_per_seq
        chains = []
        for js in range(seqs_per_iter):
            j = jj * seqs_per_iter + js
            for d in range(2):
                steps = [i * per_iter + r for r in range(per_iter)]
                chains.append((j, d, [_chunk_rows(j * seq, n - 1 - t if d else t) for t in steps]))
        return chains

    def sums_inputs(it):
        return [(g_refs[d][rw, :], table_ref[d]) for _, d, rws in layout(it) for rw in rws]

    for u, (g_split, table) in enumerate(sums_inputs(0)):
        sums_scr[0, u] = _decay_sums(g_split, table)

    def iteration(it, slot):
        chains = layout(it)
        args = [([s_scr[j, d, h] for h in range(n_heads)],
                 [(q_s[rw, :], k_refs[d][rw, :], v_s[rw, :], lvl_ref[d], bool(d)) for rw in rws])
                for j, d, rws in chains]
        upcoming = sums_inputs(jnp.minimum(it + 1, n_iters - 1))
        outs, new_states = _gated_chunks(args, sums_scr.at[slot], upcoming, sums_scr.at[1 - slot], heads_per_group)
        for c, (j, d, rws) in enumerate(chains):
            for h in range(n_heads):
                for r, rw in enumerate(rws):
                    o_scr[d, h, rw, :] = outs[c][r][h]
                s_scr[j, d, h] = new_states[c][h]

    assert n_iters % 2 == 0

    def body(it2, carry):
        iteration(2 * it2, 0)
        iteration(2 * it2 + 1, 1)
        return carry
    lax.fori_loop(0, n_iters // 2, body, 0)


def _even_kernel(*refs, seq, has_state, emit_state, final):
    it = iter(refs)
    x_ref, mod_ref, nw_ref = next(it), next(it), next(it)
    win_ref, wlow_ref, wo_ref = next(it), next(it), next(it)
    gkw_ref, gkb_ref, lb_ref, gnw_ref, fnw_ref = next(it), next(it), next(it), next(it), next(it)
    table_ref, lvl_ref = next(it), next(it)
    s0a_ref, s0b_ref = (next(it), next(it)) if has_state else (None, None)
    out_ref = next(it)
    sta_ref, stb_ref = (next(it), next(it)) if emit_state else (None, None)
    h_scr, low_scr, o_scr, gate_s, q_s, v_s, kf_s, kb_s, gf_s, gb_s, s_scr, sums_scr = it
    n_seq = STEP_TOKENS // seq
    n_tiles = STEP_TOKENS // ROW_TILE

    _modulated_norm(x_ref, mod_ref, nw_ref, h_scr)
    _clear_output(out_ref)

    def low_body(i, carry):
        rows = _row_tile(i)
        low_scr[rows, :] = _dot(h_scr[rows, :], wlow_ref[...]).astype(bf16)
        return carry
    lax.fori_loop(0, n_tiles, low_body, 0)

    width = PAIR * HEAD

    def run_scans(first_head, s0_ref, st_ref, h0, heads_per_group, k_refs):
        n_heads = PAIR * heads_per_group
        key_rows = HEAD // heads_per_group
        own_rows = [pl.ds((h % heads_per_group) * key_rows, key_rows) for h in range(n_heads)]
        for j in range(n_seq):
            for d in range(2):
                for h in range(n_heads):
                    if s0_ref is None or heads_per_group > 1:
                        s_scr[j, d, h] = jnp.zeros((HEAD, HEAD), f32)
                    if s0_ref is not None:
                        s_scr[j, d, h, own_rows[h], :] = s0_ref[j, d, h0 + h]
        _gated_scans(q_s, k_refs, v_s, (gf_s, gb_s), s_scr, sums_scr, table_ref, lvl_ref, o_scr, seq,
                     heads_per_group)
        if st_ref is not None:
            for j in range(n_seq):
                for d in range(2):
                    for h in range(n_heads):
                        st_ref[j, d, h0 + h] = s_scr[j, d, h, own_rows[h], :]
        _add_heads_output(first_head, n_heads, gate_s, o_scr, gnw_ref, wo_ref, out_ref)

    def hgrn_pair(pp, carry):
        lb = lb_ref[:, _pair_lanes(pp)]
        log_lb = jnp.log(lb)

        def proj(i, c):
            rows = _row_tile(i)
            query, value, forget_f, forget_b, gate = _project(h_scr[rows, :], win_ref, pp, HGRN_COLS)
            gate_s[rows, 0:width] = _silu(gate).astype(bf16)
            q_s[rows, :] = _silu(query)
            v_s[rows, 0:width] = value.astype(bf16)
            for a, k_s, g_s in ((forget_f, kf_s, gf_s), (forget_b, kb_s, gb_s)):
                z = log_lb - a
                u = jnp.exp(-jnp.abs(a))
                w = jnp.exp(-jnp.abs(z))
                r = 1.0 / (1.0 + u)
                _store_log2_split(g_s, rows, jnp.maximum(z, 0.0) + jnp.minimum(a, 0.0) + jnp.log((1.0 + w) * r))
                k_s[rows, :] = (1.0 - lb) * jnp.where(a >= 0.0, u * r, r)
            return c
        lax.fori_loop(0, n_tiles, proj, 0)
        run_scans(PAIR * pp, s0a_ref, sta_ref, PAIR * pp, 1, (kf_s, kb_s))
        return carry
    lax.fori_loop(0, N_HGRN // PAIR, hgrn_pair, 0)

    assert N_GLA == PAIR * GLA_HEADS_PER_GROUP

    def gla_proj(i, c):
        rows = _row_tile(i)
        h = h_scr[rows, :]
        query, key = [_dot(h, win_ref[:, c0:c0 + N_GLA * GLA_DK]) for c0 in GLA_QK_COLS]
        value, gate = [_dot(h, win_ref[:, c0:c0 + N_GLA * HEAD]) for c0 in GLA_COLS]
        gate_s[rows, :] = _silu(gate).astype(bf16)
        q_s[rows, :] = query * (GLA_DK ** -0.5)
        kf_s[rows, :] = key
        v_s[rows, :] = value.astype(bf16)
        low = low_scr[rows, :]
        for d, g_s in enumerate((gf_s, gb_s)):
            logits = _dot(low, gkw_ref[d]) + gkb_ref[d]
            log_gate = jnp.minimum(logits, 0.0) - jnp.log(1.0 + jnp.exp(-jnp.abs(logits)))
            _store_log2_split(g_s, rows, log_gate * (1.0 / GLA_GATE_NORM))
        return c
    lax.fori_loop(0, n_tiles, gla_proj, 0)
    run_scans(N_HGRN, s0b_ref, stb_ref, 0, GLA_HEADS_PER_GROUP, (kf_s, kf_s))

    _finish_output(x_ref, mod_ref, fnw_ref, out_ref, final)


def _odd_kernel(*refs, seq, has_state, emit_state, use_rope, final):
    it = iter(refs)
    lg_ref = next(it)
    x_ref, mod_ref, nw_ref = next(it), next(it), next(it)
    wr_ref, wo_ref, gnw_ref, fnw_ref = next(it), next(it), next(it), next(it)
    cos_ref, sin_ref = (next(it), next(it)) if use_rope else (None, None)
    s0_ref = next(it) if has_state else None
    out_ref = next(it)
    st_ref = next(it) if emit_state else None
    h_scr, o_scr, gate_s, q_s, kt_s, v_s, s_scr, dec_scr = it
    n_seq = STEP_TOKENS // seq
    n_tiles = STEP_TOKENS // ROW_TILE
    n_chunks = seq // CHUNK
    per_iter = min(n_chunks, RET_UNITS_PER_ITER // n_seq)
    assert n_chunks % per_iter == 0

    _modulated_norm(x_ref, mod_ref, nw_ref, h_scr)
    _clear_output(out_ref)
    t_idx = lax.broadcasted_iota(jnp.int32, (CHUNK, CHUNK), 0)
    s_idx = lax.broadcasted_iota(jnp.int32, (CHUNK, CHUNK), 1)
    row_f = lax.broadcasted_iota(jnp.int32, (CHUNK, HEAD), 0).astype(f32)
    col_f = lax.broadcasted_iota(jnp.int32, (8, CHUNK), 1).astype(f32)
    chunk_len = jnp.full((8, HEAD), CHUNK, f32)
    if use_rope:
        lane = lax.broadcasted_iota(jnp.int32, (ROW_TILE, HEAD), 1)
        first_quarter = (lane // (HEAD // 4)) % 2 == 0

    def rope(x, cos, sin_signed):
        xr = jnp.where(first_quarter, pltpu.roll(x, HEAD - HEAD // 4, axis=1), pltpu.roll(x, HEAD // 4, axis=1))
        return x * cos + xr * sin_signed

    width = PAIR * HEAD

    def pair_body(pp, carry):
        def proj(i, c):
            rows = _row_tile(i)
            q, k, value, gate = _project(h_scr[rows, :], wr_ref, pp, RET_COLS)
            gate_s[rows, :] = _silu(gate).astype(bf16)
            k = k * (HEAD ** -0.5)
            if use_rope:
                cos, sin_signed = cos_ref[rows, :], sin_ref[rows, :]
                heads = [slice(h * HEAD, (h + 1) * HEAD) for h in range(PAIR)]
                q = jnp.concatenate([rope(q[:, lanes], cos, sin_signed) for lanes in heads], axis=-1)
                k = jnp.concatenate([rope(k[:, lanes], cos, sin_signed) for lanes in heads], axis=-1)
            q_s[rows, :] = q.astype(bf16)
            kt_s[:, rows] = k.T
            v_s[rows, :] = value.astype(bf16)
            return c
        lax.fori_loop(0, n_tiles, proj, 0)
        for h in range(PAIR):
            scan_head(PAIR * pp + h, h)
        _add_heads_output(PAIR * pp, PAIR, gate_s, o_scr, gnw_ref, wo_ref, out_ref)
        return carry

    def scan_head(hh, h):
        lanes = slice(h * HEAD, (h + 1) * HEAD)
        lg_f = lg_ref[0, hh]
        lg_b = lg_ref[1, hh]
        for j in range(n_seq):
            for d in range(2):
                s_scr[j, d] = s0_ref[j, d, hh] if has_state else jnp.zeros((HEAD, HEAD), f32)

        dist = (t_idx - s_idx).astype(f32)
        dec_scr[0] = jnp.exp(lg_f * (row_f + 1.0))
        dec_scr[1] = jnp.exp(lg_b * (CHUNK - row_f))
        dec_scr[2] = (jnp.where(t_idx >= s_idx, jnp.exp(lg_f * jnp.maximum(dist, 0.0)), 0.0)
                      + jnp.where(s_idx >= t_idx, jnp.exp(lg_b * jnp.maximum(-dist, 0.0)), 0.0))
        dec_scr[3, 0:8, :] = jnp.exp(lg_f * (CHUNK - 1.0 - col_f))
        dec_scr[3, 8:16, :] = jnp.exp(lg_b * col_f)
        dec_scr[3, 16:24, :] = jnp.exp(lg_f * chunk_len)
        dec_scr[3, 24:32, :] = jnp.exp(lg_b * chunk_len)

        def body(i, c):
            units = [(j, r) for j in range(n_seq) for r in range(per_iter)]
            rows = {(j, r, d): _chunk_rows(j * seq, n_chunks - 1 - (i * per_iter + r) if d else i * per_iter + r)
                    for j, r in units for d in range(2)}
            scores = {u: _dot(q_s[rows[u + (0,)], lanes], kt_s[lanes, rows[u + (0,)]].astype(bf16)) for u in units}
            kv = {}
            for j, r in units:
                for d in range(2):
                    rw = rows[(j, r, d)]
                    keys = (kt_s[lanes, rw] * dec_scr[3, 8 * d:8 * d + 1, :]).astype(bf16)
                    kv[(j, r, d)] = _dot(keys, v_s[rw, lanes])
            state = {(j, d): s_scr[j, d] for j in range(n_seq) for d in range(2)}
            carried = {}
            for r in range(per_iter):
                for j in range(n_seq):
                    for d in range(2):
                        carried[(j, r, d)] = _dot(q_s[rows[(j, r, d)], lanes], state[(j, d)].astype(bf16))
                        state[(j, d)] = dec_scr[3, 16 + 8 * d:17 + 8 * d, :] * state[(j, d)] + kv[(j, r, d)]
            for j, r in units:
                rw = rows[(j, r, 0)]
                inside = _dot((scores[(j, r)] * dec_scr[2]).astype(bf16), v_s[rw, lanes])
                o_scr[0, h, rw, :] = inside + dec_scr[0] * carried[(j, r, 0)]
                o_scr[1, h, rows[(j, r, 1)], :] = dec_scr[1] * carried[(j, r, 1)]
            for (j, d), s in state.items():
                s_scr[j, d] = s
            return c
        lax.fori_loop(0, n_chunks // per_iter, body, 0)
        if emit_state:
            for j in range(n_seq):
                for d in range(2):
                    st_ref[j, d, hh] = s_scr[j, d]

    lax.fori_loop(0, N_RET // PAIR, pair_body, 0)

    _finish_output(x_ref, mod_ref, fnw_ref, out_ref, final)


def _const_spec(shape):
    zeros = (0,) * len(shape)
    return pl.BlockSpec(shape, lambda i: zeros, pipeline_mode=pl.Buffered(1))


def _step_spec(shape, per_step):
    zeros = (0,) * (len(shape) - 1)
    return pl.BlockSpec((per_step,) + tuple(shape[1:]), lambda i: (i,) + zeros)


def _mod_spec(d, per_sequence):
    if per_sequence:
        return pl.BlockSpec((1, 3, d), lambda i: (i + 1, 0, 0))
    return pl.BlockSpec((1, 3, d), lambda i: (0, 0, 0))


def _layer_call(body, x, mod, consts, states, state_shapes, per_sequence_mod, scratch, name, smem_inputs=()):
    n_seq, seq, d = x.shape
    per_step = STEP_TOKENS // seq
    assert per_step * seq == STEP_TOKENS and n_seq % per_step == 0 and seq % CHUNK == 0
    assert not per_sequence_mod or per_step == 1
    n_steps = n_seq // per_step
    xs = x.reshape(n_steps, STEP_TOKENS, d)
    inputs = list(smem_inputs) + [xs, mod] + list(consts) + list(states)
    in_specs = [pl.BlockSpec(memory_space=pltpu.SMEM)] * len(smem_inputs)
    in_specs += [_step_spec(xs.shape, 1), _mod_spec(d, per_sequence_mod)]
    in_specs += [_const_spec(a.shape) for a in consts]
    in_specs += [_step_spec(s.shape, per_step) for s in states]
    out_shape = [jax.ShapeDtypeStruct(xs.shape, f32)] + [jax.ShapeDtypeStruct(s, f32) for s in state_shapes]
    out_specs = [_step_spec(xs.shape, 1)] + [_step_spec(s, per_step) for s in state_shapes]
    outs = pl.pallas_call(
        body,
        grid=(n_steps,),
        in_specs=in_specs,
        out_specs=out_specs,
        out_shape=out_shape,
        scratch_shapes=scratch,
        compiler_params=pltpu.CompilerParams(dimension_semantics=("arbitrary",), vmem_limit_bytes=VMEM_LIMIT_BYTES),
        name=name,
    )(*inputs)
    return [outs[0].reshape(x.shape)] + list(outs[1:])


def _even_layer(x, mod, norm_w, w, final_w, tables, states, emit_state, per_sequence_mod, final):
    n_seq, seq, d = x.shape
    consts = [norm_w, w["win"], w["wlow"], w["wout"], w["gkw"], w["gkb"], w["lb"], w["gnw"],
              final_w] + list(tables)
    state_shapes = [(n_seq, 2, N_HGRN, HEAD, HEAD), (n_seq, 2, N_GLA, GLA_DK, HEAD)] if emit_state else []
    scan_heads = max(PAIR, N_GLA)
    scratch = [
        pltpu.VMEM((STEP_TOKENS, d), bf16),
        pltpu.VMEM((STEP_TOKENS, HEAD), bf16),
        pltpu.VMEM((2, scan_heads, STEP_TOKENS, HEAD), f32),
        pltpu.VMEM((STEP_TOKENS, scan_heads * HEAD), bf16),
        pltpu.VMEM((STEP_TOKENS, PAIR * HEAD), f32),
        pltpu.VMEM((STEP_TOKENS, scan_heads * HEAD), bf16),
        pltpu.VMEM((STEP_TOKENS, PAIR * HEAD), f32),
        pltpu.VMEM((STEP_TOKENS, PAIR * HEAD), f32),
        pltpu.VMEM((STEP_TOKENS, 2 * PAIR * HEAD), bf16),
        pltpu.VMEM((STEP_TOKENS, 2 * PAIR * HEAD), bf16),
        pltpu.VMEM((STEP_TOKENS // seq, 2, scan_heads, HEAD, HEAD), f32),
        pltpu.VMEM((2, GATED_UNITS_MAX, ROW_TOTAL + 8, PAIR * HEAD), f32),
    ]
    body = functools.partial(_even_kernel, seq=seq, has_state=states is not None, emit_state=emit_state, final=final)
    return _layer_call(body, x, mod, consts, states or (), state_shapes, per_sequence_mod, scratch,
                       "even_layer_seq%d" % seq)


def _odd_layer(x, mod, norm_w, w, final_w, log_decay, rope, state, emit_state, per_sequence_mod, final):
    n_seq, seq, d = x.shape
    consts = [norm_w, w["wr"], w["wout"], w["gnw"], final_w] + list(rope or ())
    state_shapes = [(n_seq, 2, N_RET, HEAD, HEAD)] if emit_state else []
    scratch = [
        pltpu.VMEM((STEP_TOKENS, d), bf16),
        pltpu.VMEM((2, PAIR, STEP_TOKENS, HEAD), f32),
        pltpu.VMEM((STEP_TOKENS, PAIR * HEAD), bf16),
        pltpu.VMEM((STEP_TOKENS, PAIR * HEAD), bf16),
        pltpu.VMEM((PAIR * HEAD, STEP_TOKENS), f32),
        pltpu.VMEM((STEP_TOKENS, PAIR * HEAD), bf16),
        pltpu.VMEM((STEP_TOKENS // seq, 2, HEAD, HEAD), f32),
        pltpu.VMEM((4, CHUNK, HEAD), f32),
    ]
    body = functools.partial(_odd_kernel, seq=seq, has_state=state is not None, emit_state=emit_state,
                             use_rope=rope is not None, final=final)
    return _layer_call(body, x, mod, consts, () if state is None else (state,), state_shapes, per_sequence_mod,
                       scratch, "odd_layer_seq%d" % seq, smem_inputs=(log_decay,))


def _even_weights(w_in, gk_w, gk_b, lb, gn_w, w_out):
    c0 = GLA_LOW_COLS[0]
    wlow = jnp.pad(w_in[:, c0:c0 + 2 * GLA_RANK], ((0, 0), (0, HEAD - 2 * GLA_RANK)))
    gkw = jnp.stack([jnp.pad(gk_w[d], ((d * GLA_RANK, HEAD - (d + 1) * GLA_RANK), (0, 0))) for d in range(2)])
    return {"win": w_in.astype(bf16), "wlow": wlow.astype(bf16), "gkw": gkw.astype(bf16),
            "gkb": gk_b.reshape(2, 1, -1), "lb": lb.reshape(1, -1), "gnw": gn_w.reshape(1, -1),
            "wout": w_out.astype(bf16)}


def _odd_weights(w_in, gn_w, w_out):
    return {"wr": w_in.astype(bf16), "gnw": gn_w.reshape(1, -1), "wout": w_out.astype(bf16)}


def _rope_tables(seq):
    rows = seq // GRID_W
    t_row = jnp.repeat(jnp.arange(rows), GRID_W).astype(f32)
    t_col = jnp.tile(jnp.arange(GRID_W), rows).astype(f32)
    half = HEAD // 2
    inv = ROPE_BASE ** (-jnp.arange(0, half, 2, dtype=f32) / half)
    ang_r = t_row[:, None] * inv
    ang_c = t_col[:, None] * inv
    ang = jnp.concatenate([ang_r, ang_r, ang_c, ang_c], axis=-1)
    sign = jnp.where((jnp.arange(HEAD) // (HEAD // 4)) % 2 == 0, -1.0, 1.0).astype(f32)
    return jnp.cos(ang), jnp.sin(ang) * sign


def kernel(x_prompt, x_sample, state_hgrn, state_gla, state_ret, c, c_ctx, norm_w, ada_w, ada_b, w_in_even, hgrn_lb, gla_gk_w, gla_gk_b, gn_even, w_out_even, w_in_odd, ret_decay, gn_odd, w_out_odd, final_norm_w):
    depth, d = norm_w.shape
    n_lat = x_sample.shape[0]
    n_cond = -(-(1 + n_lat) // 8) * 8
    cond = jnp.zeros((n_cond, d), f32).at[0].set(c_ctx).at[1:1 + n_lat].set(c)
    mod = _modulation(cond, ada_w, ada_b).reshape(depth, n_cond, 3, d)
    lbs = jnp.cumsum(jax.nn.softmax(hgrn_lb.astype(f32), axis=0), axis=0)
    final_w = final_norm_w.reshape(1, d)
    rope = _rope_tables(x_sample.shape[1])
    tables = _scan_tables()

    x_c, x_l = x_prompt, x_sample
    new_hgrn, new_gla, new_ret = [], [], []
    for l in range(depth):
        i = l // 2
        final = l == depth - 1
        nw = norm_w[l].reshape(1, d)
        if l % 2 == 0:
            w = _even_weights(w_in_even[i], gla_gk_w[i], gla_gk_b[i], lbs[i], gn_even[i], w_out_even[i])
            x_c, st_a, st_b = _even_layer(x_c, mod[l], nw, w, final_w, tables, None, True, False, final)
            (x_l,) = _even_layer(x_l, mod[l], nw, w, final_w, tables, (state_hgrn[:, i], state_gla[:, i]),
                                 False, True, final)
            new_hgrn.append(st_a)
            new_gla.append(st_b)
        else:
            w = _odd_weights(w_in_odd[i], gn_odd[i], w_out_odd[i])
            log_decay = jax.nn.log_sigmoid(ret_decay[i].astype(f32))
            x_c, st_c = _odd_layer(x_c, mod[l], nw, w, final_w, log_decay, None, None, True, False, final)
            (x_l,) = _odd_layer(x_l, mod[l], nw, w, final_w, log_decay, rope, state_ret[:, i], False, True, final)
            new_ret.append(st_c)
    def stacked(states):
        return states[0][:, None] if len(states) == 1 else jnp.stack(states, axis=1)
    return (x_c, x_l, stacked(new_hgrn), stacked(new_gla), stacked(new_ret))
```

```python
import functools

import numpy as np
import jax
import jax.numpy as jnp
from jax import lax
from jax.experimental import pallas as pl
from jax.experimental.pallas import tpu as pltpu

f32 = jnp.float32
bf16 = jnp.bfloat16
HIGHEST = lax.Precision.HIGHEST

EPS = 1e-6
LOG2E = 1.4426950408889634
HEAD = 128
N_HGRN = 4
N_GLA = 4
GLA_DK = 64
N_RET = 8
N_HEADS = 8
GLA_RANK = 16
GLA_GATE_NORM = 16.0
GRID_W = 64
ROPE_BASE = 10000.0

EVEN_SPLITS = (N_HGRN * HEAD,) * 5 + (N_GLA * GLA_DK,) * 2 + (N_GLA * HEAD,) * 2 + (GLA_RANK,) * 2
EVEN_STARTS = tuple(int(c) for c in np.cumsum((0,) + EVEN_SPLITS[:-1]))
HGRN_COLS = EVEN_STARTS[0:5]
GLA_QK_COLS = EVEN_STARTS[5:7]
GLA_COLS = EVEN_STARTS[7:9]
GLA_LOW_COLS = EVEN_STARTS[9:11]
RET_COLS = tuple(i * N_RET * HEAD for i in range(4))

CHUNK = 128
LEVELS = (1, 2, 4, 8, 16, 32, 64)
DIAG_LEVEL = len(LEVELS)
N_FINE = 3
BLK_CUM = N_FINE
ROW_TOTAL = (N_FINE + 1) * CHUNK
PAIR = 2
GATED_HEADS_PER_ITER = 8
GATED_UNITS_MAX = 4
GLA_HEADS_PER_GROUP = HEAD // GLA_DK
RET_UNITS_PER_ITER = 8
STEP_TOKENS = 1024
ROW_TILE = 512
MOD_COLS = 768
VMEM_LIMIT_BYTES = 61 * 1024 * 1024


def _dot(a, b, precision=None):
    return jnp.dot(a, b, precision=precision, preferred_element_type=f32)


def _dot_tn(a, b):
    return lax.dot_general(a, b, (((0,), (0,)), ((), ())), preferred_element_type=f32)


def _silu(x):
    return x * jax.nn.sigmoid(x)


def _row_tile(i):
    return pl.ds(pl.multiple_of(i * ROW_TILE, ROW_TILE), ROW_TILE)


def _chunk_rows(seq_start, c):
    return pl.ds(pl.multiple_of(seq_start + c * CHUNK, CHUNK), CHUNK)


def _mod_kernel(cond_ref, w_ref, b_ref, o_ref):
    o_ref[0] = _dot(_silu(cond_ref[...]), w_ref[0], HIGHEST) + b_ref[0]


def _modulation(cond, ada_w, ada_b):
    depth, d, d3 = ada_w.shape
    rows = cond.shape[0]
    return pl.pallas_call(
        _mod_kernel,
        grid=(depth, d3 // MOD_COLS),
        in_specs=[
            pl.BlockSpec((rows, d), lambda l, j: (0, 0)),
            pl.BlockSpec((1, d, MOD_COLS), lambda l, j: (l, 0, j)),
            pl.BlockSpec((1, 1, MOD_COLS), lambda l, j: (l, 0, j)),
        ],
        out_specs=pl.BlockSpec((1, rows, MOD_COLS), lambda l, j: (l, 0, j)),
        out_shape=jax.ShapeDtypeStruct((depth, rows, d3), f32),
        compiler_params=pltpu.CompilerParams(dimension_semantics=("arbitrary", "arbitrary")),
        name="modulation",
    )(cond, ada_w, ada_b.reshape(depth, 1, d3))


def _modulated_norm(x_ref, mod_ref, nw_ref, h_scr):
    def body(i, carry):
        rows = _row_tile(i)
        x = x_ref[0, rows, :]
        y = x * lax.rsqrt(jnp.mean(x * x, axis=-1, keepdims=True) + EPS) * nw_ref[...]
        h_scr[rows, :] = (y * (1.0 + mod_ref[0, 1:2, :]) + mod_ref[0, 0:1, :]).astype(bf16)
        return carry
    lax.fori_loop(0, STEP_TOKENS // ROW_TILE, body, 0)


def _clear_output(out_ref):
    def body(i, carry):
        out_ref[0, _row_tile(i), :] = jnp.zeros((ROW_TILE, out_ref.shape[-1]), f32)
        return carry
    lax.fori_loop(0, STEP_TOKENS // ROW_TILE, body, 0)


def _pair_lanes(pair):
    return pl.ds(pl.multiple_of(pair * (PAIR * HEAD), PAIR * HEAD), PAIR * HEAD)


def _project(h, w_ref, pair, first_cols):
    width = PAIR * HEAD
    return [_dot(h, w_ref[:, pl.ds(pl.multiple_of(c0 + pair * width, width), width)]) for c0 in first_cols]


def _add_heads_output(first_head, n_heads, gate_s, o_scr, gnw_ref, wo_ref, out_ref):
    width = n_heads * HEAD
    lanes = pl.ds(pl.multiple_of(first_head * HEAD, width), width)
    tiles = []
    for i in range(STEP_TOKENS // ROW_TILE):
        rows = pl.ds(i * ROW_TILE, ROW_TILE)
        parts = []
        for h in range(n_heads):
            o = o_scr[0, h, rows, :] + o_scr[1, h, rows, :]
            parts.append(o * lax.rsqrt(jnp.mean(o * o, axis=-1, keepdims=True) + EPS))
        y = jnp.concatenate(parts, axis=-1) * gnw_ref[:, lanes]
        tiles.append((rows, (y * gate_s[rows, 0:width].astype(f32)).astype(bf16)))
    products = [(rows, _dot(z, wo_ref[lanes, :])) for rows, z in tiles]
    for rows, product in products:
        out_ref[0, rows, :] += product


def _finish_output(x_ref, mod_ref, fnw_ref, out_ref, final):
    def body(i, carry):
        rows = _row_tile(i)
        xn = x_ref[0, rows, :] + mod_ref[0, 2:3, :] * out_ref[0, rows, :]
        if final:
            xn = xn * lax.rsqrt(jnp.mean(xn * xn, axis=-1, keepdims=True) + EPS) * fnw_ref[...]
        out_ref[0, rows, :] = xn
        return carry
    lax.fori_loop(0, STEP_TOKENS // ROW_TILE, body, 0)


def _scan_tables():
    t = np.arange(CHUNK)[:, None]
    j = np.arange(CHUNK)[None, :]
    fwd = []
    for m in LEVELS[:N_FINE]:
        mid = (t // (2 * m)) * (2 * m) + m
        right = t >= mid
        fwd.append(np.where(right, (j >= mid) & (j <= t), (j > t) & (j < mid)))
    fwd.append(j <= t)
    fwd.append(np.ones((8, CHUNK), bool))
    fwd = np.concatenate(fwd, axis=0).astype(np.float32)
    bwd = fwd.copy()
    n_sym = ROW_TOTAL // CHUNK
    bwd[:ROW_TOTAL] = fwd[:ROW_TOTAL].reshape(n_sym, CHUNK, CHUNK)[:, ::-1, ::-1].reshape(ROW_TOTAL, CHUNK)
    table = np.stack([np.tile(fwd, (1, 2)), np.tile(bwd, (1, 2))])
    x = t ^ j
    lvl = np.zeros((CHUNK, CHUNK), np.int32)
    for b in range(1, len(LEVELS)):
        lvl += (x >= (1 << b)).astype(np.int32)
    lvl_f = np.where(t > j, lvl, np.where(t == j, DIAG_LEVEL, -1)).astype(np.int32)
    return jnp.asarray(table, bf16), jnp.asarray(np.stack([lvl_f, lvl_f.T]))


def _in_chunk_scores(q, k, cum, fine, lvl, ones, rev, key_masks):
    qb = q.astype(bf16)
    q_heads = [qb if mask is None else qb * mask for mask in key_masks]
    kt = k.astype(bf16).T
    diag_keys = kt * ones
    scores = [jnp.where(lvl == DIAG_LEVEL, _dot(qh, diag_keys), 0.0) for qh in q_heads]
    for i in range(N_FINE):
        e = jnp.exp2(fine[i]).astype(bf16)
        keys = kt * e.T
        scores = [jnp.where(lvl == i, _dot(qh * e, keys), sc) for qh, sc in zip(q_heads, scores)]
    for i in range(N_FINE, len(LEVELS)):
        m = LEVELS[i]
        blocks = []
        for p0 in range(0, CHUNK, 2 * m):
            left, right = slice(p0, p0 + m), slice(p0 + m, p0 + 2 * m)
            q_side, k_side = (left, right) if rev else (right, left)
            mid_row = p0 + m if rev else p0 + m - 1
            blocks.append((q_side, k_side, cum[mid_row:mid_row + 1, :]))
        q_decay = [jnp.exp2(cum[qs] - mid).astype(bf16) for qs, _, mid in blocks]
        k_decay = []
        for _, ks, mid in blocks:
            decay = jnp.exp2(mid - cum[ks]).astype(bf16)
            zero = jnp.zeros((m, HEAD), bf16)
            k_decay += [zero, decay] if rev else [decay, zero]
        keys = kt * jnp.concatenate(k_decay, axis=0).T
        for h, qh in enumerate(q_heads):
            s = _dot(jnp.concatenate([qh[qs] * e for (qs, _, _), e in zip(blocks, q_decay)], axis=0), keys)
            rows = []
            for b, (qs, ks, _) in enumerate(blocks):
                updated = jnp.where(lvl[qs, :] == i, s[b * m:(b + 1) * m, :], scores[h][qs, :])
                rows += [updated, scores[h][ks, :]] if rev else [scores[h][ks, :], updated]
            scores[h] = jnp.concatenate(rows, axis=0)
    return [sc.astype(bf16) for sc in scores]


def _store_log2_split(g_ref, rows, g):
    width = g.shape[-1]
    x = g * LOG2E
    hi = x.astype(bf16)
    g_ref[rows, 0:width] = hi
    g_ref[rows, width:2 * width] = (x - hi.astype(f32)).astype(bf16)


def _decay_sums(g_split, table):
    width = g_split.shape[-1] // 2
    return _dot(table, jnp.concatenate([g_split[:, :width], g_split[:, width:]], axis=0))


def _key_masks(heads_per_group):
    if heads_per_group == 1:
        return [None]
    lane = lax.broadcasted_iota(jnp.int32, (1, HEAD), 1)
    return [(lane // (HEAD // heads_per_group) == sub).astype(bf16) for sub in range(heads_per_group)]


def _gated_chunks(chains, sums_ref, upcoming, upcoming_ref, heads_per_group):
    masks = _key_masks(heads_per_group)
    groups = [slice(g * HEAD, (g + 1) * HEAD) for g in range(PAIR)]
    n_heads = PAIR * heads_per_group
    group_of = [h // heads_per_group for h in range(n_heads)]
    mask_of = [masks[h % heads_per_group] for h in range(n_heads)]
    flat = [(c, r) for c, (_, units) in enumerate(chains) for r in range(len(units))]
    unit = {(c, r): chains[c][1][r] for c, r in flat}
    index = {key: u for u, key in enumerate(flat)}
    cum = {key: sums_ref[index[key], BLK_CUM * CHUNK:(BLK_CUM + 1) * CHUNK, :] for key in flat}
    total = {key: sums_ref[index[key], ROW_TOTAL:ROW_TOTAL + 1, :] for key in flat}
    vb = {key: [unit[key][2][:, h * HEAD:(h + 1) * HEAD] for h in range(n_heads)] for key in flat}
    upcoming = list(enumerate(upcoming))
    kv, whole = {}, {}
    for key in flat:
        k = unit[key][1]
        keys = (k * jnp.exp2(total[key] - cum[key])).astype(bf16)
        kv[key] = []
        for h in range(n_heads):
            own = keys[:, groups[group_of[h]]]
            kv[key].append(_dot_tn(own if mask_of[h] is None else own * mask_of[h], vb[key][h]))
        whole[key] = [jnp.broadcast_to(jnp.exp2(total[key][:, lanes]), (HEAD, HEAD)).T for lanes in groups]
    state = [list(states) for states, _ in chains]
    outs, pending = {}, None
    for r in range(max(len(units) for _, units in chains)):
        live = [key for key in flat if key[1] == r]
        for key in live:
            queries = (unit[key][0] * jnp.exp2(cum[key])).astype(bf16)
            outs[key] = [_dot(queries[:, groups[group_of[h]]], state[key[0]][h].astype(bf16))
                         for h in range(n_heads)]
            state[key[0]] = [whole[key][group_of[h]] * state[key[0]][h] + kv[key][h] for h in range(n_heads)]
        for key in live:
            q, k, _, lvl, rev = unit[key]
            ones = (lvl[0:1, :] >= -1).astype(bf16)
            for g, lanes in enumerate(groups):
                fine = [sums_ref[index[key], i * CHUNK:(i + 1) * CHUNK, lanes] for i in range(N_FINE)]
                group_scores = _in_chunk_scores(q[:, lanes], k[:, lanes], cum[key][:, lanes], fine, lvl, ones, rev,
                                                masks)
                for sub, sc in enumerate(group_scores):
                    if pending is not None:
                        pkey, ph, psc = pending
                        outs[pkey][ph] = outs[pkey][ph] + _dot(psc, vb[pkey][ph])
                    pending = (key, g * heads_per_group + sub, sc)
            if upcoming:
                u, (g_split, table) = upcoming.pop(0)
                upcoming_ref[u] = _decay_sums(g_split, table)
    for u, (g_split, table) in upcoming:
        upcoming_ref[u] = _decay_sums(g_split, table)
    pkey, ph, psc = pending
    outs[pkey][ph] = outs[pkey][ph] + _dot(psc, vb[pkey][ph])
    return [[outs[(c, r)] for r in range(len(units))] for c, (_, units) in enumerate(chains)], state


def _gated_scans(q_s, k_refs, v_s, g_refs, s_scr, sums_scr, table_ref, lvl_ref, o_scr, seq, heads_per_group):
    n = seq // CHUNK
    n_seq = STEP_TOKENS // seq
    n_heads = PAIR * heads_per_group
    units = GATED_HEADS_PER_ITER // n_heads
    per_iter = min(n, units // 2)
    seqs_per_iter = min(n_seq, units // (2 * per_iter))
    assert n % per_iter == 0 and n_seq % seqs_per_iter == 0
    iters_per_seq = n // per_iter
    n_iters = (n_seq // seqs_per_iter) * iters_per_seq

    def layout(it):
        jj, i = it // iters_per_seq, it % iters_per_seq
        chains = []
        for js in range(seqs_per_iter):
            j = jj * seqs_per_iter + js
            for d in range(2):
                steps = [i * per_iter + r for r in range(per_iter)]
                chains.append((j, d, [_chunk_rows(j * seq, n - 1 - t if d else t) for t in steps]))
        return chains

    def sums_inputs(it):
        return [(g_refs[d][rw, :], table_ref[d]) for _, d, rws in layout(it) for rw in rws]

    for u, (g_split, table) in enumerate(sums_inputs(0)):
        sums_scr[0, u] = _decay_sums(g_split, table)

    def iteration(it, slot):
        chains = layout(it)
        args = [([s_scr[j, d, h] for h in range(n_heads)],
                 [(q_s[rw, :], k_refs[d][rw, :], v_s[rw, :], lvl_ref[d], bool(d)) for rw in rws])
                for j, d, rws in chains]
        upcoming = sums_inputs(jnp.minimum(it + 1, n_iters - 1))
        outs, new_states = _gated_chunks(args, sums_scr.at[slot], upcoming, sums_scr.at[1 - slot], heads_per_group)
        for c, (j, d, rws) in enumerate(chains):
            for h in range(n_heads):
                for r, rw in enumerate(rws):
                    o_scr[d, h, rw, :] = outs[c][r][h]
                s_scr[j, d, h] = new_states[c][h]

    assert n_iters % 2 == 0

    def body(it2, carry):
        iteration(2 * it2, 0)
        iteration(2 * it2 + 1, 1)
        return carry
    lax.fori_loop(0, n_iters // 2, body, 0)


def _even_kernel(*refs, seq, has_state, emit_state, final):
    it = iter(refs)
    x_ref, mod_ref, nw_ref = next(it), next(it), next(it)
    win_ref, wlow_ref, wo_ref = next(it), next(it), next(it)
    gkw_ref, gkb_ref, lb_ref, gnw_ref, fnw_ref = next(it), next(it), next(it), next(it), next(it)
    table_ref, lvl_ref = next(it), next(it)
    s0a_ref, s0b_ref = (next(it), next(it)) if has_state else (None, None)
    out_ref = next(it)
    sta_ref, stb_ref = (next(it), next(it)) if emit_state else (None, None)
    h_scr, low_scr, o_scr, gate_s, q_s, v_s, kf_s, kb_s, gf_s, gb_s, s_scr, sums_scr = it
    n_seq = STEP_TOKENS // seq
    n_tiles = STEP_TOKENS // ROW_TILE

    _modulated_norm(x_ref, mod_ref, nw_ref, h_scr)
    _clear_output(out_ref)

    def low_body(i, carry):
        rows = _row_tile(i)
        low_scr[rows, :] = _dot(h_scr[rows, :], wlow_ref[...]).astype(bf16)
        return carry
    lax.fori_loop(0, n_tiles, low_body, 0)

    width = PAIR * HEAD

    def run_scans(first_head, s0_ref, st_ref, h0, heads_per_group, k_refs):
        n_heads = PAIR * heads_per_group
        key_rows = HEAD // heads_per_group
        own_rows = [pl.ds((h % heads_per_group) * key_rows, key_rows) for h in range(n_heads)]
        for j in range(n_seq):
            for d in range(2):
                for h in range(n_heads):
                    if s0_ref is None or heads_per_group > 1:
                        s_scr[j, d, h] = jnp.zeros((HEAD, HEAD), f32)
                    if s0_ref is not None:
                        s_scr[j, d, h, own_rows[h], :] = s0_ref[j, d, h0 + h]
        _gated_scans(q_s, k_refs, v_s, (gf_s, gb_s), s_scr, sums_scr, table_ref, lvl_ref, o_scr, seq,
                     heads_per_group)
        if st_ref is not None:
            for j in range(n_seq):
                for d in range(2):
                    for h in range(n_heads):
                        st_ref[j, d, h0 + h] = s_scr[j, d, h, own_rows[h], :]
        _add_heads_output(first_head, n_heads, gate_s, o_scr, gnw_ref, wo_ref, out_ref)

    def hgrn_pair(pp, carry):
        lb = lb_ref[:, _pair_lanes(pp)]
        log_lb = jnp.log(lb)

        tiles = [pl.ds(i * ROW_TILE, ROW_TILE) for i in range(n_tiles)]
        projected = [_project(h_scr[rows, :], win_ref, pp, HGRN_COLS) for rows in tiles]
        for rows, (query, value, forget_f, forget_b, gate) in zip(tiles, projected):
            gate_s[rows, 0:width] = _silu(gate).astype(bf16)
            q_s[rows, :] = _silu(query)
            v_s[rows, 0:width] = value.astype(bf16)
            for a, k_s, g_s in ((forget_f, kf_s, gf_s), (forget_b, kb_s, gb_s)):
                z = log_lb - a
                u = jnp.exp(-jnp.abs(a))
                w = jnp.exp(-jnp.abs(z))
                r = 1.0 / (1.0 + u)
                _store_log2_split(g_s, rows, jnp.maximum(z, 0.0) + jnp.minimum(a, 0.0) + jnp.log((1.0 + w) * r))
                k_s[rows, :] = (1.0 - lb) * jnp.where(a >= 0.0, u * r, r)
        run_scans(PAIR * pp, s0a_ref, sta_ref, PAIR * pp, 1, (kf_s, kb_s))
        return carry
    lax.fori_loop(0, N_HGRN // PAIR, hgrn_pair, 0)

    assert N_GLA == PAIR * GLA_HEADS_PER_GROUP

    def gla_project(rows):
        h = h_scr[rows, :]
        narrow = [_dot(h, win_ref[:, c0:c0 + N_GLA * GLA_DK]) for c0 in GLA_QK_COLS]
        wide = [_dot(h, win_ref[:, c0:c0 + N_GLA * HEAD]) for c0 in GLA_COLS]
        return narrow + wide

    tiles = [pl.ds(i * ROW_TILE, ROW_TILE) for i in range(n_tiles)]
    projected = [gla_project(rows) for rows in tiles]
    for rows, (query, key, value, gate) in zip(tiles, projected):
        gate_s[rows, :] = _silu(gate).astype(bf16)
        q_s[rows, :] = query * (GLA_DK ** -0.5)
        kf_s[rows, :] = key
        v_s[rows, :] = value.astype(bf16)
        low = low_scr[rows, :]
        for d, g_s in enumerate((gf_s, gb_s)):
            logits = _dot(low, gkw_ref[d]) + gkb_ref[d]
            log_gate = jnp.minimum(logits, 0.0) - jnp.log(1.0 + jnp.exp(-jnp.abs(logits)))
            _store_log2_split(g_s, rows, log_gate * (1.0 / GLA_GATE_NORM))
    run_scans(N_HGRN, s0b_ref, stb_ref, 0, GLA_HEADS_PER_GROUP, (kf_s, kf_s))

    _finish_output(x_ref, mod_ref, fnw_ref, out_ref, final)


def _odd_kernel(*refs, seq, has_state, emit_state, use_rope, final):
    it = iter(refs)
    lg_ref = next(it)
    x_ref, mod_ref, nw_ref = next(it), next(it), next(it)
    wr_ref, wo_ref, gnw_ref, fnw_ref = next(it), next(it), next(it), next(it)
    cos_ref, sin_ref = (next(it), next(it)) if use_rope else (None, None)
    s0_ref = next(it) if has_state else None
    out_ref = next(it)
    st_ref = next(it) if emit_state else None
    h_scr, o_scr, gate_s, q_s, kt_s, v_s, s_scr, dec_scr = it
    n_seq = STEP_TOKENS // seq
    n_tiles = STEP_TOKENS // ROW_TILE
    n_chunks = seq // CHUNK
    per_iter = min(n_chunks, RET_UNITS_PER_ITER // n_seq)
    assert n_chunks % per_iter == 0

    _modulated_norm(x_ref, mod_ref, nw_ref, h_scr)
    _clear_output(out_ref)
    t_idx = lax.broadcasted_iota(jnp.int32, (CHUNK, CHUNK), 0)
    s_idx = lax.broadcasted_iota(jnp.int32, (CHUNK, CHUNK), 1)
    row_f = lax.broadcasted_iota(jnp.int32, (CHUNK, HEAD), 0).astype(f32)
    col_f = lax.broadcasted_iota(jnp.int32, (8, CHUNK), 1).astype(f32)
    chunk_len = jnp.full((8, HEAD), CHUNK, f32)
    if use_rope:
        lane = lax.broadcasted_iota(jnp.int32, (ROW_TILE, HEAD), 1)
        first_quarter = (lane // (HEAD // 4)) % 2 == 0

    def rope(x, cos, sin_signed):
        xr = jnp.where(first_quarter, pltpu.roll(x, HEAD - HEAD // 4, axis=1), pltpu.roll(x, HEAD // 4, axis=1))
        return x * cos + xr * sin_signed

    width = PAIR * HEAD

    def pair_body(pp, carry):
        def proj(i, c):
            rows = _row_tile(i)
            q, k, value, gate = _project(h_scr[rows, :], wr_ref, pp, RET_COLS)
            gate_s[rows, :] = _silu(gate).astype(bf16)
            k = k * (HEAD ** -0.5)
            if use_rope:
                cos, sin_signed = cos_ref[rows, :], sin_ref[rows, :]
                heads = [slice(h * HEAD, (h + 1) * HEAD) for h in range(PAIR)]
                q = jnp.concatenate([rope(q[:, lanes], cos, sin_signed) for lanes in heads], axis=-1)
                k = jnp.concatenate([rope(k[:, lanes], cos, sin_signed) for lanes in heads], axis=-1)
            q_s[rows, :] = q.astype(bf16)
            kt_s[:, rows] = k.T
            v_s[rows, :] = value.astype(bf16)
            return c
        lax.fori_loop(0, n_tiles, proj, 0)
        for h in range(PAIR):
            scan_head(PAIR * pp + h, h)
        _add_heads_output(PAIR * pp, PAIR, gate_s, o_scr, gnw_ref, wo_ref, out_ref)
        return carry

    def scan_head(hh, h):
        lanes = slice(h * HEAD, (h + 1) * HEAD)
        lg_f = lg_ref[0, hh]
        lg_b = lg_ref[1, hh]
        for j in range(n_seq):
            for d in range(2):
                s_scr[j, d] = s0_ref[j, d, hh] if has_state else jnp.zeros((HEAD, HEAD), f32)

        dist = (t_idx - s_idx).astype(f32)
        dec_scr[0] = jnp.exp(lg_f * (row_f + 1.0))
        dec_scr[1] = jnp.exp(lg_b * (CHUNK - row_f))
        dec_scr[2] = (jnp.where(t_idx >= s_idx, jnp.exp(lg_f * jnp.maximum(dist, 0.0)), 0.0)
                      + jnp.where(s_idx >= t_idx, jnp.exp(lg_b * jnp.maximum(-dist, 0.0)), 0.0))
        dec_scr[3, 0:8, :] = jnp.exp(lg_f * (CHUNK - 1.0 - col_f))
        dec_scr[3, 8:16, :] = jnp.exp(lg_b * col_f)
        dec_scr[3, 16:24, :] = jnp.exp(lg_f * chunk_len)
        dec_scr[3, 24:32, :] = jnp.exp(lg_b * chunk_len)

        def body(i, c):
            units = [(j, r) for j in range(n_seq) for r in range(per_iter)]
            rows = {(j, r, d): _chunk_rows(j * seq, n_chunks - 1 - (i * per_iter + r) if d else i * per_iter + r)
                    for j, r in units for d in range(2)}
            scores = {u: _dot(q_s[rows[u + (0,)], lanes], kt_s[lanes, rows[u + (0,)]].astype(bf16)) for u in units}
            kv = {}
            for j, r in units:
                for d in range(2):
                    rw = rows[(j, r, d)]
                    keys = (kt_s[lanes, rw] * dec_scr[3, 8 * d:8 * d + 1, :]).astype(bf16)
                    kv[(j, r, d)] = _dot(keys, v_s[rw, lanes])
            state = {(j, d): s_scr[j, d] for j in range(n_seq) for d in range(2)}
            carried = {}
            for r in range(per_iter):
                for j in range(n_seq):
                    for d in range(2):
                        carried[(j, r, d)] = _dot(q_s[rows[(j, r, d)], lanes], state[(j, d)].astype(bf16))
                        state[(j, d)] = dec_scr[3, 16 + 8 * d:17 + 8 * d, :] * state[(j, d)] + kv[(j, r, d)]
            for j, r in units:
                rw = rows[(j, r, 0)]
                inside = _dot((scores[(j, r)] * dec_scr[2]).astype(bf16), v_s[rw, lanes])
                o_scr[0, h, rw, :] = inside + dec_scr[0] * carried[(j, r, 0)]
                o_scr[1, h, rows[(j, r, 1)], :] = dec_scr[1] * carried[(j, r, 1)]
            for (j, d), s in state.items():
                s_scr[j, d] = s
            return c
        lax.fori_loop(0, n_chunks // per_iter, body, 0)
        if emit_state:
            for j in range(n_seq):
                for d in range(2):
                    st_ref[j, d, hh] = s_scr[j, d]

    lax.fori_loop(0, N_RET // PAIR, pair_body, 0)

    _finish_output(x_ref, mod_ref, fnw_ref, out_ref, final)


def _const_spec(shape):
    zeros = (0,) * len(shape)
    return pl.BlockSpec(shape, lambda i: zeros, pipeline_mode=pl.Buffered(1))


def _step_spec(shape, per_step):
    zeros = (0,) * (len(shape) - 1)
    return pl.BlockSpec((per_step,) + tuple(shape[1:]), lambda i: (i,) + zeros)


def _mod_spec(d, per_sequence):
    if per_sequence:
        return pl.BlockSpec((1, 3, d), lambda i: (i + 1, 0, 0))
    return pl.BlockSpec((1, 3, d), lambda i: (0, 0, 0))


def _layer_call(body, x, mod, consts, states, state_shapes, per_sequence_mod, scratch, name, smem_inputs=()):
    n_seq, seq, d = x.shape
    per_step = STEP_TOKENS // seq
    assert per_step * seq == STEP_TOKENS and n_seq % per_step == 0 and seq % CHUNK == 0
    assert not per_sequence_mod or per_step == 1
    n_steps = n_seq // per_step
    xs = x.reshape(n_steps, STEP_TOKENS, d)
    inputs = list(smem_inputs) + [xs, mod] + list(consts) + list(states)
    in_specs = [pl.BlockSpec(memory_space=pltpu.SMEM)] * len(smem_inputs)
    in_specs += [_step_spec(xs.shape, 1), _mod_spec(d, per_sequence_mod)]
    in_specs += [_const_spec(a.shape) for a in consts]
    in_specs += [_step_spec(s.shape, per_step) for s in states]
    out_shape = [jax.ShapeDtypeStruct(xs.shape, f32)] + [jax.ShapeDtypeStruct(s, f32) for s in state_shapes]
    out_specs = [_step_spec(xs.shape, 1)] + [_step_spec(s, per_step) for s in state_shapes]
    outs = pl.pallas_call(
        body,
        grid=(n_steps,),
        in_specs=in_specs,
        out_specs=out_specs,
        out_shape=out_shape,
        scratch_shapes=scratch,
        compiler_params=pltpu.CompilerParams(dimension_semantics=("arbitrary",), vmem_limit_bytes=VMEM_LIMIT_BYTES),
        name=name,
    )(*inputs)
    return [outs[0].reshape(x.shape)] + list(outs[1:])


def _even_layer(x, mod, norm_w, w, final_w, tables, states, emit_state, per_sequence_mod, final):
    n_seq, seq, d = x.shape
    consts = [norm_w, w["win"], w["wlow"], w["wout"], w["gkw"], w["gkb"], w["lb"], w["gnw"],
              final_w] + list(tables)
    state_shapes = [(n_seq, 2, N_HGRN, HEAD, HEAD), (n_seq, 2, N_GLA, GLA_DK, HEAD)] if emit_state else []
    scan_heads = max(PAIR, N_GLA)
    scratch = [
        pltpu.VMEM((STEP_TOKENS, d), bf16),
        pltpu.VMEM((STEP_TOKENS, HEAD), bf16),
        pltpu.VMEM((2, scan_heads, STEP_TOKENS, HEAD), f32),
        pltpu.VMEM((STEP_TOKENS, scan_heads * HEAD), bf16),
        pltpu.VMEM((STEP_TOKENS, PAIR * HEAD), f32),
        pltpu.VMEM((STEP_TOKENS, scan_heads * HEAD), bf16),
        pltpu.VMEM((STEP_TOKENS, PAIR * HEAD), f32),
        pltpu.VMEM((STEP_TOKENS, PAIR * HEAD), f32),
        pltpu.VMEM((STEP_TOKENS, 2 * PAIR * HEAD), bf16),
        pltpu.VMEM((STEP_TOKENS, 2 * PAIR * HEAD), bf16),
        pltpu.VMEM((STEP_TOKENS // seq, 2, scan_heads, HEAD, HEAD), f32),
        pltpu.VMEM((2, GATED_UNITS_MAX, ROW_TOTAL + 8, PAIR * HEAD), f32),
    ]
    body = functools.partial(_even_kernel, seq=seq, has_state=states is not None, emit_state=emit_state, final=final)
    return _layer_call(body, x, mod, consts, states or (), state_shapes, per_sequence_mod, scratch,
                       "even_layer_seq%d" % seq)


def _odd_layer(x, mod, norm_w, w, final_w, log_decay, rope, state, emit_state, per_sequence_mod, final):
    n_seq, seq, d = x.shape
    consts = [norm_w, w["wr"], w["wout"], w["gnw"], final_w] + list(rope or ())
    state_shapes = [(n_seq, 2, N_RET, HEAD, HEAD)] if emit_state else []
    scratch = [
        pltpu.VMEM((STEP_TOKENS, d), bf16),
        pltpu.VMEM((2, PAIR, STEP_TOKENS, HEAD), f32),
        pltpu.VMEM((STEP_TOKENS, PAIR * HEAD), bf16),
        pltpu.VMEM((STEP_TOKENS, PAIR * HEAD), bf16),
        pltpu.VMEM((PAIR * HEAD, STEP_TOKENS), f32),
        pltpu.VMEM((STEP_TOKENS, PAIR * HEAD), bf16),
        pltpu.VMEM((STEP_TOKENS // seq, 2, HEAD, HEAD), f32),
        pltpu.VMEM((4, CHUNK, HEAD), f32),
    ]
    body = functools.partial(_odd_kernel, seq=seq, has_state=state is not None, emit_state=emit_state,
                             use_rope=rope is not None, final=final)
    return _layer_call(body, x, mod, consts, () if state is None else (state,), state_shapes, per_sequence_mod,
                       scratch, "odd_layer_seq%d" % seq, smem_inputs=(log_decay,))


def _even_weights(w_in, gk_w, gk_b, lb, gn_w, w_out):
    c0 = GLA_LOW_COLS[0]
    wlow = jnp.pad(w_in[:, c0:c0 + 2 * GLA_RANK], ((0, 0), (0, HEAD - 2 * GLA_RANK)))
    gkw = jnp.stack([jnp.pad(gk_w[d], ((d * GLA_RANK, HEAD - (d + 1) * GLA_RANK), (0, 0))) for d in range(2)])
    win = jnp.pad(w_in, ((0, 0), (0, -w_in.shape[1] % HEAD))).astype(bf16)
    return {"win": win, "wlow": wlow.astype(bf16), "gkw": gkw.astype(bf16),
            "gkb": gk_b.reshape(2, 1, -1), "lb": lb.reshape(1, -1), "gnw": gn_w.reshape(1, -1),
            "wout": w_out.astype(bf16)}


def _odd_weights(w_in, gn_w, w_out):
    return {"wr": w_in.astype(bf16), "gnw": gn_w.reshape(1, -1), "wout": w_out.astype(bf16)}


def _rope_tables(seq):
    rows = seq // GRID_W
    t_row = jnp.repeat(jnp.arange(rows), GRID_W).astype(f32)
    t_col = jnp.tile(jnp.arange(GRID_W), rows).astype(f32)
    half = HEAD // 2
    inv = ROPE_BASE ** (-jnp.arange(0, half, 2, dtype=f32) / half)
    ang_r = t_row[:, None] * inv
    ang_c = t_col[:, None] * inv
    ang = jnp.concatenate([ang_r, ang_r, ang_c, ang_c], axis=-1)
    sign = jnp.where((jnp.arange(HEAD) // (HEAD // 4)) % 2 == 0, -1.0, 1.0).astype(f32)
    return jnp.cos(ang), jnp.sin(ang) * sign


def kernel(x_prompt, x_sample, state_hgrn, state_gla, state_ret, c, c_ctx, norm_w, ada_w, ada_b, w_in_even, hgrn_lb, gla_gk_w, gla_gk_b, gn_even, w_out_even, w_in_odd, ret_decay, gn_odd, w_out_odd, final_norm_w):
    depth, d = norm_w.shape
    n_lat = x_sample.shape[0]
    n_cond = -(-(1 + n_lat) // 8) * 8
    cond = jnp.zeros((n_cond, d), f32).at[0].set(c_ctx).at[1:1 + n_lat].set(c)
    mod = _modulation(cond, ada_w, ada_b).reshape(depth, n_cond, 3, d)
    lbs = jnp.cumsum(jax.nn.softmax(hgrn_lb.astype(f32), axis=0), axis=0)
    final_w = final_norm_w.reshape(1, d)
    rope = _rope_tables(x_sample.shape[1])
    tables = _scan_tables()

    x_c, x_l = x_prompt, x_sample
    new_hgrn, new_gla, new_ret = [], [], []
    for l in range(depth):
        i = l // 2
        final = l == depth - 1
        nw = norm_w[l].reshape(1, d)
        if l % 2 == 0:
            w = _even_weights(w_in_even[i], gla_gk_w[i], gla_gk_b[i], lbs[i], gn_even[i], w_out_even[i])
            x_c, st_a, st_b = _even_layer(x_c, mod[l], nw, w, final_w, tables, None, True, False, final)
            (x_l,) = _even_layer(x_l, mod[l], nw, w, final_w, tables, (state_hgrn[:, i], state_gla[:, i]),
                                 False, True, final)
            new_hgrn.append(st_a)
            new_gla.append(st_b)
        else:
            w = _odd_weights(w_in_odd[i], gn_odd[i], w_out_odd[i])
            log_decay = jax.nn.log_sigmoid(ret_decay[i].astype(f32))
            x_c, st_c = _odd_layer(x_c, mod[l], nw, w, final_w, log_decay, None, None, True, False, final)
            (x_l,) = _odd_layer(x_l, mod[l], nw, w, final_w, log_decay, rope, state_ret[:, i], False, True, final)
            new_ret.append(st_c)
    def stacked(states):
        return states[0][:, None] if len(states) == 1 else jnp.stack(states, axis=1)
    return (x_c, x_l, stacked(new_hgrn), stacked(new_gla), stacked(new_ret))
```

```python
import functools
from typing import NamedTuple

import numpy as np
import jax
import jax.numpy as jnp
from jax import lax
from jax.experimental import pallas as pl
from jax.experimental.pallas import tpu as pltpu

f32 = jnp.float32
bf16 = jnp.bfloat16
HIGHEST = lax.Precision.HIGHEST

EPS = 1e-6
LOG2E = 1.4426950408889634
HEAD = 128
N_HGRN = 4
N_GLA = 4
GLA_DK = 64
N_RET = 8
N_HEADS = 8
GLA_RANK = 16
GLA_GATE_NORM = 16.0
GRID_W = 64
ROPE_BASE = 10000.0

EVEN_SPLITS = (N_HGRN * HEAD,) * 5 + (N_GLA * GLA_DK,) * 2 + (N_GLA * HEAD,) * 2 + (GLA_RANK,) * 2
EVEN_STARTS = tuple(int(c) for c in np.cumsum((0,) + EVEN_SPLITS[:-1]))
HGRN_COLS = EVEN_STARTS[0:5]
GLA_QK_COLS = EVEN_STARTS[5:7]
GLA_COLS = EVEN_STARTS[7:9]
GLA_LOW_COLS = EVEN_STARTS[9:11]
RET_COLS = tuple(i * N_RET * HEAD for i in range(4))

CHUNK = 128


class _ScanPlan(NamedTuple):
    block: int
    table_levels: tuple
    direct_levels: tuple

    @property
    def table_blocks(self):
        return (1 if self.block else 0) + len(self.table_levels)

    @property
    def row_total(self):
        return (self.table_blocks + 1) * CHUNK

    @property
    def diag_id(self):
        return len(self.table_levels) + len(self.direct_levels)


ROBUST_PLAN = _ScanPlan(0, (1, 2, 4), (8, 16, 32, 64))
BLOCK_PLAN = _ScanPlan(16, (), (16, 32, 64))
BLOCK_DECAY_LIMIT = 90.0
TABLE_ROWS_MAX = max(p.row_total for p in (ROBUST_PLAN, BLOCK_PLAN)) + 8
PAIR = 2
GATED_HEADS_PER_ITER = 8
GATED_UNITS_MAX = 4
GLA_HEADS_PER_GROUP = HEAD // GLA_DK
RET_UNITS_PER_ITER = 8
STEP_TOKENS = 1024
ROW_TILE = 512
MOD_COLS = 768
VMEM_LIMIT_BYTES = 61 * 1024 * 1024


def _dot(a, b, precision=None):
    return jnp.dot(a, b, precision=precision, preferred_element_type=f32)


def _dot_tn(a, b):
    return lax.dot_general(a, b, (((0,), (0,)), ((), ())), preferred_element_type=f32)


def _silu(x):
    return x * jax.nn.sigmoid(x)


def _row_tile(i):
    return pl.ds(pl.multiple_of(i * ROW_TILE, ROW_TILE), ROW_TILE)


def _chunk_rows(seq_start, c):
    return pl.ds(pl.multiple_of(seq_start + c * CHUNK, CHUNK), CHUNK)


def _mod_kernel(cond_ref, w_ref, b_ref, o_ref):
    o_ref[0] = _dot(_silu(cond_ref[...]), w_ref[0], HIGHEST) + b_ref[0]


def _modulation(cond, ada_w, ada_b):
    depth, d, d3 = ada_w.shape
    rows = cond.shape[0]
    return pl.pallas_call(
        _mod_kernel,
        grid=(depth, d3 // MOD_COLS),
        in_specs=[
            pl.BlockSpec((rows, d), lambda l, j: (0, 0)),
            pl.BlockSpec((1, d, MOD_COLS), lambda l, j: (l, 0, j)),
            pl.BlockSpec((1, 1, MOD_COLS), lambda l, j: (l, 0, j)),
        ],
        out_specs=pl.BlockSpec((1, rows, MOD_COLS), lambda l, j: (l, 0, j)),
        out_shape=jax.ShapeDtypeStruct((depth, rows, d3), f32),
        compiler_params=pltpu.CompilerParams(dimension_semantics=("arbitrary", "arbitrary")),
        name="modulation",
    )(cond, ada_w, ada_b.reshape(depth, 1, d3))


def _modulated_norm(x_ref, mod_ref, nw_ref, h_scr):
    def body(i, carry):
        rows = _row_tile(i)
        x = x_ref[0, rows, :]
        y = x * lax.rsqrt(jnp.mean(x * x, axis=-1, keepdims=True) + EPS) * nw_ref[...]
        h_scr[rows, :] = (y * (1.0 + mod_ref[0, 1:2, :]) + mod_ref[0, 0:1, :]).astype(bf16)
        return carry
    lax.fori_loop(0, STEP_TOKENS // ROW_TILE, body, 0)


def _clear_output(out_ref):
    def body(i, carry):
        out_ref[0, _row_tile(i), :] = jnp.zeros((ROW_TILE, out_ref.shape[-1]), f32)
        return carry
    lax.fori_loop(0, STEP_TOKENS // ROW_TILE, body, 0)


def _pair_lanes(pair):
    return pl.ds(pl.multiple_of(pair * (PAIR * HEAD), PAIR * HEAD), PAIR * HEAD)


def _project(h, w_ref, pair, first_cols):
    width = PAIR * HEAD
    return [_dot(h, w_ref[:, pl.ds(pl.multiple_of(c0 + pair * width, width), width)]) for c0 in first_cols]


def _add_heads_output(first_head, n_heads, gate_s, o_scr, gnw_ref, wo_ref, out_ref):
    width = n_heads * HEAD
    lanes = pl.ds(pl.multiple_of(first_head * HEAD, width), width)
    tiles = []
    for i in range(STEP_TOKENS // ROW_TILE):
        rows = pl.ds(i * ROW_TILE, ROW_TILE)
        parts = []
        for h in range(n_heads):
            o = o_scr[0, h, rows, :] + o_scr[1, h, rows, :]
            parts.append(o * lax.rsqrt(jnp.mean(o * o, axis=-1, keepdims=True) + EPS))
        y = jnp.concatenate(parts, axis=-1) * gnw_ref[:, lanes]
        tiles.append((rows, (y * gate_s[rows, 0:width].astype(f32)).astype(bf16)))
    products = [(rows, _dot(z, wo_ref[lanes, :])) for rows, z in tiles]
    for rows, product in products:
        out_ref[0, rows, :] += product


def _finish_output(x_ref, mod_ref, fnw_ref, out_ref, final):
    def body(i, carry):
        rows = _row_tile(i)
        xn = x_ref[0, rows, :] + mod_ref[0, 2:3, :] * out_ref[0, rows, :]
        if final:
            xn = xn * lax.rsqrt(jnp.mean(xn * xn, axis=-1, keepdims=True) + EPS) * fnw_ref[...]
        out_ref[0, rows, :] = xn
        return carry
    lax.fori_loop(0, STEP_TOKENS // ROW_TILE, body, 0)


def _scan_tables(plan):
    t = np.arange(CHUNK)[:, None]
    j = np.arange(CHUNK)[None, :]
    fwd = []
    if plan.block:
        fwd.append((j <= t) & (j // plan.block == t // plan.block))
    for m in plan.table_levels:
        mid = (t // (2 * m)) * (2 * m) + m
        right = t >= mid
        fwd.append(np.where(right, (j >= mid) & (j <= t), (j > t) & (j < mid)))
    fwd.append(j <= t)
    fwd.append(np.ones((8, CHUNK), bool))
    fwd = np.concatenate(fwd, axis=0).astype(np.float32)
    bwd = fwd.copy()
    n_sym = plan.row_total // CHUNK
    bwd[:plan.row_total] = fwd[:plan.row_total].reshape(n_sym, CHUNK, CHUNK)[:, ::-1, ::-1].reshape(-1, CHUNK)
    table = np.stack([np.tile(fwd, (1, 2)), np.tile(bwd, (1, 2))])
    first = 1 if plan.block else 0
    ids = np.full((CHUNK, CHUNK), -1, np.int32)
    for i, m in enumerate(plan.table_levels + plan.direct_levels):
        ids = np.where((t > j) & ((t ^ j) >= m) & ((t ^ j) < 2 * m), first + i, ids)
    if plan.block:
        ids = np.where((t >= j) & (t // plan.block == j // plan.block), 0, ids)
    else:
        ids = np.where(t == j, plan.diag_id, ids)
    return jnp.asarray(table, bf16), jnp.asarray(np.stack([ids, ids.T]).astype(np.int32))


def _in_chunk_scores(q, k, cum, tabled, lvl, ones, rev, key_masks, plan):
    qb = q.astype(bf16)
    q_heads = [qb if mask is None else qb * mask for mask in key_masks]
    kt = k.astype(bf16).T
    if plan.block:
        inside = tabled[0]
        grow, decay = jnp.exp2(-inside).astype(bf16), jnp.exp2(inside).astype(bf16)
        keys = kt * grow.T
        scores = [jnp.where(lvl == 0, _dot(qh * decay, keys), 0.0) for qh in q_heads]
        tabled = tabled[1:]
    else:
        diag_keys = kt * ones
        scores = [jnp.where(lvl == plan.diag_id, _dot(qh, diag_keys), 0.0) for qh in q_heads]
    first = 1 if plan.block else 0
    for i in range(len(plan.table_levels)):
        e = jnp.exp2(tabled[i]).astype(bf16)
        keys = kt * e.T
        scores = [jnp.where(lvl == first + i, _dot(qh * e, keys), sc) for qh, sc in zip(q_heads, scores)]
    for i, m in enumerate(plan.direct_levels, start=first + len(plan.table_levels)):
        blocks = []
        for p0 in range(0, CHUNK, 2 * m):
            left, right = slice(p0, p0 + m), slice(p0 + m, p0 + 2 * m)
            q_side, k_side = (left, right) if rev else (right, left)
            mid_row = p0 + m if rev else p0 + m - 1
            blocks.append((q_side, k_side, cum[mid_row:mid_row + 1, :]))
        q_decay = [jnp.exp2(cum[qs] - mid).astype(bf16) for qs, _, mid in blocks]
        k_decay = []
        for _, ks, mid in blocks:
            decay = jnp.exp2(mid - cum[ks]).astype(bf16)
            zero = jnp.zeros((m, HEAD), bf16)
            k_decay += [zero, decay] if rev else [decay, zero]
        keys = kt * jnp.concatenate(k_decay, axis=0).T
        for h, qh in enumerate(q_heads):
            s = _dot(jnp.concatenate([qh[qs] * e for (qs, _, _), e in zip(blocks, q_decay)], axis=0), keys)
            rows = []
            for b, (qs, ks, _) in enumerate(blocks):
                updated = jnp.where(lvl[qs, :] == i, s[b * m:(b + 1) * m, :], scores[h][qs, :])
                rows += [updated, scores[h][ks, :]] if rev else [scores[h][ks, :], updated]
            scores[h] = jnp.concatenate(rows, axis=0)
    return [sc.astype(bf16) for sc in scores]


def _store_log2_split(g_ref, rows, g):
    width = g.shape[-1]
    x = g * LOG2E
    hi = x.astype(bf16)
    g_ref[rows, 0:width] = hi
    g_ref[rows, width:2 * width] = (x - hi.astype(f32)).astype(bf16)
    return jnp.min(jnp.sum(x.reshape(-1, BLOCK_PLAN.block, width), axis=1))


def _decay_sums(g_split, table):
    width = g_split.shape[-1] // 2
    return _dot(table, jnp.concatenate([g_split[:, :width], g_split[:, width:]], axis=0))


def _key_masks(heads_per_group):
    if heads_per_group == 1:
        return [None]
    lane = lax.broadcasted_iota(jnp.int32, (1, HEAD), 1)
    return [(lane // (HEAD // heads_per_group) == sub).astype(bf16) for sub in range(heads_per_group)]


def _gated_chunks(chains, sums_ref, upcoming, upcoming_ref, heads_per_group, plan):
    masks = _key_masks(heads_per_group)
    groups = [slice(g * HEAD, (g + 1) * HEAD) for g in range(PAIR)]
    n_heads = PAIR * heads_per_group
    group_of = [h // heads_per_group for h in range(n_heads)]
    mask_of = [masks[h % heads_per_group] for h in range(n_heads)]
    flat = [(c, r) for c, (_, units) in enumerate(chains) for r in range(len(units))]
    unit = {(c, r): chains[c][1][r] for c, r in flat}
    index = {key: u for u, key in enumerate(flat)}
    cum_rows = slice(plan.table_blocks * CHUNK, (plan.table_blocks + 1) * CHUNK)
    cum = {key: sums_ref[index[key], cum_rows, :] for key in flat}
    total = {key: sums_ref[index[key], plan.row_total:plan.row_total + 1, :] for key in flat}
    vb = {key: [unit[key][2][:, h * HEAD:(h + 1) * HEAD] for h in range(n_heads)] for key in flat}
    upcoming = list(enumerate(upcoming))
    kv, whole = {}, {}
    for key in flat:
        k = unit[key][1]
        keys = (k * jnp.exp2(total[key] - cum[key])).astype(bf16)
        kv[key] = []
        for h in range(n_heads):
            own = keys[:, groups[group_of[h]]]
            kv[key].append(_dot_tn(own if mask_of[h] is None else own * mask_of[h], vb[key][h]))
        whole[key] = [jnp.broadcast_to(jnp.exp2(total[key][:, lanes]), (HEAD, HEAD)).T for lanes in groups]
    state = [list(states) for states, _ in chains]
    outs, pending, carried = {key: [None] * n_heads for key in flat}, None, {}

    def finish(done):
        pkey, ph, psc = done
        queries, entering = carried[pkey]
        lhs = jnp.concatenate([queries[:, groups[group_of[ph]]], psc], axis=1)
        outs[pkey][ph] = _dot(lhs, jnp.concatenate([entering[ph], vb[pkey][ph]], axis=0))

    for r in range(max(len(units) for _, units in chains)):
        live = [key for key in flat if key[1] == r]
        for key in live:
            queries = (unit[key][0] * jnp.exp2(cum[key])).astype(bf16)
            carried[key] = (queries, [s.astype(bf16) for s in state[key[0]]])
            state[key[0]] = [whole[key][group_of[h]] * state[key[0]][h] + kv[key][h] for h in range(n_heads)]
        for key in live:
            q, k, _, lvl, rev = unit[key]
            ones = (lvl[0:1, :] >= -1).astype(bf16)
            for g, lanes in enumerate(groups):
                tabled = [sums_ref[index[key], i * CHUNK:(i + 1) * CHUNK, lanes] for i in range(plan.table_blocks)]
                group_scores = _in_chunk_scores(q[:, lanes], k[:, lanes], cum[key][:, lanes], tabled, lvl, ones, rev,
                                                masks, plan)
                for sub, sc in enumerate(group_scores):
                    if pending is not None:
                        finish(pending)
                    pending = (key, g * heads_per_group + sub, sc)
            if upcoming:
                u, (g_split, table) = upcoming.pop(0)
                upcoming_ref[u, 0:plan.row_total + 8, :] = _decay_sums(g_split, table)
    for u, (g_split, table) in upcoming:
        upcoming_ref[u, 0:plan.row_total + 8, :] = _decay_sums(g_split, table)
    finish(pending)
    return [[outs[(c, r)] for r in range(len(units))] for c, (_, units) in enumerate(chains)], state


def _gated_scans(q_s, k_refs, v_s, g_refs, s_scr, sums_scr, table_ref, lvl_ref, o_scr, seq, heads_per_group, plan):
    n = seq // CHUNK
    n_seq = STEP_TOKENS // seq
    n_heads = PAIR * heads_per_group
    units = GATED_HEADS_PER_ITER // n_heads
    per_iter = min(n, units // 2)
    seqs_per_iter = min(n_seq, units // (2 * per_iter))
    assert n % per_iter == 0 and n_seq % seqs_per_iter == 0
    iters_per_seq = n // per_iter
    n_iters = (n_seq // seqs_per_iter) * iters_per_seq

    def layout(it):
        jj, i = it // iters_per_seq, it % iters_per_seq
        chains = []
        for js in range(seqs_per_iter):
            j = jj * seqs_per_iter + js
            for d in range(2):
                steps = [i * per_iter + r for r in range(per_iter)]
                chains.append((j, d, [_chunk_rows(j * seq, n - 1 - t if d else t) for t in steps]))
        return chains

    def sums_inputs(it):
        return [(g_refs[d][rw, :], table_ref[d]) for _, d, rws in layout(it) for rw in rws]

    for u, (g_split, table) in enumerate(sums_inputs(0)):
        sums_scr[0, u, 0:plan.row_total + 8, :] = _decay_sums(g_split, table)

    def iteration(it, slot):
        chains = layout(it)
        args = [([s_scr[j, d, h] for h in range(n_heads)],
                 [(q_s[rw, :], k_refs[d][rw, :], v_s[rw, :], lvl_ref[d], bool(d)) for rw in rws])
                for j, d, rws in chains]
        upcoming = sums_inputs(jnp.minimum(it + 1, n_iters - 1))
        outs, new_states = _gated_chunks(args, sums_scr.at[slot], upcoming, sums_scr.at[1 - slot], heads_per_group,
                                         plan)
        for c, (j, d, rws) in enumerate(chains):
            for h in range(n_heads):
                for r, rw in enumerate(rws):
                    o_scr[d, h, rw, :] = outs[c][r][h]
                s_scr[j, d, h] = new_states[c][h]

    assert n_iters % 2 == 0

    def body(it2, carry):
        iteration(2 * it2, 0)
        iteration(2 * it2 + 1, 1)
        return carry
    lax.fori_loop(0, n_iters // 2, body, 0)


def _even_kernel(*refs, seq, has_state, emit_state, final):
    it = iter(refs)
    x_ref, mod_ref, nw_ref = next(it), next(it), next(it)
    win_ref, wlow_ref, wo_ref = next(it), next(it), next(it)
    gkw_ref, gkb_ref, lb_ref, gnw_ref, fnw_ref = next(it), next(it), next(it), next(it), next(it)
    plan_tables = {ROBUST_PLAN: (next(it), next(it)), BLOCK_PLAN: (next(it), next(it))}
    s0a_ref, s0b_ref = (next(it), next(it)) if has_state else (None, None)
    out_ref = next(it)
    sta_ref, stb_ref = (next(it), next(it)) if emit_state else (None, None)
    h_scr, low_scr, o_scr, gate_s, q_s, v_s, kf_s, kb_s, gf_s, gb_s, s_scr, sums_scr, bound_s = it
    n_seq = STEP_TOKENS // seq
    n_tiles = STEP_TOKENS // ROW_TILE

    _modulated_norm(x_ref, mod_ref, nw_ref, h_scr)
    _clear_output(out_ref)

    def low_body(i, carry):
        rows = _row_tile(i)
        low_scr[rows, :] = _dot(h_scr[rows, :], wlow_ref[...]).astype(bf16)
        return carry
    lax.fori_loop(0, n_tiles, low_body, 0)

    width = PAIR * HEAD

    def run_scans(first_head, s0_ref, st_ref, h0, heads_per_group, k_refs, block_decays):
        bound_s[0] = (functools.reduce(jnp.minimum, block_decays) >= -BLOCK_DECAY_LIMIT).astype(jnp.int32)
        n_heads = PAIR * heads_per_group
        key_rows = HEAD // heads_per_group
        own_rows = [pl.ds((h % heads_per_group) * key_rows, key_rows) for h in range(n_heads)]
        for j in range(n_seq):
            for d in range(2):
                for h in range(n_heads):
                    if s0_ref is None or heads_per_group > 1:
                        s_scr[j, d, h] = jnp.zeros((HEAD, HEAD), f32)
                    if s0_ref is not None:
                        s_scr[j, d, h, own_rows[h], :] = s0_ref[j, d, h0 + h]
        def scans(plan):
            table_ref, lvl_ref = plan_tables[plan]
            _gated_scans(q_s, k_refs, v_s, (gf_s, gb_s), s_scr, sums_scr, table_ref, lvl_ref, o_scr, seq,
                         heads_per_group, plan)
        lax.cond(bound_s[0] == 1, functools.partial(scans, BLOCK_PLAN), functools.partial(scans, ROBUST_PLAN))
        if st_ref is not None:
            for j in range(n_seq):
                for d in range(2):
                    for h in range(n_heads):
                        st_ref[j, d, h0 + h] = s_scr[j, d, h, own_rows[h], :]
        _add_heads_output(first_head, n_heads, gate_s, o_scr, gnw_ref, wo_ref, out_ref)

    def hgrn_pair(pp, carry):
        lb = lb_ref[:, _pair_lanes(pp)]
        log_lb = jnp.log(lb)

        tiles = [pl.ds(i * ROW_TILE, ROW_TILE) for i in range(n_tiles)]
        projected = [_project(h_scr[rows, :], win_ref, pp, HGRN_COLS) for rows in tiles]
        block_decays = []
        for rows, (query, value, forget_f, forget_b, gate) in zip(tiles, projected):
            gate_s[rows, 0:width] = _silu(gate).astype(bf16)
            q_s[rows, :] = _silu(query)
            v_s[rows, 0:width] = value.astype(bf16)
            for a, k_s, g_s in ((forget_f, kf_s, gf_s), (forget_b, kb_s, gb_s)):
                z = log_lb - a
                u = jnp.exp(-jnp.abs(a))
                w = jnp.exp(-jnp.abs(z))
                r = 1.0 / (1.0 + u)
                log_f = jnp.maximum(z, 0.0) + jnp.minimum(a, 0.0) + jnp.log((1.0 + w) * r)
                block_decays.append(_store_log2_split(g_s, rows, log_f))
                k_s[rows, :] = (1.0 - lb) * jnp.where(a >= 0.0, u * r, r)
        run_scans(PAIR * pp, s0a_ref, sta_ref, PAIR * pp, 1, (kf_s, kb_s), block_decays)
        return carry
    lax.fori_loop(0, N_HGRN // PAIR, hgrn_pair, 0)

    assert N_GLA == PAIR * GLA_HEADS_PER_GROUP

    def gla_project(rows):
        h = h_scr[rows, :]
        narrow = [_dot(h, win_ref[:, c0:c0 + N_GLA * GLA_DK]) for c0 in GLA_QK_COLS]
        wide = [_dot(h, win_ref[:, c0:c0 + N_GLA * HEAD]) for c0 in GLA_COLS]
        return narrow + wide

    tiles = [pl.ds(i * ROW_TILE, ROW_TILE) for i in range(n_tiles)]
    projected = [gla_project(rows) for rows in tiles]
    block_decays = []
    for rows, (query, key, value, gate) in zip(tiles, projected):
        gate_s[rows, :] = _silu(gate).astype(bf16)
        q_s[rows, :] = query * (GLA_DK ** -0.5)
        kf_s[rows, :] = key
        v_s[rows, :] = value.astype(bf16)
        low = low_scr[rows, :]
        for d, g_s in enumerate((gf_s, gb_s)):
            logits = _dot(low, gkw_ref[d]) + gkb_ref[d]
            log_gate = jnp.minimum(logits, 0.0) - jnp.log(1.0 + jnp.exp(-jnp.abs(logits)))
            block_decays.append(_store_log2_split(g_s, rows, log_gate * (1.0 / GLA_GATE_NORM)))
    run_scans(N_HGRN, s0b_ref, stb_ref, 0, GLA_HEADS_PER_GROUP, (kf_s, kf_s), block_decays)

    _finish_output(x_ref, mod_ref, fnw_ref, out_ref, final)


def _odd_kernel(*refs, seq, has_state, emit_state, use_rope, final):
    it = iter(refs)
    lg_ref = next(it)
    x_ref, mod_ref, nw_ref = next(it), next(it), next(it)
    wr_ref, wo_ref, gnw_ref, fnw_ref = next(it), next(it), next(it), next(it)
    cos_ref, sin_ref = (next(it), next(it)) if use_rope else (None, None)
    s0_ref = next(it) if has_state else None
    out_ref = next(it)
    st_ref = next(it) if emit_state else None
    h_scr, o_scr, gate_s, q_s, kt_s, v_s, s_scr, dec_scr = it
    n_seq = STEP_TOKENS // seq
    n_tiles = STEP_TOKENS // ROW_TILE
    n_chunks = seq // CHUNK
    per_iter = min(n_chunks, RET_UNITS_PER_ITER // n_seq)
    assert n_chunks % per_iter == 0

    _modulated_norm(x_ref, mod_ref, nw_ref, h_scr)
    _clear_output(out_ref)
    t_idx = lax.broadcasted_iota(jnp.int32, (CHUNK, CHUNK), 0)
    s_idx = lax.broadcasted_iota(jnp.int32, (CHUNK, CHUNK), 1)
    row_f = lax.broadcasted_iota(jnp.int32, (CHUNK, HEAD), 0).astype(f32)
    col_f = lax.broadcasted_iota(jnp.int32, (8, CHUNK), 1).astype(f32)
    chunk_len = jnp.full((8, HEAD), CHUNK, f32)
    if use_rope:
        lane = lax.broadcasted_iota(jnp.int32, (ROW_TILE, HEAD), 1)
        first_quarter = (lane // (HEAD // 4)) % 2 == 0

    def rope(x, cos, sin_signed):
        xr = jnp.where(first_quarter, pltpu.roll(x, HEAD - HEAD // 4, axis=1), pltpu.roll(x, HEAD // 4, axis=1))
        return x * cos + xr * sin_signed

    width = PAIR * HEAD

    def pair_body(pp, carry):
        def proj(i, c):
            rows = _row_tile(i)
            q, k, value, gate = _project(h_scr[rows, :], wr_ref, pp, RET_COLS)
            gate_s[rows, :] = _silu(gate).astype(bf16)
            k = k * (HEAD ** -0.5)
            if use_rope:
                cos, sin_signed = cos_ref[rows, :], sin_ref[rows, :]
                heads = [slice(h * HEAD, (h + 1) * HEAD) for h in range(PAIR)]
                q = jnp.concatenate([rope(q[:, lanes], cos, sin_signed) for lanes in heads], axis=-1)
                k = jnp.concatenate([rope(k[:, lanes], cos, sin_signed) for lanes in heads], axis=-1)
            q_s[rows, :] = q.astype(bf16)
            kt_s[:, rows] = k.T
            v_s[rows, :] = value.astype(bf16)
            return c
        lax.fori_loop(0, n_tiles, proj, 0)
        for h in range(PAIR):
            scan_head(PAIR * pp + h, h)
        _add_heads_output(PAIR * pp, PAIR, gate_s, o_scr, gnw_ref, wo_ref, out_ref)
        return carry

    def scan_head(hh, h):
        lanes = slice(h * HEAD, (h + 1) * HEAD)
        lg_f = lg_ref[0, hh]
        lg_b = lg_ref[1, hh]
        for j in range(n_seq):
            for d in range(2):
                s_scr[j, d] = s0_ref[j, d, hh] if has_state else jnp.zeros((HEAD, HEAD), f32)

        dist = (t_idx - s_idx).astype(f32)
        dec_scr[0] = jnp.exp(lg_f * (row_f + 1.0))
        dec_scr[1] = jnp.exp(lg_b * (CHUNK - row_f))
        dec_scr[2] = (jnp.where(t_idx >= s_idx, jnp.exp(lg_f * jnp.maximum(dist, 0.0)), 0.0)
                      + jnp.where(s_idx >= t_idx, jnp.exp(lg_b * jnp.maximum(-dist, 0.0)), 0.0))
        dec_scr[3, 0:8, :] = jnp.exp(lg_f * (CHUNK - 1.0 - col_f))
        dec_scr[3, 8:16, :] = jnp.exp(lg_b * col_f)
        dec_scr[3, 16:24, :] = jnp.exp(lg_f * chunk_len)
        dec_scr[3, 24:32, :] = jnp.exp(lg_b * chunk_len)

        def body(i, c):
            units = [(j, r) for j in range(n_seq) for r in range(per_iter)]
            rows = {(j, r, d): _chunk_rows(j * seq, n_chunks - 1 - (i * per_iter + r) if d else i * per_iter + r)
                    for j, r in units for d in range(2)}
            scores = {u: _dot(q_s[rows[u + (0,)], lanes], kt_s[lanes, rows[u + (0,)]].astype(bf16)) for u in units}
            kv = {}
            for j, r in units:
                for d in range(2):
                    rw = rows[(j, r, d)]
                    keys = (kt_s[lanes, rw] * dec_scr[3, 8 * d:8 * d + 1, :]).astype(bf16)
                    kv[(j, r, d)] = _dot(keys, v_s[rw, lanes])
            state = {(j, d): s_scr[j, d] for j in range(n_seq) for d in range(2)}
            carried = {}
            for r in range(per_iter):
                for j in range(n_seq):
                    for d in range(2):
                        carried[(j, r, d)] = _dot(q_s[rows[(j, r, d)], lanes], state[(j, d)].astype(bf16))
                        state[(j, d)] = dec_scr[3, 16 + 8 * d:17 + 8 * d, :] * state[(j, d)] + kv[(j, r, d)]
            for j, r in units:
                rw = rows[(j, r, 0)]
                inside = _dot((scores[(j, r)] * dec_scr[2]).astype(bf16), v_s[rw, lanes])
                o_scr[0, h, rw, :] = inside + dec_scr[0] * carried[(j, r, 0)]
                o_scr[1, h, rows[(j, r, 1)], :] = dec_scr[1] * carried[(j, r, 1)]
            for (j, d), s in state.items():
                s_scr[j, d] = s
            return c
        lax.fori_loop(0, n_chunks // per_iter, body, 0)
        if emit_state:
            for j in range(n_seq):
                for d in range(2):
                    st_ref[j, d, hh] = s_scr[j, d]

    lax.fori_loop(0, N_RET // PAIR, pair_body, 0)

    _finish_output(x_ref, mod_ref, fnw_ref, out_ref, final)


def _const_spec(shape):
    zeros = (0,) * len(shape)
    return pl.BlockSpec(shape, lambda i: zeros, pipeline_mode=pl.Buffered(1))


def _step_spec(shape, per_step, buffers=None):
    zeros = (0,) * (len(shape) - 1)
    mode = {} if buffers is None else {"pipeline_mode": pl.Buffered(buffers)}
    return pl.BlockSpec((per_step,) + tuple(shape[1:]), lambda i: (i,) + zeros, **mode)


def _mod_spec(d, per_sequence):
    if per_sequence:
        return pl.BlockSpec((1, 3, d), lambda i: (i + 1, 0, 0))
    return pl.BlockSpec((1, 3, d), lambda i: (0, 0, 0))


def _layer_call(body, x, mod, consts, states, state_shapes, per_sequence_mod, scratch, name, smem_inputs=()):
    n_seq, seq, d = x.shape
    per_step = STEP_TOKENS // seq
    assert per_step * seq == STEP_TOKENS and n_seq % per_step == 0 and seq % CHUNK == 0
    assert not per_sequence_mod or per_step == 1
    n_steps = n_seq // per_step
    xs = x.reshape(n_steps, STEP_TOKENS, d)
    inputs = list(smem_inputs) + [xs, mod] + list(consts) + list(states)
    in_specs = [pl.BlockSpec(memory_space=pltpu.SMEM)] * len(smem_inputs)
    in_specs += [_step_spec(xs.shape, 1), _mod_spec(d, per_sequence_mod)]
    in_specs += [_const_spec(a.shape) for a in consts]
    in_specs += [_step_spec(s.shape, per_step) for s in states]
    out_shape = [jax.ShapeDtypeStruct(xs.shape, f32)] + [jax.ShapeDtypeStruct(s, f32) for s in state_shapes]
    out_specs = [_step_spec(xs.shape, 1)] + [_step_spec(s, per_step, buffers=1) for s in state_shapes]
    outs = pl.pallas_call(
        body,
        grid=(n_steps,),
        in_specs=in_specs,
        out_specs=out_specs,
        out_shape=out_shape,
        scratch_shapes=scratch,
        compiler_params=pltpu.CompilerParams(dimension_semantics=("arbitrary",), vmem_limit_bytes=VMEM_LIMIT_BYTES),
        name=name,
    )(*inputs)
    return [outs[0].reshape(x.shape)] + list(outs[1:])


def _even_layer(x, mod, norm_w, w, final_w, tables, states, emit_state, per_sequence_mod, final):
    n_seq, seq, d = x.shape
    consts = [norm_w, w["win"], w["wlow"], w["wout"], w["gkw"], w["gkb"], w["lb"], w["gnw"],
              final_w] + list(tables)
    state_shapes = [(n_seq, 2, N_HGRN, HEAD, HEAD), (n_seq, 2, N_GLA, GLA_DK, HEAD)] if emit_state else []
    scan_heads = max(PAIR, N_GLA)
    scratch = [
        pltpu.VMEM((STEP_TOKENS, d), bf16),
        pltpu.VMEM((STEP_TOKENS, HEAD), bf16),
        pltpu.VMEM((2, scan_heads, STEP_TOKENS, HEAD), f32),
        pltpu.VMEM((STEP_TOKENS, scan_heads * HEAD), bf16),
        pltpu.VMEM((STEP_TOKENS, PAIR * HEAD), f32),
        pltpu.VMEM((STEP_TOKENS, scan_heads * HEAD), bf16),
        pltpu.VMEM((STEP_TOKENS, PAIR * HEAD), f32),
        pltpu.VMEM((STEP_TOKENS, PAIR * HEAD), f32),
        pltpu.VMEM((STEP_TOKENS, 2 * PAIR * HEAD), bf16),
        pltpu.VMEM((STEP_TOKENS, 2 * PAIR * HEAD), bf16),
        pltpu.VMEM((STEP_TOKENS // seq, 2, scan_heads, HEAD, HEAD), f32),
        pltpu.VMEM((2, GATED_UNITS_MAX, TABLE_ROWS_MAX, PAIR * HEAD), f32),
        pltpu.SMEM((1,), jnp.int32),
    ]
    body = functools.partial(_even_kernel, seq=seq, has_state=states is not None, emit_state=emit_state, final=final)
    return _layer_call(body, x, mod, consts, states or (), state_shapes, per_sequence_mod, scratch,
                       "even_layer_seq%d" % seq)


def _odd_layer(x, mod, norm_w, w, final_w, log_decay, rope, state, emit_state, per_sequence_mod, final):
    n_seq, seq, d = x.shape
    consts = [norm_w, w["wr"], w["wout"], w["gnw"], final_w] + list(rope or ())
    state_shapes = [(n_seq, 2, N_RET, HEAD, HEAD)] if emit_state else []
    scratch = [
        pltpu.VMEM((STEP_TOKENS, d), bf16),
        pltpu.VMEM((2, PAIR, STEP_TOKENS, HEAD), f32),
        pltpu.VMEM((STEP_TOKENS, PAIR * HEAD), bf16),
        pltpu.VMEM((STEP_TOKENS, PAIR * HEAD), bf16),
        pltpu.VMEM((PAIR * HEAD, STEP_TOKENS), f32),
        pltpu.VMEM((STEP_TOKENS, PAIR * HEAD), bf16),
        pltpu.VMEM((STEP_TOKENS // seq, 2, HEAD, HEAD), f32),
        pltpu.VMEM((4, CHUNK, HEAD), f32),
    ]
    body = functools.partial(_odd_kernel, seq=seq, has_state=state is not None, emit_state=emit_state,
                             use_rope=rope is not None, final=final)
    return _layer_call(body, x, mod, consts, () if state is None else (state,), state_shapes, per_sequence_mod,
                       scratch, "odd_layer_seq%d" % seq, smem_inputs=(log_decay,))


def _even_weights(w_in, gk_w, gk_b, lb, gn_w, w_out):
    c0 = GLA_LOW_COLS[0]
    wlow = jnp.pad(w_in[:, c0:c0 + 2 * GLA_RANK], ((0, 0), (0, HEAD - 2 * GLA_RANK)))
    gkw = jnp.stack([jnp.pad(gk_w[d], ((d * GLA_RANK, HEAD - (d + 1) * GLA_RANK), (0, 0))) for d in range(2)])
    return {"win": w_in.astype(bf16), "wlow": wlow.astype(bf16), "gkw": gkw.astype(bf16),
            "gkb": gk_b.reshape(2, 1, -1), "lb": lb.reshape(1, -1), "gnw": gn_w.reshape(1, -1),
            "wout": w_out.astype(bf16)}


def _odd_weights(w_in, gn_w, w_out):
    return {"wr": w_in.astype(bf16), "gnw": gn_w.reshape(1, -1), "wout": w_out.astype(bf16)}


def _rope_tables(seq):
    rows = seq // GRID_W
    t_row = jnp.repeat(jnp.arange(rows), GRID_W).astype(f32)
    t_col = jnp.tile(jnp.arange(GRID_W), rows).astype(f32)
    half = HEAD // 2
    inv = ROPE_BASE ** (-jnp.arange(0, half, 2, dtype=f32) / half)
    ang_r = t_row[:, None] * inv
    ang_c = t_col[:, None] * inv
    ang = jnp.concatenate([ang_r, ang_r, ang_c, ang_c], axis=-1)
    sign = jnp.where((jnp.arange(HEAD) // (HEAD // 4)) % 2 == 0, -1.0, 1.0).astype(f32)
    return jnp.cos(ang), jnp.sin(ang) * sign


def kernel(x_prompt, x_sample, state_hgrn, state_gla, state_ret, c, c_ctx, norm_w, ada_w, ada_b, w_in_even, hgrn_lb, gla_gk_w, gla_gk_b, gn_even, w_out_even, w_in_odd, ret_decay, gn_odd, w_out_odd, final_norm_w):
    depth, d = norm_w.shape
    n_lat = x_sample.shape[0]
    n_cond = -(-(1 + n_lat) // 8) * 8
    cond = jnp.zeros((n_cond, d), f32).at[0].set(c_ctx).at[1:1 + n_lat].set(c)
    mod = _modulation(cond, ada_w, ada_b).reshape(depth, n_cond, 3, d)
    lbs = jnp.cumsum(jax.nn.softmax(hgrn_lb.astype(f32), axis=0), axis=0)
    final_w = final_norm_w.reshape(1, d)
    rope = _rope_tables(x_sample.shape[1])
    tables = _scan_tables(ROBUST_PLAN) + _scan_tables(BLOCK_PLAN)

    x_c, x_l = x_prompt, x_sample
    new_hgrn, new_gla, new_ret = [], [], []
    for l in range(depth):
        i = l // 2
        final = l == depth - 1
        nw = norm_w[l].reshape(1, d)
        if l % 2 == 0:
            w = _even_weights(w_in_even[i], gla_gk_w[i], gla_gk_b[i], lbs[i], gn_even[i], w_out_even[i])
            x_c, st_a, st_b = _even_layer(x_c, mod[l], nw, w, final_w, tables, None, True, False, final)
            (x_l,) = _even_layer(x_l, mod[l], nw, w, final_w, tables, (state_hgrn[:, i], state_gla[:, i]),
                                 False, True, final)
            new_hgrn.append(st_a)
            new_gla.append(st_b)
        else:
            w = _odd_weights(w_in_odd[i], gn_odd[i], w_out_odd[i])
            log_decay = jax.nn.log_sigmoid(ret_decay[i].astype(f32))
            x_c, st_c = _odd_layer(x_c, mod[l], nw, w, final_w, log_decay, None, None, True, False, final)
            (x_l,) = _odd_layer(x_l, mod[l], nw, w, final_w, log_decay, rope, state_ret[:, i], False, True, final)
            new_ret.append(st_c)
    def stacked(states):
        return states[0][:, None] if len(states) == 1 else jnp.stack(states, axis=1)
    return (x_c, x_l, stacked(new_hgrn), stacked(new_gla), stacked(new_ret))
```

```python
import functools
from typing import NamedTuple

import numpy as np
import jax
import jax.numpy as jnp
from jax import lax
from jax.experimental import pallas as pl
from jax.experimental.pallas import tpu as pltpu

f32 = jnp.float32
bf16 = jnp.bfloat16
HIGHEST = lax.Precision.HIGHEST

EPS = 1e-6
LOG2E = 1.4426950408889634
HEAD = 128
N_HGRN = 4
N_GLA = 4
GLA_DK = 64
N_RET = 8
N_HEADS = 8
GLA_RANK = 16
GLA_GATE_NORM = 16.0
GRID_W = 64
ROPE_BASE = 10000.0

EVEN_SPLITS = (N_HGRN * HEAD,) * 5 + (N_GLA * GLA_DK,) * 2 + (N_GLA * HEAD,) * 2 + (GLA_RANK,) * 2
EVEN_STARTS = tuple(int(c) for c in np.cumsum((0,) + EVEN_SPLITS[:-1]))
HGRN_COLS = EVEN_STARTS[0:5]
GLA_QK_COLS = EVEN_STARTS[5:7]
GLA_COLS = EVEN_STARTS[7:9]
GLA_LOW_COLS = EVEN_STARTS[9:11]
RET_COLS = tuple(i * N_RET * HEAD for i in range(4))

CHUNK = 128


class _ScanPlan(NamedTuple):
    block: int
    table_levels: tuple
    direct_levels: tuple

    @property
    def table_blocks(self):
        return (1 if self.block else 0) + len(self.table_levels)

    @property
    def row_total(self):
        return (self.table_blocks + 1) * CHUNK

    @property
    def diag_id(self):
        return len(self.table_levels) + len(self.direct_levels)


ROBUST_PLAN = _ScanPlan(0, (1, 2, 4), (8, 16, 32, 64))
BLOCK_PLAN = _ScanPlan(16, (), (16, 32, 64))
BLOCK_DECAY_LIMIT = 90.0
TABLE_ROWS_MAX = max(p.row_total for p in (ROBUST_PLAN, BLOCK_PLAN)) + 8
PAIR = 2
GATED_HEADS_PER_ITER = 8
GATED_UNITS_MAX = 4
GLA_HEADS_PER_GROUP = HEAD // GLA_DK
RET_UNITS_PER_ITER = 8
STEP_TOKENS = 1024
ROW_TILE = 512
MOD_COLS = 768
VMEM_LIMIT_BYTES = 61 * 1024 * 1024


def _dot(a, b, precision=None):
    return jnp.dot(a, b, precision=precision, preferred_element_type=f32)


def _dot_tn(a, b):
    return lax.dot_general(a, b, (((0,), (0,)), ((), ())), preferred_element_type=f32)


def _silu(x):
    return x * jax.nn.sigmoid(x)


def _row_tile(i):
    return pl.ds(pl.multiple_of(i * ROW_TILE, ROW_TILE), ROW_TILE)


def _chunk_rows(seq_start, c):
    return pl.ds(pl.multiple_of(seq_start + c * CHUNK, CHUNK), CHUNK)


def _mod_kernel(cond_ref, w_ref, b_ref, o_ref):
    o_ref[0] = _dot(_silu(cond_ref[...]), w_ref[0], HIGHEST) + b_ref[0]


def _modulation(cond, ada_w, ada_b):
    depth, d, d3 = ada_w.shape
    rows = cond.shape[0]
    return pl.pallas_call(
        _mod_kernel,
        grid=(depth, d3 // MOD_COLS),
        in_specs=[
            pl.BlockSpec((rows, d), lambda l, j: (0, 0)),
            pl.BlockSpec((1, d, MOD_COLS), lambda l, j: (l, 0, j)),
            pl.BlockSpec((1, 1, MOD_COLS), lambda l, j: (l, 0, j)),
        ],
        out_specs=pl.BlockSpec((1, rows, MOD_COLS), lambda l, j: (l, 0, j)),
        out_shape=jax.ShapeDtypeStruct((depth, rows, d3), f32),
        compiler_params=pltpu.CompilerParams(dimension_semantics=("arbitrary", "arbitrary")),
        name="modulation",
    )(cond, ada_w, ada_b.reshape(depth, 1, d3))


def _modulated_norm(x_ref, mod_ref, nw_ref, h_scr):
    def body(i, carry):
        rows = _row_tile(i)
        x = x_ref[0, rows, :]
        y = x * lax.rsqrt(jnp.mean(x * x, axis=-1, keepdims=True) + EPS) * nw_ref[...]
        h_scr[rows, :] = (y * (1.0 + mod_ref[0, 1:2, :]) + mod_ref[0, 0:1, :]).astype(bf16)
        return carry
    lax.fori_loop(0, STEP_TOKENS // ROW_TILE, body, 0)


def _pair_lanes(pair):
    return pl.ds(pair * (PAIR * HEAD), PAIR * HEAD)


def _project(h, w_ref, pair, first_cols):
    width = PAIR * HEAD
    return [_dot(h, w_ref[:, pl.ds(c0 + pair * width, width)]) for c0 in first_cols]


def _add_heads_output(first_head, n_heads, gate_s, o_scr, gnw_ref, wo_ref, out_ref, first, finish):
    width = n_heads * HEAD
    lanes = pl.ds(first_head * HEAD, width)
    tiles = []
    for i in range(STEP_TOKENS // ROW_TILE):
        rows = pl.ds(i * ROW_TILE, ROW_TILE)
        parts = []
        for h in range(n_heads):
            o = o_scr[0, h, rows, :] + o_scr[1, h, rows, :]
            parts.append(o * lax.rsqrt(jnp.mean(o * o, axis=-1, keepdims=True) + EPS))
        y = jnp.concatenate(parts, axis=-1) * gnw_ref[:, lanes]
        tiles.append((rows, (y * gate_s[rows, 0:width].astype(f32)).astype(bf16)))
    products = [(rows, _dot(z, wo_ref[lanes, :])) for rows, z in tiles]
    for rows, product in products:
        mixed = product if first else out_ref[0, rows, :] + product
        if finish is not None:
            x_ref, mod_ref, fnw_ref, final = finish
            mixed = x_ref[0, rows, :] + mod_ref[0, 2:3, :] * mixed
            if final:
                mixed = mixed * lax.rsqrt(jnp.mean(mixed * mixed, axis=-1, keepdims=True) + EPS) * fnw_ref[...]
        out_ref[0, rows, :] = mixed


def _scan_tables(plan):
    t = np.arange(CHUNK)[:, None]
    j = np.arange(CHUNK)[None, :]
    fwd = []
    if plan.block:
        fwd.append((j <= t) & (j // plan.block == t // plan.block))
    for m in plan.table_levels:
        mid = (t // (2 * m)) * (2 * m) + m
        right = t >= mid
        fwd.append(np.where(right, (j >= mid) & (j <= t), (j > t) & (j < mid)))
    fwd.append(j <= t)
    fwd.append(np.ones((8, CHUNK), bool))
    fwd = np.concatenate(fwd, axis=0).astype(np.float32)
    bwd = fwd.copy()
    n_sym = plan.row_total // CHUNK
    bwd[:plan.row_total] = fwd[:plan.row_total].reshape(n_sym, CHUNK, CHUNK)[:, ::-1, ::-1].reshape(-1, CHUNK)
    table = np.stack([np.tile(fwd, (1, 2)), np.tile(bwd, (1, 2))])
    first = 1 if plan.block else 0
    ids = np.full((CHUNK, CHUNK), -1, np.int32)
    for i, m in enumerate(plan.table_levels + plan.direct_levels):
        ids = np.where((t > j) & ((t ^ j) >= m) & ((t ^ j) < 2 * m), first + i, ids)
    if plan.block:
        ids = np.where((t >= j) & (t // plan.block == j // plan.block), 0, ids)
    else:
        ids = np.where(t == j, plan.diag_id, ids)
    return jnp.asarray(table, bf16), jnp.asarray(np.stack([ids, ids.T]).astype(np.int32))


def _in_chunk_scores(q, k, cum, tabled, lvl, ones, rev, key_masks, plan):
    qb = q.astype(bf16)
    q_heads = [qb if mask is None else qb * mask for mask in key_masks]
    kt = k.astype(bf16).T
    if plan.block:
        inside = tabled[0]
        grow, decay = jnp.exp2(-inside).astype(bf16), jnp.exp2(inside).astype(bf16)
        keys = kt * grow.T
        scores = [jnp.where(lvl == 0, _dot(qh * decay, keys), 0.0) for qh in q_heads]
        tabled = tabled[1:]
    else:
        diag_keys = kt * ones
        scores = [jnp.where(lvl == plan.diag_id, _dot(qh, diag_keys), 0.0) for qh in q_heads]
    first = 1 if plan.block else 0
    for i in range(len(plan.table_levels)):
        e = jnp.exp2(tabled[i]).astype(bf16)
        keys = kt * e.T
        scores = [jnp.where(lvl == first + i, _dot(qh * e, keys), sc) for qh, sc in zip(q_heads, scores)]
    for i, m in enumerate(plan.direct_levels, start=first + len(plan.table_levels)):
        blocks = []
        for p0 in range(0, CHUNK, 2 * m):
            left, right = slice(p0, p0 + m), slice(p0 + m, p0 + 2 * m)
            q_side, k_side = (left, right) if rev else (right, left)
            mid_row = p0 + m if rev else p0 + m - 1
            blocks.append((q_side, k_side, cum[mid_row:mid_row + 1, :]))
        q_decay = [jnp.exp2(cum[qs] - mid).astype(bf16) for qs, _, mid in blocks]
        k_decay = []
        for _, ks, mid in blocks:
            decay = jnp.exp2(mid - cum[ks]).astype(bf16)
            zero = jnp.zeros((m, HEAD), bf16)
            k_decay += [zero, decay] if rev else [decay, zero]
        keys = kt * jnp.concatenate(k_decay, axis=0).T
        for h, qh in enumerate(q_heads):
            s = _dot(jnp.concatenate([qh[qs] * e for (qs, _, _), e in zip(blocks, q_decay)], axis=0), keys)
            rows = []
            for b, (qs, ks, _) in enumerate(blocks):
                updated = jnp.where(lvl[qs, :] == i, s[b * m:(b + 1) * m, :], scores[h][qs, :])
                rows += [updated, scores[h][ks, :]] if rev else [scores[h][ks, :], updated]
            scores[h] = jnp.concatenate(rows, axis=0)
    return [sc.astype(bf16) for sc in scores]


def _store_log2_split(g_ref, rows, g):
    width = g.shape[-1]
    x = g * LOG2E
    hi = x.astype(bf16)
    g_ref[rows, 0:width] = hi
    g_ref[rows, width:2 * width] = (x - hi.astype(f32)).astype(bf16)
    return jnp.min(jnp.sum(x.reshape(-1, BLOCK_PLAN.block, width), axis=1))


def _decay_sums(g_split, table):
    width = g_split.shape[-1] // 2
    return _dot(table, jnp.concatenate([g_split[:, :width], g_split[:, width:]], axis=0))


def _key_masks(heads_per_group):
    if heads_per_group == 1:
        return [None]
    lane = lax.broadcasted_iota(jnp.int32, (1, HEAD), 1)
    return [(lane // (HEAD // heads_per_group) == sub).astype(bf16) for sub in range(heads_per_group)]


def _gated_chunks(chains, sums_ref, upcoming, upcoming_ref, heads_per_group, plan):
    masks = _key_masks(heads_per_group)
    groups = [slice(g * HEAD, (g + 1) * HEAD) for g in range(PAIR)]
    n_heads = PAIR * heads_per_group
    group_of = [h // heads_per_group for h in range(n_heads)]
    mask_of = [masks[h % heads_per_group] for h in range(n_heads)]
    flat = [(c, r) for c, (_, units) in enumerate(chains) for r in range(len(units))]
    unit = {(c, r): chains[c][1][r] for c, r in flat}
    index = {key: u for u, key in enumerate(flat)}
    cum_rows = slice(plan.table_blocks * CHUNK, (plan.table_blocks + 1) * CHUNK)
    cum = {key: sums_ref[index[key], cum_rows, :] for key in flat}
    total = {key: sums_ref[index[key], plan.row_total:plan.row_total + 1, :] for key in flat}
    vb = {key: [unit[key][2][:, h * HEAD:(h + 1) * HEAD] for h in range(n_heads)] for key in flat}
    upcoming = list(enumerate(upcoming))
    kv, whole = {}, {}
    for key in flat:
        k = unit[key][1]
        keys = (k * jnp.exp2(total[key] - cum[key])).astype(bf16)
        kv[key] = []
        for h in range(n_heads):
            own = keys[:, groups[group_of[h]]]
            kv[key].append(_dot_tn(own if mask_of[h] is None else own * mask_of[h], vb[key][h]))
        whole[key] = [jnp.broadcast_to(jnp.exp2(total[key][:, lanes]), (HEAD, HEAD)).T for lanes in groups]
    state = [list(states) for states, _ in chains]
    outs, pending, carried = {key: [None] * n_heads for key in flat}, None, {}

    def finish(done):
        pkey, ph, psc = done
        queries, entering = carried[pkey]
        lhs = jnp.concatenate([queries[:, groups[group_of[ph]]], psc], axis=1)
        outs[pkey][ph] = _dot(lhs, jnp.concatenate([entering[ph], vb[pkey][ph]], axis=0))

    for r in range(max(len(units) for _, units in chains)):
        live = [key for key in flat if key[1] == r]
        for key in live:
            queries = (unit[key][0] * jnp.exp2(cum[key])).astype(bf16)
            carried[key] = (queries, [s.astype(bf16) for s in state[key[0]]])
            state[key[0]] = [whole[key][group_of[h]] * state[key[0]][h] + kv[key][h] for h in range(n_heads)]
        for key in live:
            q, k, _, lvl, rev = unit[key]
            ones = (lvl[0:1, :] >= -1).astype(bf16)
            for g, lanes in enumerate(groups):
                tabled = [sums_ref[index[key], i * CHUNK:(i + 1) * CHUNK, lanes] for i in range(plan.table_blocks)]
                group_scores = _in_chunk_scores(q[:, lanes], k[:, lanes], cum[key][:, lanes], tabled, lvl, ones, rev,
                                                masks, plan)
                for sub, sc in enumerate(group_scores):
                    if pending is not None:
                        finish(pending)
                    pending = (key, g * heads_per_group + sub, sc)
            if upcoming:
                u, (g_split, table) = upcoming.pop(0)
                upcoming_ref[u, 0:plan.row_total + 8, :] = _decay_sums(g_split, table)
    for u, (g_split, table) in upcoming:
        upcoming_ref[u, 0:plan.row_total + 8, :] = _decay_sums(g_split, table)
    finish(pending)
    return [[outs[(c, r)] for r in range(len(units))] for c, (_, units) in enumerate(chains)], state


def _gated_scans(q_s, k_refs, v_s, g_refs, s_scr, sums_scr, table_ref, lvl_ref, o_scr, seq, heads_per_group, plan):
    n = seq // CHUNK
    n_seq = STEP_TOKENS // seq
    n_heads = PAIR * heads_per_group
    units = GATED_HEADS_PER_ITER // n_heads
    per_iter = min(n, units // 2)
    seqs_per_iter = min(n_seq, units // (2 * per_iter))
    assert n % per_iter == 0 and n_seq % seqs_per_iter == 0
    iters_per_seq = n // per_iter
    n_iters = (n_seq // seqs_per_iter) * iters_per_seq

    def layout(it):
        jj, i = it // iters_per_seq, it % iters_per_seq
        chains = []
        for js in range(seqs_per_iter):
            j = jj * seqs_per_iter + js
            for d in range(2):
                steps = [i * per_iter + r for r in range(per_iter)]
                chains.append((j, d, [_chunk_rows(j * seq, n - 1 - t if d else t) for t in steps]))
        return chains

    def sums_inputs(it):
        return [(g_refs[d][rw, :], table_ref[d]) for _, d, rws in layout(it) for rw in rws]

    for u, (g_split, table) in enumerate(sums_inputs(0)):
        sums_scr[0, u, 0:plan.row_total + 8, :] = _decay_sums(g_split, table)

    def iteration(it, slot):
        chains = layout(it)
        args = [([s_scr[j, d, h] for h in range(n_heads)],
                 [(q_s[rw, :], k_refs[d][rw, :], v_s[rw, :], lvl_ref[d], bool(d)) for rw in rws])
                for j, d, rws in chains]
        upcoming = sums_inputs(jnp.minimum(it + 1, n_iters - 1))
        outs, new_states = _gated_chunks(args, sums_scr.at[slot], upcoming, sums_scr.at[1 - slot], heads_per_group,
                                         plan)
        for c, (j, d, rws) in enumerate(chains):
            for h in range(n_heads):
                for r, rw in enumerate(rws):
                    o_scr[d, h, rw, :] = outs[c][r][h]
                s_scr[j, d, h] = new_states[c][h]

    assert n_iters % 2 == 0

    def body(it2, carry):
        iteration(2 * it2, 0)
        iteration(2 * it2 + 1, 1)
        return carry
    lax.fori_loop(0, n_iters // 2, body, 0)


def _even_kernel(*refs, seq, has_state, emit_state, final):
    it = iter(refs)
    x_ref, mod_ref, nw_ref = next(it), next(it), next(it)
    win_ref, wlow_ref, wo_ref = next(it), next(it), next(it)
    gkw_ref, gkb_ref, lb_ref, gnw_ref, fnw_ref = next(it), next(it), next(it), next(it), next(it)
    plan_tables = {ROBUST_PLAN: (next(it), next(it)), BLOCK_PLAN: (next(it), next(it))}
    s0a_ref, s0b_ref = (next(it), next(it)) if has_state else (None, None)
    out_ref = next(it)
    sta_ref, stb_ref = (next(it), next(it)) if emit_state else (None, None)
    h_scr, low_scr, o_scr, gate_s, q_s, v_s, kf_s, kb_s, gf_s, gb_s, s_scr, sums_scr, bound_s = it
    n_seq = STEP_TOKENS // seq
    n_tiles = STEP_TOKENS // ROW_TILE

    _modulated_norm(x_ref, mod_ref, nw_ref, h_scr)

    def low_body(i, carry):
        rows = _row_tile(i)
        low_scr[rows, :] = _dot(h_scr[rows, :], wlow_ref[...]).astype(bf16)
        return carry
    lax.fori_loop(0, n_tiles, low_body, 0)

    width = PAIR * HEAD

    def run_scans(first_head, s0_ref, st_ref, h0, heads_per_group, k_refs, block_decays, first=False, last=False):
        bound_s[0] = (functools.reduce(jnp.minimum, block_decays) >= -BLOCK_DECAY_LIMIT).astype(jnp.int32)
        n_heads = PAIR * heads_per_group
        key_rows = HEAD // heads_per_group
        own_rows = [pl.ds((h % heads_per_group) * key_rows, key_rows) for h in range(n_heads)]
        for j in range(n_seq):
            for d in range(2):
                for h in range(n_heads):
                    if s0_ref is None or heads_per_group > 1:
                        s_scr[j, d, h] = jnp.zeros((HEAD, HEAD), f32)
                    if s0_ref is not None:
                        s_scr[j, d, h, own_rows[h], :] = s0_ref[j, d, h0 + h]
        def scans(plan):
            table_ref, lvl_ref = plan_tables[plan]
            _gated_scans(q_s, k_refs, v_s, (gf_s, gb_s), s_scr, sums_scr, table_ref, lvl_ref, o_scr, seq,
                         heads_per_group, plan)
        lax.cond(bound_s[0] == 1, functools.partial(scans, BLOCK_PLAN), functools.partial(scans, ROBUST_PLAN))
        if st_ref is not None:
            for j in range(n_seq):
                for d in range(2):
                    for h in range(n_heads):
                        st_ref[j, d, h0 + h] = s_scr[j, d, h, own_rows[h], :]
        _add_heads_output(first_head, n_heads, gate_s, o_scr, gnw_ref, wo_ref, out_ref, first,
                          (x_ref, mod_ref, fnw_ref, final) if last else None)

    def hgrn_pair(pp):
        lb = lb_ref[:, _pair_lanes(pp)]
        log_lb = jnp.log(lb)

        tiles = [pl.ds(i * ROW_TILE, ROW_TILE) for i in range(n_tiles)]
        projected = [_project(h_scr[rows, :], win_ref, pp, HGRN_COLS) for rows in tiles]
        block_decays = []
        for rows, (query, value, forget_f, forget_b, gate) in zip(tiles, projected):
            gate_s[rows, 0:width] = _silu(gate).astype(bf16)
            q_s[rows, :] = _silu(query)
            v_s[rows, 0:width] = value.astype(bf16)
            for a, k_s, g_s in ((forget_f, kf_s, gf_s), (forget_b, kb_s, gb_s)):
                z = log_lb - a
                u = jnp.exp(-jnp.abs(a))
                w = jnp.exp(-jnp.abs(z))
                r = 1.0 / (1.0 + u)
                log_f = jnp.maximum(z, 0.0) + jnp.minimum(a, 0.0) + jnp.log((1.0 + w) * r)
                block_decays.append(_store_log2_split(g_s, rows, log_f))
                k_s[rows, :] = (1.0 - lb) * jnp.where(a >= 0.0, u * r, r)
        run_scans(PAIR * pp, s0a_ref, sta_ref, PAIR * pp, 1, (kf_s, kb_s), block_decays, first=pp == 0)

    for pp in range(N_HGRN // PAIR):
        hgrn_pair(pp)

    assert N_GLA == PAIR * GLA_HEADS_PER_GROUP

    def gla_project(rows):
        h = h_scr[rows, :]
        narrow = [_dot(h, win_ref[:, c0:c0 + N_GLA * GLA_DK]) for c0 in GLA_QK_COLS]
        wide = [_dot(h, win_ref[:, c0:c0 + N_GLA * HEAD]) for c0 in GLA_COLS]
        return narrow + wide

    tiles = [pl.ds(i * ROW_TILE, ROW_TILE) for i in range(n_tiles)]
    projected = [gla_project(rows) for rows in tiles]
    block_decays = []
    for rows, (query, key, value, gate) in zip(tiles, projected):
        gate_s[rows, :] = _silu(gate).astype(bf16)
        q_s[rows, :] = query * (GLA_DK ** -0.5)
        kf_s[rows, :] = key
        v_s[rows, :] = value.astype(bf16)
        low = low_scr[rows, :]
        for d, g_s in enumerate((gf_s, gb_s)):
            logits = _dot(low, gkw_ref[d]) + gkb_ref[d]
            log_gate = jnp.minimum(logits, 0.0) - jnp.log(1.0 + jnp.exp(-jnp.abs(logits)))
            block_decays.append(_store_log2_split(g_s, rows, log_gate * (1.0 / GLA_GATE_NORM)))
    run_scans(N_HGRN, s0b_ref, stb_ref, 0, GLA_HEADS_PER_GROUP, (kf_s, kf_s), block_decays, last=True)


def _odd_kernel(*refs, seq, has_state, emit_state, use_rope, final):
    it = iter(refs)
    lg_ref = next(it)
    x_ref, mod_ref, nw_ref = next(it), next(it), next(it)
    wr_ref, wo_ref, gnw_ref, fnw_ref = next(it), next(it), next(it), next(it)
    cos_ref, sin_ref = (next(it), next(it)) if use_rope else (None, None)
    s0_ref = next(it) if has_state else None
    out_ref = next(it)
    st_ref = next(it) if emit_state else None
    h_scr, o_scr, gate_s, q_s, kt_s, v_s, s_scr, dec_scr = it
    n_seq = STEP_TOKENS // seq
    n_tiles = STEP_TOKENS // ROW_TILE
    n_chunks = seq // CHUNK
    per_iter = min(n_chunks, RET_UNITS_PER_ITER // n_seq)
    assert n_chunks % per_iter == 0

    _modulated_norm(x_ref, mod_ref, nw_ref, h_scr)
    t_idx = lax.broadcasted_iota(jnp.int32, (CHUNK, CHUNK), 0)
    s_idx = lax.broadcasted_iota(jnp.int32, (CHUNK, CHUNK), 1)
    row_f = lax.broadcasted_iota(jnp.int32, (CHUNK, HEAD), 0).astype(f32)
    col_f = lax.broadcasted_iota(jnp.int32, (8, CHUNK), 1).astype(f32)
    chunk_len = jnp.full((8, HEAD), CHUNK, f32)
    if use_rope:
        lane = lax.broadcasted_iota(jnp.int32, (ROW_TILE, HEAD), 1)
        first_quarter = (lane // (HEAD // 4)) % 2 == 0

    def rope(x, cos, sin_signed):
        xr = jnp.where(first_quarter, pltpu.roll(x, HEAD - HEAD // 4, axis=1), pltpu.roll(x, HEAD // 4, axis=1))
        return x * cos + xr * sin_signed

    width = PAIR * HEAD

    def pair_body(pp):
        def proj(i, c):
            rows = _row_tile(i)
            q, k, value, gate = _project(h_scr[rows, :], wr_ref, pp, RET_COLS)
            gate_s[rows, :] = _silu(gate).astype(bf16)
            k = k * (HEAD ** -0.5)
            if use_rope:
                cos, sin_signed = cos_ref[rows, :], sin_ref[rows, :]
                heads = [slice(h * HEAD, (h + 1) * HEAD) for h in range(PAIR)]
                q = jnp.concatenate([rope(q[:, lanes], cos, sin_signed) for lanes in heads], axis=-1)
                k = jnp.concatenate([rope(k[:, lanes], cos, sin_signed) for lanes in heads], axis=-1)
            q_s[rows, :] = q.astype(bf16)
            kt_s[:, rows] = k.T
            v_s[rows, :] = value.astype(bf16)
            return c
        lax.fori_loop(0, n_tiles, proj, 0)
        for h in range(PAIR):
            scan_head(PAIR * pp + h, h)
        last = pp == N_RET // PAIR - 1
        _add_heads_output(PAIR * pp, PAIR, gate_s, o_scr, gnw_ref, wo_ref, out_ref, pp == 0,
                          (x_ref, mod_ref, fnw_ref, final) if last else None)

    def scan_head(hh, h):
        lanes = slice(h * HEAD, (h + 1) * HEAD)
        lg_f = lg_ref[0, hh]
        lg_b = lg_ref[1, hh]
        for j in range(n_seq):
            for d in range(2):
                s_scr[j, d] = s0_ref[j, d, hh] if has_state else jnp.zeros((HEAD, HEAD), f32)

        dist = (t_idx - s_idx).astype(f32)
        dec_scr[0] = jnp.exp(lg_f * (row_f + 1.0))
        dec_scr[1] = jnp.exp(lg_b * (CHUNK - row_f))
        dec_scr[2] = (jnp.where(t_idx >= s_idx, jnp.exp(lg_f * jnp.maximum(dist, 0.0)), 0.0)
                      + jnp.where(s_idx >= t_idx, jnp.exp(lg_b * jnp.maximum(-dist, 0.0)), 0.0))
        dec_scr[3, 0:8, :] = jnp.exp(lg_f * (CHUNK - 1.0 - col_f))
        dec_scr[3, 8:16, :] = jnp.exp(lg_b * col_f)
        dec_scr[3, 16:24, :] = jnp.exp(lg_f * chunk_len)
        dec_scr[3, 24:32, :] = jnp.exp(lg_b * chunk_len)

        def body(i, c):
            units = [(j, r) for j in range(n_seq) for r in range(per_iter)]
            rows = {(j, r, d): _chunk_rows(j * seq, n_chunks - 1 - (i * per_iter + r) if d else i * per_iter + r)
                    for j, r in units for d in range(2)}
            scores = {u: _dot(q_s[rows[u + (0,)], lanes], kt_s[lanes, rows[u + (0,)]].astype(bf16)) for u in units}
            kv = {}
            for j, r in units:
                for d in range(2):
                    rw = rows[(j, r, d)]
                    keys = (kt_s[lanes, rw] * dec_scr[3, 8 * d:8 * d + 1, :]).astype(bf16)
                    kv[(j, r, d)] = _dot(keys, v_s[rw, lanes])
            state = {(j, d): s_scr[j, d] for j in range(n_seq) for d in range(2)}
            carried = {}
            for r in range(per_iter):
                for j in range(n_seq):
                    for d in range(2):
                        carried[(j, r, d)] = _dot(q_s[rows[(j, r, d)], lanes], state[(j, d)].astype(bf16))
                        state[(j, d)] = dec_scr[3, 16 + 8 * d:17 + 8 * d, :] * state[(j, d)] + kv[(j, r, d)]
            for j, r in units:
                rw = rows[(j, r, 0)]
                inside = _dot((scores[(j, r)] * dec_scr[2]).astype(bf16), v_s[rw, lanes])
                o_scr[0, h, rw, :] = inside + dec_scr[0] * carried[(j, r, 0)]
                o_scr[1, h, rows[(j, r, 1)], :] = dec_scr[1] * carried[(j, r, 1)]
            for (j, d), s in state.items():
                s_scr[j, d] = s
            return c
        lax.fori_loop(0, n_chunks // per_iter, body, 0)
        if emit_state:
            for j in range(n_seq):
                for d in range(2):
                    st_ref[j, d, hh] = s_scr[j, d]

    for pp in range(N_RET // PAIR):
        pair_body(pp)


def _const_spec(shape):
    zeros = (0,) * len(shape)
    return pl.BlockSpec(shape, lambda i: zeros, pipeline_mode=pl.Buffered(1))


def _step_spec(shape, per_step, buffers=None):
    zeros = (0,) * (len(shape) - 1)
    mode = {} if buffers is None else {"pipeline_mode": pl.Buffered(buffers)}
    return pl.BlockSpec((per_step,) + tuple(shape[1:]), lambda i: (i,) + zeros, **mode)


def _mod_spec(d, per_sequence):
    if per_sequence:
        return pl.BlockSpec((1, 3, d), lambda i: (i + 1, 0, 0))
    return pl.BlockSpec((1, 3, d), lambda i: (0, 0, 0))


def _layer_call(body, x, mod, consts, states, state_shapes, per_sequence_mod, scratch, name, smem_inputs=(),
                state_buffers=None):
    n_seq, seq, d = x.shape
    per_step = STEP_TOKENS // seq
    assert per_step * seq == STEP_TOKENS and n_seq % per_step == 0 and seq % CHUNK == 0
    assert not per_sequence_mod or per_step == 1
    n_steps = n_seq // per_step
    xs = x.reshape(n_steps, STEP_TOKENS, d)
    inputs = list(smem_inputs) + [xs, mod] + list(consts) + list(states)
    in_specs = [pl.BlockSpec(memory_space=pltpu.SMEM)] * len(smem_inputs)
    in_specs += [_step_spec(xs.shape, 1), _mod_spec(d, per_sequence_mod)]
    in_specs += [_const_spec(a.shape) for a in consts]
    in_specs += [_step_spec(s.shape, per_step) for s in states]
    out_shape = [jax.ShapeDtypeStruct(xs.shape, f32)] + [jax.ShapeDtypeStruct(s, f32) for s in state_shapes]
    out_specs = [_step_spec(xs.shape, 1)] + [_step_spec(s, per_step, buffers=state_buffers) for s in state_shapes]
    outs = pl.pallas_call(
        body,
        grid=(n_steps,),
        in_specs=in_specs,
        out_specs=out_specs,
        out_shape=out_shape,
        scratch_shapes=scratch,
        compiler_params=pltpu.CompilerParams(dimension_semantics=("arbitrary",), vmem_limit_bytes=VMEM_LIMIT_BYTES),
        name=name,
    )(*inputs)
    return [outs[0].reshape(x.shape)] + list(outs[1:])


def _even_layer(x, mod, norm_w, w, final_w, tables, states, emit_state, per_sequence_mod, final):
    n_seq, seq, d = x.shape
    consts = [norm_w, w["win"], w["wlow"], w["wout"], w["gkw"], w["gkb"], w["lb"], w["gnw"],
              final_w] + list(tables)
    state_shapes = [(n_seq, 2, N_HGRN, HEAD, HEAD), (n_seq, 2, N_GLA, GLA_DK, HEAD)] if emit_state else []
    scan_heads = max(PAIR, N_GLA)
    scratch = [
        pltpu.VMEM((STEP_TOKENS, d), bf16),
        pltpu.VMEM((STEP_TOKENS, HEAD), bf16),
        pltpu.VMEM((2, scan_heads, STEP_TOKENS, HEAD), f32),
        pltpu.VMEM((STEP_TOKENS, scan_heads * HEAD), bf16),
        pltpu.VMEM((STEP_TOKENS, PAIR * HEAD), f32),
        pltpu.VMEM((STEP_TOKENS, scan_heads * HEAD), bf16),
        pltpu.VMEM((STEP_TOKENS, PAIR * HEAD), f32),
        pltpu.VMEM((STEP_TOKENS, PAIR * HEAD), f32),
        pltpu.VMEM((STEP_TOKENS, 2 * PAIR * HEAD), bf16),
        pltpu.VMEM((STEP_TOKENS, 2 * PAIR * HEAD), bf16),
        pltpu.VMEM((STEP_TOKENS // seq, 2, scan_heads, HEAD, HEAD), f32),
        pltpu.VMEM((2, GATED_UNITS_MAX, TABLE_ROWS_MAX, PAIR * HEAD), f32),
        pltpu.SMEM((1,), jnp.int32),
    ]
    body = functools.partial(_even_kernel, seq=seq, has_state=states is not None, emit_state=emit_state, final=final)
    return _layer_call(body, x, mod, consts, states or (), state_shapes, per_sequence_mod, scratch,
                       "even_layer_seq%d" % seq, state_buffers=1)


def _odd_layer(x, mod, norm_w, w, final_w, log_decay, rope, state, emit_state, per_sequence_mod, final):
    n_seq, seq, d = x.shape
    consts = [norm_w, w["wr"], w["wout"], w["gnw"], final_w] + list(rope or ())
    state_shapes = [(n_seq, 2, N_RET, HEAD, HEAD)] if emit_state else []
    scratch = [
        pltpu.VMEM((STEP_TOKENS, d), bf16),
        pltpu.VMEM((2, PAIR, STEP_TOKENS, HEAD), f32),
        pltpu.VMEM((STEP_TOKENS, PAIR * HEAD), bf16),
        pltpu.VMEM((STEP_TOKENS, PAIR * HEAD), bf16),
        pltpu.VMEM((PAIR * HEAD, STEP_TOKENS), f32),
        pltpu.VMEM((STEP_TOKENS, PAIR * HEAD), bf16),
        pltpu.VMEM((STEP_TOKENS // seq, 2, HEAD, HEAD), f32),
        pltpu.VMEM((4, CHUNK, HEAD), f32),
    ]
    body = functools.partial(_odd_kernel, seq=seq, has_state=state is not None, emit_state=emit_state,
                             use_rope=rope is not None, final=final)
    return _layer_call(body, x, mod, consts, () if state is None else (state,), state_shapes, per_sequence_mod,
                       scratch, "odd_layer_seq%d" % seq, smem_inputs=(log_decay,))


def _even_weights(w_in, gk_w, gk_b, lb, gn_w, w_out):
    c0 = GLA_LOW_COLS[0]
    wlow = jnp.pad(w_in[:, c0:c0 + 2 * GLA_RANK], ((0, 0), (0, HEAD - 2 * GLA_RANK)))
    gkw = jnp.stack([jnp.pad(gk_w[d], ((d * GLA_RANK, HEAD - (d + 1) * GLA_RANK), (0, 0))) for d in range(2)])
    return {"win": w_in.astype(bf16), "wlow": wlow.astype(bf16), "gkw": gkw.astype(bf16),
            "gkb": gk_b.reshape(2, 1, -1), "lb": lb.reshape(1, -1), "gnw": gn_w.reshape(1, -1),
            "wout": w_out.astype(bf16)}


def _odd_weights(w_in, gn_w, w_out):
    return {"wr": w_in.astype(bf16), "gnw": gn_w.reshape(1, -1), "wout": w_out.astype(bf16)}


def _rope_tables(seq):
    rows = seq // GRID_W
    t_row = jnp.repeat(jnp.arange(rows), GRID_W).astype(f32)
    t_col = jnp.tile(jnp.arange(GRID_W), rows).astype(f32)
    half = HEAD // 2
    inv = ROPE_BASE ** (-jnp.arange(0, half, 2, dtype=f32) / half)
    ang_r = t_row[:, None] * inv
    ang_c = t_col[:, None] * inv
    ang = jnp.concatenate([ang_r, ang_r, ang_c, ang_c], axis=-1)
    sign = jnp.where((jnp.arange(HEAD) // (HEAD // 4)) % 2 == 0, -1.0, 1.0).astype(f32)
    return jnp.cos(ang), jnp.sin(ang) * sign


def kernel(x_prompt, x_sample, state_hgrn, state_gla, state_ret, c, c_ctx, norm_w, ada_w, ada_b, w_in_even, hgrn_lb, gla_gk_w, gla_gk_b, gn_even, w_out_even, w_in_odd, ret_decay, gn_odd, w_out_odd, final_norm_w):
    depth, d = norm_w.shape
    n_lat = x_sample.shape[0]
    n_cond = -(-(1 + n_lat) // 8) * 8
    cond = jnp.zeros((n_cond, d), f32).at[0].set(c_ctx).at[1:1 + n_lat].set(c)
    mod = _modulation(cond, ada_w, ada_b).reshape(depth, n_cond, 3, d)
    lbs = jnp.cumsum(jax.nn.softmax(hgrn_lb.astype(f32), axis=0), axis=0)
    final_w = final_norm_w.reshape(1, d)
    rope = _rope_tables(x_sample.shape[1])
    tables = _scan_tables(ROBUST_PLAN) + _scan_tables(BLOCK_PLAN)

    x_c, x_l = x_prompt, x_sample
    new_hgrn, new_gla, new_ret = [], [], []
    for l in range(depth):
        i = l // 2
        final = l == depth - 1
        nw = norm_w[l].reshape(1, d)
        if l % 2 == 0:
            w = _even_weights(w_in_even[i], gla_gk_w[i], gla_gk_b[i], lbs[i], gn_even[i], w_out_even[i])
            x_c, st_a, st_b = _even_layer(x_c, mod[l], nw, w, final_w, tables, None, True, False, final)
            (x_l,) = _even_layer(x_l, mod[l], nw, w, final_w, tables, (state_hgrn[:, i], state_gla[:, i]),
                                 False, True, final)
            new_hgrn.append(st_a)
            new_gla.append(st_b)
        else:
            w = _odd_weights(w_in_odd[i], gn_odd[i], w_out_odd[i])
            log_decay = jax.nn.log_sigmoid(ret_decay[i].astype(f32))
            x_c, st_c = _odd_layer(x_c, mod[l], nw, w, final_w, log_decay, None, None, True, False, final)
            (x_l,) = _odd_layer(x_l, mod[l], nw, w, final_w, log_decay, rope, state_ret[:, i], False, True, final)
            new_ret.append(st_c)
    def stacked(states):
        return states[0][:, None] if len(states) == 1 else jnp.stack(states, axis=1)
    return (x_c, x_l, stacked(new_hgrn), stacked(new_gla), stacked(new_ret))
```

```python
import functools
from typing import NamedTuple

import numpy as np
import jax
import jax.numpy as jnp
from jax import lax
from jax.experimental import pallas as pl
from jax.experimental.pallas import tpu as pltpu

f32 = jnp.float32
bf16 = jnp.bfloat16
HIGHEST = lax.Precision.HIGHEST

EPS = 1e-6
LOG2E = 1.4426950408889634
HEAD = 128
N_HGRN = 4
N_GLA = 4
GLA_DK = 64
N_RET = 8
N_HEADS = 8
GLA_RANK = 16
GLA_GATE_NORM = 16.0
GRID_W = 64
ROPE_BASE = 10000.0

EVEN_SPLITS = (N_HGRN * HEAD,) * 5 + (N_GLA * GLA_DK,) * 2 + (N_GLA * HEAD,) * 2 + (GLA_RANK,) * 2
EVEN_STARTS = tuple(int(c) for c in np.cumsum((0,) + EVEN_SPLITS[:-1]))
HGRN_COLS = EVEN_STARTS[0:5]
GLA_QK_COLS = EVEN_STARTS[5:7]
GLA_COLS = EVEN_STARTS[7:9]
GLA_LOW_COLS = EVEN_STARTS[9:11]
RET_COLS = tuple(i * N_RET * HEAD for i in range(4))

CHUNK = 128


class _ScanPlan(NamedTuple):
    block: int
    table_levels: tuple
    direct_levels: tuple

    @property
    def table_blocks(self):
        return (1 if self.block else 0) + len(self.table_levels)

    @property
    def row_total(self):
        return (self.table_blocks + 1) * CHUNK

    @property
    def diag_id(self):
        return len(self.table_levels) + len(self.direct_levels)


ROBUST_PLAN = _ScanPlan(0, (1, 2, 4), (8, 16, 32, 64))
BLOCK_PLAN = _ScanPlan(16, (), (16, 32, 64))
BLOCK_DECAY_LIMIT = 90.0
TABLE_ROWS_MAX = max(p.row_total for p in (ROBUST_PLAN, BLOCK_PLAN)) + 8
PAIR = 2
GATED_HEADS_PER_ITER = 8
GATED_UNITS_MAX = 4
GLA_HEADS_PER_GROUP = HEAD // GLA_DK
RET_UNITS_PER_ITER = 8
STEP_TOKENS = 1024
ROW_TILE = 512
MOD_COLS = 768
VMEM_LIMIT_BYTES = 61 * 1024 * 1024


def _dot(a, b, precision=None):
    return jnp.dot(a, b, precision=precision, preferred_element_type=f32)


def _dot_tn(a, b):
    return lax.dot_general(a, b, (((0,), (0,)), ((), ())), preferred_element_type=f32)


def _silu(x):
    return x * jax.nn.sigmoid(x)


def _row_tile(i):
    return pl.ds(pl.multiple_of(i * ROW_TILE, ROW_TILE), ROW_TILE)


def _chunk_rows(seq_start, c):
    return pl.ds(pl.multiple_of(seq_start + c * CHUNK, CHUNK), CHUNK)


def _mod_kernel(cond_ref, w_ref, b_ref, o_ref):
    o_ref[0] = _dot(_silu(cond_ref[...]), w_ref[0], HIGHEST) + b_ref[0]


def _modulation(cond, ada_w, ada_b):
    depth, d, d3 = ada_w.shape
    rows = cond.shape[0]
    return pl.pallas_call(
        _mod_kernel,
        grid=(depth, d3 // MOD_COLS),
        in_specs=[
            pl.BlockSpec((rows, d), lambda l, j: (0, 0)),
            pl.BlockSpec((1, d, MOD_COLS), lambda l, j: (l, 0, j)),
            pl.BlockSpec((1, 1, MOD_COLS), lambda l, j: (l, 0, j)),
        ],
        out_specs=pl.BlockSpec((1, rows, MOD_COLS), lambda l, j: (l, 0, j)),
        out_shape=jax.ShapeDtypeStruct((depth, rows, d3), f32),
        compiler_params=pltpu.CompilerParams(dimension_semantics=("arbitrary", "arbitrary")),
        name="modulation",
    )(cond, ada_w, ada_b.reshape(depth, 1, d3))


def _modulated_norm(x_ref, mod_ref, nw_ref, h_scr):
    def body(i, carry):
        rows = _row_tile(i)
        x = x_ref[0, rows, :]
        y = x * lax.rsqrt(jnp.mean(x * x, axis=-1, keepdims=True) + EPS) * nw_ref[...]
        h_scr[rows, :] = (y * (1.0 + mod_ref[0, 1:2, :]) + mod_ref[0, 0:1, :]).astype(bf16)
        return carry
    lax.fori_loop(0, STEP_TOKENS // ROW_TILE, body, 0)


def _aligned_slice(start, width):
    return pl.ds(start if isinstance(start, int) else pl.multiple_of(start, width), width)


def _pair_lanes(pair):
    return _aligned_slice(pair * (PAIR * HEAD), PAIR * HEAD)


def _project(h, w_ref, pair, first_cols):
    width = PAIR * HEAD
    return [_dot(h, w_ref[:, _aligned_slice(c0 + pair * width, width)]) for c0 in first_cols]


def _add_heads_output(first_head, n_heads, gate_s, o_scr, gnw_ref, wo_ref, out_ref, first, finish):
    width = n_heads * HEAD
    lanes = _aligned_slice(first_head * HEAD, width)
    tiles = []
    for i in range(STEP_TOKENS // ROW_TILE):
        rows = pl.ds(i * ROW_TILE, ROW_TILE)
        parts = []
        for h in range(n_heads):
            o = o_scr[0, h, rows, :] + o_scr[1, h, rows, :]
            parts.append(o * lax.rsqrt(jnp.mean(o * o, axis=-1, keepdims=True) + EPS))
        y = jnp.concatenate(parts, axis=-1) * gnw_ref[:, lanes]
        tiles.append((rows, (y * gate_s[rows, 0:width].astype(f32)).astype(bf16)))
    products = [(rows, _dot(z, wo_ref[lanes, :])) for rows, z in tiles]
    for rows, product in products:
        mixed = product if first else out_ref[0, rows, :] + product
        if finish is not None:
            x_ref, mod_ref, fnw_ref, final = finish
            mixed = x_ref[0, rows, :] + mod_ref[0, 2:3, :] * mixed
            if final:
                mixed = mixed * lax.rsqrt(jnp.mean(mixed * mixed, axis=-1, keepdims=True) + EPS) * fnw_ref[...]
        out_ref[0, rows, :] = mixed


def _scan_tables(plan):
    t = np.arange(CHUNK)[:, None]
    j = np.arange(CHUNK)[None, :]
    fwd = []
    if plan.block:
        fwd.append((j <= t) & (j // plan.block == t // plan.block))
    for m in plan.table_levels:
        mid = (t // (2 * m)) * (2 * m) + m
        right = t >= mid
        fwd.append(np.where(right, (j >= mid) & (j <= t), (j > t) & (j < mid)))
    fwd.append(j <= t)
    fwd.append(np.ones((8, CHUNK), bool))
    fwd = np.concatenate(fwd, axis=0).astype(np.float32)
    bwd = fwd.copy()
    n_sym = plan.row_total // CHUNK
    bwd[:plan.row_total] = fwd[:plan.row_total].reshape(n_sym, CHUNK, CHUNK)[:, ::-1, ::-1].reshape(-1, CHUNK)
    table = np.stack([np.tile(fwd, (1, 2)), np.tile(bwd, (1, 2))])
    first = 1 if plan.block else 0
    ids = np.full((CHUNK, CHUNK), -1, np.int32)
    for i, m in enumerate(plan.table_levels + plan.direct_levels):
        ids = np.where((t > j) & ((t ^ j) >= m) & ((t ^ j) < 2 * m), first + i, ids)
    if plan.block:
        ids = np.where((t >= j) & (t // plan.block == j // plan.block), 0, ids)
    else:
        ids = np.where(t == j, plan.diag_id, ids)
    return jnp.asarray(table, bf16), jnp.asarray(np.stack([ids, ids.T]).astype(np.int32))


def _in_chunk_scores(q, k, cum, tabled, lvl, ones, rev, key_masks, plan):
    qb = q.astype(bf16)
    q_heads = [qb if mask is None else qb * mask for mask in key_masks]
    kt = k.astype(bf16).T
    if plan.block:
        inside = tabled[0]
        grow, decay = jnp.exp2(-inside).astype(bf16), jnp.exp2(inside).astype(bf16)
        keys = kt * grow.T
        scores = [jnp.where(lvl == 0, _dot(qh * decay, keys), 0.0) for qh in q_heads]
        tabled = tabled[1:]
    else:
        diag_keys = kt * ones
        scores = [jnp.where(lvl == plan.diag_id, _dot(qh, diag_keys), 0.0) for qh in q_heads]
    first = 1 if plan.block else 0
    for i in range(len(plan.table_levels)):
        e = jnp.exp2(tabled[i]).astype(bf16)
        keys = kt * e.T
        scores = [jnp.where(lvl == first + i, _dot(qh * e, keys), sc) for qh, sc in zip(q_heads, scores)]
    for i, m in enumerate(plan.direct_levels, start=first + len(plan.table_levels)):
        blocks = []
        for p0 in range(0, CHUNK, 2 * m):
            left, right = slice(p0, p0 + m), slice(p0 + m, p0 + 2 * m)
            q_side, k_side = (left, right) if rev else (right, left)
            mid_row = p0 + m if rev else p0 + m - 1
            blocks.append((q_side, k_side, cum[mid_row:mid_row + 1, :]))
        q_decay = [jnp.exp2(cum[qs] - mid).astype(bf16) for qs, _, mid in blocks]
        k_decay = []
        for _, ks, mid in blocks:
            decay = jnp.exp2(mid - cum[ks]).astype(bf16)
            zero = jnp.zeros((m, HEAD), bf16)
            k_decay += [zero, decay] if rev else [decay, zero]
        keys = kt * jnp.concatenate(k_decay, axis=0).T
        for h, qh in enumerate(q_heads):
            s = _dot(jnp.concatenate([qh[qs] * e for (qs, _, _), e in zip(blocks, q_decay)], axis=0), keys)
            rows = []
            for b, (qs, ks, _) in enumerate(blocks):
                updated = jnp.where(lvl[qs, :] == i, s[b * m:(b + 1) * m, :], scores[h][qs, :])
                rows += [updated, scores[h][ks, :]] if rev else [scores[h][ks, :], updated]
            scores[h] = jnp.concatenate(rows, axis=0)
    return [sc.astype(bf16) for sc in scores]


def _store_log2_split(g_ref, rows, g):
    width = g.shape[-1]
    x = g * LOG2E
    hi = x.astype(bf16)
    g_ref[rows, 0:width] = hi
    g_ref[rows, width:2 * width] = (x - hi.astype(f32)).astype(bf16)
    return jnp.min(jnp.sum(x.reshape(-1, BLOCK_PLAN.block, width), axis=1))


def _decay_sums(g_split, table):
    width = g_split.shape[-1] // 2
    return _dot(table, jnp.concatenate([g_split[:, :width], g_split[:, width:]], axis=0))


def _key_masks(heads_per_group):
    if heads_per_group == 1:
        return [None]
    lane = lax.broadcasted_iota(jnp.int32, (1, HEAD), 1)
    return [(lane // (HEAD // heads_per_group) == sub).astype(bf16) for sub in range(heads_per_group)]


def _gated_chunks(chains, sums_ref, upcoming, upcoming_ref, heads_per_group, plan):
    masks = _key_masks(heads_per_group)
    groups = [slice(g * HEAD, (g + 1) * HEAD) for g in range(PAIR)]
    n_heads = PAIR * heads_per_group
    group_of = [h // heads_per_group for h in range(n_heads)]
    mask_of = [masks[h % heads_per_group] for h in range(n_heads)]
    flat = [(c, r) for c, (_, units) in enumerate(chains) for r in range(len(units))]
    unit = {(c, r): chains[c][1][r] for c, r in flat}
    index = {key: u for u, key in enumerate(flat)}
    cum_rows = slice(plan.table_blocks * CHUNK, (plan.table_blocks + 1) * CHUNK)
    cum = {key: sums_ref[index[key], cum_rows, :] for key in flat}
    total = {key: sums_ref[index[key], plan.row_total:plan.row_total + 1, :] for key in flat}
    vb = {key: [unit[key][2][:, h * HEAD:(h + 1) * HEAD] for h in range(n_heads)] for key in flat}
    upcoming = list(enumerate(upcoming))
    kv, whole = {}, {}
    for key in flat:
        k = unit[key][1]
        keys = (k * jnp.exp2(total[key] - cum[key])).astype(bf16)
        kv[key] = []
        for h in range(n_heads):
            own = keys[:, groups[group_of[h]]]
            kv[key].append(_dot_tn(own if mask_of[h] is None else own * mask_of[h], vb[key][h]))
        whole[key] = [jnp.broadcast_to(jnp.exp2(total[key][:, lanes]), (HEAD, HEAD)).T for lanes in groups]
    state = [list(states) for states, _ in chains]
    outs, pending, carried = {key: [None] * n_heads for key in flat}, None, {}

    def finish(done):
        pkey, ph, psc = done
        queries, entering = carried[pkey]
        lhs = jnp.concatenate([queries[:, groups[group_of[ph]]], psc], axis=1)
        outs[pkey][ph] = _dot(lhs, jnp.concatenate([entering[ph], vb[pkey][ph]], axis=0))

    for r in range(max(len(units) for _, units in chains)):
        live = [key for key in flat if key[1] == r]
        for key in live:
            queries = (unit[key][0] * jnp.exp2(cum[key])).astype(bf16)
            carried[key] = (queries, [s.astype(bf16) for s in state[key[0]]])
            state[key[0]] = [whole[key][group_of[h]] * state[key[0]][h] + kv[key][h] for h in range(n_heads)]
        for key in live:
            q, k, _, lvl, rev = unit[key]
            ones = (lvl[0:1, :] >= -1).astype(bf16)
            for g, lanes in enumerate(groups):
                tabled = [sums_ref[index[key], i * CHUNK:(i + 1) * CHUNK, lanes] for i in range(plan.table_blocks)]
                group_scores = _in_chunk_scores(q[:, lanes], k[:, lanes], cum[key][:, lanes], tabled, lvl, ones, rev,
                                                masks, plan)
                for sub, sc in enumerate(group_scores):
                    if pending is not None:
                        finish(pending)
                    pending = (key, g * heads_per_group + sub, sc)
            if upcoming:
                u, (g_split, table) = upcoming.pop(0)
                upcoming_ref[u, 0:plan.row_total + 8, :] = _decay_sums(g_split, table)
    for u, (g_split, table) in upcoming:
        upcoming_ref[u, 0:plan.row_total + 8, :] = _decay_sums(g_split, table)
    finish(pending)
    return [[outs[(c, r)] for r in range(len(units))] for c, (_, units) in enumerate(chains)], state


def _gated_scans(q_s, k_refs, v_s, g_refs, s_scr, sums_scr, table_ref, lvl_ref, o_scr, seq, heads_per_group, plan):
    n = seq // CHUNK
    n_seq = STEP_TOKENS // seq
    n_heads = PAIR * heads_per_group
    units = GATED_HEADS_PER_ITER // n_heads
    per_iter = min(n, units // 2)
    seqs_per_iter = min(n_seq, units // (2 * per_iter))
    assert n % per_iter == 0 and n_seq % seqs_per_iter == 0
    iters_per_seq = n // per_iter
    n_iters = (n_seq // seqs_per_iter) * iters_per_seq

    def layout(it):
        jj, i = it // iters_per_seq, it % iters_per_seq
        chains = []
        for js in range(seqs_per_iter):
            j = jj * seqs_per_iter + js
            for d in range(2):
                steps = [i * per_iter + r for r in range(per_iter)]
                chains.append((j, d, [_chunk_rows(j * seq, n - 1 - t if d else t) for t in steps]))
        return chains

    def sums_inputs(it):
        return [(g_refs[d][rw, :], table_ref[d]) for _, d, rws in layout(it) for rw in rws]

    for u, (g_split, table) in enumerate(sums_inputs(0)):
        sums_scr[0, u, 0:plan.row_total + 8, :] = _decay_sums(g_split, table)

    def iteration(it, slot):
        chains = layout(it)
        args = [([s_scr[j, d, h] for h in range(n_heads)],
                 [(q_s[rw, :], k_refs[d][rw, :], v_s[rw, :], lvl_ref[d], bool(d)) for rw in rws])
                for j, d, rws in chains]
        upcoming = sums_inputs(jnp.minimum(it + 1, n_iters - 1))
        outs, new_states = _gated_chunks(args, sums_scr.at[slot], upcoming, sums_scr.at[1 - slot], heads_per_group,
                                         plan)
        for c, (j, d, rws) in enumerate(chains):
            for h in range(n_heads):
                for r, rw in enumerate(rws):
                    o_scr[d, h, rw, :] = outs[c][r][h]
                s_scr[j, d, h] = new_states[c][h]

    assert n_iters % 2 == 0

    def body(it2, carry):
        iteration(2 * it2, 0)
        iteration(2 * it2 + 1, 1)
        return carry
    lax.fori_loop(0, n_iters // 2, body, 0)


def _even_kernel(*refs, seq, has_state, emit_state, final):
    it = iter(refs)
    x_ref, mod_ref, nw_ref = next(it), next(it), next(it)
    win_ref, wlow_ref, wo_ref = next(it), next(it), next(it)
    gkw_ref, gkb_ref, lb_ref, gnw_ref, fnw_ref = next(it), next(it), next(it), next(it), next(it)
    plan_tables = {ROBUST_PLAN: (next(it), next(it)), BLOCK_PLAN: (next(it), next(it))}
    s0a_ref, s0b_ref = (next(it), next(it)) if has_state else (None, None)
    out_ref = next(it)
    sta_ref, stb_ref = (next(it), next(it)) if emit_state else (None, None)
    h_scr, low_scr, o_scr, gate_s, q_s, v_s, kf_s, kb_s, gf_s, gb_s, s_scr, sums_scr, bound_s = it
    n_seq = STEP_TOKENS // seq
    n_tiles = STEP_TOKENS // ROW_TILE

    _modulated_norm(x_ref, mod_ref, nw_ref, h_scr)

    def low_body(i, carry):
        rows = _row_tile(i)
        low_scr[rows, :] = _dot(h_scr[rows, :], wlow_ref[...]).astype(bf16)
        return carry
    lax.fori_loop(0, n_tiles, low_body, 0)

    width = PAIR * HEAD

    def run_scans(first_head, s0_ref, st_ref, h0, heads_per_group, k_refs, block_decays, first=False, last=False):
        bound_s[0] = (functools.reduce(jnp.minimum, block_decays) >= -BLOCK_DECAY_LIMIT).astype(jnp.int32)
        n_heads = PAIR * heads_per_group
        key_rows = HEAD // heads_per_group
        own_rows = [pl.ds((h % heads_per_group) * key_rows, key_rows) for h in range(n_heads)]
        for j in range(n_seq):
            for d in range(2):
                for h in range(n_heads):
                    if s0_ref is None or heads_per_group > 1:
                        s_scr[j, d, h] = jnp.zeros((HEAD, HEAD), f32)
                    if s0_ref is not None:
                        s_scr[j, d, h, own_rows[h], :] = s0_ref[j, d, h0 + h]
        def scans(plan):
            table_ref, lvl_ref = plan_tables[plan]
            _gated_scans(q_s, k_refs, v_s, (gf_s, gb_s), s_scr, sums_scr, table_ref, lvl_ref, o_scr, seq,
                         heads_per_group, plan)
        lax.cond(bound_s[0] == 1, functools.partial(scans, BLOCK_PLAN), functools.partial(scans, ROBUST_PLAN))
        if st_ref is not None:
            for j in range(n_seq):
                for d in range(2):
                    for h in range(n_heads):
                        st_ref[j, d, h0 + h] = s_scr[j, d, h, own_rows[h], :]
        _add_heads_output(first_head, n_heads, gate_s, o_scr, gnw_ref, wo_ref, out_ref, first,
                          (x_ref, mod_ref, fnw_ref, final) if last else None)

    def hgrn_pair(pp, carry):
        lb = lb_ref[:, _pair_lanes(pp)]
        log_lb = jnp.log(lb)

        tiles = [pl.ds(i * ROW_TILE, ROW_TILE) for i in range(n_tiles)]
        projected = [_project(h_scr[rows, :], win_ref, pp, HGRN_COLS) for rows in tiles]
        block_decays = []
        for rows, (query, value, forget_f, forget_b, gate) in zip(tiles, projected):
            gate_s[rows, 0:width] = _silu(gate).astype(bf16)
            q_s[rows, :] = _silu(query)
            v_s[rows, 0:width] = value.astype(bf16)
            for a, k_s, g_s in ((forget_f, kf_s, gf_s), (forget_b, kb_s, gb_s)):
                z = log_lb - a
                u = jnp.exp(-jnp.abs(a))
                w = jnp.exp(-jnp.abs(z))
                r = 1.0 / (1.0 + u)
                log_f = jnp.maximum(z, 0.0) + jnp.minimum(a, 0.0) + jnp.log((1.0 + w) * r)
                block_decays.append(_store_log2_split(g_s, rows, log_f))
                k_s[rows, :] = (1.0 - lb) * jnp.where(a >= 0.0, u * r, r)
        run_scans(PAIR * pp, s0a_ref, sta_ref, PAIR * pp, 1, (kf_s, kb_s), block_decays)
        return carry

    def clear(i, carry):
        out_ref[0, _row_tile(i), :] = jnp.zeros((ROW_TILE, out_ref.shape[-1]), f32)
        return carry
    lax.fori_loop(0, n_tiles, clear, 0)
    lax.fori_loop(0, N_HGRN // PAIR, hgrn_pair, 0)

    assert N_GLA == PAIR * GLA_HEADS_PER_GROUP

    def gla_project(rows):
        h = h_scr[rows, :]
        narrow = [_dot(h, win_ref[:, c0:c0 + N_GLA * GLA_DK]) for c0 in GLA_QK_COLS]
        wide = [_dot(h, win_ref[:, c0:c0 + N_GLA * HEAD]) for c0 in GLA_COLS]
        return narrow + wide

    tiles = [pl.ds(i * ROW_TILE, ROW_TILE) for i in range(n_tiles)]
    projected = [gla_project(rows) for rows in tiles]
    block_decays = []
    for rows, (query, key, value, gate) in zip(tiles, projected):
        gate_s[rows, :] = _silu(gate).astype(bf16)
        q_s[rows, :] = query * (GLA_DK ** -0.5)
        kf_s[rows, :] = key
        v_s[rows, :] = value.astype(bf16)
        low = low_scr[rows, :]
        for d, g_s in enumerate((gf_s, gb_s)):
            logits = _dot(low, gkw_ref[d]) + gkb_ref[d]
            log_gate = jnp.minimum(logits, 0.0) - jnp.log(1.0 + jnp.exp(-jnp.abs(logits)))
            block_decays.append(_store_log2_split(g_s, rows, log_gate * (1.0 / GLA_GATE_NORM)))
    run_scans(N_HGRN, s0b_ref, stb_ref, 0, GLA_HEADS_PER_GROUP, (kf_s, kf_s), block_decays, last=True)


def _odd_kernel(*refs, seq, has_state, emit_state, use_rope, final):
    it = iter(refs)
    lg_ref = next(it)
    x_ref, mod_ref, nw_ref = next(it), next(it), next(it)
    wr_ref, wo_ref, gnw_ref, fnw_ref = next(it), next(it), next(it), next(it)
    cos_ref, sin_ref = (next(it), next(it)) if use_rope else (None, None)
    s0_ref = next(it) if has_state else None
    out_ref = next(it)
    st_ref = next(it) if emit_state else None
    h_scr, o_scr, gate_s, q_s, kt_s, v_s, s_scr, dec_scr = it
    n_seq = STEP_TOKENS // seq
    n_tiles = STEP_TOKENS // ROW_TILE
    n_chunks = seq // CHUNK
    per_iter = min(n_chunks, RET_UNITS_PER_ITER // n_seq)
    assert n_chunks % per_iter == 0

    _modulated_norm(x_ref, mod_ref, nw_ref, h_scr)
    t_idx = lax.broadcasted_iota(jnp.int32, (CHUNK, CHUNK), 0)
    s_idx = lax.broadcasted_iota(jnp.int32, (CHUNK, CHUNK), 1)
    row_f = lax.broadcasted_iota(jnp.int32, (CHUNK, HEAD), 0).astype(f32)
    col_f = lax.broadcasted_iota(jnp.int32, (8, CHUNK), 1).astype(f32)
    chunk_len = jnp.full((8, HEAD), CHUNK, f32)
    if use_rope:
        lane = lax.broadcasted_iota(jnp.int32, (ROW_TILE, HEAD), 1)
        first_quarter = (lane // (HEAD // 4)) % 2 == 0

    def rope(x, cos, sin_signed):
        xr = jnp.where(first_quarter, pltpu.roll(x, HEAD - HEAD // 4, axis=1), pltpu.roll(x, HEAD // 4, axis=1))
        return x * cos + xr * sin_signed

    width = PAIR * HEAD

    def pair_body(pp):
        def proj(i, c):
            rows = _row_tile(i)
            q, k, value, gate = _project(h_scr[rows, :], wr_ref, pp, RET_COLS)
            gate_s[rows, :] = _silu(gate).astype(bf16)
            k = k * (HEAD ** -0.5)
            if use_rope:
                cos, sin_signed = cos_ref[rows, :], sin_ref[rows, :]
                heads = [slice(h * HEAD, (h + 1) * HEAD) for h in range(PAIR)]
                q = jnp.concatenate([rope(q[:, lanes], cos, sin_signed) for lanes in heads], axis=-1)
                k = jnp.concatenate([rope(k[:, lanes], cos, sin_signed) for lanes in heads], axis=-1)
            q_s[rows, :] = q.astype(bf16)
            kt_s[:, rows] = k.T
            v_s[rows, :] = value.astype(bf16)
            return c
        lax.fori_loop(0, n_tiles, proj, 0)
        for h in range(PAIR):
            scan_head(PAIR * pp + h, h)
        last = pp == N_RET // PAIR - 1
        _add_heads_output(PAIR * pp, PAIR, gate_s, o_scr, gnw_ref, wo_ref, out_ref, pp == 0,
                          (x_ref, mod_ref, fnw_ref, final) if last else None)

    def scan_head(hh, h):
        lanes = slice(h * HEAD, (h + 1) * HEAD)
        lg_f = lg_ref[0, hh]
        lg_b = lg_ref[1, hh]
        for j in range(n_seq):
            for d in range(2):
                s_scr[j, d] = s0_ref[j, d, hh] if has_state else jnp.zeros((HEAD, HEAD), f32)

        dist = (t_idx - s_idx).astype(f32)
        dec_scr[0] = jnp.exp(lg_f * (row_f + 1.0))
        dec_scr[1] = jnp.exp(lg_b * (CHUNK - row_f))
        dec_scr[2] = (jnp.where(t_idx >= s_idx, jnp.exp(lg_f * jnp.maximum(dist, 0.0)), 0.0)
                      + jnp.where(s_idx >= t_idx, jnp.exp(lg_b * jnp.maximum(-dist, 0.0)), 0.0))
        dec_scr[3, 0:8, :] = jnp.exp(lg_f * (CHUNK - 1.0 - col_f))
        dec_scr[3, 8:16, :] = jnp.exp(lg_b * col_f)
        dec_scr[3, 16:24, :] = jnp.exp(lg_f * chunk_len)
        dec_scr[3, 24:32, :] = jnp.exp(lg_b * chunk_len)

        def body(i, c):
            units = [(j, r) for j in range(n_seq) for r in range(per_iter)]
            rows = {(j, r, d): _chunk_rows(j * seq, n_chunks - 1 - (i * per_iter + r) if d else i * per_iter + r)
                    for j, r in units for d in range(2)}
            scores = {u: _dot(q_s[rows[u + (0,)], lanes], kt_s[lanes, rows[u + (0,)]].astype(bf16)) for u in units}
            kv = {}
            for j, r in units:
                for d in range(2):
                    rw = rows[(j, r, d)]
                    keys = (kt_s[lanes, rw] * dec_scr[3, 8 * d:8 * d + 1, :]).astype(bf16)
                    kv[(j, r, d)] = _dot(keys, v_s[rw, lanes])
            state = {(j, d): s_scr[j, d] for j in range(n_seq) for d in range(2)}
            carried = {}
            for r in range(per_iter):
                for j in range(n_seq):
                    for d in range(2):
                        carried[(j, r, d)] = _dot(q_s[rows[(j, r, d)], lanes], state[(j, d)].astype(bf16))
                        state[(j, d)] = dec_scr[3, 16 + 8 * d:17 + 8 * d, :] * state[(j, d)] + kv[(j, r, d)]
            for j, r in units:
                rw = rows[(j, r, 0)]
                inside = _dot((scores[(j, r)] * dec_scr[2]).astype(bf16), v_s[rw, lanes])
                o_scr[0, h, rw, :] = inside + dec_scr[0] * carried[(j, r, 0)]
                o_scr[1, h, rows[(j, r, 1)], :] = dec_scr[1] * carried[(j, r, 1)]
            for (j, d), s in state.items():
                s_scr[j, d] = s
            return c
        lax.fori_loop(0, n_chunks // per_iter, body, 0)
        if emit_state:
            for j in range(n_seq):
                for d in range(2):
                    st_ref[j, d, hh] = s_scr[j, d]

    for pp in range(N_RET // PAIR):
        pair_body(pp)


def _const_spec(shape):
    zeros = (0,) * len(shape)
    return pl.BlockSpec(shape, lambda i: zeros, pipeline_mode=pl.Buffered(1))


def _step_spec(shape, per_step, buffers=None):
    zeros = (0,) * (len(shape) - 1)
    mode = {} if buffers is None else {"pipeline_mode": pl.Buffered(buffers)}
    return pl.BlockSpec((per_step,) + tuple(shape[1:]), lambda i: (i,) + zeros, **mode)


def _mod_spec(d, per_sequence):
    if per_sequence:
        return pl.BlockSpec((1, 3, d), lambda i: (i + 1, 0, 0))
    return pl.BlockSpec((1, 3, d), lambda i: (0, 0, 0))


def _layer_call(body, x, mod, consts, states, state_shapes, per_sequence_mod, scratch, name, smem_inputs=(),
                state_buffers=None):
    n_seq, seq, d = x.shape
    per_step = STEP_TOKENS // seq
    assert per_step * seq == STEP_TOKENS and n_seq % per_step == 0 and seq % CHUNK == 0
    assert not per_sequence_mod or per_step == 1
    n_steps = n_seq // per_step
    xs = x.reshape(n_steps, STEP_TOKENS, d)
    inputs = list(smem_inputs) + [xs, mod] + list(consts) + list(states)
    in_specs = [pl.BlockSpec(memory_space=pltpu.SMEM)] * len(smem_inputs)
    in_specs += [_step_spec(xs.shape, 1), _mod_spec(d, per_sequence_mod)]
    in_specs += [_const_spec(a.shape) for a in consts]
    in_specs += [_step_spec(s.shape, per_step) for s in states]
    out_shape = [jax.ShapeDtypeStruct(xs.shape, f32)] + [jax.ShapeDtypeStruct(s, f32) for s in state_shapes]
    out_specs = [_step_spec(xs.shape, 1)] + [_step_spec(s, per_step, buffers=state_buffers) for s in state_shapes]
    outs = pl.pallas_call(
        body,
        grid=(n_steps,),
        in_specs=in_specs,
        out_specs=out_specs,
        out_shape=out_shape,
        scratch_shapes=scratch,
        compiler_params=pltpu.CompilerParams(dimension_semantics=("arbitrary",), vmem_limit_bytes=VMEM_LIMIT_BYTES),
        name=name,
    )(*inputs)
    return [outs[0].reshape(x.shape)] + list(outs[1:])


def _even_layer(x, mod, norm_w, w, final_w, tables, states, emit_state, per_sequence_mod, final):
    n_seq, seq, d = x.shape
    consts = [norm_w, w["win"], w["wlow"], w["wout"], w["gkw"], w["gkb"], w["lb"], w["gnw"],
              final_w] + list(tables)
    state_shapes = [(n_seq, 2, N_HGRN, HEAD, HEAD), (n_seq, 2, N_GLA, GLA_DK, HEAD)] if emit_state else []
    scan_heads = max(PAIR, N_GLA)
    scratch = [
        pltpu.VMEM((STEP_TOKENS, d), bf16),
        pltpu.VMEM((STEP_TOKENS, HEAD), bf16),
        pltpu.VMEM((2, scan_heads, STEP_TOKENS, HEAD), f32),
        pltpu.VMEM((STEP_TOKENS, scan_heads * HEAD), bf16),
        pltpu.VMEM((STEP_TOKENS, PAIR * HEAD), f32),
        pltpu.VMEM((STEP_TOKENS, scan_heads * HEAD), bf16),
        pltpu.VMEM((STEP_TOKENS, PAIR * HEAD), f32),
        pltpu.VMEM((STEP_TOKENS, PAIR * HEAD), f32),
        pltpu.VMEM((STEP_TOKENS, 2 * PAIR * HEAD), bf16),
        pltpu.VMEM((STEP_TOKENS, 2 * PAIR * HEAD), bf16),
        pltpu.VMEM((STEP_TOKENS // seq, 2, scan_heads, HEAD, HEAD), f32),
        pltpu.VMEM((2, GATED_UNITS_MAX, TABLE_ROWS_MAX, PAIR * HEAD), f32),
        pltpu.SMEM((1,), jnp.int32),
    ]
    body = functools.partial(_even_kernel, seq=seq, has_state=states is not None, emit_state=emit_state, final=final)
    return _layer_call(body, x, mod, consts, states or (), state_shapes, per_sequence_mod, scratch,
                       "even_layer_seq%d" % seq, state_buffers=1)


def _odd_layer(x, mod, norm_w, w, final_w, log_decay, rope, state, emit_state, per_sequence_mod, final):
    n_seq, seq, d = x.shape
    consts = [norm_w, w["wr"], w["wout"], w["gnw"], final_w] + list(rope or ())
    state_shapes = [(n_seq, 2, N_RET, HEAD, HEAD)] if emit_state else []
    scratch = [
        pltpu.VMEM((STEP_TOKENS, d), bf16),
        pltpu.VMEM((2, PAIR, STEP_TOKENS, HEAD), f32),
        pltpu.VMEM((STEP_TOKENS, PAIR * HEAD), bf16),
        pltpu.VMEM((STEP_TOKENS, PAIR * HEAD), bf16),
        pltpu.VMEM((PAIR * HEAD, STEP_TOKENS), f32),
        pltpu.VMEM((STEP_TOKENS, PAIR * HEAD), bf16),
        pltpu.VMEM((STEP_TOKENS // seq, 2, HEAD, HEAD), f32),
        pltpu.VMEM((4, CHUNK, HEAD), f32),
    ]
    body = functools.partial(_odd_kernel, seq=seq, has_state=state is not None, emit_state=emit_state,
                             use_rope=rope is not None, final=final)
    return _layer_call(body, x, mod, consts, () if state is None else (state,), state_shapes, per_sequence_mod,
                       scratch, "odd_layer_seq%d" % seq, smem_inputs=(log_decay,))


def _even_weights(w_in, gk_w, gk_b, lb, gn_w, w_out):
    c0 = GLA_LOW_COLS[0]
    wlow = jnp.pad(w_in[:, c0:c0 + 2 * GLA_RANK], ((0, 0), (0, HEAD - 2 * GLA_RANK)))
    gkw = jnp.stack([jnp.pad(gk_w[d], ((d * GLA_RANK, HEAD - (d + 1) * GLA_RANK), (0, 0))) for d in range(2)])
    return {"win": w_in.astype(bf16), "wlow": wlow.astype(bf16), "gkw": gkw.astype(bf16),
            "gkb": gk_b.reshape(2, 1, -1), "lb": lb.reshape(1, -1), "gnw": gn_w.reshape(1, -1),
            "wout": w_out.astype(bf16)}


def _odd_weights(w_in, gn_w, w_out):
    return {"wr": w_in.astype(bf16), "gnw": gn_w.reshape(1, -1), "wout": w_out.astype(bf16)}


def _rope_tables(seq):
    rows = seq // GRID_W
    t_row = jnp.repeat(jnp.arange(rows), GRID_W).astype(f32)
    t_col = jnp.tile(jnp.arange(GRID_W), rows).astype(f32)
    half = HEAD // 2
    inv = ROPE_BASE ** (-jnp.arange(0, half, 2, dtype=f32) / half)
    ang_r = t_row[:, None] * inv
    ang_c = t_col[:, None] * inv
    ang = jnp.concatenate([ang_r, ang_r, ang_c, ang_c], axis=-1)
    sign = jnp.where((jnp.arange(HEAD) // (HEAD // 4)) % 2 == 0, -1.0, 1.0).astype(f32)
    return jnp.cos(ang), jnp.sin(ang) * sign


def kernel(x_prompt, x_sample, state_hgrn, state_gla, state_ret, c, c_ctx, norm_w, ada_w, ada_b, w_in_even, hgrn_lb, gla_gk_w, gla_gk_b, gn_even, w_out_even, w_in_odd, ret_decay, gn_odd, w_out_odd, final_norm_w):
    depth, d = norm_w.shape
    n_lat = x_sample.shape[0]
    n_cond = -(-(1 + n_lat) // 8) * 8
    cond = jnp.zeros((n_cond, d), f32).at[0].set(c_ctx).at[1:1 + n_lat].set(c)
    mod = _modulation(cond, ada_w, ada_b).reshape(depth, n_cond, 3, d)
    lbs = jnp.cumsum(jax.nn.softmax(hgrn_lb.astype(f32), axis=0), axis=0)
    final_w = final_norm_w.reshape(1, d)
    rope = _rope_tables(x_sample.shape[1])
    tables = _scan_tables(ROBUST_PLAN) + _scan_tables(BLOCK_PLAN)

    x_c, x_l = x_prompt, x_sample
    new_hgrn, new_gla, new_ret = [], [], []
    for l in range(depth):
        i = l // 2
        final = l == depth - 1
        nw = norm_w[l].reshape(1, d)
        if l % 2 == 0:
            w = _even_weights(w_in_even[i], gla_gk_w[i], gla_gk_b[i], lbs[i], gn_even[i], w_out_even[i])
            x_c, st_a, st_b = _even_layer(x_c, mod[l], nw, w, final_w, tables, None, True, False, final)
            (x_l,) = _even_layer(x_l, mod[l], nw, w, final_w, tables, (state_hgrn[:, i], state_gla[:, i]),
                                 False, True, final)
            new_hgrn.append(st_a)
            new_gla.append(st_b)
        else:
            w = _odd_weights(w_in_odd[i], gn_odd[i], w_out_odd[i])
            log_decay = jax.nn.log_sigmoid(ret_decay[i].astype(f32))
            x_c, st_c = _odd_layer(x_c, mod[l], nw, w, final_w, log_decay, None, None, True, False, final)
            (x_l,) = _odd_layer(x_l, mod[l], nw, w, final_w, log_decay, rope, state_ret[:, i], False, True, final)
            new_ret.append(st_c)
    def stacked(states):
        return states[0][:, None] if len(states) == 1 else jnp.stack(states, axis=1)
    return (x_c, x_l, stacked(new_hgrn), stacked(new_gla), stacked(new_ret))
```

```python
import functools
from typing import NamedTuple

import numpy as np
import jax
import jax.numpy as jnp
from jax import lax
from jax.experimental import pallas as pl
from jax.experimental.pallas import tpu as pltpu

f32 = jnp.float32
bf16 = jnp.bfloat16
HIGHEST = lax.Precision.HIGHEST

EPS = 1e-6
LOG2E = 1.4426950408889634
HEAD = 128
N_HGRN = 4
N_GLA = 4
GLA_DK = 64
N_RET = 8
N_HEADS = 8
GLA_RANK = 16
GLA_GATE_NORM = 16.0
GRID_W = 64
ROPE_BASE = 10000.0

EVEN_SPLITS = (N_HGRN * HEAD,) * 5 + (N_GLA * GLA_DK,) * 2 + (N_GLA * HEAD,) * 2 + (GLA_RANK,) * 2
EVEN_STARTS = tuple(int(c) for c in np.cumsum((0,) + EVEN_SPLITS[:-1]))
HGRN_COLS = EVEN_STARTS[0:5]
GLA_QK_COLS = EVEN_STARTS[5:7]
GLA_COLS = EVEN_STARTS[7:9]
GLA_LOW_COLS = EVEN_STARTS[9:11]
RET_COLS = tuple(i * N_RET * HEAD for i in range(4))

CHUNK = 128


class _ScanPlan(NamedTuple):
    block: int
    table_levels: tuple
    direct_levels: tuple

    @property
    def block_rows(self):
        return 1 if 0 < self.block < CHUNK else 0

    @property
    def table_blocks(self):
        return self.block_rows + len(self.table_levels)

    @property
    def row_total(self):
        return (self.table_blocks + 1) * CHUNK

    @property
    def diag_id(self):
        return len(self.table_levels) + len(self.direct_levels)


ROBUST_PLAN = _ScanPlan(0, (1, 2, 4), (8, 16, 32, 64))
BLOCK_PLAN = _ScanPlan(16, (), (16, 32, 64))
CHUNK_PLAN = _ScanPlan(CHUNK, (), ())
HGRN_PLANS = (BLOCK_PLAN, ROBUST_PLAN)
GLA_PLANS = (CHUNK_PLAN, ROBUST_PLAN)
ALL_PLANS = (ROBUST_PLAN, BLOCK_PLAN, CHUNK_PLAN)
BLOCK_DECAY_LIMIT = 90.0
TABLE_ROWS_MAX = max(p.row_total for p in ALL_PLANS) + 8
PAIR = 2
GATED_HEADS_PER_ITER = 8
GATED_UNITS_MAX = 4
GLA_HEADS_PER_GROUP = HEAD // GLA_DK
RET_UNITS_PER_ITER = 8
STEP_TOKENS = 1024
ROW_TILE = 512
MOD_COLS = 768
VMEM_LIMIT_BYTES = 61 * 1024 * 1024


def _dot(a, b, precision=None):
    return jnp.dot(a, b, precision=precision, preferred_element_type=f32)


def _dot_tn(a, b):
    return lax.dot_general(a, b, (((0,), (0,)), ((), ())), preferred_element_type=f32)


def _silu(x):
    return x * jax.nn.sigmoid(x)


def _row_tile(i):
    return pl.ds(pl.multiple_of(i * ROW_TILE, ROW_TILE), ROW_TILE)


def _chunk_rows(seq_start, c):
    return pl.ds(pl.multiple_of(seq_start + c * CHUNK, CHUNK), CHUNK)


def _mod_kernel(cond_ref, w_ref, b_ref, o_ref):
    o_ref[0] = _dot(_silu(cond_ref[...]), w_ref[0], HIGHEST) + b_ref[0]


def _modulation(cond, ada_w, ada_b):
    depth, d, d3 = ada_w.shape
    rows = cond.shape[0]
    return pl.pallas_call(
        _mod_kernel,
        grid=(depth, d3 // MOD_COLS),
        in_specs=[
            pl.BlockSpec((rows, d), lambda l, j: (0, 0)),
            pl.BlockSpec((1, d, MOD_COLS), lambda l, j: (l, 0, j)),
            pl.BlockSpec((1, 1, MOD_COLS), lambda l, j: (l, 0, j)),
        ],
        out_specs=pl.BlockSpec((1, rows, MOD_COLS), lambda l, j: (l, 0, j)),
        out_shape=jax.ShapeDtypeStruct((depth, rows, d3), f32),
        compiler_params=pltpu.CompilerParams(dimension_semantics=("arbitrary", "arbitrary")),
        name="modulation",
    )(cond, ada_w, ada_b.reshape(depth, 1, d3))


def _modulated_norm(x_ref, mod_ref, nw_ref, h_scr):
    def body(i, carry):
        rows = _row_tile(i)
        x = x_ref[0, rows, :]
        y = x * lax.rsqrt(jnp.mean(x * x, axis=-1, keepdims=True) + EPS) * nw_ref[...]
        h_scr[rows, :] = (y * (1.0 + mod_ref[0, 1:2, :]) + mod_ref[0, 0:1, :]).astype(bf16)
        return carry
    lax.fori_loop(0, STEP_TOKENS // ROW_TILE, body, 0)


def _aligned_slice(start, width):
    return pl.ds(start if isinstance(start, int) else pl.multiple_of(start, width), width)


def _pair_lanes(pair):
    return _aligned_slice(pair * (PAIR * HEAD), PAIR * HEAD)


def _project(h, w_ref, pair, first_cols):
    width = PAIR * HEAD
    return [_dot(h, w_ref[:, _aligned_slice(c0 + pair * width, width)]) for c0 in first_cols]


def _add_heads_output(first_head, n_heads, gate_s, o_scr, gnw_ref, wo_ref, out_ref, first, finish):
    width = n_heads * HEAD
    lanes = _aligned_slice(first_head * HEAD, width)
    tiles = []
    for i in range(STEP_TOKENS // ROW_TILE):
        rows = pl.ds(i * ROW_TILE, ROW_TILE)
        parts = []
        for h in range(n_heads):
            o = o_scr[0, h, rows, :] + o_scr[1, h, rows, :]
            parts.append(o * lax.rsqrt(jnp.mean(o * o, axis=-1, keepdims=True) + EPS))
        y = jnp.concatenate(parts, axis=-1) * gnw_ref[:, lanes]
        tiles.append((rows, (y * gate_s[rows, 0:width].astype(f32)).astype(bf16)))
    products = [(rows, _dot(z, wo_ref[lanes, :])) for rows, z in tiles]
    for rows, product in products:
        mixed = product if first else out_ref[0, rows, :] + product
        if finish is not None:
            x_ref, mod_ref, fnw_ref, final = finish
            mixed = x_ref[0, rows, :] + mod_ref[0, 2:3, :] * mixed
            if final:
                mixed = mixed * lax.rsqrt(jnp.mean(mixed * mixed, axis=-1, keepdims=True) + EPS) * fnw_ref[...]
        out_ref[0, rows, :] = mixed


def _scan_tables(plan):
    t = np.arange(CHUNK)[:, None]
    j = np.arange(CHUNK)[None, :]
    fwd = []
    if plan.block_rows:
        fwd.append((j <= t) & (j // plan.block == t // plan.block))
    for m in plan.table_levels:
        mid = (t // (2 * m)) * (2 * m) + m
        right = t >= mid
        fwd.append(np.where(right, (j >= mid) & (j <= t), (j > t) & (j < mid)))
    fwd.append(j <= t)
    fwd.append(np.ones((8, CHUNK), bool))
    fwd = np.concatenate(fwd, axis=0).astype(np.float32)
    bwd = fwd.copy()
    n_sym = plan.row_total // CHUNK
    bwd[:plan.row_total] = fwd[:plan.row_total].reshape(n_sym, CHUNK, CHUNK)[:, ::-1, ::-1].reshape(-1, CHUNK)
    table = np.stack([np.tile(fwd, (1, 2)), np.tile(bwd, (1, 2))])
    first = 1 if plan.block else 0
    ids = np.full((CHUNK, CHUNK), -1, np.int32)
    for i, m in enumerate(plan.table_levels + plan.direct_levels):
        ids = np.where((t > j) & ((t ^ j) >= m) & ((t ^ j) < 2 * m), first + i, ids)
    if plan.block:
        ids = np.where((t >= j) & (t // plan.block == j // plan.block), 0, ids)
    else:
        ids = np.where(t == j, plan.diag_id, ids)
    return jnp.asarray(table, bf16), jnp.asarray(np.stack([ids, ids.T]).astype(np.int32))


def _in_chunk_scores(q, k, cum, tabled, lvl, ones, rev, key_masks, plan):
    qb = q.astype(bf16)
    q_heads = [qb if mask is None else qb * mask for mask in key_masks]
    kt = k.astype(bf16).T
    if plan.block:
        inside = tabled[0] if plan.block_rows else cum
        grow, decay = jnp.exp2(-inside).astype(bf16), jnp.exp2(inside).astype(bf16)
        keys = kt * grow.T
        scores = [jnp.where(lvl == 0, _dot(qh * decay, keys), 0.0) for qh in q_heads]
        tabled = tabled[plan.block_rows:]
    else:
        diag_keys = kt * ones
        scores = [jnp.where(lvl == plan.diag_id, _dot(qh, diag_keys), 0.0) for qh in q_heads]
    first = 1 if plan.block else 0
    for i in range(len(plan.table_levels)):
        e = jnp.exp2(tabled[i]).astype(bf16)
        keys = kt * e.T
        scores = [jnp.where(lvl == first + i, _dot(qh * e, keys), sc) for qh, sc in zip(q_heads, scores)]
    for i, m in enumerate(plan.direct_levels, start=first + len(plan.table_levels)):
        blocks = []
        for p0 in range(0, CHUNK, 2 * m):
            left, right = slice(p0, p0 + m), slice(p0 + m, p0 + 2 * m)
            q_side, k_side = (left, right) if rev else (right, left)
            mid_row = p0 + m if rev else p0 + m - 1
            blocks.append((q_side, k_side, cum[mid_row:mid_row + 1, :]))
        q_decay = [jnp.exp2(cum[qs] - mid).astype(bf16) for qs, _, mid in blocks]
        k_decay = []
        for _, ks, mid in blocks:
            decay = jnp.exp2(mid - cum[ks]).astype(bf16)
            zero = jnp.zeros((m, HEAD), bf16)
            k_decay += [zero, decay] if rev else [decay, zero]
        keys = kt * jnp.concatenate(k_decay, axis=0).T
        for h, qh in enumerate(q_heads):
            s = _dot(jnp.concatenate([qh[qs] * e for (qs, _, _), e in zip(blocks, q_decay)], axis=0), keys)
            rows = []
            for b, (qs, ks, _) in enumerate(blocks):
                updated = jnp.where(lvl[qs, :] == i, s[b * m:(b + 1) * m, :], scores[h][qs, :])
                rows += [updated, scores[h][ks, :]] if rev else [scores[h][ks, :], updated]
            scores[h] = jnp.concatenate(rows, axis=0)
    return [sc.astype(bf16) for sc in scores]


def _store_log2_split(g_ref, rows, g, block):
    width = g.shape[-1]
    x = g * LOG2E
    hi = x.astype(bf16)
    g_ref[rows, 0:width] = hi
    g_ref[rows, width:2 * width] = (x - hi.astype(f32)).astype(bf16)
    return jnp.min(jnp.sum(x.reshape(-1, block, width), axis=1))


def _decay_sums(g_split, table):
    width = g_split.shape[-1] // 2
    return _dot(table, jnp.concatenate([g_split[:, :width], g_split[:, width:]], axis=0))


def _key_masks(heads_per_group):
    if heads_per_group == 1:
        return [None]
    lane = lax.broadcasted_iota(jnp.int32, (1, HEAD), 1)
    return [(lane // (HEAD // heads_per_group) == sub).astype(bf16) for sub in range(heads_per_group)]


def _gated_chunks(chains, sums_ref, upcoming, upcoming_ref, heads_per_group, plan):
    masks = _key_masks(heads_per_group)
    groups = [slice(g * HEAD, (g + 1) * HEAD) for g in range(PAIR)]
    n_heads = PAIR * heads_per_group
    group_of = [h // heads_per_group for h in range(n_heads)]
    mask_of = [masks[h % heads_per_group] for h in range(n_heads)]
    flat = [(c, r) for c, (_, units) in enumerate(chains) for r in range(len(units))]
    unit = {(c, r): chains[c][1][r] for c, r in flat}
    index = {key: u for u, key in enumerate(flat)}
    cum_rows = slice(plan.table_blocks * CHUNK, (plan.table_blocks + 1) * CHUNK)
    cum = {key: sums_ref[index[key], cum_rows, :] for key in flat}
    total = {key: sums_ref[index[key], plan.row_total:plan.row_total + 1, :] for key in flat}
    vb = {key: [unit[key][2][:, h * HEAD:(h + 1) * HEAD] for h in range(n_heads)] for key in flat}
    upcoming = list(enumerate(upcoming))
    kv, whole = {}, {}
    for key in flat:
        k = unit[key][1]
        keys = (k * jnp.exp2(total[key] - cum[key])).astype(bf16)
        kv[key] = []
        for h in range(n_heads):
            own = keys[:, groups[group_of[h]]]
            kv[key].append(_dot_tn(own if mask_of[h] is None else own * mask_of[h], vb[key][h]))
        whole[key] = [jnp.broadcast_to(jnp.exp2(total[key][:, lanes]), (HEAD, HEAD)).T for lanes in groups]
    state = [list(states) for states, _ in chains]
    outs, pending, carried = {key: [None] * n_heads for key in flat}, None, {}

    def finish(done):
        pkey, ph, psc = done
        queries, entering = carried[pkey]
        lhs = jnp.concatenate([queries[:, groups[group_of[ph]]], psc], axis=1)
        outs[pkey][ph] = _dot(lhs, jnp.concatenate([entering[ph], vb[pkey][ph]], axis=0))

    for r in range(max(len(units) for _, units in chains)):
        live = [key for key in flat if key[1] == r]
        for key in live:
            queries = (unit[key][0] * jnp.exp2(cum[key])).astype(bf16)
            carried[key] = (queries, [s.astype(bf16) for s in state[key[0]]])
            state[key[0]] = [whole[key][group_of[h]] * state[key[0]][h] + kv[key][h] for h in range(n_heads)]
        for key in live:
            q, k, _, lvl, rev = unit[key]
            ones = (lvl[0:1, :] >= -1).astype(bf16)
            for g, lanes in enumerate(groups):
                tabled = [sums_ref[index[key], i * CHUNK:(i + 1) * CHUNK, lanes] for i in range(plan.table_blocks)]
                group_scores = _in_chunk_scores(q[:, lanes], k[:, lanes], cum[key][:, lanes], tabled, lvl, ones, rev,
                                                masks, plan)
                for sub, sc in enumerate(group_scores):
                    if pending is not None:
                        finish(pending)
                    pending = (key, g * heads_per_group + sub, sc)
            if upcoming:
                u, (g_split, table) = upcoming.pop(0)
                upcoming_ref[u, 0:plan.row_total + 8, :] = _decay_sums(g_split, table)
    for u, (g_split, table) in upcoming:
        upcoming_ref[u, 0:plan.row_total + 8, :] = _decay_sums(g_split, table)
    finish(pending)
    return [[outs[(c, r)] for r in range(len(units))] for c, (_, units) in enumerate(chains)], state


def _gated_scans(q_s, k_refs, v_s, g_refs, s_scr, sums_scr, table_ref, lvl_ref, o_scr, seq, heads_per_group, plan):
    n = seq // CHUNK
    n_seq = STEP_TOKENS // seq
    n_heads = PAIR * heads_per_group
    units = GATED_HEADS_PER_ITER // n_heads
    per_iter = min(n, units // 2)
    seqs_per_iter = min(n_seq, units // (2 * per_iter))
    assert n % per_iter == 0 and n_seq % seqs_per_iter == 0
    iters_per_seq = n // per_iter
    n_iters = (n_seq // seqs_per_iter) * iters_per_seq

    def layout(it):
        jj, i = it // iters_per_seq, it % iters_per_seq
        chains = []
        for js in range(seqs_per_iter):
            j = jj * seqs_per_iter + js
            for d in range(2):
                steps = [i * per_iter + r for r in range(per_iter)]
                chains.append((j, d, [_chunk_rows(j * seq, n - 1 - t if d else t) for t in steps]))
        return chains

    def sums_inputs(it):
        return [(g_refs[d][rw, :], table_ref[d]) for _, d, rws in layout(it) for rw in rws]

    for u, (g_split, table) in enumerate(sums_inputs(0)):
        sums_scr[0, u, 0:plan.row_total + 8, :] = _decay_sums(g_split, table)

    def iteration(it, slot):
        chains = layout(it)
        args = [([s_scr[j, d, h] for h in range(n_heads)],
                 [(q_s[rw, :], k_refs[d][rw, :], v_s[rw, :], lvl_ref[d], bool(d)) for rw in rws])
                for j, d, rws in chains]
        upcoming = sums_inputs(jnp.minimum(it + 1, n_iters - 1))
        outs, new_states = _gated_chunks(args, sums_scr.at[slot], upcoming, sums_scr.at[1 - slot], heads_per_group,
                                         plan)
        for c, (j, d, rws) in enumerate(chains):
            for h in range(n_heads):
                for r, rw in enumerate(rws):
                    o_scr[d, h, rw, :] = outs[c][r][h]
                s_scr[j, d, h] = new_states[c][h]

    assert n_iters % 2 == 0

    def body(it2, carry):
        iteration(2 * it2, 0)
        iteration(2 * it2 + 1, 1)
        return carry
    lax.fori_loop(0, n_iters // 2, body, 0)


def _even_kernel(*refs, seq, has_state, emit_state, final):
    it = iter(refs)
    x_ref, mod_ref, nw_ref = next(it), next(it), next(it)
    win_ref, wlow_ref, wo_ref = next(it), next(it), next(it)
    gkw_ref, gkb_ref, lb_ref, gnw_ref, fnw_ref = next(it), next(it), next(it), next(it), next(it)
    plan_tables = {plan: (next(it), next(it)) for plan in ALL_PLANS}
    s0a_ref, s0b_ref = (next(it), next(it)) if has_state else (None, None)
    out_ref = next(it)
    sta_ref, stb_ref = (next(it), next(it)) if emit_state else (None, None)
    h_scr, low_scr, o_scr, gate_s, q_s, v_s, kf_s, kb_s, gf_s, gb_s, s_scr, sums_scr, bound_s = it
    n_seq = STEP_TOKENS // seq
    n_tiles = STEP_TOKENS // ROW_TILE

    _modulated_norm(x_ref, mod_ref, nw_ref, h_scr)

    def low_body(i, carry):
        rows = _row_tile(i)
        low_scr[rows, :] = _dot(h_scr[rows, :], wlow_ref[...]).astype(bf16)
        return carry
    lax.fori_loop(0, n_tiles, low_body, 0)

    width = PAIR * HEAD

    def run_scans(first_head, s0_ref, st_ref, h0, heads_per_group, k_refs, plans, block_decays, first=False,
                  last=False):
        bounded, robust = plans
        bound_s[0] = (functools.reduce(jnp.minimum, block_decays) >= -BLOCK_DECAY_LIMIT).astype(jnp.int32)
        n_heads = PAIR * heads_per_group
        key_rows = HEAD // heads_per_group
        own_rows = [pl.ds((h % heads_per_group) * key_rows, key_rows) for h in range(n_heads)]
        for j in range(n_seq):
            for d in range(2):
                for h in range(n_heads):
                    if s0_ref is None or heads_per_group > 1:
                        s_scr[j, d, h] = jnp.zeros((HEAD, HEAD), f32)
                    if s0_ref is not None:
                        s_scr[j, d, h, own_rows[h], :] = s0_ref[j, d, h0 + h]
        def scans(plan):
            table_ref, lvl_ref = plan_tables[plan]
            _gated_scans(q_s, k_refs, v_s, (gf_s, gb_s), s_scr, sums_scr, table_ref, lvl_ref, o_scr, seq,
                         heads_per_group, plan)
        lax.cond(bound_s[0] == 1, functools.partial(scans, bounded), functools.partial(scans, robust))
        if st_ref is not None:
            for j in range(n_seq):
                for d in range(2):
                    for h in range(n_heads):
                        st_ref[j, d, h0 + h] = s_scr[j, d, h, own_rows[h], :]
        _add_heads_output(first_head, n_heads, gate_s, o_scr, gnw_ref, wo_ref, out_ref, first,
                          (x_ref, mod_ref, fnw_ref, final) if last else None)

    def hgrn_pair(pp, carry):
        lb = lb_ref[:, _pair_lanes(pp)]
        log_lb = jnp.log(lb)

        tiles = [pl.ds(i * ROW_TILE, ROW_TILE) for i in range(n_tiles)]
        projected = [_project(h_scr[rows, :], win_ref, pp, HGRN_COLS) for rows in tiles]
        block_decays = []
        for rows, (query, value, forget_f, forget_b, gate) in zip(tiles, projected):
            gate_s[rows, 0:width] = _silu(gate).astype(bf16)
            q_s[rows, :] = _silu(query)
            v_s[rows, 0:width] = value.astype(bf16)
            for a, k_s, g_s in ((forget_f, kf_s, gf_s), (forget_b, kb_s, gb_s)):
                z = log_lb - a
                u = jnp.exp(-jnp.abs(a))
                w = jnp.exp(-jnp.abs(z))
                r = 1.0 / (1.0 + u)
                log_f = jnp.maximum(z, 0.0) + jnp.minimum(a, 0.0) + jnp.log((1.0 + w) * r)
                block_decays.append(_store_log2_split(g_s, rows, log_f, HGRN_PLANS[0].block))
                k_s[rows, :] = (1.0 - lb) * jnp.where(a >= 0.0, u * r, r)
        run_scans(PAIR * pp, s0a_ref, sta_ref, PAIR * pp, 1, (kf_s, kb_s), HGRN_PLANS, block_decays)
        return carry

    def clear(i, carry):
        out_ref[0, _row_tile(i), :] = jnp.zeros((ROW_TILE, out_ref.shape[-1]), f32)
        return carry
    lax.fori_loop(0, n_tiles, clear, 0)
    lax.fori_loop(0, N_HGRN // PAIR, hgrn_pair, 0)

    assert N_GLA == PAIR * GLA_HEADS_PER_GROUP

    def gla_project(rows):
        h = h_scr[rows, :]
        narrow = [_dot(h, win_ref[:, c0:c0 + N_GLA * GLA_DK]) for c0 in GLA_QK_COLS]
        wide = [_dot(h, win_ref[:, c0:c0 + N_GLA * HEAD]) for c0 in GLA_COLS]
        return narrow + wide

    tiles = [pl.ds(i * ROW_TILE, ROW_TILE) for i in range(n_tiles)]
    projected = [gla_project(rows) for rows in tiles]
    block_decays = []
    for rows, (query, key, value, gate) in zip(tiles, projected):
        gate_s[rows, :] = _silu(gate).astype(bf16)
        q_s[rows, :] = query * (GLA_DK ** -0.5)
        kf_s[rows, :] = key
        v_s[rows, :] = value.astype(bf16)
        low = low_scr[rows, :]
        for d, g_s in enumerate((gf_s, gb_s)):
            logits = _dot(low, gkw_ref[d]) + gkb_ref[d]
            log_gate = jnp.minimum(logits, 0.0) - jnp.log(1.0 + jnp.exp(-jnp.abs(logits)))
            block_decays.append(_store_log2_split(g_s, rows, log_gate * (1.0 / GLA_GATE_NORM), GLA_PLANS[0].block))
    run_scans(N_HGRN, s0b_ref, stb_ref, 0, GLA_HEADS_PER_GROUP, (kf_s, kf_s), GLA_PLANS, block_decays, last=True)


def _odd_kernel(*refs, seq, has_state, emit_state, use_rope, final):
    it = iter(refs)
    lg_ref = next(it)
    x_ref, mod_ref, nw_ref = next(it), next(it), next(it)
    wr_ref, wo_ref, gnw_ref, fnw_ref = next(it), next(it), next(it), next(it)
    cos_ref, sin_ref = (next(it), next(it)) if use_rope else (None, None)
    s0_ref = next(it) if has_state else None
    out_ref = next(it)
    st_ref = next(it) if emit_state else None
    h_scr, o_scr, gate_s, q_s, kt_s, v_s, s_scr, dec_scr = it
    n_seq = STEP_TOKENS // seq
    n_tiles = STEP_TOKENS // ROW_TILE
    n_chunks = seq // CHUNK
    per_iter = min(n_chunks, RET_UNITS_PER_ITER // n_seq)
    assert n_chunks % per_iter == 0

    _modulated_norm(x_ref, mod_ref, nw_ref, h_scr)
    t_idx = lax.broadcasted_iota(jnp.int32, (CHUNK, CHUNK), 0)
    s_idx = lax.broadcasted_iota(jnp.int32, (CHUNK, CHUNK), 1)
    row_f = lax.broadcasted_iota(jnp.int32, (CHUNK, HEAD), 0).astype(f32)
    col_f = lax.broadcasted_iota(jnp.int32, (8, CHUNK), 1).astype(f32)
    chunk_len = jnp.full((8, HEAD), CHUNK, f32)
    if use_rope:
        lane = lax.broadcasted_iota(jnp.int32, (ROW_TILE, HEAD), 1)
        first_quarter = (lane // (HEAD // 4)) % 2 == 0

    def rope(x, cos, sin_signed):
        xr = jnp.where(first_quarter, pltpu.roll(x, HEAD - HEAD // 4, axis=1), pltpu.roll(x, HEAD // 4, axis=1))
        return x * cos + xr * sin_signed

    width = PAIR * HEAD

    def pair_body(pp):
        def proj(i, c):
            rows = _row_tile(i)
            q, k, value, gate = _project(h_scr[rows, :], wr_ref, pp, RET_COLS)
            gate_s[rows, :] = _silu(gate).astype(bf16)
            k = k * (HEAD ** -0.5)
            if use_rope:
                cos, sin_signed = cos_ref[rows, :], sin_ref[rows, :]
                heads = [slice(h * HEAD, (h + 1) * HEAD) for h in range(PAIR)]
                q = jnp.concatenate([rope(q[:, lanes], cos, sin_signed) for lanes in heads], axis=-1)
                k = jnp.concatenate([rope(k[:, lanes], cos, sin_signed) for lanes in heads], axis=-1)
            q_s[rows, :] = q.astype(bf16)
            kt_s[:, rows] = k.T
            v_s[rows, :] = value.astype(bf16)
            return c
        lax.fori_loop(0, n_tiles, proj, 0)
        for h in range(PAIR):
            scan_head(PAIR * pp + h, h)
        last = pp == N_RET // PAIR - 1
        _add_heads_output(PAIR * pp, PAIR, gate_s, o_scr, gnw_ref, wo_ref, out_ref, pp == 0,
                          (x_ref, mod_ref, fnw_ref, final) if last else None)

    def scan_head(hh, h):
        lanes = slice(h * HEAD, (h + 1) * HEAD)
        lg_f = lg_ref[0, hh]
        lg_b = lg_ref[1, hh]
        for j in range(n_seq):
            for d in range(2):
                s_scr[j, d] = s0_ref[j, d, hh] if has_state else jnp.zeros((HEAD, HEAD), f32)

        dist = (t_idx - s_idx).astype(f32)
        dec_scr[0] = jnp.exp(lg_f * (row_f + 1.0))
        dec_scr[1] = jnp.exp(lg_b * (CHUNK - row_f))
        dec_scr[2] = (jnp.where(t_idx >= s_idx, jnp.exp(lg_f * jnp.maximum(dist, 0.0)), 0.0)
                      + jnp.where(s_idx >= t_idx, jnp.exp(lg_b * jnp.maximum(-dist, 0.0)), 0.0))
        dec_scr[3, 0:8, :] = jnp.exp(lg_f * (CHUNK - 1.0 - col_f))
        dec_scr[3, 8:16, :] = jnp.exp(lg_b * col_f)
        dec_scr[3, 16:24, :] = jnp.exp(lg_f * chunk_len)
        dec_scr[3, 24:32, :] = jnp.exp(lg_b * chunk_len)

        def body(i, c):
            units = [(j, r) for j in range(n_seq) for r in range(per_iter)]
            rows = {(j, r, d): _chunk_rows(j * seq, n_chunks - 1 - (i * per_iter + r) if d else i * per_iter + r)
                    for j, r in units for d in range(2)}
            scores = {u: _dot(q_s[rows[u + (0,)], lanes], kt_s[lanes, rows[u + (0,)]].astype(bf16)) for u in units}
            kv = {}
            for j, r in units:
                for d in range(2):
                    rw = rows[(j, r, d)]
                    keys = (kt_s[lanes, rw] * dec_scr[3, 8 * d:8 * d + 1, :]).astype(bf16)
                    kv[(j, r, d)] = _dot(keys, v_s[rw, lanes])
            state = {(j, d): s_scr[j, d] for j in range(n_seq) for d in range(2)}
            carried = {}
            for r in range(per_iter):
                for j in range(n_seq):
                    for d in range(2):
                        carried[(j, r, d)] = _dot(q_s[rows[(j, r, d)], lanes], state[(j, d)].astype(bf16))
                        state[(j, d)] = dec_scr[3, 16 + 8 * d:17 + 8 * d, :] * state[(j, d)] + kv[(j, r, d)]
            for j, r in units:
                rw = rows[(j, r, 0)]
                inside = _dot((scores[(j, r)] * dec_scr[2]).astype(bf16), v_s[rw, lanes])
                o_scr[0, h, rw, :] = inside + dec_scr[0] * carried[(j, r, 0)]
                o_scr[1, h, rows[(j, r, 1)], :] = dec_scr[1] * carried[(j, r, 1)]
            for (j, d), s in state.items():
                s_scr[j, d] = s
            return c
        lax.fori_loop(0, n_chunks // per_iter, body, 0)
        if emit_state:
            for j in range(n_seq):
                for d in range(2):
                    st_ref[j, d, hh] = s_scr[j, d]

    for pp in range(N_RET // PAIR):
        pair_body(pp)


def _const_spec(shape):
    zeros = (0,) * len(shape)
    return pl.BlockSpec(shape, lambda i: zeros, pipeline_mode=pl.Buffered(1))


def _step_spec(shape, per_step, buffers=None):
    zeros = (0,) * (len(shape) - 1)
    mode = {} if buffers is None else {"pipeline_mode": pl.Buffered(buffers)}
    return pl.BlockSpec((per_step,) + tuple(shape[1:]), lambda i: (i,) + zeros, **mode)


def _mod_spec(d, per_sequence):
    if per_sequence:
        return pl.BlockSpec((1, 3, d), lambda i: (i + 1, 0, 0))
    return pl.BlockSpec((1, 3, d), lambda i: (0, 0, 0))


def _layer_call(body, x, mod, consts, states, state_shapes, per_sequence_mod, scratch, name, smem_inputs=(),
                state_buffers=None):
    n_seq, seq, d = x.shape
    per_step = STEP_TOKENS // seq
    assert per_step * seq == STEP_TOKENS and n_seq % per_step == 0 and seq % CHUNK == 0
    assert not per_sequence_mod or per_step == 1
    n_steps = n_seq // per_step
    xs = x.reshape(n_steps, STEP_TOKENS, d)
    inputs = list(smem_inputs) + [xs, mod] + list(consts) + list(states)
    in_specs = [pl.BlockSpec(memory_space=pltpu.SMEM)] * len(smem_inputs)
    in_specs += [_step_spec(xs.shape, 1), _mod_spec(d, per_sequence_mod)]
    in_specs += [_const_spec(a.shape) for a in consts]
    in_specs += [_step_spec(s.shape, per_step) for s in states]
    out_shape = [jax.ShapeDtypeStruct(xs.shape, f32)] + [jax.ShapeDtypeStruct(s, f32) for s in state_shapes]
    out_specs = [_step_spec(xs.shape, 1)] + [_step_spec(s, per_step, buffers=state_buffers) for s in state_shapes]
    outs = pl.pallas_call(
        body,
        grid=(n_steps,),
        in_specs=in_specs,
        out_specs=out_specs,
        out_shape=out_shape,
        scratch_shapes=scratch,
        compiler_params=pltpu.CompilerParams(dimension_semantics=("arbitrary",), vmem_limit_bytes=VMEM_LIMIT_BYTES),
        name=name,
    )(*inputs)
    return [outs[0].reshape(x.shape)] + list(outs[1:])


def _even_layer(x, mod, norm_w, w, final_w, tables, states, emit_state, per_sequence_mod, final):
    n_seq, seq, d = x.shape
    consts = [norm_w, w["win"], w["wlow"], w["wout"], w["gkw"], w["gkb"], w["lb"], w["gnw"],
              final_w] + list(tables)
    state_shapes = [(n_seq, 2, N_HGRN, HEAD, HEAD), (n_seq, 2, N_GLA, GLA_DK, HEAD)] if emit_state else []
    scan_heads = max(PAIR, N_GLA)
    scratch = [
        pltpu.VMEM((STEP_TOKENS, d), bf16),
        pltpu.VMEM((STEP_TOKENS, HEAD), bf16),
        pltpu.VMEM((2, scan_heads, STEP_TOKENS, HEAD), f32),
        pltpu.VMEM((STEP_TOKENS, scan_heads * HEAD), bf16),
        pltpu.VMEM((STEP_TOKENS, PAIR * HEAD), f32),
        pltpu.VMEM((STEP_TOKENS, scan_heads * HEAD), bf16),
        pltpu.VMEM((STEP_TOKENS, PAIR * HEAD), f32),
        pltpu.VMEM((STEP_TOKENS, PAIR * HEAD), f32),
        pltpu.VMEM((STEP_TOKENS, 2 * PAIR * HEAD), bf16),
        pltpu.VMEM((STEP_TOKENS, 2 * PAIR * HEAD), bf16),
        pltpu.VMEM((STEP_TOKENS // seq, 2, scan_heads, HEAD, HEAD), f32),
        pltpu.VMEM((2, GATED_UNITS_MAX, TABLE_ROWS_MAX, PAIR * HEAD), f32),
        pltpu.SMEM((1,), jnp.int32),
    ]
    body = functools.partial(_even_kernel, seq=seq, has_state=states is not None, emit_state=emit_state, final=final)
    return _layer_call(body, x, mod, consts, states or (), state_shapes, per_sequence_mod, scratch,
                       "even_layer_seq%d" % seq, state_buffers=1)


def _odd_layer(x, mod, norm_w, w, final_w, log_decay, rope, state, emit_state, per_sequence_mod, final):
    n_seq, seq, d = x.shape
    consts = [norm_w, w["wr"], w["wout"], w["gnw"], final_w] + list(rope or ())
    state_shapes = [(n_seq, 2, N_RET, HEAD, HEAD)] if emit_state else []
    scratch = [
        pltpu.VMEM((STEP_TOKENS, d), bf16),
        pltpu.VMEM((2, PAIR, STEP_TOKENS, HEAD), f32),
        pltpu.VMEM((STEP_TOKENS, PAIR * HEAD), bf16),
        pltpu.VMEM((STEP_TOKENS, PAIR * HEAD), bf16),
        pltpu.VMEM((PAIR * HEAD, STEP_TOKENS), f32),
        pltpu.VMEM((STEP_TOKENS, PAIR * HEAD), bf16),
        pltpu.VMEM((STEP_TOKENS // seq, 2, HEAD, HEAD), f32),
        pltpu.VMEM((4, CHUNK, HEAD), f32),
    ]
    body = functools.partial(_odd_kernel, seq=seq, has_state=state is not None, emit_state=emit_state,
                             use_rope=rope is not None, final=final)
    return _layer_call(body, x, mod, consts, () if state is None else (state,), state_shapes, per_sequence_mod,
                       scratch, "odd_layer_seq%d" % seq, smem_inputs=(log_decay,))


def _even_weights(w_in, gk_w, gk_b, lb, gn_w, w_out):
    c0 = GLA_LOW_COLS[0]
    wlow = jnp.pad(w_in[:, c0:c0 + 2 * GLA_RANK], ((0, 0), (0, HEAD - 2 * GLA_RANK)))
    gkw = jnp.stack([jnp.pad(gk_w[d], ((d * GLA_RANK, HEAD - (d + 1) * GLA_RANK), (0, 0))) for d in range(2)])
    return {"win": w_in.astype(bf16), "wlow": wlow.astype(bf16), "gkw": gkw.astype(bf16),
            "gkb": gk_b.reshape(2, 1, -1), "lb": lb.reshape(1, -1), "gnw": gn_w.reshape(1, -1),
            "wout": w_out.astype(bf16)}


def _odd_weights(w_in, gn_w, w_out):
    return {"wr": w_in.astype(bf16), "gnw": gn_w.reshape(1, -1), "wout": w_out.astype(bf16)}


def _rope_tables(seq):
    rows = seq // GRID_W
    t_row = jnp.repeat(jnp.arange(rows), GRID_W).astype(f32)
    t_col = jnp.tile(jnp.arange(GRID_W), rows).astype(f32)
    half = HEAD // 2
    inv = ROPE_BASE ** (-jnp.arange(0, half, 2, dtype=f32) / half)
    ang_r = t_row[:, None] * inv
    ang_c = t_col[:, None] * inv
    ang = jnp.concatenate([ang_r, ang_r, ang_c, ang_c], axis=-1)
    sign = jnp.where((jnp.arange(HEAD) // (HEAD // 4)) % 2 == 0, -1.0, 1.0).astype(f32)
    return jnp.cos(ang), jnp.sin(ang) * sign


def kernel(x_prompt, x_sample, state_hgrn, state_gla, state_ret, c, c_ctx, norm_w, ada_w, ada_b, w_in_even, hgrn_lb, gla_gk_w, gla_gk_b, gn_even, w_out_even, w_in_odd, ret_decay, gn_odd, w_out_odd, final_norm_w):
    depth, d = norm_w.shape
    n_lat = x_sample.shape[0]
    n_cond = -(-(1 + n_lat) // 8) * 8
    cond = jnp.zeros((n_cond, d), f32).at[0].set(c_ctx).at[1:1 + n_lat].set(c)
    mod = _modulation(cond, ada_w, ada_b).reshape(depth, n_cond, 3, d)
    lbs = jnp.cumsum(jax.nn.softmax(hgrn_lb.astype(f32), axis=0), axis=0)
    final_w = final_norm_w.reshape(1, d)
    rope = _rope_tables(x_sample.shape[1])
    tables = sum((_scan_tables(plan) for plan in ALL_PLANS), ())

    x_c, x_l = x_prompt, x_sample
    new_hgrn, new_gla, new_ret = [], [], []
    for l in range(depth):
        i = l // 2
        final = l == depth - 1
        nw = norm_w[l].reshape(1, d)
        if l % 2 == 0:
            w = _even_weights(w_in_even[i], gla_gk_w[i], gla_gk_b[i], lbs[i], gn_even[i], w_out_even[i])
            x_c, st_a, st_b = _even_layer(x_c, mod[l], nw, w, final_w, tables, None, True, False, final)
            (x_l,) = _even_layer(x_l, mod[l], nw, w, final_w, tables, (state_hgrn[:, i], state_gla[:, i]),
                                 False, True, final)
            new_hgrn.append(st_a)
            new_gla.append(st_b)
        else:
            w = _odd_weights(w_in_odd[i], gn_odd[i], w_out_odd[i])
            log_decay = jax.nn.log_sigmoid(ret_decay[i].astype(f32))
            x_c, st_c = _odd_layer(x_c, mod[l], nw, w, final_w, log_decay, None, None, True, False, final)
            (x_l,) = _odd_layer(x_l, mod[l], nw, w, final_w, log_decay, rope, state_ret[:, i], False, True, final)
            new_ret.append(st_c)
    def stacked(states):
        return states[0][:, None] if len(states) == 1 else jnp.stack(states, axis=1)
    return (x_c, x_l, stacked(new_hgrn), stacked(new_gla), stacked(new_ret))
```

```python
import functools
from typing import NamedTuple

import numpy as np
import jax
import jax.numpy as jnp
from jax import lax
from jax.experimental import pallas as pl
from jax.experimental.pallas import tpu as pltpu

f32 = jnp.float32
bf16 = jnp.bfloat16
HIGHEST = lax.Precision.HIGHEST

EPS = 1e-6
LOG2E = 1.4426950408889634
HEAD = 128
N_HGRN = 4
N_GLA = 4
GLA_DK = 64
N_RET = 8
N_HEADS = 8
GLA_RANK = 16
GLA_GATE_NORM = 16.0
GRID_W = 64
ROPE_BASE = 10000.0

EVEN_SPLITS = (N_HGRN * HEAD,) * 5 + (N_GLA * GLA_DK,) * 2 + (N_GLA * HEAD,) * 2 + (GLA_RANK,) * 2
EVEN_STARTS = tuple(int(c) for c in np.cumsum((0,) + EVEN_SPLITS[:-1]))
HGRN_COLS = EVEN_STARTS[0:5]
GLA_QK_COLS = EVEN_STARTS[5:7]
GLA_COLS = EVEN_STARTS[7:9]
GLA_LOW_COLS = EVEN_STARTS[9:11]
RET_COLS = tuple(i * N_RET * HEAD for i in range(4))

CHUNK = 128


class _ScanPlan(NamedTuple):
    block: int
    table_levels: tuple
    direct_levels: tuple

    @property
    def block_rows(self):
        return 1 if 0 < self.block < CHUNK else 0

    @property
    def table_blocks(self):
        return self.block_rows + len(self.table_levels)

    @property
    def row_total(self):
        return (self.table_blocks + 1) * CHUNK

    @property
    def diag_id(self):
        return len(self.table_levels) + len(self.direct_levels)


ROBUST_PLAN = _ScanPlan(0, (1, 2, 4), (8, 16, 32, 64))
BLOCK_PLAN = _ScanPlan(64, (), (64,))
CHUNK_PLAN = _ScanPlan(CHUNK, (), ())
HGRN_PLANS = (BLOCK_PLAN, ROBUST_PLAN)
GLA_PLANS = (CHUNK_PLAN, ROBUST_PLAN)
ALL_PLANS = (ROBUST_PLAN, BLOCK_PLAN, CHUNK_PLAN)
HGRN_DECAY_LIMIT = 104.0
GLA_DECAY_LIMIT = 90.0
TABLE_ROWS_MAX = max(p.row_total for p in ALL_PLANS) + 8
PAIR = 2
GATED_HEADS_PER_ITER = 8
GATED_UNITS_MAX = 4
GLA_HEADS_PER_GROUP = HEAD // GLA_DK
RET_UNITS_PER_ITER = 8
STEP_TOKENS = 1024
ROW_TILE = 512
MOD_COLS = 768
VMEM_LIMIT_BYTES = 61 * 1024 * 1024


def _dot(a, b, precision=None):
    return jnp.dot(a, b, precision=precision, preferred_element_type=f32)


def _dot_tn(a, b):
    return lax.dot_general(a, b, (((0,), (0,)), ((), ())), preferred_element_type=f32)


def _silu(x):
    return x * jax.nn.sigmoid(x)


def _row_tile(i):
    return pl.ds(pl.multiple_of(i * ROW_TILE, ROW_TILE), ROW_TILE)


def _chunk_rows(seq_start, c):
    return pl.ds(pl.multiple_of(seq_start + c * CHUNK, CHUNK), CHUNK)


def _mod_kernel(cond_ref, w_ref, b_ref, o_ref):
    o_ref[0] = _dot(_silu(cond_ref[...]), w_ref[0], HIGHEST) + b_ref[0]


def _modulation(cond, ada_w, ada_b):
    depth, d, d3 = ada_w.shape
    rows = cond.shape[0]
    return pl.pallas_call(
        _mod_kernel,
        grid=(depth, d3 // MOD_COLS),
        in_specs=[
            pl.BlockSpec((rows, d), lambda l, j: (0, 0)),
            pl.BlockSpec((1, d, MOD_COLS), lambda l, j: (l, 0, j)),
            pl.BlockSpec((1, 1, MOD_COLS), lambda l, j: (l, 0, j)),
        ],
        out_specs=pl.BlockSpec((1, rows, MOD_COLS), lambda l, j: (l, 0, j)),
        out_shape=jax.ShapeDtypeStruct((depth, rows, d3), f32),
        compiler_params=pltpu.CompilerParams(dimension_semantics=("arbitrary", "arbitrary")),
        name="modulation",
    )(cond, ada_w, ada_b.reshape(depth, 1, d3))


def _modulated_norm(x_ref, mod_ref, nw_ref, h_scr):
    def body(i, carry):
        rows = _row_tile(i)
        x = x_ref[0, rows, :]
        y = x * lax.rsqrt(jnp.mean(x * x, axis=-1, keepdims=True) + EPS) * nw_ref[...]
        h_scr[rows, :] = (y * (1.0 + mod_ref[0, 1:2, :]) + mod_ref[0, 0:1, :]).astype(bf16)
        return carry
    lax.fori_loop(0, STEP_TOKENS // ROW_TILE, body, 0)


def _aligned_slice(start, width):
    return pl.ds(start if isinstance(start, int) else pl.multiple_of(start, width), width)


def _pair_lanes(pair):
    return _aligned_slice(pair * (PAIR * HEAD), PAIR * HEAD)


def _project(h, w_ref, pair, first_cols):
    width = PAIR * HEAD
    return [_dot(h, w_ref[:, _aligned_slice(c0 + pair * width, width)]) for c0 in first_cols]


def _add_heads_output(first_head, n_heads, gate_s, o_scr, gnw_ref, wo_ref, out_ref, first, finish):
    width = n_heads * HEAD
    lanes = _aligned_slice(first_head * HEAD, width)
    tiles = []
    for i in range(STEP_TOKENS // ROW_TILE):
        rows = pl.ds(i * ROW_TILE, ROW_TILE)
        parts = []
        for h in range(n_heads):
            o = o_scr[0, h, rows, :] + o_scr[1, h, rows, :]
            parts.append(o * lax.rsqrt(jnp.mean(o * o, axis=-1, keepdims=True) + EPS))
        y = jnp.concatenate(parts, axis=-1) * gnw_ref[:, lanes]
        tiles.append((rows, (y * gate_s[rows, 0:width].astype(f32)).astype(bf16)))
    products = [(rows, _dot(z, wo_ref[lanes, :])) for rows, z in tiles]
    for rows, product in products:
        mixed = product if first else out_ref[0, rows, :] + product
        if finish is not None:
            x_ref, mod_ref, fnw_ref, final = finish
            mixed = x_ref[0, rows, :] + mod_ref[0, 2:3, :] * mixed
            if final:
                mixed = mixed * lax.rsqrt(jnp.mean(mixed * mixed, axis=-1, keepdims=True) + EPS) * fnw_ref[...]
        out_ref[0, rows, :] = mixed


def _scan_tables(plan):
    t = np.arange(CHUNK)[:, None]
    j = np.arange(CHUNK)[None, :]
    fwd = []
    if plan.block_rows:
        fwd.append((j <= t) & (j // plan.block == t // plan.block))
    for m in plan.table_levels:
        mid = (t // (2 * m)) * (2 * m) + m
        right = t >= mid
        fwd.append(np.where(right, (j >= mid) & (j <= t), (j > t) & (j < mid)))
    fwd.append(j <= t)
    fwd.append(np.ones((8, CHUNK), bool))
    fwd = np.concatenate(fwd, axis=0).astype(np.float32)
    bwd = fwd.copy()
    n_sym = plan.row_total // CHUNK
    bwd[:plan.row_total] = fwd[:plan.row_total].reshape(n_sym, CHUNK, CHUNK)[:, ::-1, ::-1].reshape(-1, CHUNK)
    table = np.stack([np.tile(fwd, (1, 2)), np.tile(bwd, (1, 2))])
    first = 1 if plan.block else 0
    ids = np.full((CHUNK, CHUNK), -1, np.int32)
    for i, m in enumerate(plan.table_levels + plan.direct_levels):
        ids = np.where((t > j) & ((t ^ j) >= m) & ((t ^ j) < 2 * m), first + i, ids)
    if plan.block:
        ids = np.where((t >= j) & (t // plan.block == j // plan.block), 0, ids)
    else:
        ids = np.where(t == j, plan.diag_id, ids)
    return jnp.asarray(table, bf16), jnp.asarray(np.stack([ids, ids.T]).astype(np.int32))


def _in_chunk_scores(q, k, cum, tabled, lvl, ones, rev, key_masks, plan):
    qb = q.astype(bf16)
    q_heads = [qb if mask is None else qb * mask for mask in key_masks]
    kt = k.astype(bf16).T
    if plan.block:
        inside = tabled[0] if plan.block_rows else cum
        grow, decay = jnp.exp2(-inside).astype(bf16), jnp.exp2(inside).astype(bf16)
        keys = kt * grow.T
        scores = [jnp.where(lvl == 0, _dot(qh * decay, keys), 0.0) for qh in q_heads]
        tabled = tabled[plan.block_rows:]
    else:
        diag_keys = kt * ones
        scores = [jnp.where(lvl == plan.diag_id, _dot(qh, diag_keys), 0.0) for qh in q_heads]
    first = 1 if plan.block else 0
    for i in range(len(plan.table_levels)):
        e = jnp.exp2(tabled[i]).astype(bf16)
        keys = kt * e.T
        scores = [jnp.where(lvl == first + i, _dot(qh * e, keys), sc) for qh, sc in zip(q_heads, scores)]
    for i, m in enumerate(plan.direct_levels, start=first + len(plan.table_levels)):
        blocks = []
        for p0 in range(0, CHUNK, 2 * m):
            left, right = slice(p0, p0 + m), slice(p0 + m, p0 + 2 * m)
            q_side, k_side = (left, right) if rev else (right, left)
            mid_row = p0 + m if rev else p0 + m - 1
            blocks.append((q_side, k_side, cum[mid_row:mid_row + 1, :]))
        q_decay = [jnp.exp2(cum[qs] - mid).astype(bf16) for qs, _, mid in blocks]
        k_decay = []
        for _, ks, mid in blocks:
            decay = jnp.exp2(mid - cum[ks]).astype(bf16)
            zero = jnp.zeros((m, HEAD), bf16)
            k_decay += [zero, decay] if rev else [decay, zero]
        keys = kt * jnp.concatenate(k_decay, axis=0).T
        for h, qh in enumerate(q_heads):
            s = _dot(jnp.concatenate([qh[qs] * e for (qs, _, _), e in zip(blocks, q_decay)], axis=0), keys)
            rows = []
            for b, (qs, ks, _) in enumerate(blocks):
                updated = jnp.where(lvl[qs, :] == i, s[b * m:(b + 1) * m, :], scores[h][qs, :])
                rows += [updated, scores[h][ks, :]] if rev else [scores[h][ks, :], updated]
            scores[h] = jnp.concatenate(rows, axis=0)
    return [sc.astype(bf16) for sc in scores]


def _store_log2_split(g_ref, rows, g, block):
    width = g.shape[-1]
    x = g * LOG2E
    hi = x.astype(bf16)
    g_ref[rows, 0:width] = hi
    g_ref[rows, width:2 * width] = (x - hi.astype(f32)).astype(bf16)
    return jnp.min(jnp.sum(x.reshape(-1, block, width), axis=1))


def _decay_sums(g_split, table):
    width = g_split.shape[-1] // 2
    return _dot(table, jnp.concatenate([g_split[:, :width], g_split[:, width:]], axis=0))


def _key_masks(heads_per_group):
    if heads_per_group == 1:
        return [None]
    lane = lax.broadcasted_iota(jnp.int32, (1, HEAD), 1)
    return [(lane // (HEAD // heads_per_group) == sub).astype(bf16) for sub in range(heads_per_group)]


def _gated_chunks(chains, sums_ref, upcoming, upcoming_ref, heads_per_group, plan):
    masks = _key_masks(heads_per_group)
    groups = [slice(g * HEAD, (g + 1) * HEAD) for g in range(PAIR)]
    n_heads = PAIR * heads_per_group
    group_of = [h // heads_per_group for h in range(n_heads)]
    mask_of = [masks[h % heads_per_group] for h in range(n_heads)]
    flat = [(c, r) for c, (_, units) in enumerate(chains) for r in range(len(units))]
    unit = {(c, r): chains[c][1][r] for c, r in flat}
    index = {key: u for u, key in enumerate(flat)}
    cum_rows = slice(plan.table_blocks * CHUNK, (plan.table_blocks + 1) * CHUNK)
    cum = {key: sums_ref[index[key], cum_rows, :] for key in flat}
    total = {key: sums_ref[index[key], plan.row_total:plan.row_total + 1, :] for key in flat}
    vb = {key: [unit[key][2][:, h * HEAD:(h + 1) * HEAD] for h in range(n_heads)] for key in flat}
    upcoming = list(enumerate(upcoming))
    kv, whole = {}, {}
    for key in flat:
        k = unit[key][1]
        keys = (k * jnp.exp2(total[key] - cum[key])).astype(bf16)
        kv[key] = []
        for h in range(n_heads):
            own = keys[:, groups[group_of[h]]]
            kv[key].append(_dot_tn(own if mask_of[h] is None else own * mask_of[h], vb[key][h]))
        whole[key] = [jnp.broadcast_to(jnp.exp2(total[key][:, lanes]), (HEAD, HEAD)).T for lanes in groups]
    state = [list(states) for states, _ in chains]
    outs, pending, carried = {key: [None] * n_heads for key in flat}, None, {}

    def finish(done):
        pkey, ph, psc = done
        queries, entering = carried[pkey]
        lhs = jnp.concatenate([queries[:, groups[group_of[ph]]], psc], axis=1)
        outs[pkey][ph] = _dot(lhs, jnp.concatenate([entering[ph], vb[pkey][ph]], axis=0))

    for r in range(max(len(units) for _, units in chains)):
        live = [key for key in flat if key[1] == r]
        for key in live:
            queries = (unit[key][0] * jnp.exp2(cum[key])).astype(bf16)
            carried[key] = (queries, [s.astype(bf16) for s in state[key[0]]])
            state[key[0]] = [whole[key][group_of[h]] * state[key[0]][h] + kv[key][h] for h in range(n_heads)]
        for key in live:
            q, k, _, lvl, rev = unit[key]
            ones = (lvl[0:1, :] >= -1).astype(bf16)
            for g, lanes in enumerate(groups):
                tabled = [sums_ref[index[key], i * CHUNK:(i + 1) * CHUNK, lanes] for i in range(plan.table_blocks)]
                group_scores = _in_chunk_scores(q[:, lanes], k[:, lanes], cum[key][:, lanes], tabled, lvl, ones, rev,
                                                masks, plan)
                for sub, sc in enumerate(group_scores):
                    if pending is not None:
                        finish(pending)
                    pending = (key, g * heads_per_group + sub, sc)
            if upcoming:
                u, (g_split, table) = upcoming.pop(0)
                upcoming_ref[u, 0:plan.row_total + 8, :] = _decay_sums(g_split, table)
    for u, (g_split, table) in upcoming:
        upcoming_ref[u, 0:plan.row_total + 8, :] = _decay_sums(g_split, table)
    finish(pending)
    return [[outs[(c, r)] for r in range(len(units))] for c, (_, units) in enumerate(chains)], state


def _gated_scans(q_s, k_refs, v_s, g_refs, s_scr, sums_scr, table_ref, lvl_ref, o_scr, seq, heads_per_group, plan):
    n = seq // CHUNK
    n_seq = STEP_TOKENS // seq
    n_heads = PAIR * heads_per_group
    units = GATED_HEADS_PER_ITER // n_heads
    per_iter = min(n, units // 2)
    seqs_per_iter = min(n_seq, units // (2 * per_iter))
    assert n % per_iter == 0 and n_seq % seqs_per_iter == 0
    iters_per_seq = n // per_iter
    n_iters = (n_seq // seqs_per_iter) * iters_per_seq

    def layout(it):
        jj, i = it // iters_per_seq, it % iters_per_seq
        chains = []
        for js in range(seqs_per_iter):
            j = jj * seqs_per_iter + js
            for d in range(2):
                steps = [i * per_iter + r for r in range(per_iter)]
                chains.append((j, d, [_chunk_rows(j * seq, n - 1 - t if d else t) for t in steps]))
        return chains

    def sums_inputs(it):
        return [(g_refs[d][rw, :], table_ref[d]) for _, d, rws in layout(it) for rw in rws]

    for u, (g_split, table) in enumerate(sums_inputs(0)):
        sums_scr[0, u, 0:plan.row_total + 8, :] = _decay_sums(g_split, table)

    def iteration(it, slot):
        chains = layout(it)
        args = [([s_scr[j, d, h] for h in range(n_heads)],
                 [(q_s[rw, :], k_refs[d][rw, :], v_s[rw, :], lvl_ref[d], bool(d)) for rw in rws])
                for j, d, rws in chains]
        upcoming = sums_inputs(jnp.minimum(it + 1, n_iters - 1))
        outs, new_states = _gated_chunks(args, sums_scr.at[slot], upcoming, sums_scr.at[1 - slot], heads_per_group,
                                         plan)
        for c, (j, d, rws) in enumerate(chains):
            for h in range(n_heads):
                for r, rw in enumerate(rws):
                    o_scr[d, h, rw, :] = outs[c][r][h]
                s_scr[j, d, h] = new_states[c][h]

    assert n_iters % 2 == 0

    def body(it2, carry):
        iteration(2 * it2, 0)
        iteration(2 * it2 + 1, 1)
        return carry
    lax.fori_loop(0, n_iters // 2, body, 0)


def _even_kernel(*refs, seq, has_state, emit_state, final):
    it = iter(refs)
    x_ref, mod_ref, nw_ref = next(it), next(it), next(it)
    win_ref, wlow_ref, wo_ref = next(it), next(it), next(it)
    gkw_ref, gkb_ref, lb_ref, gnw_ref, fnw_ref = next(it), next(it), next(it), next(it), next(it)
    plan_tables = {plan: (next(it), next(it)) for plan in ALL_PLANS}
    s0a_ref, s0b_ref = (next(it), next(it)) if has_state else (None, None)
    out_ref = next(it)
    sta_ref, stb_ref = (next(it), next(it)) if emit_state else (None, None)
    h_scr, o_scr, gate_s, q_s, v_s, kf_s, kb_s, gf_s, gb_s, s_scr, sums_scr, bound_s = it
    n_seq = STEP_TOKENS // seq
    n_tiles = STEP_TOKENS // ROW_TILE

    _modulated_norm(x_ref, mod_ref, nw_ref, h_scr)

    width = PAIR * HEAD

    def run_scans(first_head, s0_ref, st_ref, h0, heads_per_group, k_refs, plans, block_decays, first=False,
                  last=False):
        bounded, robust, limit = plans
        bound_s[0] = (functools.reduce(jnp.minimum, block_decays) >= -limit).astype(jnp.int32)
        n_heads = PAIR * heads_per_group
        key_rows = HEAD // heads_per_group
        own_rows = [pl.ds((h % heads_per_group) * key_rows, key_rows) for h in range(n_heads)]
        for j in range(n_seq):
            for d in range(2):
                for h in range(n_heads):
                    if s0_ref is None or heads_per_group > 1:
                        s_scr[j, d, h] = jnp.zeros((HEAD, HEAD), f32)
                    if s0_ref is not None:
                        s_scr[j, d, h, own_rows[h], :] = s0_ref[j, d, h0 + h]
        def scans(plan):
            table_ref, lvl_ref = plan_tables[plan]
            _gated_scans(q_s, k_refs, v_s, (gf_s, gb_s), s_scr, sums_scr, table_ref, lvl_ref, o_scr, seq,
                         heads_per_group, plan)
        lax.cond(bound_s[0] == 1, functools.partial(scans, bounded), functools.partial(scans, robust))
        if st_ref is not None:
            for j in range(n_seq):
                for d in range(2):
                    for h in range(n_heads):
                        st_ref[j, d, h0 + h] = s_scr[j, d, h, own_rows[h], :]
        _add_heads_output(first_head, n_heads, gate_s, o_scr, gnw_ref, wo_ref, out_ref, first,
                          (x_ref, mod_ref, fnw_ref, final) if last else None)

    def hgrn_pair(pp, carry):
        lb = lb_ref[:, _pair_lanes(pp)]
        log_lb = jnp.log(lb)

        tiles = [pl.ds(i * ROW_TILE, ROW_TILE) for i in range(n_tiles)]
        projected = [_project(h_scr[rows, :], win_ref, pp, HGRN_COLS) for rows in tiles]
        block_decays = []
        for rows, (query, value, forget_f, forget_b, gate) in zip(tiles, projected):
            gate_s[rows, 0:width] = _silu(gate).astype(bf16)
            q_s[rows, :] = _silu(query)
            v_s[rows, 0:width] = value.astype(bf16)
            for a, k_s, g_s in ((forget_f, kf_s, gf_s), (forget_b, kb_s, gb_s)):
                z = log_lb - a
                u = jnp.exp(-jnp.abs(a))
                w = jnp.exp(-jnp.abs(z))
                r = 1.0 / (1.0 + u)
                log_f = jnp.maximum(z, 0.0) + jnp.minimum(a, 0.0) + jnp.log((1.0 + w) * r)
                block_decays.append(_store_log2_split(g_s, rows, log_f, HGRN_PLANS[0].block))
                k_s[rows, :] = (1.0 - lb) * jnp.where(a >= 0.0, u * r, r)
        run_scans(PAIR * pp, s0a_ref, sta_ref, PAIR * pp, 1, (kf_s, kb_s), HGRN_PLANS + (HGRN_DECAY_LIMIT,), block_decays)
        return carry

    def clear(i, carry):
        out_ref[0, _row_tile(i), :] = jnp.zeros((ROW_TILE, out_ref.shape[-1]), f32)
        return carry
    lax.fori_loop(0, n_tiles, clear, 0)
    lax.fori_loop(0, N_HGRN // PAIR, hgrn_pair, 0)

    assert N_GLA == PAIR * GLA_HEADS_PER_GROUP

    def gla_project(rows):
        h = h_scr[rows, :]
        narrow = [_dot(h, win_ref[:, c0:c0 + N_GLA * GLA_DK]) for c0 in GLA_QK_COLS]
        wide = [_dot(h, win_ref[:, c0:c0 + N_GLA * HEAD]) for c0 in GLA_COLS]
        return narrow + wide + [_dot(h, wlow_ref[...])]

    tiles = [pl.ds(i * ROW_TILE, ROW_TILE) for i in range(n_tiles)]
    projected = [gla_project(rows) for rows in tiles]
    block_decays = []
    for rows, (query, key, value, gate, low) in zip(tiles, projected):
        gate_s[rows, :] = _silu(gate).astype(bf16)
        q_s[rows, :] = query * (GLA_DK ** -0.5)
        kf_s[rows, :] = key
        v_s[rows, :] = value.astype(bf16)
        low = low.astype(bf16)
        for d, g_s in enumerate((gf_s, gb_s)):
            logits = _dot(low, gkw_ref[d]) + gkb_ref[d]
            log_gate = jnp.minimum(logits, 0.0) - jnp.log(1.0 + jnp.exp(-jnp.abs(logits)))
            block_decays.append(_store_log2_split(g_s, rows, log_gate * (1.0 / GLA_GATE_NORM), GLA_PLANS[0].block))
    run_scans(N_HGRN, s0b_ref, stb_ref, 0, GLA_HEADS_PER_GROUP, (kf_s, kf_s), GLA_PLANS + (GLA_DECAY_LIMIT,), block_decays, last=True)


def _odd_kernel(*refs, seq, has_state, emit_state, use_rope, final):
    it = iter(refs)
    lg_ref = next(it)
    x_ref, mod_ref, nw_ref = next(it), next(it), next(it)
    wr_ref, wo_ref, gnw_ref, fnw_ref = next(it), next(it), next(it), next(it)
    cos_ref, sin_ref = (next(it), next(it)) if use_rope else (None, None)
    s0_ref = next(it) if has_state else None
    out_ref = next(it)
    st_ref = next(it) if emit_state else None
    h_scr, o_scr, gate_s, q_s, kt_s, v_s, s_scr, dec_scr = it
    n_seq = STEP_TOKENS // seq
    n_tiles = STEP_TOKENS // ROW_TILE
    n_chunks = seq // CHUNK
    per_iter = min(n_chunks, RET_UNITS_PER_ITER // n_seq)
    assert n_chunks % per_iter == 0

    _modulated_norm(x_ref, mod_ref, nw_ref, h_scr)
    t_idx = lax.broadcasted_iota(jnp.int32, (CHUNK, CHUNK), 0)
    s_idx = lax.broadcasted_iota(jnp.int32, (CHUNK, CHUNK), 1)
    row_f = lax.broadcasted_iota(jnp.int32, (CHUNK, HEAD), 0).astype(f32)
    col_f = lax.broadcasted_iota(jnp.int32, (8, CHUNK), 1).astype(f32)
    chunk_len = jnp.full((8, HEAD), CHUNK, f32)
    if use_rope:
        lane = lax.broadcasted_iota(jnp.int32, (ROW_TILE, HEAD), 1)
        first_quarter = (lane // (HEAD // 4)) % 2 == 0

    def rope(x, cos, sin_signed):
        xr = jnp.where(first_quarter, pltpu.roll(x, HEAD - HEAD // 4, axis=1), pltpu.roll(x, HEAD // 4, axis=1))
        return x * cos + xr * sin_signed

    width = PAIR * HEAD

    def pair_body(pp):
        def proj(i, c):
            rows = _row_tile(i)
            q, k, value, gate = _project(h_scr[rows, :], wr_ref, pp, RET_COLS)
            gate_s[rows, :] = _silu(gate).astype(bf16)
            k = k * (HEAD ** -0.5)
            if use_rope:
                cos, sin_signed = cos_ref[rows, :], sin_ref[rows, :]
                heads = [slice(h * HEAD, (h + 1) * HEAD) for h in range(PAIR)]
                q = jnp.concatenate([rope(q[:, lanes], cos, sin_signed) for lanes in heads], axis=-1)
                k = jnp.concatenate([rope(k[:, lanes], cos, sin_signed) for lanes in heads], axis=-1)
            q_s[rows, :] = q.astype(bf16)
            kt_s[:, rows] = k.T
            v_s[rows, :] = value.astype(bf16)
            return c
        lax.fori_loop(0, n_tiles, proj, 0)
        for h in range(PAIR):
            scan_head(PAIR * pp + h, h)
        last = pp == N_RET // PAIR - 1
        _add_heads_output(PAIR * pp, PAIR, gate_s, o_scr, gnw_ref, wo_ref, out_ref, pp == 0,
                          (x_ref, mod_ref, fnw_ref, final) if last else None)

    def scan_head(hh, h):
        lanes = slice(h * HEAD, (h + 1) * HEAD)
        lg_f = lg_ref[0, hh]
        lg_b = lg_ref[1, hh]
        for j in range(n_seq):
            for d in range(2):
                s_scr[j, d] = s0_ref[j, d, hh] if has_state else jnp.zeros((HEAD, HEAD), f32)

        dist = (t_idx - s_idx).astype(f32)
        dec_scr[0] = jnp.exp(lg_f * (row_f + 1.0))
        dec_scr[1] = jnp.exp(lg_b * (CHUNK - row_f))
        dec_scr[2] = (jnp.where(t_idx >= s_idx, jnp.exp(lg_f * jnp.maximum(dist, 0.0)), 0.0)
                      + jnp.where(s_idx >= t_idx, jnp.exp(lg_b * jnp.maximum(-dist, 0.0)), 0.0))
        dec_scr[3, 0:8, :] = jnp.exp(lg_f * (CHUNK - 1.0 - col_f))
        dec_scr[3, 8:16, :] = jnp.exp(lg_b * col_f)
        dec_scr[3, 16:24, :] = jnp.exp(lg_f * chunk_len)
        dec_scr[3, 24:32, :] = jnp.exp(lg_b * chunk_len)

        def body(i, c):
            units = [(j, r) for j in range(n_seq) for r in range(per_iter)]
            rows = {(j, r, d): _chunk_rows(j * seq, n_chunks - 1 - (i * per_iter + r) if d else i * per_iter + r)
                    for j, r in units for d in range(2)}
            scores = {u: _dot(q_s[rows[u + (0,)], lanes], kt_s[lanes, rows[u + (0,)]].astype(bf16)) for u in units}
            kv = {}
            for j, r in units:
                for d in range(2):
                    rw = rows[(j, r, d)]
                    keys = (kt_s[lanes, rw] * dec_scr[3, 8 * d:8 * d + 1, :]).astype(bf16)
                    kv[(j, r, d)] = _dot(keys, v_s[rw, lanes])
            state = {(j, d): s_scr[j, d] for j in range(n_seq) for d in range(2)}
            carried = {}
            for r in range(per_iter):
                for j in range(n_seq):
                    for d in range(2):
                        carried[(j, r, d)] = _dot(q_s[rows[(j, r, d)], lanes], state[(j, d)].astype(bf16))
                        state[(j, d)] = dec_scr[3, 16 + 8 * d:17 + 8 * d, :] * state[(j, d)] + kv[(j, r, d)]
            for j, r in units:
                rw = rows[(j, r, 0)]
                inside = _dot((scores[(j, r)] * dec_scr[2]).astype(bf16), v_s[rw, lanes])
                o_scr[0, h, rw, :] = inside + dec_scr[0] * carried[(j, r, 0)]
                o_scr[1, h, rows[(j, r, 1)], :] = dec_scr[1] * carried[(j, r, 1)]
            for (j, d), s in state.items():
                s_scr[j, d] = s
            return c
        lax.fori_loop(0, n_chunks // per_iter, body, 0)
        if emit_state:
            for j in range(n_seq):
                for d in range(2):
                    st_ref[j, d, hh] = s_scr[j, d]

    for pp in range(N_RET // PAIR):
        pair_body(pp)


def _const_spec(shape):
    zeros = (0,) * len(shape)
    return pl.BlockSpec(shape, lambda i: zeros, pipeline_mode=pl.Buffered(1))


def _step_spec(shape, per_step, buffers=None):
    zeros = (0,) * (len(shape) - 1)
    mode = {} if buffers is None else {"pipeline_mode": pl.Buffered(buffers)}
    return pl.BlockSpec((per_step,) + tuple(shape[1:]), lambda i: (i,) + zeros, **mode)


def _mod_spec(d, per_sequence):
    if per_sequence:
        return pl.BlockSpec((1, 3, d), lambda i: (i + 1, 0, 0))
    return pl.BlockSpec((1, 3, d), lambda i: (0, 0, 0))


def _layer_call(body, x, mod, consts, states, state_shapes, per_sequence_mod, scratch, name, smem_inputs=(),
                state_buffers=None):
    n_seq, seq, d = x.shape
    per_step = STEP_TOKENS // seq
    assert per_step * seq == STEP_TOKENS and n_seq % per_step == 0 and seq % CHUNK == 0
    assert not per_sequence_mod or per_step == 1
    n_steps = n_seq // per_step
    xs = x.reshape(n_steps, STEP_TOKENS, d)
    inputs = list(smem_inputs) + [xs, mod] + list(consts) + list(states)
    in_specs = [pl.BlockSpec(memory_space=pltpu.SMEM)] * len(smem_inputs)
    in_specs += [_step_spec(xs.shape, 1), _mod_spec(d, per_sequence_mod)]
    in_specs += [_const_spec(a.shape) for a in consts]
    in_specs += [_step_spec(s.shape, per_step) for s in states]
    out_shape = [jax.ShapeDtypeStruct(xs.shape, f32)] + [jax.ShapeDtypeStruct(s, f32) for s in state_shapes]
    out_specs = [_step_spec(xs.shape, 1)] + [_step_spec(s, per_step, buffers=state_buffers) for s in state_shapes]
    outs = pl.pallas_call(
        body,
        grid=(n_steps,),
        in_specs=in_specs,
        out_specs=out_specs,
        out_shape=out_shape,
        scratch_shapes=scratch,
        compiler_params=pltpu.CompilerParams(dimension_semantics=("arbitrary",), vmem_limit_bytes=VMEM_LIMIT_BYTES),
        name=name,
    )(*inputs)
    return [outs[0].reshape(x.shape)] + list(outs[1:])


def _even_layer(x, mod, norm_w, w, final_w, tables, states, emit_state, per_sequence_mod, final):
    n_seq, seq, d = x.shape
    consts = [norm_w, w["win"], w["wlow"], w["wout"], w["gkw"], w["gkb"], w["lb"], w["gnw"],
              final_w] + list(tables)
    state_shapes = [(n_seq, 2, N_HGRN, HEAD, HEAD), (n_seq, 2, N_GLA, GLA_DK, HEAD)] if emit_state else []
    scan_heads = max(PAIR, N_GLA)
    scratch = [
        pltpu.VMEM((STEP_TOKENS, d), bf16),
        pltpu.VMEM((2, scan_heads, STEP_TOKENS, HEAD), f32),
        pltpu.VMEM((STEP_TOKENS, scan_heads * HEAD), bf16),
        pltpu.VMEM((STEP_TOKENS, PAIR * HEAD), f32),
        pltpu.VMEM((STEP_TOKENS, scan_heads * HEAD), bf16),
        pltpu.VMEM((STEP_TOKENS, PAIR * HEAD), f32),
        pltpu.VMEM((STEP_TOKENS, PAIR * HEAD), f32),
        pltpu.VMEM((STEP_TOKENS, 2 * PAIR * HEAD), bf16),
        pltpu.VMEM((STEP_TOKENS, 2 * PAIR * HEAD), bf16),
        pltpu.VMEM((STEP_TOKENS // seq, 2, scan_heads, HEAD, HEAD), f32),
        pltpu.VMEM((2, GATED_UNITS_MAX, TABLE_ROWS_MAX, PAIR * HEAD), f32),
        pltpu.SMEM((1,), jnp.int32),
    ]
    body = functools.partial(_even_kernel, seq=seq, has_state=states is not None, emit_state=emit_state, final=final)
    return _layer_call(body, x, mod, consts, states or (), state_shapes, per_sequence_mod, scratch,
                       "even_layer_seq%d" % seq, state_buffers=1)


def _odd_layer(x, mod, norm_w, w, final_w, log_decay, rope, state, emit_state, per_sequence_mod, final):
    n_seq, seq, d = x.shape
    consts = [norm_w, w["wr"], w["wout"], w["gnw"], final_w] + list(rope or ())
    state_shapes = [(n_seq, 2, N_RET, HEAD, HEAD)] if emit_state else []
    scratch = [
        pltpu.VMEM((STEP_TOKENS, d), bf16),
        pltpu.VMEM((2, PAIR, STEP_TOKENS, HEAD), f32),
        pltpu.VMEM((STEP_TOKENS, PAIR * HEAD), bf16),
        pltpu.VMEM((STEP_TOKENS, PAIR * HEAD), bf16),
        pltpu.VMEM((PAIR * HEAD, STEP_TOKENS), f32),
        pltpu.VMEM((STEP_TOKENS, PAIR * HEAD), bf16),
        pltpu.VMEM((STEP_TOKENS // seq, 2, HEAD, HEAD), f32),
        pltpu.VMEM((4, CHUNK, HEAD), f32),
    ]
    body = functools.partial(_odd_kernel, seq=seq, has_state=state is not None, emit_state=emit_state,
                             use_rope=rope is not None, final=final)
    return _layer_call(body, x, mod, consts, () if state is None else (state,), state_shapes, per_sequence_mod,
                       scratch, "odd_layer_seq%d" % seq, smem_inputs=(log_decay,))


def _even_weights(w_in, gk_w, gk_b, lb, gn_w, w_out):
    c0 = GLA_LOW_COLS[0]
    wlow = jnp.pad(w_in[:, c0:c0 + 2 * GLA_RANK], ((0, 0), (0, HEAD - 2 * GLA_RANK)))
    gkw = jnp.stack([jnp.pad(gk_w[d], ((d * GLA_RANK, HEAD - (d + 1) * GLA_RANK), (0, 0))) for d in range(2)])
    return {"win": w_in.astype(bf16), "wlow": wlow.astype(bf16), "gkw": gkw.astype(bf16),
            "gkb": gk_b.reshape(2, 1, -1), "lb": lb.reshape(1, -1), "gnw": gn_w.reshape(1, -1),
            "wout": w_out.astype(bf16)}


def _odd_weights(w_in, gn_w, w_out):
    return {"wr": w_in.astype(bf16), "gnw": gn_w.reshape(1, -1), "wout": w_out.astype(bf16)}


def _rope_tables(seq):
    rows = seq // GRID_W
    t_row = np.repeat(np.arange(rows), GRID_W).astype(np.float32)
    t_col = np.tile(np.arange(GRID_W), rows).astype(np.float32)
    half = HEAD // 2
    inv = (ROPE_BASE ** (-np.arange(0, half, 2, dtype=np.float32) / half)).astype(np.float32)
    ang_r = t_row[:, None] * inv
    ang_c = t_col[:, None] * inv
    ang = np.concatenate([ang_r, ang_r, ang_c, ang_c], axis=-1).astype(np.float32)
    sign = np.where((np.arange(HEAD) // (HEAD // 4)) % 2 == 0, -1.0, 1.0).astype(np.float32)
    return jnp.asarray(np.cos(ang), f32), jnp.asarray(np.sin(ang) * sign, f32)


def kernel(x_prompt, x_sample, state_hgrn, state_gla, state_ret, c, c_ctx, norm_w, ada_w, ada_b, w_in_even, hgrn_lb, gla_gk_w, gla_gk_b, gn_even, w_out_even, w_in_odd, ret_decay, gn_odd, w_out_odd, final_norm_w):
    depth, d = norm_w.shape
    n_lat = x_sample.shape[0]
    n_cond = -(-(1 + n_lat) // 8) * 8
    cond = jnp.zeros((n_cond, d), f32).at[0].set(c_ctx).at[1:1 + n_lat].set(c)
    mod = _modulation(cond, ada_w, ada_b).reshape(depth, n_cond, 3, d)
    lbs = jnp.cumsum(jax.nn.softmax(hgrn_lb.astype(f32), axis=0), axis=0)
    final_w = final_norm_w.reshape(1, d)
    rope = _rope_tables(x_sample.shape[1])
    tables = sum((_scan_tables(plan) for plan in ALL_PLANS), ())

    x_c, x_l = x_prompt, x_sample
    new_hgrn, new_gla, new_ret = [], [], []
    for l in range(depth):
        i = l // 2
        final = l == depth - 1
        nw = norm_w[l].reshape(1, d)
        if l % 2 == 0:
            w = _even_weights(w_in_even[i], gla_gk_w[i], gla_gk_b[i], lbs[i], gn_even[i], w_out_even[i])
            x_c, st_a, st_b = _even_layer(x_c, mod[l], nw, w, final_w, tables, None, True, False, final)
            (x_l,) = _even_layer(x_l, mod[l], nw, w, final_w, tables, (state_hgrn[:, i], state_gla[:, i]),
                                 False, True, final)
            new_hgrn.append(st_a)
            new_gla.append(st_b)
        else:
            w = _odd_weights(w_in_odd[i], gn_odd[i], w_out_odd[i])
            log_decay = jax.nn.log_sigmoid(ret_decay[i].astype(f32))
            x_c, st_c = _odd_layer(x_c, mod[l], nw, w, final_w, log_decay, None, None, True, False, final)
            (x_l,) = _odd_layer(x_l, mod[l], nw, w, final_w, log_decay, rope, state_ret[:, i], False, True, final)
            new_ret.append(st_c)
    def stacked(states):
        return states[0][:, None] if len(states) == 1 else jnp.stack(states, axis=1)
    return (x_c, x_l, stacked(new_hgrn), stacked(new_gla), stacked(new_ret))
```

```python
import functools
from typing import NamedTuple

import numpy as np
import jax
import jax.numpy as jnp
from jax import lax
from jax.experimental import pallas as pl
from jax.experimental.pallas import tpu as pltpu

f32 = jnp.float32
bf16 = jnp.bfloat16
HIGHEST = lax.Precision.HIGHEST

EPS = 1e-6
LOG2E = 1.4426950408889634
HEAD = 128
N_HGRN = 4
N_GLA = 4
GLA_DK = 64
N_RET = 8
N_HEADS = 8
GLA_RANK = 16
GLA_GATE_NORM = 16.0
GRID_W = 64
ROPE_BASE = 10000.0

EVEN_SPLITS = (N_HGRN * HEAD,) * 5 + (N_GLA * GLA_DK,) * 2 + (N_GLA * HEAD,) * 2 + (GLA_RANK,) * 2
EVEN_STARTS = tuple(int(c) for c in np.cumsum((0,) + EVEN_SPLITS[:-1]))
HGRN_COLS = EVEN_STARTS[0:5]
GLA_QK_COLS = EVEN_STARTS[5:7]
GLA_COLS = EVEN_STARTS[7:9]
GLA_LOW_COLS = EVEN_STARTS[9:11]
RET_COLS = tuple(i * N_RET * HEAD for i in range(4))

CHUNK = 128


class _ScanPlan(NamedTuple):
    block: int
    table_levels: tuple
    direct_levels: tuple

    @property
    def block_rows(self):
        return 1 if 0 < self.block < CHUNK else 0

    @property
    def table_blocks(self):
        return self.block_rows + len(self.table_levels)

    @property
    def row_total(self):
        return (self.table_blocks + 1) * CHUNK

    @property
    def diag_id(self):
        return len(self.table_levels) + len(self.direct_levels)


ROBUST_PLAN = _ScanPlan(0, (1, 2, 4), (8, 16, 32, 64))
BLOCK_PLAN = _ScanPlan(64, (), (64,))
CHUNK_PLAN = _ScanPlan(CHUNK, (), ())
HGRN_PLANS = (BLOCK_PLAN, ROBUST_PLAN)
GLA_PLANS = (CHUNK_PLAN, ROBUST_PLAN)
ALL_PLANS = (ROBUST_PLAN, BLOCK_PLAN, CHUNK_PLAN)
HGRN_DECAY_LIMIT = 104.0
GLA_DECAY_LIMIT = 90.0
TABLE_ROWS_MAX = max(p.row_total for p in ALL_PLANS) + 8
PAIR = 2
GATED_HEADS_PER_ITER = 8
GATED_UNITS_MAX = 4
GLA_HEADS_PER_GROUP = HEAD // GLA_DK
RET_UNITS_PER_ITER = 8
STEP_TOKENS = 1024
ROW_TILE = 512
MOD_COLS = 768
VMEM_LIMIT_BYTES = 61 * 1024 * 1024


def _dot(a, b, precision=None):
    return jnp.dot(a, b, precision=precision, preferred_element_type=f32)


def _dot_tn(a, b):
    return lax.dot_general(a, b, (((0,), (0,)), ((), ())), preferred_element_type=f32)


def _silu(x):
    return x * jax.nn.sigmoid(x)


def _row_tile(i):
    return pl.ds(pl.multiple_of(i * ROW_TILE, ROW_TILE), ROW_TILE)


def _chunk_rows(seq_start, c):
    return pl.ds(pl.multiple_of(seq_start + c * CHUNK, CHUNK), CHUNK)


def _mod_kernel(cond_ref, w_ref, b_ref, o_ref):
    o_ref[0] = _dot(_silu(cond_ref[...]), w_ref[0], HIGHEST) + b_ref[0]


def _modulation(cond, ada_w, ada_b):
    depth, d, d3 = ada_w.shape
    rows = cond.shape[0]
    return pl.pallas_call(
        _mod_kernel,
        grid=(depth, d3 // MOD_COLS),
        in_specs=[
            pl.BlockSpec((rows, d), lambda l, j: (0, 0)),
            pl.BlockSpec((1, d, MOD_COLS), lambda l, j: (l, 0, j)),
            pl.BlockSpec((1, 1, MOD_COLS), lambda l, j: (l, 0, j)),
        ],
        out_specs=pl.BlockSpec((1, rows, MOD_COLS), lambda l, j: (l, 0, j)),
        out_shape=jax.ShapeDtypeStruct((depth, rows, d3), f32),
        compiler_params=pltpu.CompilerParams(dimension_semantics=("arbitrary", "arbitrary")),
        name="modulation",
    )(cond, ada_w, ada_b.reshape(depth, 1, d3))


def _modulated_norm(x_ref, mod_ref, nw_ref, h_scr):
    def body(i, carry):
        rows = _row_tile(i)
        x = x_ref[0, rows, :]
        y = x * lax.rsqrt(jnp.mean(x * x, axis=-1, keepdims=True) + EPS) * nw_ref[...]
        h_scr[rows, :] = (y * (1.0 + mod_ref[0, 1:2, :]) + mod_ref[0, 0:1, :]).astype(bf16)
        return carry
    lax.fori_loop(0, STEP_TOKENS // ROW_TILE, body, 0)


def _aligned_slice(start, width):
    return pl.ds(start if isinstance(start, int) else pl.multiple_of(start, width), width)


def _pair_lanes(pair):
    return _aligned_slice(pair * (PAIR * HEAD), PAIR * HEAD)


def _project(h, w_ref, pair, first_cols):
    width = PAIR * HEAD
    return [_dot(h, w_ref[:, _aligned_slice(c0 + pair * width, width)]) for c0 in first_cols]


def _add_heads_output(first_head, n_heads, gate_s, o_scr, gnw_ref, wo_ref, out_ref, first, finish):
    width = n_heads * HEAD
    lanes = _aligned_slice(first_head * HEAD, width)
    tiles = []
    for i in range(STEP_TOKENS // ROW_TILE):
        rows = pl.ds(i * ROW_TILE, ROW_TILE)
        parts = []
        for h in range(n_heads):
            o = o_scr[0, h, rows, :] + o_scr[1, h, rows, :]
            parts.append(o * lax.rsqrt(jnp.mean(o * o, axis=-1, keepdims=True) + EPS))
        y = jnp.concatenate(parts, axis=-1) * gnw_ref[:, lanes]
        tiles.append((rows, (y * gate_s[rows, 0:width].astype(f32)).astype(bf16)))
    products = [(rows, _dot(z, wo_ref[lanes, :])) for rows, z in tiles]
    for rows, product in products:
        mixed = product if first else out_ref[0, rows, :] + product
        if finish is not None:
            x_ref, mod_ref, fnw_ref, final = finish
            mixed = x_ref[0, rows, :] + mod_ref[0, 2:3, :] * mixed
            if final:
                mixed = mixed * lax.rsqrt(jnp.mean(mixed * mixed, axis=-1, keepdims=True) + EPS) * fnw_ref[...]
        out_ref[0, rows, :] = mixed


def _scan_tables(plan):
    t = np.arange(CHUNK)[:, None]
    j = np.arange(CHUNK)[None, :]
    fwd = []
    if plan.block_rows:
        fwd.append((j <= t) & (j // plan.block == t // plan.block))
    for m in plan.table_levels:
        mid = (t // (2 * m)) * (2 * m) + m
        right = t >= mid
        fwd.append(np.where(right, (j >= mid) & (j <= t), (j > t) & (j < mid)))
    fwd.append(j <= t)
    fwd.append(np.ones((8, CHUNK), bool))
    fwd = np.concatenate(fwd, axis=0).astype(np.float32)
    bwd = fwd.copy()
    n_sym = plan.row_total // CHUNK
    bwd[:plan.row_total] = fwd[:plan.row_total].reshape(n_sym, CHUNK, CHUNK)[:, ::-1, ::-1].reshape(-1, CHUNK)
    table = np.stack([np.tile(fwd, (1, 2)), np.tile(bwd, (1, 2))])
    first = 1 if plan.block else 0
    ids = np.full((CHUNK, CHUNK), -1, np.int32)
    for i, m in enumerate(plan.table_levels + plan.direct_levels):
        ids = np.where((t > j) & ((t ^ j) >= m) & ((t ^ j) < 2 * m), first + i, ids)
    if plan.block:
        ids = np.where((t >= j) & (t // plan.block == j // plan.block), 0, ids)
    else:
        ids = np.where(t == j, plan.diag_id, ids)
    return jnp.asarray(table, bf16), jnp.asarray(np.stack([ids, ids.T]).astype(np.int32))


def _in_chunk_scores(q, k, cum, tabled, lvl, ones, rev, key_masks, plan):
    qb = q.astype(bf16)
    q_heads = [qb if mask is None else qb * mask for mask in key_masks]
    kt = k.astype(bf16).T
    if plan.block:
        inside = tabled[0] if plan.block_rows else cum
        grow, decay = jnp.exp2(-inside).astype(bf16), jnp.exp2(inside).astype(bf16)
        keys = kt * grow.T
        scores = [jnp.where(lvl == 0, _dot(qh * decay, keys), 0.0) for qh in q_heads]
        tabled = tabled[plan.block_rows:]
    else:
        diag_keys = kt * ones
        scores = [jnp.where(lvl == plan.diag_id, _dot(qh, diag_keys), 0.0) for qh in q_heads]
    first = 1 if plan.block else 0
    for i in range(len(plan.table_levels)):
        e = jnp.exp2(tabled[i]).astype(bf16)
        keys = kt * e.T
        scores = [jnp.where(lvl == first + i, _dot(qh * e, keys), sc) for qh, sc in zip(q_heads, scores)]
    for i, m in enumerate(plan.direct_levels, start=first + len(plan.table_levels)):
        blocks = []
        for p0 in range(0, CHUNK, 2 * m):
            left, right = slice(p0, p0 + m), slice(p0 + m, p0 + 2 * m)
            q_side, k_side = (left, right) if rev else (right, left)
            mid_row = p0 + m if rev else p0 + m - 1
            blocks.append((q_side, k_side, cum[mid_row:mid_row + 1, :]))
        q_decay = [jnp.exp2(cum[qs] - mid).astype(bf16) for qs, _, mid in blocks]
        k_decay = []
        for _, ks, mid in blocks:
            decay = jnp.exp2(mid - cum[ks]).astype(bf16)
            zero = jnp.zeros((m, HEAD), bf16)
            k_decay += [zero, decay] if rev else [decay, zero]
        keys = kt * jnp.concatenate(k_decay, axis=0).T
        for h, qh in enumerate(q_heads):
            s = _dot(jnp.concatenate([qh[qs] * e for (qs, _, _), e in zip(blocks, q_decay)], axis=0), keys)
            rows = []
            for b, (qs, ks, _) in enumerate(blocks):
                updated = jnp.where(lvl[qs, :] == i, s[b * m:(b + 1) * m, :], scores[h][qs, :])
                rows += [updated, scores[h][ks, :]] if rev else [scores[h][ks, :], updated]
            scores[h] = jnp.concatenate(rows, axis=0)
    return [sc.astype(bf16) for sc in scores]


def _store_log2_split(g_ref, rows, g, block):
    width = g.shape[-1]
    x = g * LOG2E
    hi = x.astype(bf16)
    g_ref[rows, 0:width] = hi
    g_ref[rows, width:2 * width] = (x - hi.astype(f32)).astype(bf16)
    return jnp.min(jnp.sum(x.reshape(-1, block, width), axis=1))


def _decay_sums(g_split, table):
    width = g_split.shape[-1] // 2
    return _dot(table, jnp.concatenate([g_split[:, :width], g_split[:, width:]], axis=0))


def _key_masks(heads_per_group):
    if heads_per_group == 1:
        return [None]
    lane = lax.broadcasted_iota(jnp.int32, (1, HEAD), 1)
    return [(lane // (HEAD // heads_per_group) == sub).astype(bf16) for sub in range(heads_per_group)]


def _gated_chunks(chains, sums_ref, upcoming, upcoming_ref, heads_per_group, plan):
    masks = _key_masks(heads_per_group)
    groups = [slice(g * HEAD, (g + 1) * HEAD) for g in range(PAIR)]
    n_heads = PAIR * heads_per_group
    group_of = [h // heads_per_group for h in range(n_heads)]
    mask_of = [masks[h % heads_per_group] for h in range(n_heads)]
    flat = [(c, r) for c, (_, units) in enumerate(chains) for r in range(len(units))]
    unit = {(c, r): chains[c][1][r] for c, r in flat}
    index = {key: u for u, key in enumerate(flat)}
    cum_rows = slice(plan.table_blocks * CHUNK, (plan.table_blocks + 1) * CHUNK)
    cum = {key: sums_ref[index[key], cum_rows, :] for key in flat}
    total = {key: sums_ref[index[key], plan.row_total:plan.row_total + 1, :] for key in flat}
    vb = {key: [unit[key][2][:, h * HEAD:(h + 1) * HEAD] for h in range(n_heads)] for key in flat}
    upcoming = list(enumerate(upcoming))
    kv, whole = {}, {}
    for key in flat:
        k = unit[key][1]
        keys = (k * jnp.exp2(total[key] - cum[key])).astype(bf16)
        kv[key] = []
        for h in range(n_heads):
            own = keys[:, groups[group_of[h]]]
            kv[key].append(_dot_tn(own if mask_of[h] is None else own * mask_of[h], vb[key][h]))
        whole[key] = [jnp.broadcast_to(jnp.exp2(total[key][:, lanes]), (HEAD, HEAD)).T for lanes in groups]
    state = [list(states) for states, _ in chains]
    outs, pending, carried = {key: [None] * n_heads for key in flat}, None, {}

    def finish(done):
        pkey, ph, psc = done
        queries, entering = carried[pkey]
        lhs = jnp.concatenate([queries[:, groups[group_of[ph]]], psc], axis=1)
        outs[pkey][ph] = _dot(lhs, jnp.concatenate([entering[ph], vb[pkey][ph]], axis=0))

    for r in range(max(len(units) for _, units in chains)):
        live = [key for key in flat if key[1] == r]
        for key in live:
            queries = (unit[key][0] * jnp.exp2(cum[key])).astype(bf16)
            carried[key] = (queries, [s.astype(bf16) for s in state[key[0]]])
            state[key[0]] = [whole[key][group_of[h]] * state[key[0]][h] + kv[key][h] for h in range(n_heads)]
        for key in live:
            q, k, _, lvl, rev = unit[key]
            ones = (lvl[0:1, :] >= -1).astype(bf16)
            for g, lanes in enumerate(groups):
                tabled = [sums_ref[index[key], i * CHUNK:(i + 1) * CHUNK, lanes] for i in range(plan.table_blocks)]
                group_scores = _in_chunk_scores(q[:, lanes], k[:, lanes], cum[key][:, lanes], tabled, lvl, ones, rev,
                                                masks, plan)
                for sub, sc in enumerate(group_scores):
                    if pending is not None:
                        finish(pending)
                    pending = (key, g * heads_per_group + sub, sc)
            if upcoming:
                u, (g_split, table) = upcoming.pop(0)
                upcoming_ref[u, 0:plan.row_total + 8, :] = _decay_sums(g_split, table)
    for u, (g_split, table) in upcoming:
        upcoming_ref[u, 0:plan.row_total + 8, :] = _decay_sums(g_split, table)
    finish(pending)
    return [[outs[(c, r)] for r in range(len(units))] for c, (_, units) in enumerate(chains)], state


def _gated_scans(q_s, k_refs, v_s, g_refs, s_scr, sums_scr, table_ref, lvl_ref, o_scr, seq, heads_per_group, plan):
    n = seq // CHUNK
    n_seq = STEP_TOKENS // seq
    n_heads = PAIR * heads_per_group
    units = GATED_HEADS_PER_ITER // n_heads
    per_iter = min(n, units // 2)
    seqs_per_iter = min(n_seq, units // (2 * per_iter))
    assert n % per_iter == 0 and n_seq % seqs_per_iter == 0
    iters_per_seq = n // per_iter
    n_iters = (n_seq // seqs_per_iter) * iters_per_seq

    def layout(it):
        jj, i = it // iters_per_seq, it % iters_per_seq
        chains = []
        for js in range(seqs_per_iter):
            j = jj * seqs_per_iter + js
            for d in range(2):
                steps = [i * per_iter + r for r in range(per_iter)]
                chains.append((j, d, [_chunk_rows(j * seq, n - 1 - t if d else t) for t in steps]))
        return chains

    def sums_inputs(it):
        return [(g_refs[d][rw, :], table_ref[d]) for _, d, rws in layout(it) for rw in rws]

    for u, (g_split, table) in enumerate(sums_inputs(0)):
        sums_scr[0, u, 0:plan.row_total + 8, :] = _decay_sums(g_split, table)

    def iteration(it, slot):
        chains = layout(it)
        args = [([s_scr[j, d, h] for h in range(n_heads)],
                 [(q_s[rw, :], k_refs[d][rw, :], v_s[rw, :], lvl_ref[d], bool(d)) for rw in rws])
                for j, d, rws in chains]
        upcoming = sums_inputs(jnp.minimum(it + 1, n_iters - 1))
        outs, new_states = _gated_chunks(args, sums_scr.at[slot], upcoming, sums_scr.at[1 - slot], heads_per_group,
                                         plan)
        for c, (j, d, rws) in enumerate(chains):
            for h in range(n_heads):
                for r, rw in enumerate(rws):
                    o_scr[d, h, rw, :] = outs[c][r][h]
                s_scr[j, d, h] = new_states[c][h]

    assert n_iters % 2 == 0

    def body(it2, carry):
        iteration(2 * it2, 0)
        iteration(2 * it2 + 1, 1)
        return carry
    lax.fori_loop(0, n_iters // 2, body, 0)


def _even_kernel(*refs, seq, has_state, emit_state, final):
    it = iter(refs)
    x_ref, mod_ref, nw_ref = next(it), next(it), next(it)
    win_ref, wlow_ref, wo_ref = next(it), next(it), next(it)
    gkw_ref, gkb_ref, lb_ref, gnw_ref, fnw_ref = next(it), next(it), next(it), next(it), next(it)
    plan_tables = {plan: (next(it), next(it)) for plan in ALL_PLANS}
    s0a_ref, s0b_ref = (next(it), next(it)) if has_state else (None, None)
    out_ref = next(it)
    sta_ref, stb_ref = (next(it), next(it)) if emit_state else (None, None)
    h_scr, o_scr, gate_s, q_s, v_s, kf_s, kb_s, gf_s, gb_s, s_scr, sums_scr, bound_s = it
    n_seq = STEP_TOKENS // seq
    n_tiles = STEP_TOKENS // ROW_TILE

    _modulated_norm(x_ref, mod_ref, nw_ref, h_scr)

    width = PAIR * HEAD

    def run_scans(first_head, s0_ref, st_ref, h0, heads_per_group, k_refs, plans, block_decays, first=False,
                  last=False):
        bounded, robust, limit = plans
        bound_s[0] = (functools.reduce(jnp.minimum, block_decays) >= -limit).astype(jnp.int32)
        n_heads = PAIR * heads_per_group
        key_rows = HEAD // heads_per_group
        own_rows = [pl.ds((h % heads_per_group) * key_rows, key_rows) for h in range(n_heads)]
        for j in range(n_seq):
            for d in range(2):
                for h in range(n_heads):
                    if s0_ref is None or heads_per_group > 1:
                        s_scr[j, d, h] = jnp.zeros((HEAD, HEAD), f32)
                    if s0_ref is not None:
                        s_scr[j, d, h, own_rows[h], :] = s0_ref[j, d, h0 + h]
        def scans(plan):
            table_ref, lvl_ref = plan_tables[plan]
            _gated_scans(q_s, k_refs, v_s, (gf_s, gb_s), s_scr, sums_scr, table_ref, lvl_ref, o_scr, seq,
                         heads_per_group, plan)
        lax.cond(bound_s[0] == 1, functools.partial(scans, bounded), functools.partial(scans, robust))
        if st_ref is not None:
            for j in range(n_seq):
                for d in range(2):
                    for h in range(n_heads):
                        st_ref[j, d, h0 + h] = s_scr[j, d, h, own_rows[h], :]
        _add_heads_output(first_head, n_heads, gate_s, o_scr, gnw_ref, wo_ref, out_ref, first,
                          (x_ref, mod_ref, fnw_ref, final) if last else None)

    def hgrn_pair(pp, carry):
        lb = lb_ref[:, _pair_lanes(pp)]
        log_lb = jnp.log(lb)

        tiles = [pl.ds(i * ROW_TILE, ROW_TILE) for i in range(n_tiles)]
        projected = [_project(h_scr[rows, :], win_ref, pp, HGRN_COLS) for rows in tiles]
        block_decays = []
        for rows, (query, value, forget_f, forget_b, gate) in zip(tiles, projected):
            gate_s[rows, 0:width] = _silu(gate).astype(bf16)
            q_s[rows, :] = _silu(query)
            v_s[rows, 0:width] = value.astype(bf16)
            for a, k_s, g_s in ((forget_f, kf_s, gf_s), (forget_b, kb_s, gb_s)):
                z = log_lb - a
                u = jnp.exp(-jnp.abs(a))
                w = jnp.exp(-jnp.abs(z))
                r = 1.0 / (1.0 + u)
                log_f = jnp.maximum(z, 0.0) + jnp.minimum(a, 0.0) + jnp.log((1.0 + w) * r)
                block_decays.append(_store_log2_split(g_s, rows, log_f, HGRN_PLANS[0].block))
                k_s[rows, :] = (1.0 - lb) * jnp.where(a >= 0.0, u * r, r)
        run_scans(PAIR * pp, s0a_ref, sta_ref, PAIR * pp, 1, (kf_s, kb_s), HGRN_PLANS + (HGRN_DECAY_LIMIT,), block_decays)
        return carry

    def clear(i, carry):
        out_ref[0, _row_tile(i), :] = jnp.zeros((ROW_TILE, out_ref.shape[-1]), f32)
        return carry
    lax.fori_loop(0, n_tiles, clear, 0)
    lax.fori_loop(0, N_HGRN // PAIR, hgrn_pair, 0)

    assert N_GLA == PAIR * GLA_HEADS_PER_GROUP

    def gla_project(rows):
        h = h_scr[rows, :]
        narrow = [_dot(h, win_ref[:, c0:c0 + N_GLA * GLA_DK]) for c0 in GLA_QK_COLS]
        wide = [_dot(h, win_ref[:, c0:c0 + N_GLA * HEAD]) for c0 in GLA_COLS]
        return narrow + wide + [_dot(h, wlow_ref[...])]

    tiles = [pl.ds(i * ROW_TILE, ROW_TILE) for i in range(n_tiles)]
    projected = [gla_project(rows) for rows in tiles]
    block_decays = []
    for rows, (query, key, value, gate, low) in zip(tiles, projected):
        gate_s[rows, :] = _silu(gate).astype(bf16)
        q_s[rows, :] = query * (GLA_DK ** -0.5)
        kf_s[rows, :] = key
        v_s[rows, :] = value.astype(bf16)
        low = low.astype(bf16)
        for d, g_s in enumerate((gf_s, gb_s)):
            logits = _dot(low, gkw_ref[d]) + gkb_ref[d]
            log_gate = jnp.minimum(logits, 0.0) - jnp.log(1.0 + jnp.exp(-jnp.abs(logits)))
            block_decays.append(_store_log2_split(g_s, rows, log_gate * (1.0 / GLA_GATE_NORM), GLA_PLANS[0].block))
    run_scans(N_HGRN, s0b_ref, stb_ref, 0, GLA_HEADS_PER_GROUP, (kf_s, kf_s), GLA_PLANS + (GLA_DECAY_LIMIT,), block_decays, last=True)


def _odd_kernel(*refs, seq, has_state, emit_state, use_rope, final):
    it = iter(refs)
    lg_ref = next(it)
    x_ref, mod_ref, nw_ref = next(it), next(it), next(it)
    wr_ref, wo_ref, gnw_ref, fnw_ref = next(it), next(it), next(it), next(it)
    cos_ref, sin_ref = (next(it), next(it)) if use_rope else (None, None)
    s0_ref = next(it) if has_state else None
    out_ref = next(it)
    st_ref = next(it) if emit_state else None
    h_scr, o_scr, gate_s, q_s, kt_s, v_s, s_scr, dec_scr = it
    n_seq = STEP_TOKENS // seq
    n_tiles = STEP_TOKENS // ROW_TILE
    n_chunks = seq // CHUNK
    per_iter = min(n_chunks, RET_UNITS_PER_ITER // n_seq)
    assert n_chunks % per_iter == 0

    _modulated_norm(x_ref, mod_ref, nw_ref, h_scr)
    t_idx = lax.broadcasted_iota(jnp.int32, (CHUNK, CHUNK), 0)
    s_idx = lax.broadcasted_iota(jnp.int32, (CHUNK, CHUNK), 1)
    row_f = lax.broadcasted_iota(jnp.int32, (CHUNK, HEAD), 0).astype(f32)
    col_f = lax.broadcasted_iota(jnp.int32, (8, CHUNK), 1).astype(f32)
    chunk_len = jnp.full((8, HEAD), CHUNK, f32)
    if use_rope:
        lane = lax.broadcasted_iota(jnp.int32, (ROW_TILE, HEAD), 1)
        first_quarter = (lane // (HEAD // 4)) % 2 == 0

    def rope(x, cos, sin_signed):
        xr = jnp.where(first_quarter, pltpu.roll(x, HEAD - HEAD // 4, axis=1), pltpu.roll(x, HEAD // 4, axis=1))
        return x * cos + xr * sin_signed

    width = PAIR * HEAD

    def pair_body(pp):
        tiles = [pl.ds(i * ROW_TILE, ROW_TILE) for i in range(n_tiles)]
        projected = [_project(h_scr[rows, :], wr_ref, pp, RET_COLS) for rows in tiles]
        for rows, (q, k, value, gate) in zip(tiles, projected):
            gate_s[rows, :] = _silu(gate).astype(bf16)
            k = k * (HEAD ** -0.5)
            if use_rope:
                cos, sin_signed = cos_ref[rows, :], sin_ref[rows, :]
                heads = [slice(h * HEAD, (h + 1) * HEAD) for h in range(PAIR)]
                q = jnp.concatenate([rope(q[:, lanes], cos, sin_signed) for lanes in heads], axis=-1)
                k = jnp.concatenate([rope(k[:, lanes], cos, sin_signed) for lanes in heads], axis=-1)
            q_s[rows, :] = q.astype(bf16)
            kt_s[:, rows] = k.T
            v_s[rows, :] = value.astype(bf16)
        for h in range(PAIR):
            scan_head(PAIR * pp + h, h)
        last = pp == N_RET // PAIR - 1
        _add_heads_output(PAIR * pp, PAIR, gate_s, o_scr, gnw_ref, wo_ref, out_ref, pp == 0,
                          (x_ref, mod_ref, fnw_ref, final) if last else None)

    def scan_head(hh, h):
        lanes = slice(h * HEAD, (h + 1) * HEAD)
        lg_f = lg_ref[0, hh]
        lg_b = lg_ref[1, hh]
        for j in range(n_seq):
            for d in range(2):
                s_scr[j, d] = s0_ref[j, d, hh] if has_state else jnp.zeros((HEAD, HEAD), f32)

        dist = (t_idx - s_idx).astype(f32)
        dec_scr[0] = jnp.exp(lg_f * (row_f + 1.0))
        dec_scr[1] = jnp.exp(lg_b * (CHUNK - row_f))
        dec_scr[2] = (jnp.where(t_idx >= s_idx, jnp.exp(lg_f * jnp.maximum(dist, 0.0)), 0.0)
                      + jnp.where(s_idx >= t_idx, jnp.exp(lg_b * jnp.maximum(-dist, 0.0)), 0.0))
        dec_scr[3, 0:8, :] = jnp.exp(lg_f * (CHUNK - 1.0 - col_f))
        dec_scr[3, 8:16, :] = jnp.exp(lg_b * col_f)
        dec_scr[3, 16:24, :] = jnp.exp(lg_f * chunk_len)
        dec_scr[3, 24:32, :] = jnp.exp(lg_b * chunk_len)

        def body(i, c):
            units = [(j, r) for j in range(n_seq) for r in range(per_iter)]
            rows = {(j, r, d): _chunk_rows(j * seq, n_chunks - 1 - (i * per_iter + r) if d else i * per_iter + r)
                    for j, r in units for d in range(2)}
            scores = {u: _dot(q_s[rows[u + (0,)], lanes], kt_s[lanes, rows[u + (0,)]].astype(bf16)) for u in units}
            kv = {}
            for j, r in units:
                for d in range(2):
                    rw = rows[(j, r, d)]
                    keys = (kt_s[lanes, rw] * dec_scr[3, 8 * d:8 * d + 1, :]).astype(bf16)
                    kv[(j, r, d)] = _dot(keys, v_s[rw, lanes])
            state = {(j, d): s_scr[j, d] for j in range(n_seq) for d in range(2)}
            entering, carried = {}, {}
            for r in range(per_iter):
                for j in range(n_seq):
                    entering[(j, r)] = state[(j, 0)].astype(bf16)
                    carried[(j, r)] = _dot(q_s[rows[(j, r, 1)], lanes], state[(j, 1)].astype(bf16))
                    for d in range(2):
                        state[(j, d)] = dec_scr[3, 16 + 8 * d:17 + 8 * d, :] * state[(j, d)] + kv[(j, r, d)]
            seen_f = dec_scr[0].astype(bf16)
            for j, r in units:
                rw = rows[(j, r, 0)]
                lhs = jnp.concatenate([q_s[rw, lanes] * seen_f, (scores[(j, r)] * dec_scr[2]).astype(bf16)], axis=1)
                o_scr[0, h, rw, :] = _dot(lhs, jnp.concatenate([entering[(j, r)], v_s[rw, lanes]], axis=0))
                o_scr[1, h, rows[(j, r, 1)], :] = dec_scr[1] * carried[(j, r)]
            for (j, d), s in state.items():
                s_scr[j, d] = s
            return c
        lax.fori_loop(0, n_chunks // per_iter, body, 0)
        if emit_state:
            for j in range(n_seq):
                for d in range(2):
                    st_ref[j, d, hh] = s_scr[j, d]

    for pp in range(N_RET // PAIR):
        pair_body(pp)


def _const_spec(shape):
    zeros = (0,) * len(shape)
    return pl.BlockSpec(shape, lambda i: zeros, pipeline_mode=pl.Buffered(1))


def _step_spec(shape, per_step, buffers=None):
    zeros = (0,) * (len(shape) - 1)
    mode = {} if buffers is None else {"pipeline_mode": pl.Buffered(buffers)}
    return pl.BlockSpec((per_step,) + tuple(shape[1:]), lambda i: (i,) + zeros, **mode)


def _mod_spec(d, per_sequence):
    if per_sequence:
        return pl.BlockSpec((1, 3, d), lambda i: (i + 1, 0, 0))
    return pl.BlockSpec((1, 3, d), lambda i: (0, 0, 0))


def _layer_call(body, x, mod, consts, states, state_shapes, per_sequence_mod, scratch, name, smem_inputs=(),
                state_buffers=None):
    n_seq, seq, d = x.shape
    per_step = STEP_TOKENS // seq
    assert per_step * seq == STEP_TOKENS and n_seq % per_step == 0 and seq % CHUNK == 0
    assert not per_sequence_mod or per_step == 1
    n_steps = n_seq // per_step
    xs = x.reshape(n_steps, STEP_TOKENS, d)
    inputs = list(smem_inputs) + [xs, mod] + list(consts) + list(states)
    in_specs = [pl.BlockSpec(memory_space=pltpu.SMEM)] * len(smem_inputs)
    in_specs += [_step_spec(xs.shape, 1), _mod_spec(d, per_sequence_mod)]
    in_specs += [_const_spec(a.shape) for a in consts]
    in_specs += [_step_spec(s.shape, per_step) for s in states]
    out_shape = [jax.ShapeDtypeStruct(xs.shape, f32)] + [jax.ShapeDtypeStruct(s, f32) for s in state_shapes]
    out_specs = [_step_spec(xs.shape, 1)] + [_step_spec(s, per_step, buffers=state_buffers) for s in state_shapes]
    outs = pl.pallas_call(
        body,
        grid=(n_steps,),
        in_specs=in_specs,
        out_specs=out_specs,
        out_shape=out_shape,
        scratch_shapes=scratch,
        compiler_params=pltpu.CompilerParams(dimension_semantics=("arbitrary",), vmem_limit_bytes=VMEM_LIMIT_BYTES),
        name=name,
    )(*inputs)
    return [outs[0].reshape(x.shape)] + list(outs[1:])


def _even_layer(x, mod, norm_w, w, final_w, tables, states, emit_state, per_sequence_mod, final):
    n_seq, seq, d = x.shape
    consts = [norm_w, w["win"], w["wlow"], w["wout"], w["gkw"], w["gkb"], w["lb"], w["gnw"],
              final_w] + list(tables)
    state_shapes = [(n_seq, 2, N_HGRN, HEAD, HEAD), (n_seq, 2, N_GLA, GLA_DK, HEAD)] if emit_state else []
    scan_heads = max(PAIR, N_GLA)
    scratch = [
        pltpu.VMEM((STEP_TOKENS, d), bf16),
        pltpu.VMEM((2, scan_heads, STEP_TOKENS, HEAD), f32),
        pltpu.VMEM((STEP_TOKENS, scan_heads * HEAD), bf16),
        pltpu.VMEM((STEP_TOKENS, PAIR * HEAD), f32),
        pltpu.VMEM((STEP_TOKENS, scan_heads * HEAD), bf16),
        pltpu.VMEM((STEP_TOKENS, PAIR * HEAD), f32),
        pltpu.VMEM((STEP_TOKENS, PAIR * HEAD), f32),
        pltpu.VMEM((STEP_TOKENS, 2 * PAIR * HEAD), bf16),
        pltpu.VMEM((STEP_TOKENS, 2 * PAIR * HEAD), bf16),
        pltpu.VMEM((STEP_TOKENS // seq, 2, scan_heads, HEAD, HEAD), f32),
        pltpu.VMEM((2, GATED_UNITS_MAX, TABLE_ROWS_MAX, PAIR * HEAD), f32),
        pltpu.SMEM((1,), jnp.int32),
    ]
    body = functools.partial(_even_kernel, seq=seq, has_state=states is not None, emit_state=emit_state, final=final)
    return _layer_call(body, x, mod, consts, states or (), state_shapes, per_sequence_mod, scratch,
                       "even_layer_seq%d" % seq, state_buffers=1)


def _odd_layer(x, mod, norm_w, w, final_w, log_decay, rope, state, emit_state, per_sequence_mod, final):
    n_seq, seq, d = x.shape
    consts = [norm_w, w["wr"], w["wout"], w["gnw"], final_w] + list(rope or ())
    state_shapes = [(n_seq, 2, N_RET, HEAD, HEAD)] if emit_state else []
    scratch = [
        pltpu.VMEM((STEP_TOKENS, d), bf16),
        pltpu.VMEM((2, PAIR, STEP_TOKENS, HEAD), f32),
        pltpu.VMEM((STEP_TOKENS, PAIR * HEAD), bf16),
        pltpu.VMEM((STEP_TOKENS, PAIR * HEAD), bf16),
        pltpu.VMEM((PAIR * HEAD, STEP_TOKENS), f32),
        pltpu.VMEM((STEP_TOKENS, PAIR * HEAD), bf16),
        pltpu.VMEM((STEP_TOKENS // seq, 2, HEAD, HEAD), f32),
        pltpu.VMEM((4, CHUNK, HEAD), f32),
    ]
    body = functools.partial(_odd_kernel, seq=seq, has_state=state is not None, emit_state=emit_state,
                             use_rope=rope is not None, final=final)
    return _layer_call(body, x, mod, consts, () if state is None else (state,), state_shapes, per_sequence_mod,
                       scratch, "odd_layer_seq%d" % seq, smem_inputs=(log_decay,))


def _even_weights(w_in, gk_w, gk_b, lb, gn_w, w_out):
    c0 = GLA_LOW_COLS[0]
    wlow = jnp.pad(w_in[:, c0:c0 + 2 * GLA_RANK], ((0, 0), (0, HEAD - 2 * GLA_RANK)))
    gkw = jnp.stack([jnp.pad(gk_w[d], ((d * GLA_RANK, HEAD - (d + 1) * GLA_RANK), (0, 0))) for d in range(2)])
    return {"win": w_in.astype(bf16), "wlow": wlow.astype(bf16), "gkw": gkw.astype(bf16),
            "gkb": gk_b.reshape(2, 1, -1), "lb": lb.reshape(1, -1), "gnw": gn_w.reshape(1, -1),
            "wout": w_out.astype(bf16)}


def _odd_weights(w_in, gn_w, w_out):
    return {"wr": w_in.astype(bf16), "gnw": gn_w.reshape(1, -1), "wout": w_out.astype(bf16)}


def _rope_tables(seq):
    rows = seq // GRID_W
    t_row = np.repeat(np.arange(rows), GRID_W).astype(np.float32)
    t_col = np.tile(np.arange(GRID_W), rows).astype(np.float32)
    half = HEAD // 2
    inv = (ROPE_BASE ** (-np.arange(0, half, 2, dtype=np.float32) / half)).astype(np.float32)
    ang_r = t_row[:, None] * inv
    ang_c = t_col[:, None] * inv
    ang = np.concatenate([ang_r, ang_r, ang_c, ang_c], axis=-1).astype(np.float32)
    sign = np.where((np.arange(HEAD) // (HEAD // 4)) % 2 == 0, -1.0, 1.0).astype(np.float32)
    return jnp.asarray(np.cos(ang), f32), jnp.asarray(np.sin(ang) * sign, f32)


def kernel(x_prompt, x_sample, state_hgrn, state_gla, state_ret, c, c_ctx, norm_w, ada_w, ada_b, w_in_even, hgrn_lb, gla_gk_w, gla_gk_b, gn_even, w_out_even, w_in_odd, ret_decay, gn_odd, w_out_odd, final_norm_w):
    depth, d = norm_w.shape
    n_lat = x_sample.shape[0]
    n_cond = -(-(1 + n_lat) // 8) * 8
    cond = jnp.zeros((n_cond, d), f32).at[0].set(c_ctx).at[1:1 + n_lat].set(c)
    mod = _modulation(cond, ada_w, ada_b).reshape(depth, n_cond, 3, d)
    lbs = jnp.cumsum(jax.nn.softmax(hgrn_lb.astype(f32), axis=0), axis=0)
    final_w = final_norm_w.reshape(1, d)
    rope = _rope_tables(x_sample.shape[1])
    tables = sum((_scan_tables(plan) for plan in ALL_PLANS), ())

    x_c, x_l = x_prompt, x_sample
    new_hgrn, new_gla, new_ret = [], [], []
    for l in range(depth):
        i = l // 2
        final = l == depth - 1
        nw = norm_w[l].reshape(1, d)
        if l % 2 == 0:
            w = _even_weights(w_in_even[i], gla_gk_w[i], gla_gk_b[i], lbs[i], gn_even[i], w_out_even[i])
            x_c, st_a, st_b = _even_layer(x_c, mod[l], nw, w, final_w, tables, None, True, False, final)
            (x_l,) = _even_layer(x_l, mod[l], nw, w, final_w, tables, (state_hgrn[:, i], state_gla[:, i]),
                                 False, True, final)
            new_hgrn.append(st_a)
            new_gla.append(st_b)
        else:
            w = _odd_weights(w_in_odd[i], gn_odd[i], w_out_odd[i])
            log_decay = jax.nn.log_sigmoid(ret_decay[i].astype(f32))
            x_c, st_c = _odd_layer(x_c, mod[l], nw, w, final_w, log_decay, None, None, True, False, final)
            (x_l,) = _odd_layer(x_l, mod[l], nw, w, final_w, log_decay, rope, state_ret[:, i], False, True, final)
            new_ret.append(st_c)
    def stacked(states):
        return states[0][:, None] if len(states) == 1 else jnp.stack(states, axis=1)
    return (x_c, x_l, stacked(new_hgrn), stacked(new_gla), stacked(new_ret))
```

```python
import functools
from typing import NamedTuple

import numpy as np
import jax
import jax.numpy as jnp
from jax import lax
from jax.experimental import pallas as pl
from jax.experimental.pallas import tpu as pltpu

f32 = jnp.float32
bf16 = jnp.bfloat16
HIGHEST = lax.Precision.HIGHEST

EPS = 1e-6
LOG2E = 1.4426950408889634
HEAD = 128
N_HGRN = 4
N_GLA = 4
GLA_DK = 64
N_RET = 8
N_HEADS = 8
GLA_RANK = 16
GLA_GATE_NORM = 16.0
GRID_W = 64
ROPE_BASE = 10000.0

EVEN_SPLITS = (N_HGRN * HEAD,) * 5 + (N_GLA * GLA_DK,) * 2 + (N_GLA * HEAD,) * 2 + (GLA_RANK,) * 2
EVEN_STARTS = tuple(int(c) for c in np.cumsum((0,) + EVEN_SPLITS[:-1]))
HGRN_COLS = EVEN_STARTS[0:5]
GLA_QK_COLS = EVEN_STARTS[5:7]
GLA_COLS = EVEN_STARTS[7:9]
GLA_LOW_COLS = EVEN_STARTS[9:11]
RET_COLS = tuple(i * N_RET * HEAD for i in range(4))

CHUNK = 128


class _ScanPlan(NamedTuple):
    block: int
    table_levels: tuple
    direct_levels: tuple

    @property
    def block_rows(self):
        return 1 if 0 < self.block < CHUNK else 0

    @property
    def table_blocks(self):
        return self.block_rows + len(self.table_levels)

    @property
    def row_total(self):
        return (self.table_blocks + 1) * CHUNK

    @property
    def diag_id(self):
        return len(self.table_levels) + len(self.direct_levels)


ROBUST_PLAN = _ScanPlan(0, (1, 2, 4), (8, 16, 32, 64))
BLOCK_PLAN = _ScanPlan(64, (), (64,))
CHUNK_PLAN = _ScanPlan(CHUNK, (), ())
HGRN_PLANS = (BLOCK_PLAN, ROBUST_PLAN)
GLA_PLANS = (CHUNK_PLAN, ROBUST_PLAN)
ALL_PLANS = (ROBUST_PLAN, BLOCK_PLAN, CHUNK_PLAN)
HGRN_DECAY_LIMIT = 104.0
GLA_DECAY_LIMIT = 90.0
TABLE_ROWS_MAX = max(p.row_total for p in ALL_PLANS) + 8
PAIR = 2
GATED_HEADS_PER_ITER = 8
GATED_UNITS_MAX = 4
GLA_HEADS_PER_GROUP = HEAD // GLA_DK
RET_UNITS_PER_ITER = 8
STEP_TOKENS = 1024
ROW_TILE = 512
MOD_COLS = 768
VMEM_LIMIT_BYTES = 61 * 1024 * 1024


def _dot(a, b, precision=None):
    return jnp.dot(a, b, precision=precision, preferred_element_type=f32)


def _dot_tn(a, b):
    return lax.dot_general(a, b, (((0,), (0,)), ((), ())), preferred_element_type=f32)


def _silu(x):
    return x * jax.nn.sigmoid(x)


def _row_tile(i):
    return pl.ds(pl.multiple_of(i * ROW_TILE, ROW_TILE), ROW_TILE)


def _chunk_rows(seq_start, c):
    return pl.ds(pl.multiple_of(seq_start + c * CHUNK, CHUNK), CHUNK)


def _mod_kernel(cond_ref, w_ref, b_ref, o_ref):
    o_ref[0] = _dot(_silu(cond_ref[...]), w_ref[0], HIGHEST) + b_ref[0]


def _modulation(cond, ada_w, ada_b):
    depth, d, d3 = ada_w.shape
    rows = cond.shape[0]
    return pl.pallas_call(
        _mod_kernel,
        grid=(depth, d3 // MOD_COLS),
        in_specs=[
            pl.BlockSpec((rows, d), lambda l, j: (0, 0)),
            pl.BlockSpec((1, d, MOD_COLS), lambda l, j: (l, 0, j)),
            pl.BlockSpec((1, 1, MOD_COLS), lambda l, j: (l, 0, j)),
        ],
        out_specs=pl.BlockSpec((1, rows, MOD_COLS), lambda l, j: (l, 0, j)),
        out_shape=jax.ShapeDtypeStruct((depth, rows, d3), f32),
        compiler_params=pltpu.CompilerParams(dimension_semantics=("arbitrary", "arbitrary")),
        name="modulation",
    )(cond, ada_w, ada_b.reshape(depth, 1, d3))


def _modulated_norm(x_ref, mod_ref, nw_ref, h_scr):
    def body(i, carry):
        rows = _row_tile(i)
        x = x_ref[0, rows, :]
        y = x * lax.rsqrt(jnp.mean(x * x, axis=-1, keepdims=True) + EPS) * nw_ref[...]
        h_scr[rows, :] = (y * (1.0 + mod_ref[0, 1:2, :]) + mod_ref[0, 0:1, :]).astype(bf16)
        return carry
    lax.fori_loop(0, STEP_TOKENS // ROW_TILE, body, 0)


def _aligned_slice(start, width):
    return pl.ds(start if isinstance(start, int) else pl.multiple_of(start, width), width)


def _pair_lanes(pair):
    return _aligned_slice(pair * (PAIR * HEAD), PAIR * HEAD)


def _project(h, w_ref, pair, first_cols):
    width = PAIR * HEAD
    return [_dot(h, w_ref[:, _aligned_slice(c0 + pair * width, width)]) for c0 in first_cols]


def _add_heads_output(first_head, n_heads, gate_s, o_scr, gnw_ref, wo_ref, out_ref, first, finish):
    width = n_heads * HEAD
    lanes = _aligned_slice(first_head * HEAD, width)
    tiles = []
    for i in range(STEP_TOKENS // ROW_TILE):
        rows = pl.ds(i * ROW_TILE, ROW_TILE)
        parts = []
        for h in range(n_heads):
            o = o_scr[0, h, rows, :] + o_scr[1, h, rows, :]
            parts.append(o * lax.rsqrt(jnp.mean(o * o, axis=-1, keepdims=True) + EPS))
        y = jnp.concatenate(parts, axis=-1) * gnw_ref[:, lanes]
        tiles.append((rows, (y * gate_s[rows, 0:width].astype(f32)).astype(bf16)))
    products = [(rows, _dot(z, wo_ref[lanes, :])) for rows, z in tiles]
    for rows, product in products:
        mixed = product if first else out_ref[0, rows, :] + product
        if finish is not None:
            x_ref, mod_ref, fnw_ref, final = finish
            mixed = x_ref[0, rows, :] + mod_ref[0, 2:3, :] * mixed
            if final:
                mixed = mixed * lax.rsqrt(jnp.mean(mixed * mixed, axis=-1, keepdims=True) + EPS) * fnw_ref[...]
        out_ref[0, rows, :] = mixed


def _scan_tables(plan):
    t = np.arange(CHUNK)[:, None]
    j = np.arange(CHUNK)[None, :]
    fwd = []
    if plan.block_rows:
        fwd.append((j <= t) & (j // plan.block == t // plan.block))
    for m in plan.table_levels:
        mid = (t // (2 * m)) * (2 * m) + m
        right = t >= mid
        fwd.append(np.where(right, (j >= mid) & (j <= t), (j > t) & (j < mid)))
    fwd.append(j <= t)
    fwd.append(np.ones((8, CHUNK), bool))
    fwd = np.concatenate(fwd, axis=0).astype(np.float32)
    bwd = fwd.copy()
    n_sym = plan.row_total // CHUNK
    bwd[:plan.row_total] = fwd[:plan.row_total].reshape(n_sym, CHUNK, CHUNK)[:, ::-1, ::-1].reshape(-1, CHUNK)
    table = np.stack([np.tile(fwd, (1, 2)), np.tile(bwd, (1, 2))])
    first = 1 if plan.block else 0
    ids = np.full((CHUNK, CHUNK), -1, np.int32)
    for i, m in enumerate(plan.table_levels + plan.direct_levels):
        ids = np.where((t > j) & ((t ^ j) >= m) & ((t ^ j) < 2 * m), first + i, ids)
    if plan.block:
        ids = np.where((t >= j) & (t // plan.block == j // plan.block), 0, ids)
    else:
        ids = np.where(t == j, plan.diag_id, ids)
    return jnp.asarray(table, bf16), jnp.asarray(np.stack([ids, ids.T]).astype(np.int32))


def _in_chunk_scores(q, k, cum, tabled, lvl, ones, rev, key_masks, plan):
    qb = q.astype(bf16)
    q_heads = [qb if mask is None else qb * mask for mask in key_masks]
    kt = k.astype(bf16).T
    if plan.block:
        inside = tabled[0] if plan.block_rows else cum
        grow, decay = jnp.exp2(-inside).astype(bf16), jnp.exp2(inside).astype(bf16)
        keys = kt * grow.T
        scores = [jnp.where(lvl == 0, _dot(qh * decay, keys), 0.0) for qh in q_heads]
        tabled = tabled[plan.block_rows:]
    else:
        diag_keys = kt * ones
        scores = [jnp.where(lvl == plan.diag_id, _dot(qh, diag_keys), 0.0) for qh in q_heads]
    first = 1 if plan.block else 0
    for i in range(len(plan.table_levels)):
        e = jnp.exp2(tabled[i]).astype(bf16)
        keys = kt * e.T
        scores = [jnp.where(lvl == first + i, _dot(qh * e, keys), sc) for qh, sc in zip(q_heads, scores)]
    for i, m in enumerate(plan.direct_levels, start=first + len(plan.table_levels)):
        blocks = []
        for p0 in range(0, CHUNK, 2 * m):
            left, right = slice(p0, p0 + m), slice(p0 + m, p0 + 2 * m)
            q_side, k_side = (left, right) if rev else (right, left)
            mid_row = p0 + m if rev else p0 + m - 1
            blocks.append((q_side, k_side, cum[mid_row:mid_row + 1, :]))
        q_decay = [jnp.exp2(cum[qs] - mid).astype(bf16) for qs, _, mid in blocks]
        k_decay = []
        for _, ks, mid in blocks:
            decay = jnp.exp2(mid - cum[ks]).astype(bf16)
            zero = jnp.zeros((m, HEAD), bf16)
            k_decay += [zero, decay] if rev else [decay, zero]
        keys = kt * jnp.concatenate(k_decay, axis=0).T
        for h, qh in enumerate(q_heads):
            s = _dot(jnp.concatenate([qh[qs] * e for (qs, _, _), e in zip(blocks, q_decay)], axis=0), keys)
            rows = []
            for b, (qs, ks, _) in enumerate(blocks):
                if len(blocks) == 1:
                    updated = scores[h][qs, :] + s
                else:
                    updated = jnp.where(lvl[qs, :] == i, s[b * m:(b + 1) * m, :], scores[h][qs, :])
                rows += [updated, scores[h][ks, :]] if rev else [scores[h][ks, :], updated]
            scores[h] = jnp.concatenate(rows, axis=0)
    return [sc.astype(bf16) for sc in scores]


def _store_log2_split(g_ref, rows, g, block):
    width = g.shape[-1]
    x = g * LOG2E
    hi = x.astype(bf16)
    g_ref[rows, 0:width] = hi
    g_ref[rows, width:2 * width] = (x - hi.astype(f32)).astype(bf16)
    return jnp.min(jnp.sum(x.reshape(-1, block, width), axis=1))


def _decay_sums(g_split, table):
    width = g_split.shape[-1] // 2
    return _dot(table, jnp.concatenate([g_split[:, :width], g_split[:, width:]], axis=0))


def _key_masks(heads_per_group):
    if heads_per_group == 1:
        return [None]
    lane = lax.broadcasted_iota(jnp.int32, (1, HEAD), 1)
    return [(lane // (HEAD // heads_per_group) == sub).astype(bf16) for sub in range(heads_per_group)]


def _gated_chunks(chains, sums_ref, upcoming, upcoming_ref, heads_per_group, plan):
    masks = _key_masks(heads_per_group)
    groups = [slice(g * HEAD, (g + 1) * HEAD) for g in range(PAIR)]
    n_heads = PAIR * heads_per_group
    group_of = [h // heads_per_group for h in range(n_heads)]
    mask_of = [masks[h % heads_per_group] for h in range(n_heads)]
    flat = [(c, r) for c, (_, units) in enumerate(chains) for r in range(len(units))]
    unit = {(c, r): chains[c][1][r] for c, r in flat}
    index = {key: u for u, key in enumerate(flat)}
    cum_rows = slice(plan.table_blocks * CHUNK, (plan.table_blocks + 1) * CHUNK)
    cum = {key: sums_ref[index[key], cum_rows, :] for key in flat}
    total = {key: sums_ref[index[key], plan.row_total:plan.row_total + 1, :] for key in flat}
    vb = {key: [unit[key][2][:, h * HEAD:(h + 1) * HEAD] for h in range(n_heads)] for key in flat}
    upcoming = list(enumerate(upcoming))
    kv, whole = {}, {}
    for key in flat:
        k = unit[key][1]
        keys = (k * jnp.exp2(total[key] - cum[key])).astype(bf16)
        kv[key] = []
        for h in range(n_heads):
            own = keys[:, groups[group_of[h]]]
            kv[key].append(_dot_tn(own if mask_of[h] is None else own * mask_of[h], vb[key][h]))
        whole[key] = [jnp.broadcast_to(jnp.exp2(total[key][:, lanes]), (HEAD, HEAD)).T for lanes in groups]
    state = [list(states) for states, _ in chains]
    outs, pending, carried = {key: [None] * n_heads for key in flat}, None, {}

    def finish(done):
        pkey, ph, psc = done
        queries, entering = carried[pkey]
        lhs = jnp.concatenate([queries[:, groups[group_of[ph]]], psc], axis=1)
        outs[pkey][ph] = _dot(lhs, jnp.concatenate([entering[ph], vb[pkey][ph]], axis=0))

    for r in range(max(len(units) for _, units in chains)):
        live = [key for key in flat if key[1] == r]
        for key in live:
            queries = (unit[key][0] * jnp.exp2(cum[key])).astype(bf16)
            carried[key] = (queries, [s.astype(bf16) for s in state[key[0]]])
            state[key[0]] = [whole[key][group_of[h]] * state[key[0]][h] + kv[key][h] for h in range(n_heads)]
        for key in live:
            q, k, _, lvl, rev = unit[key]
            ones = (lvl[0:1, :] >= -1).astype(bf16)
            for g, lanes in enumerate(groups):
                tabled = [sums_ref[index[key], i * CHUNK:(i + 1) * CHUNK, lanes] for i in range(plan.table_blocks)]
                group_scores = _in_chunk_scores(q[:, lanes], k[:, lanes], cum[key][:, lanes], tabled, lvl, ones, rev,
                                                masks, plan)
                for sub, sc in enumerate(group_scores):
                    if pending is not None:
                        finish(pending)
                    pending = (key, g * heads_per_group + sub, sc)
            if upcoming:
                u, (g_split, table) = upcoming.pop(0)
                upcoming_ref[u, 0:plan.row_total + 8, :] = _decay_sums(g_split, table)
    for u, (g_split, table) in upcoming:
        upcoming_ref[u, 0:plan.row_total + 8, :] = _decay_sums(g_split, table)
    finish(pending)
    return [[outs[(c, r)] for r in range(len(units))] for c, (_, units) in enumerate(chains)], state


def _gated_scans(q_s, k_refs, v_s, g_refs, s_scr, sums_scr, table_ref, lvl_ref, o_scr, seq, heads_per_group, plan):
    n = seq // CHUNK
    n_seq = STEP_TOKENS // seq
    n_heads = PAIR * heads_per_group
    units = GATED_HEADS_PER_ITER // n_heads
    per_iter = min(n, units // 2)
    seqs_per_iter = min(n_seq, units // (2 * per_iter))
    assert n % per_iter == 0 and n_seq % seqs_per_iter == 0
    iters_per_seq = n // per_iter
    n_iters = (n_seq // seqs_per_iter) * iters_per_seq

    def layout(it):
        jj, i = it // iters_per_seq, it % iters_per_seq
        chains = []
        for js in range(seqs_per_iter):
            j = jj * seqs_per_iter + js
            for d in range(2):
                steps = [i * per_iter + r for r in range(per_iter)]
                chains.append((j, d, [_chunk_rows(j * seq, n - 1 - t if d else t) for t in steps]))
        return chains

    def sums_inputs(it):
        return [(g_refs[d][rw, :], table_ref[d]) for _, d, rws in layout(it) for rw in rws]

    for u, (g_split, table) in enumerate(sums_inputs(0)):
        sums_scr[0, u, 0:plan.row_total + 8, :] = _decay_sums(g_split, table)

    def iteration(it, slot):
        chains = layout(it)
        args = [([s_scr[j, d, h] for h in range(n_heads)],
                 [(q_s[rw, :], k_refs[d][rw, :], v_s[rw, :], lvl_ref[d], bool(d)) for rw in rws])
                for j, d, rws in chains]
        upcoming = sums_inputs(jnp.minimum(it + 1, n_iters - 1))
        outs, new_states = _gated_chunks(args, sums_scr.at[slot], upcoming, sums_scr.at[1 - slot], heads_per_group,
                                         plan)
        for c, (j, d, rws) in enumerate(chains):
            for h in range(n_heads):
                for r, rw in enumerate(rws):
                    o_scr[d, h, rw, :] = outs[c][r][h]
                s_scr[j, d, h] = new_states[c][h]

    assert n_iters % 2 == 0

    def body(it2, carry):
        iteration(2 * it2, 0)
        iteration(2 * it2 + 1, 1)
        return carry
    lax.fori_loop(0, n_iters // 2, body, 0)


def _even_kernel(*refs, seq, has_state, emit_state, final):
    it = iter(refs)
    x_ref, mod_ref, nw_ref = next(it), next(it), next(it)
    win_ref, wlow_ref, wo_ref = next(it), next(it), next(it)
    gkw_ref, gkb_ref, lb_ref, gnw_ref, fnw_ref = next(it), next(it), next(it), next(it), next(it)
    plan_tables = {plan: (next(it), next(it)) for plan in ALL_PLANS}
    s0a_ref, s0b_ref = (next(it), next(it)) if has_state else (None, None)
    out_ref = next(it)
    sta_ref, stb_ref = (next(it), next(it)) if emit_state else (None, None)
    h_scr, o_scr, gate_s, q_s, v_s, kf_s, kb_s, gf_s, gb_s, s_scr, sums_scr, bound_s = it
    n_seq = STEP_TOKENS // seq
    n_tiles = STEP_TOKENS // ROW_TILE

    _modulated_norm(x_ref, mod_ref, nw_ref, h_scr)

    width = PAIR * HEAD

    def run_scans(first_head, s0_ref, st_ref, h0, heads_per_group, k_refs, plans, block_decays, first=False,
                  last=False):
        bounded, robust, limit = plans
        bound_s[0] = (functools.reduce(jnp.minimum, block_decays) >= -limit).astype(jnp.int32)
        n_heads = PAIR * heads_per_group
        key_rows = HEAD // heads_per_group
        own_rows = [pl.ds((h % heads_per_group) * key_rows, key_rows) for h in range(n_heads)]
        for j in range(n_seq):
            for d in range(2):
                for h in range(n_heads):
                    if s0_ref is None or heads_per_group > 1:
                        s_scr[j, d, h] = jnp.zeros((HEAD, HEAD), f32)
                    if s0_ref is not None:
                        s_scr[j, d, h, own_rows[h], :] = s0_ref[j, d, h0 + h]
        def scans(plan):
            table_ref, lvl_ref = plan_tables[plan]
            _gated_scans(q_s, k_refs, v_s, (gf_s, gb_s), s_scr, sums_scr, table_ref, lvl_ref, o_scr, seq,
                         heads_per_group, plan)
        lax.cond(bound_s[0] == 1, functools.partial(scans, bounded), functools.partial(scans, robust))
        if st_ref is not None:
            for j in range(n_seq):
                for d in range(2):
                    for h in range(n_heads):
                        st_ref[j, d, h0 + h] = s_scr[j, d, h, own_rows[h], :]
        _add_heads_output(first_head, n_heads, gate_s, o_scr, gnw_ref, wo_ref, out_ref, first,
                          (x_ref, mod_ref, fnw_ref, final) if last else None)

    def hgrn_pair(pp, carry):
        lb = lb_ref[:, _pair_lanes(pp)]
        log_lb = jnp.log(lb)

        tiles = [pl.ds(i * ROW_TILE, ROW_TILE) for i in range(n_tiles)]
        projected = [_project(h_scr[rows, :], win_ref, pp, HGRN_COLS) for rows in tiles]
        block_decays = []
        for rows, (query, value, forget_f, forget_b, gate) in zip(tiles, projected):
            gate_s[rows, 0:width] = _silu(gate).astype(bf16)
            q_s[rows, :] = _silu(query)
            v_s[rows, 0:width] = value.astype(bf16)
            for a, k_s, g_s in ((forget_f, kf_s, gf_s), (forget_b, kb_s, gb_s)):
                z = log_lb - a
                u = jnp.exp(-jnp.abs(a))
                w = jnp.exp(-jnp.abs(z))
                r = 1.0 / (1.0 + u)
                log_f = jnp.maximum(z, 0.0) + jnp.minimum(a, 0.0) + jnp.log((1.0 + w) * r)
                block_decays.append(_store_log2_split(g_s, rows, log_f, HGRN_PLANS[0].block))
                k_s[rows, :] = (1.0 - lb) * jnp.where(a >= 0.0, u * r, r)
        run_scans(PAIR * pp, s0a_ref, sta_ref, PAIR * pp, 1, (kf_s, kb_s), HGRN_PLANS + (HGRN_DECAY_LIMIT,), block_decays)
        return carry

    def clear(i, carry):
        out_ref[0, _row_tile(i), :] = jnp.zeros((ROW_TILE, out_ref.shape[-1]), f32)
        return carry
    lax.fori_loop(0, n_tiles, clear, 0)
    lax.fori_loop(0, N_HGRN // PAIR, hgrn_pair, 0)

    assert N_GLA == PAIR * GLA_HEADS_PER_GROUP

    def gla_project(rows):
        h = h_scr[rows, :]
        narrow = [_dot(h, win_ref[:, c0:c0 + N_GLA * GLA_DK]) for c0 in GLA_QK_COLS]
        wide = [_dot(h, win_ref[:, c0:c0 + N_GLA * HEAD]) for c0 in GLA_COLS]
        return narrow + wide + [_dot(h, wlow_ref[...])]

    tiles = [pl.ds(i * ROW_TILE, ROW_TILE) for i in range(n_tiles)]
    projected = [gla_project(rows) for rows in tiles]
    block_decays = []
    for rows, (query, key, value, gate, low) in zip(tiles, projected):
        gate_s[rows, :] = _silu(gate).astype(bf16)
        q_s[rows, :] = query * (GLA_DK ** -0.5)
        kf_s[rows, :] = key
        v_s[rows, :] = value.astype(bf16)
        low = low.astype(bf16)
        for d, g_s in enumerate((gf_s, gb_s)):
            logits = _dot(low, gkw_ref[d]) + gkb_ref[d]
            log_gate = jnp.minimum(logits, 0.0) - jnp.log(1.0 + jnp.exp(-jnp.abs(logits)))
            block_decays.append(_store_log2_split(g_s, rows, log_gate * (1.0 / GLA_GATE_NORM), GLA_PLANS[0].block))
    run_scans(N_HGRN, s0b_ref, stb_ref, 0, GLA_HEADS_PER_GROUP, (kf_s, kf_s), GLA_PLANS + (GLA_DECAY_LIMIT,), block_decays, last=True)


def _odd_kernel(*refs, seq, has_state, emit_state, use_rope, final):
    it = iter(refs)
    lg_ref = next(it)
    x_ref, mod_ref, nw_ref = next(it), next(it), next(it)
    wr_ref, wo_ref, gnw_ref, fnw_ref = next(it), next(it), next(it), next(it)
    cos_ref, sin_ref = (next(it), next(it)) if use_rope else (None, None)
    s0_ref = next(it) if has_state else None
    out_ref = next(it)
    st_ref = next(it) if emit_state else None
    h_scr, o_scr, gate_s, q_s, kt_s, v_s, s_scr, dec_scr = it
    n_seq = STEP_TOKENS // seq
    n_tiles = STEP_TOKENS // ROW_TILE
    n_chunks = seq // CHUNK
    per_iter = min(n_chunks, RET_UNITS_PER_ITER // n_seq)
    assert n_chunks % per_iter == 0

    _modulated_norm(x_ref, mod_ref, nw_ref, h_scr)
    t_idx = lax.broadcasted_iota(jnp.int32, (CHUNK, CHUNK), 0)
    s_idx = lax.broadcasted_iota(jnp.int32, (CHUNK, CHUNK), 1)
    row_f = lax.broadcasted_iota(jnp.int32, (CHUNK, HEAD), 0).astype(f32)
    col_f = lax.broadcasted_iota(jnp.int32, (8, CHUNK), 1).astype(f32)
    chunk_len = jnp.full((8, HEAD), CHUNK, f32)
    if use_rope:
        lane = lax.broadcasted_iota(jnp.int32, (ROW_TILE, HEAD), 1)
        first_quarter = (lane // (HEAD // 4)) % 2 == 0

    def rope(x, cos, sin_signed):
        xr = jnp.where(first_quarter, pltpu.roll(x, HEAD - HEAD // 4, axis=1), pltpu.roll(x, HEAD // 4, axis=1))
        return x * cos + xr * sin_signed

    width = PAIR * HEAD

    def pair_body(pp):
        tiles = [pl.ds(i * ROW_TILE, ROW_TILE) for i in range(n_tiles)]
        projected = [_project(h_scr[rows, :], wr_ref, pp, RET_COLS) for rows in tiles]
        for rows, (q, k, value, gate) in zip(tiles, projected):
            gate_s[rows, :] = _silu(gate).astype(bf16)
            k = k * (HEAD ** -0.5)
            if use_rope:
                cos, sin_signed = cos_ref[rows, :], sin_ref[rows, :]
                heads = [slice(h * HEAD, (h + 1) * HEAD) for h in range(PAIR)]
                q = jnp.concatenate([rope(q[:, lanes], cos, sin_signed) for lanes in heads], axis=-1)
                k = jnp.concatenate([rope(k[:, lanes], cos, sin_signed) for lanes in heads], axis=-1)
            q_s[rows, :] = q.astype(bf16)
            kt_s[:, rows] = k.T
            v_s[rows, :] = value.astype(bf16)
        for h in range(PAIR):
            scan_head(PAIR * pp + h, h)
        last = pp == N_RET // PAIR - 1
        _add_heads_output(PAIR * pp, PAIR, gate_s, o_scr, gnw_ref, wo_ref, out_ref, pp == 0,
                          (x_ref, mod_ref, fnw_ref, final) if last else None)

    def scan_head(hh, h):
        lanes = slice(h * HEAD, (h + 1) * HEAD)
        lg_f = lg_ref[0, hh]
        lg_b = lg_ref[1, hh]
        for j in range(n_seq):
            for d in range(2):
                s_scr[j, d] = s0_ref[j, d, hh] if has_state else jnp.zeros((HEAD, HEAD), f32)

        dist = (t_idx - s_idx).astype(f32)
        dec_scr[0] = jnp.exp(lg_f * (row_f + 1.0))
        dec_scr[1] = jnp.exp(lg_b * (CHUNK - row_f))
        dec_scr[2] = (jnp.where(t_idx >= s_idx, jnp.exp(lg_f * jnp.maximum(dist, 0.0)), 0.0)
                      + jnp.where(s_idx >= t_idx, jnp.exp(lg_b * jnp.maximum(-dist, 0.0)), 0.0))
        dec_scr[3, 0:8, :] = jnp.exp(lg_f * (CHUNK - 1.0 - col_f))
        dec_scr[3, 8:16, :] = jnp.exp(lg_b * col_f)
        dec_scr[3, 16:24, :] = jnp.exp(lg_f * chunk_len)
        dec_scr[3, 24:32, :] = jnp.exp(lg_b * chunk_len)

        def body(i, c):
            units = [(j, r) for j in range(n_seq) for r in range(per_iter)]
            rows = {(j, r, d): _chunk_rows(j * seq, n_chunks - 1 - (i * per_iter + r) if d else i * per_iter + r)
                    for j, r in units for d in range(2)}
            scores = {u: _dot(q_s[rows[u + (0,)], lanes], kt_s[lanes, rows[u + (0,)]].astype(bf16)) for u in units}
            kv = {}
            for j, r in units:
                for d in range(2):
                    rw = rows[(j, r, d)]
                    keys = (kt_s[lanes, rw] * dec_scr[3, 8 * d:8 * d + 1, :]).astype(bf16)
                    kv[(j, r, d)] = _dot(keys, v_s[rw, lanes])
            state = {(j, d): s_scr[j, d] for j in range(n_seq) for d in range(2)}
            entering, carried = {}, {}
            for r in range(per_iter):
                for j in range(n_seq):
                    entering[(j, r)] = state[(j, 0)].astype(bf16)
                    carried[(j, r)] = _dot(q_s[rows[(j, r, 1)], lanes], state[(j, 1)].astype(bf16))
                    for d in range(2):
                        state[(j, d)] = dec_scr[3, 16 + 8 * d:17 + 8 * d, :] * state[(j, d)] + kv[(j, r, d)]
            seen_f = dec_scr[0].astype(bf16)
            for j, r in units:
                rw = rows[(j, r, 0)]
                lhs = jnp.concatenate([q_s[rw, lanes] * seen_f, (scores[(j, r)] * dec_scr[2]).astype(bf16)], axis=1)
                o_scr[0, h, rw, :] = _dot(lhs, jnp.concatenate([entering[(j, r)], v_s[rw, lanes]], axis=0))
                o_scr[1, h, rows[(j, r, 1)], :] = dec_scr[1] * carried[(j, r)]
            for (j, d), s in state.items():
                s_scr[j, d] = s
            return c
        lax.fori_loop(0, n_chunks // per_iter, body, 0)
        if emit_state:
            for j in range(n_seq):
                for d in range(2):
                    st_ref[j, d, hh] = s_scr[j, d]

    for pp in range(N_RET // PAIR):
        pair_body(pp)


def _const_spec(shape):
    zeros = (0,) * len(shape)
    return pl.BlockSpec(shape, lambda i: zeros, pipeline_mode=pl.Buffered(1))


def _step_spec(shape, per_step, buffers=None):
    zeros = (0,) * (len(shape) - 1)
    mode = {} if buffers is None else {"pipeline_mode": pl.Buffered(buffers)}
    return pl.BlockSpec((per_step,) + tuple(shape[1:]), lambda i: (i,) + zeros, **mode)


def _mod_spec(d, per_sequence):
    if per_sequence:
        return pl.BlockSpec((1, 3, d), lambda i: (i + 1, 0, 0))
    return pl.BlockSpec((1, 3, d), lambda i: (0, 0, 0))


def _layer_call(body, x, mod, consts, states, state_shapes, per_sequence_mod, scratch, name, smem_inputs=(),
                state_buffers=None):
    n_seq, seq, d = x.shape
    per_step = STEP_TOKENS // seq
    assert per_step * seq == STEP_TOKENS and n_seq % per_step == 0 and seq % CHUNK == 0
    assert not per_sequence_mod or per_step == 1
    n_steps = n_seq // per_step
    xs = x.reshape(n_steps, STEP_TOKENS, d)
    inputs = list(smem_inputs) + [xs, mod] + list(consts) + list(states)
    in_specs = [pl.BlockSpec(memory_space=pltpu.SMEM)] * len(smem_inputs)
    in_specs += [_step_spec(xs.shape, 1), _mod_spec(d, per_sequence_mod)]
    in_specs += [_const_spec(a.shape) for a in consts]
    in_specs += [_step_spec(s.shape, per_step) for s in states]
    out_shape = [jax.ShapeDtypeStruct(xs.shape, f32)] + [jax.ShapeDtypeStruct(s, f32) for s in state_shapes]
    out_specs = [_step_spec(xs.shape, 1)] + [_step_spec(s, per_step, buffers=state_buffers) for s in state_shapes]
    outs = pl.pallas_call(
        body,
        grid=(n_steps,),
        in_specs=in_specs,
        out_specs=out_specs,
        out_shape=out_shape,
        scratch_shapes=scratch,
        compiler_params=pltpu.CompilerParams(dimension_semantics=("arbitrary",), vmem_limit_bytes=VMEM_LIMIT_BYTES),
        name=name,
    )(*inputs)
    return [outs[0].reshape(x.shape)] + list(outs[1:])


def _even_layer(x, mod, norm_w, w, final_w, tables, states, emit_state, per_sequence_mod, final):
    n_seq, seq, d = x.shape
    consts = [norm_w, w["win"], w["wlow"], w["wout"], w["gkw"], w["gkb"], w["lb"], w["gnw"],
              final_w] + list(tables)
    state_shapes = [(n_seq, 2, N_HGRN, HEAD, HEAD), (n_seq, 2, N_GLA, GLA_DK, HEAD)] if emit_state else []
    scan_heads = max(PAIR, N_GLA)
    scratch = [
        pltpu.VMEM((STEP_TOKENS, d), bf16),
        pltpu.VMEM((2, scan_heads, STEP_TOKENS, HEAD), f32),
        pltpu.VMEM((STEP_TOKENS, scan_heads * HEAD), bf16),
        pltpu.VMEM((STEP_TOKENS, PAIR * HEAD), f32),
        pltpu.VMEM((STEP_TOKENS, scan_heads * HEAD), bf16),
        pltpu.VMEM((STEP_TOKENS, PAIR * HEAD), f32),
        pltpu.VMEM((STEP_TOKENS, PAIR * HEAD), f32),
        pltpu.VMEM((STEP_TOKENS, 2 * PAIR * HEAD), bf16),
        pltpu.VMEM((STEP_TOKENS, 2 * PAIR * HEAD), bf16),
        pltpu.VMEM((STEP_TOKENS // seq, 2, scan_heads, HEAD, HEAD), f32),
        pltpu.VMEM((2, GATED_UNITS_MAX, TABLE_ROWS_MAX, PAIR * HEAD), f32),
        pltpu.SMEM((1,), jnp.int32),
    ]
    body = functools.partial(_even_kernel, seq=seq, has_state=states is not None, emit_state=emit_state, final=final)
    return _layer_call(body, x, mod, consts, states or (), state_shapes, per_sequence_mod, scratch,
                       "even_layer_seq%d" % seq, state_buffers=1)


def _odd_layer(x, mod, norm_w, w, final_w, log_decay, rope, state, emit_state, per_sequence_mod, final):
    n_seq, seq, d = x.shape
    consts = [norm_w, w["wr"], w["wout"], w["gnw"], final_w] + list(rope or ())
    state_shapes = [(n_seq, 2, N_RET, HEAD, HEAD)] if emit_state else []
    scratch = [
        pltpu.VMEM((STEP_TOKENS, d), bf16),
        pltpu.VMEM((2, PAIR, STEP_TOKENS, HEAD), f32),
        pltpu.VMEM((STEP_TOKENS, PAIR * HEAD), bf16),
        pltpu.VMEM((STEP_TOKENS, PAIR * HEAD), bf16),
        pltpu.VMEM((PAIR * HEAD, STEP_TOKENS), f32),
        pltpu.VMEM((STEP_TOKENS, PAIR * HEAD), bf16),
        pltpu.VMEM((STEP_TOKENS // seq, 2, HEAD, HEAD), f32),
        pltpu.VMEM((4, CHUNK, HEAD), f32),
    ]
    body = functools.partial(_odd_kernel, seq=seq, has_state=state is not None, emit_state=emit_state,
                             use_rope=rope is not None, final=final)
    return _layer_call(body, x, mod, consts, () if state is None else (state,), state_shapes, per_sequence_mod,
                       scratch, "odd_layer_seq%d" % seq, smem_inputs=(log_decay,))


def _even_weights(w_in, gk_w, gk_b, lb, gn_w, w_out):
    c0 = GLA_LOW_COLS[0]
    wlow = jnp.pad(w_in[:, c0:c0 + 2 * GLA_RANK], ((0, 0), (0, HEAD - 2 * GLA_RANK)))
    gkw = jnp.stack([jnp.pad(gk_w[d], ((d * GLA_RANK, HEAD - (d + 1) * GLA_RANK), (0, 0))) for d in range(2)])
    return {"win": w_in.astype(bf16), "wlow": wlow.astype(bf16), "gkw": gkw.astype(bf16),
            "gkb": gk_b.reshape(2, 1, -1), "lb": lb.reshape(1, -1), "gnw": gn_w.reshape(1, -1),
            "wout": w_out.astype(bf16)}


def _odd_weights(w_in, gn_w, w_out):
    return {"wr": w_in.astype(bf16), "gnw": gn_w.reshape(1, -1), "wout": w_out.astype(bf16)}


def _rope_tables(seq):
    rows = seq // GRID_W
    t_row = np.repeat(np.arange(rows), GRID_W).astype(np.float32)
    t_col = np.tile(np.arange(GRID_W), rows).astype(np.float32)
    half = HEAD // 2
    inv = (ROPE_BASE ** (-np.arange(0, half, 2, dtype=np.float32) / half)).astype(np.float32)
    ang_r = t_row[:, None] * inv
    ang_c = t_col[:, None] * inv
    ang = np.concatenate([ang_r, ang_r, ang_c, ang_c], axis=-1).astype(np.float32)
    sign = np.where((np.arange(HEAD) // (HEAD // 4)) % 2 == 0, -1.0, 1.0).astype(np.float32)
    return jnp.asarray(np.cos(ang), f32), jnp.asarray(np.sin(ang) * sign, f32)


def kernel(x_prompt, x_sample, state_hgrn, state_gla, state_ret, c, c_ctx, norm_w, ada_w, ada_b, w_in_even, hgrn_lb, gla_gk_w, gla_gk_b, gn_even, w_out_even, w_in_odd, ret_decay, gn_odd, w_out_odd, final_norm_w):
    depth, d = norm_w.shape
    n_lat = x_sample.shape[0]
    n_cond = -(-(1 + n_lat) // 8) * 8
    cond = jnp.zeros((n_cond, d), f32).at[0].set(c_ctx).at[1:1 + n_lat].set(c)
    mod = _modulation(cond, ada_w, ada_b).reshape(depth, n_cond, 3, d)
    lbs = jnp.cumsum(jax.nn.softmax(hgrn_lb.astype(f32), axis=0), axis=0)
    final_w = final_norm_w.reshape(1, d)
    rope = _rope_tables(x_sample.shape[1])
    tables = sum((_scan_tables(plan) for plan in ALL_PLANS), ())

    x_c, x_l = x_prompt, x_sample
    new_hgrn, new_gla, new_ret = [], [], []
    for l in range(depth):
        i = l // 2
        final = l == depth - 1
        nw = norm_w[l].reshape(1, d)
        if l % 2 == 0:
            w = _even_weights(w_in_even[i], gla_gk_w[i], gla_gk_b[i], lbs[i], gn_even[i], w_out_even[i])
            x_c, st_a, st_b = _even_layer(x_c, mod[l], nw, w, final_w, tables, None, True, False, final)
            (x_l,) = _even_layer(x_l, mod[l], nw, w, final_w, tables, (state_hgrn[:, i], state_gla[:, i]),
                                 False, True, final)
            new_hgrn.append(st_a)
            new_gla.append(st_b)
        else:
            w = _odd_weights(w_in_odd[i], gn_odd[i], w_out_odd[i])
            log_decay = jax.nn.log_sigmoid(ret_decay[i].astype(f32))
            x_c, st_c = _odd_layer(x_c, mod[l], nw, w, final_w, log_decay, None, None, True, False, final)
            (x_l,) = _odd_layer(x_l, mod[l], nw, w, final_w, log_decay, rope, state_ret[:, i], False, True, final)
            new_ret.append(st_c)
    def stacked(states):
        return states[0][:, None] if len(states) == 1 else jnp.stack(states, axis=1)
    return (x_c, x_l, stacked(new_hgrn), stacked(new_gla), stacked(new_ret))
```

```python
import functools
from typing import NamedTuple

import numpy as np
import jax
import jax.numpy as jnp
from jax import lax
from jax.experimental import pallas as pl
from jax.experimental.pallas import tpu as pltpu

f32 = jnp.float32
bf16 = jnp.bfloat16
HIGHEST = lax.Precision.HIGHEST

EPS = 1e-6
LOG2E = 1.4426950408889634
HEAD = 128
N_HGRN = 4
N_GLA = 4
GLA_DK = 64
N_RET = 8
N_HEADS = 8
GLA_RANK = 16
GLA_GATE_NORM = 16.0
GRID_W = 64
ROPE_BASE = 10000.0

EVEN_SPLITS = (N_HGRN * HEAD,) * 5 + (N_GLA * GLA_DK,) * 2 + (N_GLA * HEAD,) * 2 + (GLA_RANK,) * 2
EVEN_STARTS = tuple(int(c) for c in np.cumsum((0,) + EVEN_SPLITS[:-1]))
HGRN_COLS = EVEN_STARTS[0:5]
GLA_QK_COLS = EVEN_STARTS[5:7]
GLA_COLS = EVEN_STARTS[7:9]
GLA_LOW_COLS = EVEN_STARTS[9:11]
RET_COLS = tuple(i * N_RET * HEAD for i in range(4))

CHUNK = 128


class _ScanPlan(NamedTuple):
    block: int
    table_levels: tuple
    direct_levels: tuple

    @property
    def block_rows(self):
        return 1 if 0 < self.block < CHUNK else 0

    @property
    def table_blocks(self):
        return self.block_rows + len(self.table_levels)

    @property
    def row_total(self):
        return (self.table_blocks + 1) * CHUNK

    @property
    def diag_id(self):
        return len(self.table_levels) + len(self.direct_levels)


ROBUST_PLAN = _ScanPlan(0, (1, 2, 4), (8, 16, 32, 64))
BLOCK_PLAN = _ScanPlan(64, (), (64,))
CHUNK_PLAN = _ScanPlan(CHUNK, (), ())
HGRN_PLANS = (BLOCK_PLAN, ROBUST_PLAN)
GLA_PLANS = (CHUNK_PLAN, ROBUST_PLAN)
ALL_PLANS = (ROBUST_PLAN, BLOCK_PLAN, CHUNK_PLAN)
HGRN_DECAY_LIMIT = 104.0
GLA_DECAY_LIMIT = 90.0
TABLE_ROWS_MAX = max(p.row_total for p in ALL_PLANS) + 8
PAIR = 2
GATED_HEADS_PER_ITER = 8
GATED_UNITS_MAX = 4
GLA_HEADS_PER_GROUP = HEAD // GLA_DK
RET_UNITS_PER_ITER = 8
STEP_TOKENS = 1024
ROW_TILE = 512
MOD_COLS = 768
VMEM_LIMIT_BYTES = 61 * 1024 * 1024


def _dot(a, b, precision=None):
    return jnp.dot(a, b, precision=precision, preferred_element_type=f32)


def _dot_tn(a, b):
    return lax.dot_general(a, b, (((0,), (0,)), ((), ())), preferred_element_type=f32)


def _silu(x):
    return x * jax.nn.sigmoid(x)


def _row_tile(i):
    return pl.ds(pl.multiple_of(i * ROW_TILE, ROW_TILE), ROW_TILE)


def _chunk_rows(seq_start, c):
    return pl.ds(pl.multiple_of(seq_start + c * CHUNK, CHUNK), CHUNK)


def _mod_kernel(cond_ref, w_ref, b_ref, o_ref):
    o_ref[0] = _dot(_silu(cond_ref[...]), w_ref[0], HIGHEST) + b_ref[0]


def _modulation(cond, ada_w, ada_b):
    depth, d, d3 = ada_w.shape
    rows = cond.shape[0]
    return pl.pallas_call(
        _mod_kernel,
        grid=(depth, d3 // MOD_COLS),
        in_specs=[
            pl.BlockSpec((rows, d), lambda l, j: (0, 0)),
            pl.BlockSpec((1, d, MOD_COLS), lambda l, j: (l, 0, j)),
            pl.BlockSpec((1, 1, MOD_COLS), lambda l, j: (l, 0, j)),
        ],
        out_specs=pl.BlockSpec((1, rows, MOD_COLS), lambda l, j: (l, 0, j)),
        out_shape=jax.ShapeDtypeStruct((depth, rows, d3), f32),
        compiler_params=pltpu.CompilerParams(dimension_semantics=("arbitrary", "arbitrary")),
        name="modulation",
    )(cond, ada_w, ada_b.reshape(depth, 1, d3))


def _modulated_norm(x_ref, mod_ref, nw_ref, h_scr):
    def body(i, carry):
        rows = _row_tile(i)
        x = x_ref[0, rows, :]
        y = x * lax.rsqrt(jnp.mean(x * x, axis=-1, keepdims=True) + EPS) * nw_ref[...]
        h_scr[rows, :] = (y * (1.0 + mod_ref[0, 1:2, :]) + mod_ref[0, 0:1, :]).astype(bf16)
        return carry
    lax.fori_loop(0, STEP_TOKENS // ROW_TILE, body, 0)


def _aligned_slice(start, width):
    return pl.ds(start if isinstance(start, int) else pl.multiple_of(start, width), width)


def _pair_lanes(pair):
    return _aligned_slice(pair * (PAIR * HEAD), PAIR * HEAD)


def _project(h, w_ref, pair, first_cols):
    width = PAIR * HEAD
    return [_dot(h, w_ref[:, _aligned_slice(c0 + pair * width, width)]) for c0 in first_cols]


def _add_heads_output(first_head, n_heads, gate_s, o_scr, gnw_ref, wo_ref, out_ref, first, finish):
    width = n_heads * HEAD
    lanes = _aligned_slice(first_head * HEAD, width)
    tiles = []
    for i in range(STEP_TOKENS // ROW_TILE):
        rows = pl.ds(i * ROW_TILE, ROW_TILE)
        parts = []
        for h in range(n_heads):
            o = o_scr[0, h, rows, :].astype(f32) + o_scr[1, h, rows, :].astype(f32)
            parts.append(o * lax.rsqrt(jnp.mean(o * o, axis=-1, keepdims=True) + EPS))
        y = jnp.concatenate(parts, axis=-1) * gnw_ref[:, lanes]
        tiles.append((rows, (y * gate_s[rows, 0:width].astype(f32)).astype(bf16)))
    products = [(rows, _dot(z, wo_ref[lanes, :])) for rows, z in tiles]
    for rows, product in products:
        mixed = product if first else out_ref[0, rows, :] + product
        if finish is not None:
            x_ref, mod_ref, fnw_ref, final = finish
            mixed = x_ref[0, rows, :] + mod_ref[0, 2:3, :] * mixed
            if final:
                mixed = mixed * lax.rsqrt(jnp.mean(mixed * mixed, axis=-1, keepdims=True) + EPS) * fnw_ref[...]
        out_ref[0, rows, :] = mixed


def _scan_tables(plan):
    t = np.arange(CHUNK)[:, None]
    j = np.arange(CHUNK)[None, :]
    fwd = []
    if plan.block_rows:
        fwd.append((j <= t) & (j // plan.block == t // plan.block))
    for m in plan.table_levels:
        mid = (t // (2 * m)) * (2 * m) + m
        right = t >= mid
        fwd.append(np.where(right, (j >= mid) & (j <= t), (j > t) & (j < mid)))
    fwd.append(j <= t)
    fwd.append(np.ones((8, CHUNK), bool))
    fwd = np.concatenate(fwd, axis=0).astype(np.float32)
    bwd = fwd.copy()
    n_sym = plan.row_total // CHUNK
    bwd[:plan.row_total] = fwd[:plan.row_total].reshape(n_sym, CHUNK, CHUNK)[:, ::-1, ::-1].reshape(-1, CHUNK)
    table = np.stack([np.tile(fwd, (1, 2)), np.tile(bwd, (1, 2))])
    first = 1 if plan.block else 0
    ids = np.full((CHUNK, CHUNK), -1, np.int32)
    for i, m in enumerate(plan.table_levels + plan.direct_levels):
        ids = np.where((t > j) & ((t ^ j) >= m) & ((t ^ j) < 2 * m), first + i, ids)
    if plan.block:
        ids = np.where((t >= j) & (t // plan.block == j // plan.block), 0, ids)
    else:
        ids = np.where(t == j, plan.diag_id, ids)
    return jnp.asarray(table, bf16), jnp.asarray(np.stack([ids, ids.T]).astype(np.int32))


def _in_chunk_scores(q, k, cum, tabled, lvl, ones, rev, key_masks, plan):
    qb = q.astype(bf16)
    q_heads = [qb if mask is None else qb * mask for mask in key_masks]
    kt = k.astype(bf16).T
    if plan.block:
        inside = tabled[0] if plan.block_rows else cum
        grow, decay = jnp.exp2(-inside).astype(bf16), jnp.exp2(inside).astype(bf16)
        keys = kt * grow.T
        scores = [jnp.where(lvl == 0, _dot(qh * decay, keys), 0.0) for qh in q_heads]
        tabled = tabled[plan.block_rows:]
    else:
        diag_keys = kt * ones
        scores = [jnp.where(lvl == plan.diag_id, _dot(qh, diag_keys), 0.0) for qh in q_heads]
    first = 1 if plan.block else 0
    for i in range(len(plan.table_levels)):
        e = jnp.exp2(tabled[i]).astype(bf16)
        keys = kt * e.T
        scores = [jnp.where(lvl == first + i, _dot(qh * e, keys), sc) for qh, sc in zip(q_heads, scores)]
    for i, m in enumerate(plan.direct_levels, start=first + len(plan.table_levels)):
        blocks = []
        for p0 in range(0, CHUNK, 2 * m):
            left, right = slice(p0, p0 + m), slice(p0 + m, p0 + 2 * m)
            q_side, k_side = (left, right) if rev else (right, left)
            mid_row = p0 + m if rev else p0 + m - 1
            blocks.append((q_side, k_side, cum[mid_row:mid_row + 1, :]))
        q_decay = [jnp.exp2(cum[qs] - mid).astype(bf16) for qs, _, mid in blocks]
        k_decay = []
        for _, ks, mid in blocks:
            decay = jnp.exp2(mid - cum[ks]).astype(bf16)
            zero = jnp.zeros((m, HEAD), bf16)
            k_decay += [zero, decay] if rev else [decay, zero]
        keys = kt * jnp.concatenate(k_decay, axis=0).T
        for h, qh in enumerate(q_heads):
            s = _dot(jnp.concatenate([qh[qs] * e for (qs, _, _), e in zip(blocks, q_decay)], axis=0), keys)
            rows = []
            for b, (qs, ks, _) in enumerate(blocks):
                if len(blocks) == 1:
                    updated = scores[h][qs, :] + s
                else:
                    updated = jnp.where(lvl[qs, :] == i, s[b * m:(b + 1) * m, :], scores[h][qs, :])
                rows += [updated, scores[h][ks, :]] if rev else [scores[h][ks, :], updated]
            scores[h] = jnp.concatenate(rows, axis=0)
    return [sc.astype(bf16) for sc in scores]


def _store_log2_split(g_ref, rows, g, block):
    width = g.shape[-1]
    x = g * LOG2E
    hi = x.astype(bf16)
    g_ref[rows, 0:width] = hi
    g_ref[rows, width:2 * width] = (x - hi.astype(f32)).astype(bf16)
    return jnp.min(jnp.sum(x.reshape(-1, block, width), axis=1))


def _decay_sums(g_split, table):
    width = g_split.shape[-1] // 2
    return _dot(table, jnp.concatenate([g_split[:, :width], g_split[:, width:]], axis=0))


def _key_masks(heads_per_group):
    if heads_per_group == 1:
        return [None]
    lane = lax.broadcasted_iota(jnp.int32, (1, HEAD), 1)
    return [(lane // (HEAD // heads_per_group) == sub).astype(bf16) for sub in range(heads_per_group)]


def _gated_chunks(chains, sums_ref, upcoming, upcoming_ref, heads_per_group, plan):
    masks = _key_masks(heads_per_group)
    groups = [slice(g * HEAD, (g + 1) * HEAD) for g in range(PAIR)]
    n_heads = PAIR * heads_per_group
    group_of = [h // heads_per_group for h in range(n_heads)]
    mask_of = [masks[h % heads_per_group] for h in range(n_heads)]
    flat = [(c, r) for c, (_, units) in enumerate(chains) for r in range(len(units))]
    unit = {(c, r): chains[c][1][r] for c, r in flat}
    index = {key: u for u, key in enumerate(flat)}
    cum_rows = slice(plan.table_blocks * CHUNK, (plan.table_blocks + 1) * CHUNK)
    cum = {key: sums_ref[index[key], cum_rows, :] for key in flat}
    total = {key: sums_ref[index[key], plan.row_total:plan.row_total + 1, :] for key in flat}
    vb = {key: [unit[key][2][:, h * HEAD:(h + 1) * HEAD] for h in range(n_heads)] for key in flat}
    upcoming = list(enumerate(upcoming))
    kv, whole = {}, {}
    for key in flat:
        k = unit[key][1]
        keys = (k * jnp.exp2(total[key] - cum[key])).astype(bf16)
        kv[key] = []
        for h in range(n_heads):
            own = keys[:, groups[group_of[h]]]
            kv[key].append(_dot_tn(own if mask_of[h] is None else own * mask_of[h], vb[key][h]))
        whole[key] = [jnp.broadcast_to(jnp.exp2(total[key][:, lanes]), (HEAD, HEAD)).T for lanes in groups]
    state = [list(states) for states, _ in chains]
    outs, pending, carried = {key: [None] * n_heads for key in flat}, None, {}

    def finish(done):
        pkey, ph, psc = done
        queries, entering = carried[pkey]
        lhs = jnp.concatenate([queries[:, groups[group_of[ph]]], psc], axis=1)
        outs[pkey][ph] = _dot(lhs, jnp.concatenate([entering[ph], vb[pkey][ph]], axis=0))

    for r in range(max(len(units) for _, units in chains)):
        live = [key for key in flat if key[1] == r]
        for key in live:
            queries = (unit[key][0] * jnp.exp2(cum[key])).astype(bf16)
            carried[key] = (queries, [s.astype(bf16) for s in state[key[0]]])
            state[key[0]] = [whole[key][group_of[h]] * state[key[0]][h] + kv[key][h] for h in range(n_heads)]
        for key in live:
            q, k, _, lvl, rev = unit[key]
            ones = (lvl[0:1, :] >= -1).astype(bf16)
            for g, lanes in enumerate(groups):
                tabled = [sums_ref[index[key], i * CHUNK:(i + 1) * CHUNK, lanes] for i in range(plan.table_blocks)]
                group_scores = _in_chunk_scores(q[:, lanes], k[:, lanes], cum[key][:, lanes], tabled, lvl, ones, rev,
                                                masks, plan)
                for sub, sc in enumerate(group_scores):
                    if pending is not None:
                        finish(pending)
                    pending = (key, g * heads_per_group + sub, sc)
            if upcoming:
                u, (g_split, table) = upcoming.pop(0)
                upcoming_ref[u, 0:plan.row_total + 8, :] = _decay_sums(g_split, table)
    for u, (g_split, table) in upcoming:
        upcoming_ref[u, 0:plan.row_total + 8, :] = _decay_sums(g_split, table)
    finish(pending)
    return [[outs[(c, r)] for r in range(len(units))] for c, (_, units) in enumerate(chains)], state


def _gated_scans(q_s, k_refs, v_s, g_refs, s_scr, sums_scr, table_ref, lvl_ref, o_scr, seq, heads_per_group, plan):
    n = seq // CHUNK
    n_seq = STEP_TOKENS // seq
    n_heads = PAIR * heads_per_group
    units = GATED_HEADS_PER_ITER // n_heads
    per_iter = min(n, units // 2)
    seqs_per_iter = min(n_seq, units // (2 * per_iter))
    assert n % per_iter == 0 and n_seq % seqs_per_iter == 0
    iters_per_seq = n // per_iter
    n_iters = (n_seq // seqs_per_iter) * iters_per_seq

    def layout(it):
        jj, i = it // iters_per_seq, it % iters_per_seq
        chains = []
        for js in range(seqs_per_iter):
            j = jj * seqs_per_iter + js
            for d in range(2):
                steps = [i * per_iter + r for r in range(per_iter)]
                chains.append((j, d, [_chunk_rows(j * seq, n - 1 - t if d else t) for t in steps]))
        return chains

    def sums_inputs(it):
        return [(g_refs[d][rw, :], table_ref[d]) for _, d, rws in layout(it) for rw in rws]

    for u, (g_split, table) in enumerate(sums_inputs(0)):
        sums_scr[0, u, 0:plan.row_total + 8, :] = _decay_sums(g_split, table)

    def iteration(it, slot):
        chains = layout(it)
        args = [([s_scr[j, d, h] for h in range(n_heads)],
                 [(q_s[rw, :], k_refs[d][rw, :], v_s[rw, :], lvl_ref[d], bool(d)) for rw in rws])
                for j, d, rws in chains]
        upcoming = sums_inputs(jnp.minimum(it + 1, n_iters - 1))
        outs, new_states = _gated_chunks(args, sums_scr.at[slot], upcoming, sums_scr.at[1 - slot], heads_per_group,
                                         plan)
        for c, (j, d, rws) in enumerate(chains):
            for h in range(n_heads):
                for r, rw in enumerate(rws):
                    o_scr[d, h, rw, :] = outs[c][r][h].astype(o_scr.dtype)
                s_scr[j, d, h] = new_states[c][h]

    assert n_iters % 2 == 0

    def body(it2, carry):
        iteration(2 * it2, 0)
        iteration(2 * it2 + 1, 1)
        return carry
    lax.fori_loop(0, n_iters // 2, body, 0)


def _even_kernel(*refs, seq, has_state, emit_state, final):
    it = iter(refs)
    x_ref, mod_ref, nw_ref = next(it), next(it), next(it)
    win_ref, wlow_ref, wo_ref = next(it), next(it), next(it)
    gkw_ref, gkb_ref, lb_ref, gnw_ref, fnw_ref = next(it), next(it), next(it), next(it), next(it)
    plan_tables = {plan: (next(it), next(it)) for plan in ALL_PLANS}
    s0a_ref, s0b_ref = (next(it), next(it)) if has_state else (None, None)
    out_ref = next(it)
    sta_ref, stb_ref = (next(it), next(it)) if emit_state else (None, None)
    h_scr, o_scr, gate_s, q_s, v_s, kf_s, kb_s, gf_s, gb_s, s_scr, sums_scr, bound_s = it
    n_seq = STEP_TOKENS // seq
    n_tiles = STEP_TOKENS // ROW_TILE

    _modulated_norm(x_ref, mod_ref, nw_ref, h_scr)

    width = PAIR * HEAD

    def run_scans(first_head, s0_ref, st_ref, h0, heads_per_group, k_refs, plans, block_decays, first=False,
                  last=False):
        bounded, robust, limit = plans
        bound_s[0] = (functools.reduce(jnp.minimum, block_decays) >= -limit).astype(jnp.int32)
        n_heads = PAIR * heads_per_group
        key_rows = HEAD // heads_per_group
        own_rows = [pl.ds((h % heads_per_group) * key_rows, key_rows) for h in range(n_heads)]
        for j in range(n_seq):
            for d in range(2):
                for h in range(n_heads):
                    if s0_ref is None or heads_per_group > 1:
                        s_scr[j, d, h] = jnp.zeros((HEAD, HEAD), f32)
                    if s0_ref is not None:
                        s_scr[j, d, h, own_rows[h], :] = s0_ref[j, d, h0 + h]
        def scans(plan):
            table_ref, lvl_ref = plan_tables[plan]
            _gated_scans(q_s, k_refs, v_s, (gf_s, gb_s), s_scr, sums_scr, table_ref, lvl_ref, o_scr, seq,
                         heads_per_group, plan)
        lax.cond(bound_s[0] == 1, functools.partial(scans, bounded), functools.partial(scans, robust))
        if st_ref is not None:
            for j in range(n_seq):
                for d in range(2):
                    for h in range(n_heads):
                        st_ref[j, d, h0 + h] = s_scr[j, d, h, own_rows[h], :]
        _add_heads_output(first_head, n_heads, gate_s, o_scr, gnw_ref, wo_ref, out_ref, first,
                          (x_ref, mod_ref, fnw_ref, final) if last else None)

    def hgrn_pair(pp, carry):
        lb = lb_ref[:, _pair_lanes(pp)]
        log_lb = jnp.log(lb)

        tiles = [pl.ds(i * ROW_TILE, ROW_TILE) for i in range(n_tiles)]
        projected = [_project(h_scr[rows, :], win_ref, pp, HGRN_COLS) for rows in tiles]
        block_decays = []
        for rows, (query, value, forget_f, forget_b, gate) in zip(tiles, projected):
            gate_s[rows, 0:width] = _silu(gate).astype(bf16)
            q_s[rows, :] = _silu(query)
            v_s[rows, 0:width] = value.astype(bf16)
            for a, k_s, g_s in ((forget_f, kf_s, gf_s), (forget_b, kb_s, gb_s)):
                z = log_lb - a
                u = jnp.exp(-jnp.abs(a))
                w = jnp.exp(-jnp.abs(z))
                r = 1.0 / (1.0 + u)
                log_f = jnp.maximum(z, 0.0) + jnp.minimum(a, 0.0) + jnp.log((1.0 + w) * r)
                block_decays.append(_store_log2_split(g_s, rows, log_f, HGRN_PLANS[0].block))
                k_s[rows, :] = (1.0 - lb) * jnp.where(a >= 0.0, u * r, r)
        run_scans(PAIR * pp, s0a_ref, sta_ref, PAIR * pp, 1, (kf_s, kb_s), HGRN_PLANS + (HGRN_DECAY_LIMIT,), block_decays)
        return carry

    def clear(i, carry):
        out_ref[0, _row_tile(i), :] = jnp.zeros((ROW_TILE, out_ref.shape[-1]), f32)
        return carry
    lax.fori_loop(0, n_tiles, clear, 0)
    lax.fori_loop(0, N_HGRN // PAIR, hgrn_pair, 0)

    assert N_GLA == PAIR * GLA_HEADS_PER_GROUP

    def gla_project(rows):
        h = h_scr[rows, :]
        narrow = [_dot(h, win_ref[:, c0:c0 + N_GLA * GLA_DK]) for c0 in GLA_QK_COLS]
        wide = [_dot(h, win_ref[:, c0:c0 + N_GLA * HEAD]) for c0 in GLA_COLS]
        return narrow + wide + [_dot(h, wlow_ref[...])]

    tiles = [pl.ds(i * ROW_TILE, ROW_TILE) for i in range(n_tiles)]
    projected = [gla_project(rows) for rows in tiles]
    block_decays = []
    for rows, (query, key, value, gate, low) in zip(tiles, projected):
        gate_s[rows, :] = _silu(gate).astype(bf16)
        q_s[rows, :] = query * (GLA_DK ** -0.5)
        kf_s[rows, :] = key
        v_s[rows, :] = value.astype(bf16)
        low = low.astype(bf16)
        for d, g_s in enumerate((gf_s, gb_s)):
            logits = _dot(low, gkw_ref[d]) + gkb_ref[d]
            log_gate = jnp.minimum(logits, 0.0) - jnp.log(1.0 + jnp.exp(-jnp.abs(logits)))
            block_decays.append(_store_log2_split(g_s, rows, log_gate * (1.0 / GLA_GATE_NORM), GLA_PLANS[0].block))
    run_scans(N_HGRN, s0b_ref, stb_ref, 0, GLA_HEADS_PER_GROUP, (kf_s, kf_s), GLA_PLANS + (GLA_DECAY_LIMIT,), block_decays, last=True)


def _odd_kernel(*refs, seq, has_state, emit_state, use_rope, final):
    it = iter(refs)
    lg_ref = next(it)
    x_ref, mod_ref, nw_ref = next(it), next(it), next(it)
    wr_ref, wo_ref, gnw_ref, fnw_ref = next(it), next(it), next(it), next(it)
    cos_ref, sin_ref = (next(it), next(it)) if use_rope else (None, None)
    s0_ref = next(it) if has_state else None
    out_ref = next(it)
    st_ref = next(it) if emit_state else None
    h_scr, o_scr, gate_s, q_s, kt_s, v_s, s_scr, dec_scr = it
    n_seq = STEP_TOKENS // seq
    n_tiles = STEP_TOKENS // ROW_TILE
    n_chunks = seq // CHUNK
    per_iter = min(n_chunks, RET_UNITS_PER_ITER // n_seq)
    assert n_chunks % per_iter == 0

    _modulated_norm(x_ref, mod_ref, nw_ref, h_scr)
    t_idx = lax.broadcasted_iota(jnp.int32, (CHUNK, CHUNK), 0)
    s_idx = lax.broadcasted_iota(jnp.int32, (CHUNK, CHUNK), 1)
    row_f = lax.broadcasted_iota(jnp.int32, (CHUNK, HEAD), 0).astype(f32)
    col_f = lax.broadcasted_iota(jnp.int32, (8, CHUNK), 1).astype(f32)
    chunk_len = jnp.full((8, HEAD), CHUNK, f32)
    if use_rope:
        lane = lax.broadcasted_iota(jnp.int32, (ROW_TILE, HEAD), 1)
        first_quarter = (lane // (HEAD // 4)) % 2 == 0

    def rope(x, cos, sin_signed):
        xr = jnp.where(first_quarter, pltpu.roll(x, HEAD - HEAD // 4, axis=1), pltpu.roll(x, HEAD // 4, axis=1))
        return x * cos + xr * sin_signed

    width = PAIR * HEAD

    def pair_body(pp):
        tiles = [pl.ds(i * ROW_TILE, ROW_TILE) for i in range(n_tiles)]
        projected = [_project(h_scr[rows, :], wr_ref, pp, RET_COLS) for rows in tiles]
        for rows, (q, k, value, gate) in zip(tiles, projected):
            gate_s[rows, :] = _silu(gate).astype(bf16)
            k = k * (HEAD ** -0.5)
            if use_rope:
                cos, sin_signed = cos_ref[rows, :], sin_ref[rows, :]
                heads = [slice(h * HEAD, (h + 1) * HEAD) for h in range(PAIR)]
                q = jnp.concatenate([rope(q[:, lanes], cos, sin_signed) for lanes in heads], axis=-1)
                k = jnp.concatenate([rope(k[:, lanes], cos, sin_signed) for lanes in heads], axis=-1)
            q_s[rows, :] = q.astype(bf16)
            kt_s[:, rows] = k.T
            v_s[rows, :] = value.astype(bf16)
        for h in range(PAIR):
            scan_head(PAIR * pp + h, h)
        last = pp == N_RET // PAIR - 1
        _add_heads_output(PAIR * pp, PAIR, gate_s, o_scr, gnw_ref, wo_ref, out_ref, pp == 0,
                          (x_ref, mod_ref, fnw_ref, final) if last else None)

    def scan_head(hh, h):
        lanes = slice(h * HEAD, (h + 1) * HEAD)
        lg_f = lg_ref[0, hh]
        lg_b = lg_ref[1, hh]
        for j in range(n_seq):
            for d in range(2):
                s_scr[j, d] = s0_ref[j, d, hh] if has_state else jnp.zeros((HEAD, HEAD), f32)

        dist = (t_idx - s_idx).astype(f32)
        dec_scr[0] = jnp.exp(lg_f * (row_f + 1.0))
        dec_scr[1] = jnp.exp(lg_b * (CHUNK - row_f))
        dec_scr[2] = (jnp.where(t_idx >= s_idx, jnp.exp(lg_f * jnp.maximum(dist, 0.0)), 0.0)
                      + jnp.where(s_idx >= t_idx, jnp.exp(lg_b * jnp.maximum(-dist, 0.0)), 0.0))
        dec_scr[3, 0:8, :] = jnp.exp(lg_f * (CHUNK - 1.0 - col_f))
        dec_scr[3, 8:16, :] = jnp.exp(lg_b * col_f)
        dec_scr[3, 16:24, :] = jnp.exp(lg_f * chunk_len)
        dec_scr[3, 24:32, :] = jnp.exp(lg_b * chunk_len)

        def body(i, c):
            units = [(j, r) for j in range(n_seq) for r in range(per_iter)]
            rows = {(j, r, d): _chunk_rows(j * seq, n_chunks - 1 - (i * per_iter + r) if d else i * per_iter + r)
                    for j, r in units for d in range(2)}
            scores = {u: _dot(q_s[rows[u + (0,)], lanes], kt_s[lanes, rows[u + (0,)]].astype(bf16)) for u in units}
            kv = {}
            for j, r in units:
                for d in range(2):
                    rw = rows[(j, r, d)]
                    keys = (kt_s[lanes, rw] * dec_scr[3, 8 * d:8 * d + 1, :]).astype(bf16)
                    kv[(j, r, d)] = _dot(keys, v_s[rw, lanes])
            state = {(j, d): s_scr[j, d] for j in range(n_seq) for d in range(2)}
            entering, carried = {}, {}
            for r in range(per_iter):
                for j in range(n_seq):
                    entering[(j, r)] = state[(j, 0)].astype(bf16)
                    carried[(j, r)] = _dot(q_s[rows[(j, r, 1)], lanes], state[(j, 1)].astype(bf16))
                    for d in range(2):
                        state[(j, d)] = dec_scr[3, 16 + 8 * d:17 + 8 * d, :] * state[(j, d)] + kv[(j, r, d)]
            seen_f = dec_scr[0].astype(bf16)
            for j, r in units:
                rw = rows[(j, r, 0)]
                lhs = jnp.concatenate([q_s[rw, lanes] * seen_f, (scores[(j, r)] * dec_scr[2]).astype(bf16)], axis=1)
                o_scr[0, h, rw, :] = _dot(lhs, jnp.concatenate([entering[(j, r)], v_s[rw, lanes]], axis=0))
                o_scr[1, h, rows[(j, r, 1)], :] = dec_scr[1] * carried[(j, r)]
            for (j, d), s in state.items():
                s_scr[j, d] = s
            return c
        lax.fori_loop(0, n_chunks // per_iter, body, 0)
        if emit_state:
            for j in range(n_seq):
                for d in range(2):
                    st_ref[j, d, hh] = s_scr[j, d]

    for pp in range(N_RET // PAIR):
        pair_body(pp)


def _const_spec(shape):
    zeros = (0,) * len(shape)
    return pl.BlockSpec(shape, lambda i: zeros, pipeline_mode=pl.Buffered(1))


def _step_spec(shape, per_step, buffers=None):
    zeros = (0,) * (len(shape) - 1)
    mode = {} if buffers is None else {"pipeline_mode": pl.Buffered(buffers)}
    return pl.BlockSpec((per_step,) + tuple(shape[1:]), lambda i: (i,) + zeros, **mode)


def _mod_spec(d, per_sequence):
    if per_sequence:
        return pl.BlockSpec((1, 3, d), lambda i: (i + 1, 0, 0))
    return pl.BlockSpec((1, 3, d), lambda i: (0, 0, 0))


def _layer_call(body, x, mod, consts, states, state_shapes, per_sequence_mod, scratch, name, smem_inputs=(),
                state_buffers=None):
    n_seq, seq, d = x.shape
    per_step = STEP_TOKENS // seq
    assert per_step * seq == STEP_TOKENS and n_seq % per_step == 0 and seq % CHUNK == 0
    assert not per_sequence_mod or per_step == 1
    n_steps = n_seq // per_step
    xs = x.reshape(n_steps, STEP_TOKENS, d)
    inputs = list(smem_inputs) + [xs, mod] + list(consts) + list(states)
    in_specs = [pl.BlockSpec(memory_space=pltpu.SMEM)] * len(smem_inputs)
    in_specs += [_step_spec(xs.shape, 1), _mod_spec(d, per_sequence_mod)]
    in_specs += [_const_spec(a.shape) for a in consts]
    in_specs += [_step_spec(s.shape, per_step) for s in states]
    out_shape = [jax.ShapeDtypeStruct(xs.shape, f32)] + [jax.ShapeDtypeStruct(s, f32) for s in state_shapes]
    out_specs = [_step_spec(xs.shape, 1)] + [_step_spec(s, per_step, buffers=state_buffers) for s in state_shapes]
    outs = pl.pallas_call(
        body,
        grid=(n_steps,),
        in_specs=in_specs,
        out_specs=out_specs,
        out_shape=out_shape,
        scratch_shapes=scratch,
        compiler_params=pltpu.CompilerParams(dimension_semantics=("arbitrary",), vmem_limit_bytes=VMEM_LIMIT_BYTES),
        name=name,
    )(*inputs)
    return [outs[0].reshape(x.shape)] + list(outs[1:])


def _even_layer(x, mod, norm_w, w, final_w, tables, states, emit_state, per_sequence_mod, final):
    n_seq, seq, d = x.shape
    consts = [norm_w, w["win"], w["wlow"], w["wout"], w["gkw"], w["gkb"], w["lb"], w["gnw"],
              final_w] + list(tables)
    state_shapes = [(n_seq, 2, N_HGRN, HEAD, HEAD), (n_seq, 2, N_GLA, GLA_DK, HEAD)] if emit_state else []
    scan_heads = max(PAIR, N_GLA)
    scratch = [
        pltpu.VMEM((STEP_TOKENS, d), bf16),
        pltpu.VMEM((2, scan_heads, STEP_TOKENS, HEAD), bf16),
        pltpu.VMEM((STEP_TOKENS, scan_heads * HEAD), bf16),
        pltpu.VMEM((STEP_TOKENS, PAIR * HEAD), f32),
        pltpu.VMEM((STEP_TOKENS, scan_heads * HEAD), bf16),
        pltpu.VMEM((STEP_TOKENS, PAIR * HEAD), f32),
        pltpu.VMEM((STEP_TOKENS, PAIR * HEAD), f32),
        pltpu.VMEM((STEP_TOKENS, 2 * PAIR * HEAD), bf16),
        pltpu.VMEM((STEP_TOKENS, 2 * PAIR * HEAD), bf16),
        pltpu.VMEM((STEP_TOKENS // seq, 2, scan_heads, HEAD, HEAD), f32),
        pltpu.VMEM((2, GATED_UNITS_MAX, TABLE_ROWS_MAX, PAIR * HEAD), f32),
        pltpu.SMEM((1,), jnp.int32),
    ]
    body = functools.partial(_even_kernel, seq=seq, has_state=states is not None, emit_state=emit_state, final=final)
    return _layer_call(body, x, mod, consts, states or (), state_shapes, per_sequence_mod, scratch,
                       "even_layer_seq%d" % seq)


def _odd_layer(x, mod, norm_w, w, final_w, log_decay, rope, state, emit_state, per_sequence_mod, final):
    n_seq, seq, d = x.shape
    consts = [norm_w, w["wr"], w["wout"], w["gnw"], final_w] + list(rope or ())
    state_shapes = [(n_seq, 2, N_RET, HEAD, HEAD)] if emit_state else []
    scratch = [
        pltpu.VMEM((STEP_TOKENS, d), bf16),
        pltpu.VMEM((2, PAIR, STEP_TOKENS, HEAD), f32),
        pltpu.VMEM((STEP_TOKENS, PAIR * HEAD), bf16),
        pltpu.VMEM((STEP_TOKENS, PAIR * HEAD), bf16),
        pltpu.VMEM((PAIR * HEAD, STEP_TOKENS), f32),
        pltpu.VMEM((STEP_TOKENS, PAIR * HEAD), bf16),
        pltpu.VMEM((STEP_TOKENS // seq, 2, HEAD, HEAD), f32),
        pltpu.VMEM((4, CHUNK, HEAD), f32),
    ]
    body = functools.partial(_odd_kernel, seq=seq, has_state=state is not None, emit_state=emit_state,
                             use_rope=rope is not None, final=final)
    return _layer_call(body, x, mod, consts, () if state is None else (state,), state_shapes, per_sequence_mod,
                       scratch, "odd_layer_seq%d" % seq, smem_inputs=(log_decay,))


def _even_weights(w_in, gk_w, gk_b, lb, gn_w, w_out):
    c0 = GLA_LOW_COLS[0]
    wlow = jnp.pad(w_in[:, c0:c0 + 2 * GLA_RANK], ((0, 0), (0, HEAD - 2 * GLA_RANK)))
    gkw = jnp.stack([jnp.pad(gk_w[d], ((d * GLA_RANK, HEAD - (d + 1) * GLA_RANK), (0, 0))) for d in range(2)])
    return {"win": w_in.astype(bf16), "wlow": wlow.astype(bf16), "gkw": gkw.astype(bf16),
            "gkb": gk_b.reshape(2, 1, -1), "lb": lb.reshape(1, -1), "gnw": gn_w.reshape(1, -1),
            "wout": w_out.astype(bf16)}


def _odd_weights(w_in, gn_w, w_out):
    return {"wr": w_in.astype(bf16), "gnw": gn_w.reshape(1, -1), "wout": w_out.astype(bf16)}


def _rope_tables(seq):
    rows = seq // GRID_W
    t_row = np.repeat(np.arange(rows), GRID_W).astype(np.float32)
    t_col = np.tile(np.arange(GRID_W), rows).astype(np.float32)
    half = HEAD // 2
    inv = (ROPE_BASE ** (-np.arange(0, half, 2, dtype=np.float32) / half)).astype(np.float32)
    ang_r = t_row[:, None] * inv
    ang_c = t_col[:, None] * inv
    ang = np.concatenate([ang_r, ang_r, ang_c, ang_c], axis=-1).astype(np.float32)
    sign = np.where((np.arange(HEAD) // (HEAD // 4)) % 2 == 0, -1.0, 1.0).astype(np.float32)
    return jnp.asarray(np.cos(ang), f32), jnp.asarray(np.sin(ang) * sign, f32)


def kernel(x_prompt, x_sample, state_hgrn, state_gla, state_ret, c, c_ctx, norm_w, ada_w, ada_b, w_in_even, hgrn_lb, gla_gk_w, gla_gk_b, gn_even, w_out_even, w_in_odd, ret_decay, gn_odd, w_out_odd, final_norm_w):
    depth, d = norm_w.shape
    n_lat = x_sample.shape[0]
    n_cond = -(-(1 + n_lat) // 8) * 8
    cond = jnp.zeros((n_cond, d), f32).at[0].set(c_ctx).at[1:1 + n_lat].set(c)
    mod = _modulation(cond, ada_w, ada_b).reshape(depth, n_cond, 3, d)
    lbs = jnp.cumsum(jax.nn.softmax(hgrn_lb.astype(f32), axis=0), axis=0)
    final_w = final_norm_w.reshape(1, d)
    rope = _rope_tables(x_sample.shape[1])
    tables = sum((_scan_tables(plan) for plan in ALL_PLANS), ())

    x_c, x_l = x_prompt, x_sample
    new_hgrn, new_gla, new_ret = [], [], []
    for l in range(depth):
        i = l // 2
        final = l == depth - 1
        nw = norm_w[l].reshape(1, d)
        if l % 2 == 0:
            w = _even_weights(w_in_even[i], gla_gk_w[i], gla_gk_b[i], lbs[i], gn_even[i], w_out_even[i])
            x_c, st_a, st_b = _even_layer(x_c, mod[l], nw, w, final_w, tables, None, True, False, final)
            (x_l,) = _even_layer(x_l, mod[l], nw, w, final_w, tables, (state_hgrn[:, i], state_gla[:, i]),
                                 False, True, final)
            new_hgrn.append(st_a)
            new_gla.append(st_b)
        else:
            w = _odd_weights(w_in_odd[i], gn_odd[i], w_out_odd[i])
            log_decay = jax.nn.log_sigmoid(ret_decay[i].astype(f32))
            x_c, st_c = _odd_layer(x_c, mod[l], nw, w, final_w, log_decay, None, None, True, False, final)
            (x_l,) = _odd_layer(x_l, mod[l], nw, w, final_w, log_decay, rope, state_ret[:, i], False, True, final)
            new_ret.append(st_c)
    def stacked(states):
        return states[0][:, None] if len(states) == 1 else jnp.stack(states, axis=1)
    return (x_c, x_l, stacked(new_hgrn), stacked(new_gla), stacked(new_ret))
```

```python
import functools
from typing import NamedTuple

import numpy as np
import jax
import jax.numpy as jnp
from jax import lax
from jax.experimental import pallas as pl
from jax.experimental.pallas import tpu as pltpu

f32 = jnp.float32
bf16 = jnp.bfloat16

EPS = 1e-6
LOG2E = 1.4426950408889634
HEAD = 128
N_HGRN = 4
N_GLA = 4
GLA_DK = 64
N_RET = 8
N_HEADS = 8
GLA_RANK = 16
GLA_GATE_NORM = 16.0
GRID_W = 64
ROPE_BASE = 10000.0

EVEN_SPLITS = (N_HGRN * HEAD,) * 5 + (N_GLA * GLA_DK,) * 2 + (N_GLA * HEAD,) * 2 + (GLA_RANK,) * 2
EVEN_STARTS = tuple(int(c) for c in np.cumsum((0,) + EVEN_SPLITS[:-1]))
HGRN_COLS = EVEN_STARTS[0:5]
GLA_QK_COLS = EVEN_STARTS[5:7]
GLA_COLS = EVEN_STARTS[7:9]
GLA_LOW_COLS = EVEN_STARTS[9:11]
RET_COLS = tuple(i * N_RET * HEAD for i in range(4))

CHUNK = 128


class _ScanPlan(NamedTuple):
    block: int
    table_levels: tuple
    direct_levels: tuple

    @property
    def block_rows(self):
        return 1 if 0 < self.block < CHUNK else 0

    @property
    def table_blocks(self):
        return self.block_rows + len(self.table_levels)

    @property
    def row_total(self):
        return (self.table_blocks + 1) * CHUNK

    @property
    def diag_id(self):
        return len(self.table_levels) + len(self.direct_levels)


ROBUST_PLAN = _ScanPlan(0, (1, 2, 4), (8, 16, 32, 64))
BLOCK_PLAN = _ScanPlan(64, (), (64,))
CHUNK_PLAN = _ScanPlan(CHUNK, (), ())
HGRN_PLANS = (BLOCK_PLAN, ROBUST_PLAN)
GLA_PLANS = (CHUNK_PLAN, ROBUST_PLAN)
ALL_PLANS = (ROBUST_PLAN, BLOCK_PLAN, CHUNK_PLAN)
HGRN_DECAY_LIMIT = 104.0
GLA_DECAY_LIMIT = 90.0
TABLE_ROWS_MAX = max(p.row_total for p in ALL_PLANS) + 8
PAIR = 2
GATED_HEADS_PER_ITER = 8
GATED_UNITS_MAX = 4
GLA_HEADS_PER_GROUP = HEAD // GLA_DK
RET_UNITS_PER_ITER = 8
STEP_TOKENS = 1024
ROW_TILE = 512
MOD_COLS = 768
VMEM_LIMIT_BYTES = 61 * 1024 * 1024


def _dot(a, b):
    return jnp.dot(a, b, preferred_element_type=f32)


def _dot_tn(a, b):
    return lax.dot_general(a, b, (((0,), (0,)), ((), ())), preferred_element_type=f32)


def _silu(x):
    return x * jax.nn.sigmoid(x)


def _row_tile(i):
    return pl.ds(pl.multiple_of(i * ROW_TILE, ROW_TILE), ROW_TILE)


def _chunk_rows(seq_start, c):
    return pl.ds(pl.multiple_of(seq_start + c * CHUNK, CHUNK), CHUNK)


def _mod_kernel(cond_ref, w_ref, b_ref, o_ref):
    o_ref[0] = _dot(_silu(cond_ref[...]).astype(bf16), w_ref[0].astype(bf16)) + b_ref[0]


def _modulation(cond, ada_w, ada_b):
    depth, d, d3 = ada_w.shape
    rows = cond.shape[0]
    return pl.pallas_call(
        _mod_kernel,
        grid=(depth, d3 // MOD_COLS),
        in_specs=[
            pl.BlockSpec((rows, d), lambda l, j: (0, 0)),
            pl.BlockSpec((1, d, MOD_COLS), lambda l, j: (l, 0, j)),
            pl.BlockSpec((1, 1, MOD_COLS), lambda l, j: (l, 0, j)),
        ],
        out_specs=pl.BlockSpec((1, rows, MOD_COLS), lambda l, j: (l, 0, j)),
        out_shape=jax.ShapeDtypeStruct((depth, rows, d3), f32),
        compiler_params=pltpu.CompilerParams(dimension_semantics=("arbitrary", "arbitrary")),
        name="modulation",
    )(cond, ada_w, ada_b.reshape(depth, 1, d3))


def _modulated_norm(x_ref, mod_ref, nw_ref, h_scr):
    def body(i, carry):
        rows = _row_tile(i)
        x = x_ref[0, rows, :]
        y = x * lax.rsqrt(jnp.mean(x * x, axis=-1, keepdims=True) + EPS) * nw_ref[...]
        h_scr[rows, :] = (y * (1.0 + mod_ref[0, 1:2, :]) + mod_ref[0, 0:1, :]).astype(bf16)
        return carry
    lax.fori_loop(0, STEP_TOKENS // ROW_TILE, body, 0)


def _aligned_slice(start, width):
    return pl.ds(start if isinstance(start, int) else pl.multiple_of(start, width), width)


def _pair_lanes(pair):
    return _aligned_slice(pair * (PAIR * HEAD), PAIR * HEAD)


def _project(h, w_ref, pair, first_cols):
    width = PAIR * HEAD
    return [_dot(h, w_ref[:, _aligned_slice(c0 + pair * width, width)]) for c0 in first_cols]


def _add_heads_output(first_head, n_heads, gate_s, o_scr, gnw_ref, wo_ref, out_ref, first, finish):
    width = n_heads * HEAD
    lanes = _aligned_slice(first_head * HEAD, width)
    tiles = []
    for i in range(STEP_TOKENS // ROW_TILE):
        rows = pl.ds(i * ROW_TILE, ROW_TILE)
        parts = []
        for h in range(n_heads):
            o = o_scr[0, h, rows, :].astype(f32) + o_scr[1, h, rows, :].astype(f32)
            parts.append(o * lax.rsqrt(jnp.mean(o * o, axis=-1, keepdims=True) + EPS))
        y = jnp.concatenate(parts, axis=-1) * gnw_ref[:, lanes]
        tiles.append((rows, (y * gate_s[rows, 0:width].astype(f32)).astype(bf16)))
    products = [(rows, _dot(z, wo_ref[lanes, :])) for rows, z in tiles]
    for rows, product in products:
        mixed = product if first else out_ref[0, rows, :] + product
        if finish is not None:
            x_ref, mod_ref, fnw_ref, final = finish
            mixed = x_ref[0, rows, :] + mod_ref[0, 2:3, :] * mixed
            if final:
                mixed = mixed * lax.rsqrt(jnp.mean(mixed * mixed, axis=-1, keepdims=True) + EPS) * fnw_ref[...]
        out_ref[0, rows, :] = mixed


def _scan_tables(plan):
    t = np.arange(CHUNK)[:, None]
    j = np.arange(CHUNK)[None, :]
    fwd = []
    if plan.block_rows:
        fwd.append((j <= t) & (j // plan.block == t // plan.block))
    for m in plan.table_levels:
        mid = (t // (2 * m)) * (2 * m) + m
        right = t >= mid
        fwd.append(np.where(right, (j >= mid) & (j <= t), (j > t) & (j < mid)))
    fwd.append(j <= t)
    fwd.append(np.ones((8, CHUNK), bool))
    fwd = np.concatenate(fwd, axis=0).astype(np.float32)
    bwd = fwd.copy()
    n_sym = plan.row_total // CHUNK
    bwd[:plan.row_total] = fwd[:plan.row_total].reshape(n_sym, CHUNK, CHUNK)[:, ::-1, ::-1].reshape(-1, CHUNK)
    table = np.stack([np.tile(fwd, (1, 2)), np.tile(bwd, (1, 2))])
    first = 1 if plan.block else 0
    ids = np.full((CHUNK, CHUNK), -1, np.int32)
    for i, m in enumerate(plan.table_levels + plan.direct_levels):
        ids = np.where((t > j) & ((t ^ j) >= m) & ((t ^ j) < 2 * m), first + i, ids)
    if plan.block:
        ids = np.where((t >= j) & (t // plan.block == j // plan.block), 0, ids)
    else:
        ids = np.where(t == j, plan.diag_id, ids)
    return jnp.asarray(table, bf16), jnp.asarray(np.stack([ids, ids.T]).astype(np.int32))


def _in_chunk_scores(q, k, cum, tabled, lvl, ones, rev, key_masks, plan):
    qb = q.astype(bf16)
    q_heads = [qb if mask is None else qb * mask for mask in key_masks]
    kt = k.astype(bf16).T
    if plan.block:
        inside = tabled[0] if plan.block_rows else cum
        grow, decay = jnp.exp2(-inside).astype(bf16), jnp.exp2(inside).astype(bf16)
        keys = kt * grow.T
        scores = [jnp.where(lvl == 0, _dot(qh * decay, keys), 0.0) for qh in q_heads]
        tabled = tabled[plan.block_rows:]
    else:
        diag_keys = kt * ones
        scores = [jnp.where(lvl == plan.diag_id, _dot(qh, diag_keys), 0.0) for qh in q_heads]
    first = 1 if plan.block else 0
    for i in range(len(plan.table_levels)):
        e = jnp.exp2(tabled[i]).astype(bf16)
        keys = kt * e.T
        scores = [jnp.where(lvl == first + i, _dot(qh * e, keys), sc) for qh, sc in zip(q_heads, scores)]
    for i, m in enumerate(plan.direct_levels, start=first + len(plan.table_levels)):
        blocks = []
        for p0 in range(0, CHUNK, 2 * m):
            left, right = slice(p0, p0 + m), slice(p0 + m, p0 + 2 * m)
            q_side, k_side = (left, right) if rev else (right, left)
            mid_row = p0 + m if rev else p0 + m - 1
            blocks.append((q_side, k_side, cum[mid_row:mid_row + 1, :]))
        q_decay = [jnp.exp2(cum[qs] - mid).astype(bf16) for qs, _, mid in blocks]
        k_decay = []
        for _, ks, mid in blocks:
            decay = jnp.exp2(mid - cum[ks]).astype(bf16)
            zero = jnp.zeros((m, HEAD), bf16)
            k_decay += [zero, decay] if rev else [decay, zero]
        keys = kt * jnp.concatenate(k_decay, axis=0).T
        for h, qh in enumerate(q_heads):
            s = _dot(jnp.concatenate([qh[qs] * e for (qs, _, _), e in zip(blocks, q_decay)], axis=0), keys)
            rows = []
            for b, (qs, ks, _) in enumerate(blocks):
                if len(blocks) == 1:
                    updated = scores[h][qs, :] + s
                else:
                    updated = jnp.where(lvl[qs, :] == i, s[b * m:(b + 1) * m, :], scores[h][qs, :])
                rows += [updated, scores[h][ks, :]] if rev else [scores[h][ks, :], updated]
            scores[h] = jnp.concatenate(rows, axis=0)
    return [sc.astype(bf16) for sc in scores]


def _store_log2_split(g_ref, rows, g, block):
    width = g.shape[-1]
    x = g * LOG2E
    hi = x.astype(bf16)
    g_ref[rows, 0:width] = hi
    g_ref[rows, width:2 * width] = (x - hi.astype(f32)).astype(bf16)
    return jnp.min(jnp.sum(x.reshape(-1, block, width), axis=1))


def _decay_sums(g_split, table):
    width = g_split.shape[-1] // 2
    return _dot(table, jnp.concatenate([g_split[:, :width], g_split[:, width:]], axis=0))


def _key_masks(heads_per_group):
    if heads_per_group == 1:
        return [None]
    lane = lax.broadcasted_iota(jnp.int32, (1, HEAD), 1)
    return [(lane // (HEAD // heads_per_group) == sub).astype(bf16) for sub in range(heads_per_group)]


def _gated_chunks(chains, sums_ref, upcoming, upcoming_ref, heads_per_group, plan):
    masks = _key_masks(heads_per_group)
    groups = [slice(g * HEAD, (g + 1) * HEAD) for g in range(PAIR)]
    n_heads = PAIR * heads_per_group
    group_of = [h // heads_per_group for h in range(n_heads)]
    mask_of = [masks[h % heads_per_group] for h in range(n_heads)]
    flat = [(c, r) for c, (_, units) in enumerate(chains) for r in range(len(units))]
    unit = {(c, r): chains[c][1][r] for c, r in flat}
    index = {key: u for u, key in enumerate(flat)}
    cum_rows = slice(plan.table_blocks * CHUNK, (plan.table_blocks + 1) * CHUNK)
    cum = {key: sums_ref[index[key], cum_rows, :] for key in flat}
    total = {key: sums_ref[index[key], plan.row_total:plan.row_total + 1, :] for key in flat}
    vb = {key: [unit[key][2][:, h * HEAD:(h + 1) * HEAD] for h in range(n_heads)] for key in flat}
    upcoming = list(enumerate(upcoming))
    kv, whole = {}, {}
    for key in flat:
        k = unit[key][1]
        keys = (k * jnp.exp2(total[key] - cum[key])).astype(bf16)
        kv[key] = []
        for h in range(n_heads):
            own = keys[:, groups[group_of[h]]]
            kv[key].append(_dot_tn(own if mask_of[h] is None else own * mask_of[h], vb[key][h]))
        whole[key] = [jnp.broadcast_to(jnp.exp2(total[key][:, lanes]), (HEAD, HEAD)).T for lanes in groups]
    state = [list(states) for states, _ in chains]
    outs, pending, carried = {key: [None] * n_heads for key in flat}, None, {}

    def finish(done):
        pkey, ph, psc = done
        queries, entering = carried[pkey]
        lhs = jnp.concatenate([queries[:, groups[group_of[ph]]], psc], axis=1)
        outs[pkey][ph] = _dot(lhs, jnp.concatenate([entering[ph], vb[pkey][ph]], axis=0))

    for r in range(max(len(units) for _, units in chains)):
        live = [key for key in flat if key[1] == r]
        for key in live:
            queries = (unit[key][0] * jnp.exp2(cum[key])).astype(bf16)
            carried[key] = (queries, [s.astype(bf16) for s in state[key[0]]])
            state[key[0]] = [whole[key][group_of[h]] * state[key[0]][h] + kv[key][h] for h in range(n_heads)]
        for key in live:
            q, k, _, lvl, rev = unit[key]
            ones = (lvl[0:1, :] >= -1).astype(bf16)
            for g, lanes in enumerate(groups):
                tabled = [sums_ref[index[key], i * CHUNK:(i + 1) * CHUNK, lanes] for i in range(plan.table_blocks)]
                group_scores = _in_chunk_scores(q[:, lanes], k[:, lanes], cum[key][:, lanes], tabled, lvl, ones, rev,
                                                masks, plan)
                for sub, sc in enumerate(group_scores):
                    if pending is not None:
                        finish(pending)
                    pending = (key, g * heads_per_group + sub, sc)
            if upcoming:
                u, (g_split, table) = upcoming.pop(0)
                upcoming_ref[u, 0:plan.row_total + 8, :] = _decay_sums(g_split, table)
    for u, (g_split, table) in upcoming:
        upcoming_ref[u, 0:plan.row_total + 8, :] = _decay_sums(g_split, table)
    finish(pending)
    return [[outs[(c, r)] for r in range(len(units))] for c, (_, units) in enumerate(chains)], state


def _gated_scans(q_s, k_refs, v_s, g_refs, s_scr, sums_scr, table_ref, lvl_ref, o_scr, seq, heads_per_group, plan):
    n = seq // CHUNK
    n_seq = STEP_TOKENS // seq
    n_heads = PAIR * heads_per_group
    units = GATED_HEADS_PER_ITER // n_heads
    per_iter = min(n, units // 2)
    seqs_per_iter = min(n_seq, units // (2 * per_iter))
    assert n % per_iter == 0 and n_seq % seqs_per_iter == 0
    iters_per_seq = n // per_iter
    n_iters = (n_seq // seqs_per_iter) * iters_per_seq

    def layout(it):
        jj, i = it // iters_per_seq, it % iters_per_seq
        chains = []
        for js in range(seqs_per_iter):
            j = jj * seqs_per_iter + js
            for d in range(2):
                steps = [i * per_iter + r for r in range(per_iter)]
                chains.append((j, d, [_chunk_rows(j * seq, n - 1 - t if d else t) for t in steps]))
        return chains

    def sums_inputs(it):
        return [(g_refs[d][rw, :], table_ref[d]) for _, d, rws in layout(it) for rw in rws]

    for u, (g_split, table) in enumerate(sums_inputs(0)):
        sums_scr[0, u, 0:plan.row_total + 8, :] = _decay_sums(g_split, table)

    def iteration(it, slot):
        chains = layout(it)
        args = [([s_scr[j, d, h] for h in range(n_heads)],
                 [(q_s[rw, :], k_refs[d][rw, :], v_s[rw, :], lvl_ref[d], bool(d)) for rw in rws])
                for j, d, rws in chains]
        upcoming = sums_inputs(jnp.minimum(it + 1, n_iters - 1))
        outs, new_states = _gated_chunks(args, sums_scr.at[slot], upcoming, sums_scr.at[1 - slot], heads_per_group,
                                         plan)
        for c, (j, d, rws) in enumerate(chains):
            for h in range(n_heads):
                for r, rw in enumerate(rws):
                    o_scr[d, h, rw, :] = outs[c][r][h].astype(o_scr.dtype)
                s_scr[j, d, h] = new_states[c][h]

    assert n_iters % 2 == 0

    def body(it2, carry):
        iteration(2 * it2, 0)
        iteration(2 * it2 + 1, 1)
        return carry
    lax.fori_loop(0, n_iters // 2, body, 0)


def _even_kernel(*refs, seq, has_state, emit_state, final):
    it = iter(refs)
    x_ref, mod_ref, nw_ref = next(it), next(it), next(it)
    win_ref, wlow_ref, wo_ref = next(it), next(it), next(it)
    gkw_ref, gkb_ref, lb_ref, gnw_ref, fnw_ref = next(it), next(it), next(it), next(it), next(it)
    plan_tables = {plan: (next(it), next(it)) for plan in ALL_PLANS}
    s0a_ref, s0b_ref = (next(it), next(it)) if has_state else (None, None)
    out_ref = next(it)
    sta_ref, stb_ref = (next(it), next(it)) if emit_state else (None, None)
    h_scr, o_scr, gate_s, q_s, v_s, kf_s, kb_s, gf_s, gb_s, s_scr, sums_scr, bound_s = it
    n_seq = STEP_TOKENS // seq
    n_tiles = STEP_TOKENS // ROW_TILE

    _modulated_norm(x_ref, mod_ref, nw_ref, h_scr)

    width = PAIR * HEAD

    def run_scans(first_head, s0_ref, st_ref, h0, heads_per_group, k_refs, plans, block_decays, first=False,
                  last=False):
        bounded, robust, limit = plans
        bound_s[0] = (functools.reduce(jnp.minimum, block_decays) >= -limit).astype(jnp.int32)
        n_heads = PAIR * heads_per_group
        key_rows = HEAD // heads_per_group
        own_rows = [pl.ds((h % heads_per_group) * key_rows, key_rows) for h in range(n_heads)]
        for j in range(n_seq):
            for d in range(2):
                for h in range(n_heads):
                    if s0_ref is None or heads_per_group > 1:
                        s_scr[j, d, h] = jnp.zeros((HEAD, HEAD), f32)
                    if s0_ref is not None:
                        s_scr[j, d, h, own_rows[h], :] = s0_ref[j, d, h0 + h]
        def scans(plan):
            table_ref, lvl_ref = plan_tables[plan]
            _gated_scans(q_s, k_refs, v_s, (gf_s, gb_s), s_scr, sums_scr, table_ref, lvl_ref, o_scr, seq,
                         heads_per_group, plan)
        lax.cond(bound_s[0] == 1, functools.partial(scans, bounded), functools.partial(scans, robust))
        if st_ref is not None:
            for j in range(n_seq):
                for d in range(2):
                    for h in range(n_heads):
                        st_ref[j, d, h0 + h] = s_scr[j, d, h, own_rows[h], :]
        _add_heads_output(first_head, n_heads, gate_s, o_scr, gnw_ref, wo_ref, out_ref, first,
                          (x_ref, mod_ref, fnw_ref, final) if last else None)

    def hgrn_pair(pp, carry):
        lb = lb_ref[:, _pair_lanes(pp)]
        log_lb = jnp.log(lb)

        tiles = [pl.ds(i * ROW_TILE, ROW_TILE) for i in range(n_tiles)]
        projected = [_project(h_scr[rows, :], win_ref, pp, HGRN_COLS) for rows in tiles]
        block_decays = []
        for rows, (query, value, forget_f, forget_b, gate) in zip(tiles, projected):
            gate_s[rows, 0:width] = _silu(gate).astype(bf16)
            q_s[rows, :] = _silu(query)
            v_s[rows, 0:width] = value.astype(bf16)
            for a, k_s, g_s in ((forget_f, kf_s, gf_s), (forget_b, kb_s, gb_s)):
                z = log_lb - a
                u = jnp.exp(-jnp.abs(a))
                w = jnp.exp(-jnp.abs(z))
                r = 1.0 / (1.0 + u)
                log_f = jnp.maximum(z, 0.0) + jnp.minimum(a, 0.0) + jnp.log((1.0 + w) * r)
                block_decays.append(_store_log2_split(g_s, rows, log_f, HGRN_PLANS[0].block))
                k_s[rows, :] = (1.0 - lb) * jnp.where(a >= 0.0, u * r, r)
        run_scans(PAIR * pp, s0a_ref, sta_ref, PAIR * pp, 1, (kf_s, kb_s), HGRN_PLANS + (HGRN_DECAY_LIMIT,), block_decays)
        return carry

    def clear(i, carry):
        out_ref[0, _row_tile(i), :] = jnp.zeros((ROW_TILE, out_ref.shape[-1]), f32)
        return carry
    lax.fori_loop(0, n_tiles, clear, 0)
    lax.fori_loop(0, N_HGRN // PAIR, hgrn_pair, 0)

    assert N_GLA == PAIR * GLA_HEADS_PER_GROUP

    def gla_project(rows):
        h = h_scr[rows, :]
        narrow = [_dot(h, win_ref[:, c0:c0 + N_GLA * GLA_DK]) for c0 in GLA_QK_COLS]
        wide = [_dot(h, win_ref[:, c0:c0 + N_GLA * HEAD]) for c0 in GLA_COLS]
        return narrow + wide + [_dot(h, wlow_ref[...])]

    tiles = [pl.ds(i * ROW_TILE, ROW_TILE) for i in range(n_tiles)]
    projected = [gla_project(rows) for rows in tiles]
    block_decays = []
    for rows, (query, key, value, gate, low) in zip(tiles, projected):
        gate_s[rows, :] = _silu(gate).astype(bf16)
        q_s[rows, :] = query * (GLA_DK ** -0.5)
        kf_s[rows, :] = key
        v_s[rows, :] = value.astype(bf16)
        low = low.astype(bf16)
        for d, g_s in enumerate((gf_s, gb_s)):
            logits = _dot(low, gkw_ref[d]) + gkb_ref[d]
            log_gate = jnp.minimum(logits, 0.0) - jnp.log(1.0 + jnp.exp(-jnp.abs(logits)))
            block_decays.append(_store_log2_split(g_s, rows, log_gate * (1.0 / GLA_GATE_NORM), GLA_PLANS[0].block))
    run_scans(N_HGRN, s0b_ref, stb_ref, 0, GLA_HEADS_PER_GROUP, (kf_s, kf_s), GLA_PLANS + (GLA_DECAY_LIMIT,), block_decays, last=True)


def _odd_kernel(*refs, seq, has_state, emit_state, use_rope, final):
    it = iter(refs)
    lg_ref = next(it)
    x_ref, mod_ref, nw_ref = next(it), next(it), next(it)
    wr_ref, wo_ref, gnw_ref, fnw_ref = next(it), next(it), next(it), next(it)
    cos_ref, sin_ref = (next(it), next(it)) if use_rope else (None, None)
    s0_ref = next(it) if has_state else None
    out_ref = next(it)
    st_ref = next(it) if emit_state else None
    h_scr, o_scr, gate_s, q_s, kt_s, v_s, s_scr, dec_scr = it
    n_seq = STEP_TOKENS // seq
    n_tiles = STEP_TOKENS // ROW_TILE
    n_chunks = seq // CHUNK
    per_iter = min(n_chunks, RET_UNITS_PER_ITER // n_seq)
    assert n_chunks % per_iter == 0

    _modulated_norm(x_ref, mod_ref, nw_ref, h_scr)
    t_idx = lax.broadcasted_iota(jnp.int32, (CHUNK, CHUNK), 0)
    s_idx = lax.broadcasted_iota(jnp.int32, (CHUNK, CHUNK), 1)
    row_f = lax.broadcasted_iota(jnp.int32, (CHUNK, HEAD), 0).astype(f32)
    col_f = lax.broadcasted_iota(jnp.int32, (8, CHUNK), 1).astype(f32)
    chunk_len = jnp.full((8, HEAD), CHUNK, f32)
    if use_rope:
        lane = lax.broadcasted_iota(jnp.int32, (ROW_TILE, HEAD), 1)
        first_quarter = (lane // (HEAD // 4)) % 2 == 0

    def rope(x, cos, sin_signed):
        xr = jnp.where(first_quarter, pltpu.roll(x, HEAD - HEAD // 4, axis=1), pltpu.roll(x, HEAD // 4, axis=1))
        return x * cos + xr * sin_signed

    width = PAIR * HEAD

    def pair_body(pp):
        tiles = [pl.ds(i * ROW_TILE, ROW_TILE) for i in range(n_tiles)]
        projected = [_project(h_scr[rows, :], wr_ref, pp, RET_COLS) for rows in tiles]
        for rows, (q, k, value, gate) in zip(tiles, projected):
            gate_s[rows, :] = _silu(gate).astype(bf16)
            k = k * (HEAD ** -0.5)
            if use_rope:
                cos, sin_signed = cos_ref[rows, :], sin_ref[rows, :]
                heads = [slice(h * HEAD, (h + 1) * HEAD) for h in range(PAIR)]
                q = jnp.concatenate([rope(q[:, lanes], cos, sin_signed) for lanes in heads], axis=-1)
                k = jnp.concatenate([rope(k[:, lanes], cos, sin_signed) for lanes in heads], axis=-1)
            q_s[rows, :] = q.astype(bf16)
            kt_s[:, rows] = k.T
            v_s[rows, :] = value.astype(bf16)
        for h in range(PAIR):
            scan_head(PAIR * pp + h, h)
        last = pp == N_RET // PAIR - 1
        _add_heads_output(PAIR * pp, PAIR, gate_s, o_scr, gnw_ref, wo_ref, out_ref, pp == 0,
                          (x_ref, mod_ref, fnw_ref, final) if last else None)

    def scan_head(hh, h):
        lanes = slice(h * HEAD, (h + 1) * HEAD)
        lg_f = lg_ref[0, hh]
        lg_b = lg_ref[1, hh]
        for j in range(n_seq):
            for d in range(2):
                s_scr[j, d] = s0_ref[j, d, hh] if has_state else jnp.zeros((HEAD, HEAD), f32)

        dist = (t_idx - s_idx).astype(f32)
        dec_scr[0] = jnp.exp(lg_f * (row_f + 1.0))
        dec_scr[1] = jnp.exp(lg_b * (CHUNK - row_f))
        dec_scr[2] = (jnp.where(t_idx >= s_idx, jnp.exp(lg_f * jnp.maximum(dist, 0.0)), 0.0)
                      + jnp.where(s_idx >= t_idx, jnp.exp(lg_b * jnp.maximum(-dist, 0.0)), 0.0))
        dec_scr[3, 0:8, :] = jnp.exp(lg_f * (CHUNK - 1.0 - col_f))
        dec_scr[3, 8:16, :] = jnp.exp(lg_b * col_f)
        dec_scr[3, 16:24, :] = jnp.exp(lg_f * chunk_len)
        dec_scr[3, 24:32, :] = jnp.exp(lg_b * chunk_len)

        def body(i, c):
            units = [(j, r) for j in range(n_seq) for r in range(per_iter)]
            rows = {(j, r, d): _chunk_rows(j * seq, n_chunks - 1 - (i * per_iter + r) if d else i * per_iter + r)
                    for j, r in units for d in range(2)}
            scores = {u: _dot(q_s[rows[u + (0,)], lanes], kt_s[lanes, rows[u + (0,)]].astype(bf16)) for u in units}
            kv = {}
            for j, r in units:
                for d in range(2):
                    rw = rows[(j, r, d)]
                    keys = (kt_s[lanes, rw] * dec_scr[3, 8 * d:8 * d + 1, :]).astype(bf16)
                    kv[(j, r, d)] = _dot(keys, v_s[rw, lanes])
            state = {(j, d): s_scr[j, d] for j in range(n_seq) for d in range(2)}
            entering, carried = {}, {}
            for r in range(per_iter):
                for j in range(n_seq):
                    entering[(j, r)] = state[(j, 0)].astype(bf16)
                    carried[(j, r)] = _dot(q_s[rows[(j, r, 1)], lanes], state[(j, 1)].astype(bf16))
                    for d in range(2):
                        state[(j, d)] = dec_scr[3, 16 + 8 * d:17 + 8 * d, :] * state[(j, d)] + kv[(j, r, d)]
            seen_f = dec_scr[0].astype(bf16)
            for j, r in units:
                rw = rows[(j, r, 0)]
                lhs = jnp.concatenate([q_s[rw, lanes] * seen_f, (scores[(j, r)] * dec_scr[2]).astype(bf16)], axis=1)
                o_scr[0, h, rw, :] = _dot(lhs, jnp.concatenate([entering[(j, r)], v_s[rw, lanes]], axis=0))
                o_scr[1, h, rows[(j, r, 1)], :] = dec_scr[1] * carried[(j, r)]
            for (j, d), s in state.items():
                s_scr[j, d] = s
            return c
        lax.fori_loop(0, n_chunks // per_iter, body, 0)
        if emit_state:
            for j in range(n_seq):
                for d in range(2):
                    st_ref[j, d, hh] = s_scr[j, d]

    for pp in range(N_RET // PAIR):
        pair_body(pp)


def _const_spec(shape):
    zeros = (0,) * len(shape)
    return pl.BlockSpec(shape, lambda i: zeros, pipeline_mode=pl.Buffered(1))


def _step_spec(shape, per_step, buffers=None):
    zeros = (0,) * (len(shape) - 1)
    mode = {} if buffers is None else {"pipeline_mode": pl.Buffered(buffers)}
    return pl.BlockSpec((per_step,) + tuple(shape[1:]), lambda i: (i,) + zeros, **mode)


def _mod_spec(d, per_sequence):
    if per_sequence:
        return pl.BlockSpec((1, 3, d), lambda i: (i + 1, 0, 0))
    return pl.BlockSpec((1, 3, d), lambda i: (0, 0, 0))


def _layer_call(body, x, mod, consts, states, state_shapes, per_sequence_mod, scratch, name, smem_inputs=(),
                state_buffers=None):
    n_seq, seq, d = x.shape
    per_step = STEP_TOKENS // seq
    assert per_step * seq == STEP_TOKENS and n_seq % per_step == 0 and seq % CHUNK == 0
    assert not per_sequence_mod or per_step == 1
    n_steps = n_seq // per_step
    xs = x.reshape(n_steps, STEP_TOKENS, d)
    inputs = list(smem_inputs) + [xs, mod] + list(consts) + list(states)
    in_specs = [pl.BlockSpec(memory_space=pltpu.SMEM)] * len(smem_inputs)
    in_specs += [_step_spec(xs.shape, 1), _mod_spec(d, per_sequence_mod)]
    in_specs += [_const_spec(a.shape) for a in consts]
    in_specs += [_step_spec(s.shape, per_step) for s in states]
    out_shape = [jax.ShapeDtypeStruct(xs.shape, f32)] + [jax.ShapeDtypeStruct(s, f32) for s in state_shapes]
    out_specs = [_step_spec(xs.shape, 1)] + [_step_spec(s, per_step, buffers=state_buffers) for s in state_shapes]
    outs = pl.pallas_call(
        body,
        grid=(n_steps,),
        in_specs=in_specs,
        out_specs=out_specs,
        out_shape=out_shape,
        scratch_shapes=scratch,
        compiler_params=pltpu.CompilerParams(dimension_semantics=("arbitrary",), vmem_limit_bytes=VMEM_LIMIT_BYTES),
        name=name,
    )(*inputs)
    return [outs[0].reshape(x.shape)] + list(outs[1:])


def _even_layer(x, mod, norm_w, w, final_w, tables, states, emit_state, per_sequence_mod, final):
    n_seq, seq, d = x.shape
    consts = [norm_w, w["win"], w["wlow"], w["wout"], w["gkw"], w["gkb"], w["lb"], w["gnw"],
              final_w] + list(tables)
    state_shapes = [(n_seq, 2, N_HGRN, HEAD, HEAD), (n_seq, 2, N_GLA, GLA_DK, HEAD)] if emit_state else []
    scan_heads = max(PAIR, N_GLA)
    scratch = [
        pltpu.VMEM((STEP_TOKENS, d), bf16),
        pltpu.VMEM((2, scan_heads, STEP_TOKENS, HEAD), bf16),
        pltpu.VMEM((STEP_TOKENS, scan_heads * HEAD), bf16),
        pltpu.VMEM((STEP_TOKENS, PAIR * HEAD), f32),
        pltpu.VMEM((STEP_TOKENS, scan_heads * HEAD), bf16),
        pltpu.VMEM((STEP_TOKENS, PAIR * HEAD), f32),
        pltpu.VMEM((STEP_TOKENS, PAIR * HEAD), f32),
        pltpu.VMEM((STEP_TOKENS, 2 * PAIR * HEAD), bf16),
        pltpu.VMEM((STEP_TOKENS, 2 * PAIR * HEAD), bf16),
        pltpu.VMEM((STEP_TOKENS // seq, 2, scan_heads, HEAD, HEAD), f32),
        pltpu.VMEM((2, GATED_UNITS_MAX, TABLE_ROWS_MAX, PAIR * HEAD), f32),
        pltpu.SMEM((1,), jnp.int32),
    ]
    body = functools.partial(_even_kernel, seq=seq, has_state=states is not None, emit_state=emit_state, final=final)
    return _layer_call(body, x, mod, consts, states or (), state_shapes, per_sequence_mod, scratch,
                       "even_layer_seq%d" % seq)


def _odd_layer(x, mod, norm_w, w, final_w, log_decay, rope, state, emit_state, per_sequence_mod, final):
    n_seq, seq, d = x.shape
    consts = [norm_w, w["wr"], w["wout"], w["gnw"], final_w] + list(rope or ())
    state_shapes = [(n_seq, 2, N_RET, HEAD, HEAD)] if emit_state else []
    scratch = [
        pltpu.VMEM((STEP_TOKENS, d), bf16),
        pltpu.VMEM((2, PAIR, STEP_TOKENS, HEAD), f32),
        pltpu.VMEM((STEP_TOKENS, PAIR * HEAD), bf16),
        pltpu.VMEM((STEP_TOKENS, PAIR * HEAD), bf16),
        pltpu.VMEM((PAIR * HEAD, STEP_TOKENS), f32),
        pltpu.VMEM((STEP_TOKENS, PAIR * HEAD), bf16),
        pltpu.VMEM((STEP_TOKENS // seq, 2, HEAD, HEAD), f32),
        pltpu.VMEM((4, CHUNK, HEAD), f32),
    ]
    body = functools.partial(_odd_kernel, seq=seq, has_state=state is not None, emit_state=emit_state,
                             use_rope=rope is not None, final=final)
    return _layer_call(body, x, mod, consts, () if state is None else (state,), state_shapes, per_sequence_mod,
                       scratch, "odd_layer_seq%d" % seq, smem_inputs=(log_decay,))


def _even_weights(w_in, gk_w, gk_b, lb, gn_w, w_out):
    c0 = GLA_LOW_COLS[0]
    wlow = jnp.pad(w_in[:, c0:c0 + 2 * GLA_RANK], ((0, 0), (0, HEAD - 2 * GLA_RANK)))
    gkw = jnp.stack([jnp.pad(gk_w[d], ((d * GLA_RANK, HEAD - (d + 1) * GLA_RANK), (0, 0))) for d in range(2)])
    return {"win": w_in.astype(bf16), "wlow": wlow.astype(bf16), "gkw": gkw.astype(bf16),
            "gkb": gk_b.reshape(2, 1, -1), "lb": lb.reshape(1, -1), "gnw": gn_w.reshape(1, -1),
            "wout": w_out.astype(bf16)}


def _odd_weights(w_in, gn_w, w_out):
    return {"wr": w_in.astype(bf16), "gnw": gn_w.reshape(1, -1), "wout": w_out.astype(bf16)}


def _rope_tables(seq):
    rows = seq // GRID_W
    t_row = np.repeat(np.arange(rows), GRID_W).astype(np.float32)
    t_col = np.tile(np.arange(GRID_W), rows).astype(np.float32)
    half = HEAD // 2
    inv = (ROPE_BASE ** (-np.arange(0, half, 2, dtype=np.float32) / half)).astype(np.float32)
    ang_r = t_row[:, None] * inv
    ang_c = t_col[:, None] * inv
    ang = np.concatenate([ang_r, ang_r, ang_c, ang_c], axis=-1).astype(np.float32)
    sign = np.where((np.arange(HEAD) // (HEAD // 4)) % 2 == 0, -1.0, 1.0).astype(np.float32)
    return jnp.asarray(np.cos(ang), f32), jnp.asarray(np.sin(ang) * sign, f32)


def kernel(x_prompt, x_sample, state_hgrn, state_gla, state_ret, c, c_ctx, norm_w, ada_w, ada_b, w_in_even, hgrn_lb, gla_gk_w, gla_gk_b, gn_even, w_out_even, w_in_odd, ret_decay, gn_odd, w_out_odd, final_norm_w):
    depth, d = norm_w.shape
    n_lat = x_sample.shape[0]
    n_cond = -(-(1 + n_lat) // 8) * 8
    cond = jnp.zeros((n_cond, d), f32).at[0].set(c_ctx).at[1:1 + n_lat].set(c)
    mod = _modulation(cond, ada_w, ada_b).reshape(depth, n_cond, 3, d)
    lbs = jnp.cumsum(jax.nn.softmax(hgrn_lb.astype(f32), axis=0), axis=0)
    final_w = final_norm_w.reshape(1, d)
    rope = _rope_tables(x_sample.shape[1])
    tables = sum((_scan_tables(plan) for plan in ALL_PLANS), ())

    x_c, x_l = x_prompt, x_sample
    new_hgrn, new_gla, new_ret = [], [], []
    for l in range(depth):
        i = l // 2
        final = l == depth - 1
        nw = norm_w[l].reshape(1, d)
        if l % 2 == 0:
            w = _even_weights(w_in_even[i], gla_gk_w[i], gla_gk_b[i], lbs[i], gn_even[i], w_out_even[i])
            x_c, st_a, st_b = _even_layer(x_c, mod[l], nw, w, final_w, tables, None, True, False, final)
            (x_l,) = _even_layer(x_l, mod[l], nw, w, final_w, tables, (state_hgrn[:, i], state_gla[:, i]),
                                 False, True, final)
            new_hgrn.append(st_a)
            new_gla.append(st_b)
        else:
            w = _odd_weights(w_in_odd[i], gn_odd[i], w_out_odd[i])
            log_decay = jax.nn.log_sigmoid(ret_decay[i].astype(f32))
            x_c, st_c = _odd_layer(x_c, mod[l], nw, w, final_w, log_decay, None, None, True, False, final)
            (x_l,) = _odd_layer(x_l, mod[l], nw, w, final_w, log_decay, rope, state_ret[:, i], False, True, final)
            new_ret.append(st_c)
    def stacked(states):
        return states[0][:, None] if len(states) == 1 else jnp.stack(states, axis=1)
    return (x_c, x_l, stacked(new_hgrn), stacked(new_gla), stacked(new_ret))
```

```python
import functools
from typing import NamedTuple

import numpy as np
import jax
import jax.numpy as jnp
from jax import lax
from jax.experimental import pallas as pl
from jax.experimental.pallas import tpu as pltpu

f32 = jnp.float32
bf16 = jnp.bfloat16

EPS = 1e-6
LOG2E = 1.4426950408889634
HEAD = 128
N_HGRN = 4
N_GLA = 4
GLA_DK = 64
N_RET = 8
N_HEADS = 8
GLA_RANK = 16
GLA_GATE_NORM = 16.0
GRID_W = 64
ROPE_BASE = 10000.0

EVEN_SPLITS = (N_HGRN * HEAD,) * 5 + (N_GLA * GLA_DK,) * 2 + (N_GLA * HEAD,) * 2 + (GLA_RANK,) * 2
EVEN_STARTS = tuple(int(c) for c in np.cumsum((0,) + EVEN_SPLITS[:-1]))
HGRN_COLS = EVEN_STARTS[0:5]
GLA_QK_COLS = EVEN_STARTS[5:7]
GLA_COLS = EVEN_STARTS[7:9]
GLA_LOW_COLS = EVEN_STARTS[9:11]
RET_COLS = tuple(i * N_RET * HEAD for i in range(4))

CHUNK = 128
SUBLANES = 8


class _ScanPlan(NamedTuple):
    block: int
    table_levels: tuple
    direct_levels: tuple

    @property
    def block_rows(self):
        return 1 if 0 < self.block < CHUNK else 0

    @property
    def table_blocks(self):
        return self.block_rows + len(self.table_levels)

    @property
    def row_total(self):
        return (self.table_blocks + 1) * CHUNK

    @property
    def diag_id(self):
        return len(self.table_levels) + len(self.direct_levels)


ROBUST_PLAN = _ScanPlan(0, (1, 2, 4), (8, 16, 32, 64))
BLOCK_PLAN = _ScanPlan(64, (), (64,))
CHUNK_PLAN = _ScanPlan(CHUNK, (), ())
HGRN_PLANS = (BLOCK_PLAN, ROBUST_PLAN)
GLA_PLANS = (CHUNK_PLAN, ROBUST_PLAN)
ALL_PLANS = (ROBUST_PLAN, BLOCK_PLAN, CHUNK_PLAN)
HGRN_DECAY_LIMIT = 104.0
GLA_DECAY_LIMIT = 90.0
TABLE_ROWS_MAX = max(p.row_total for p in ALL_PLANS) + SUBLANES
PAIR = 2
GATED_HEADS_PER_ITER = 8
GATED_UNITS_MAX = 4
GLA_HEADS_PER_GROUP = HEAD // GLA_DK
RET_UNITS_PER_ITER = 8
STEP_TOKENS = 1024
ROW_TILE = 512
MOD_COLS = 768
VMEM_LIMIT_BYTES = 61 * 1024 * 1024


def _dot(a, b):
    return jnp.dot(a, b, preferred_element_type=f32)


def _dot_tn(a, b):
    return lax.dot_general(a, b, (((0,), (0,)), ((), ())), preferred_element_type=f32)


def _silu(x):
    return x * jax.nn.sigmoid(x)


def _row_tile(i):
    return pl.ds(pl.multiple_of(i * ROW_TILE, ROW_TILE), ROW_TILE)


def _chunk_rows(seq_start, c):
    return pl.ds(pl.multiple_of(seq_start + c * CHUNK, CHUNK), CHUNK)


def _mod_kernel(cond_ref, w_ref, b_ref, o_ref):
    o_ref[0] = _dot(_silu(cond_ref[...]).astype(bf16), w_ref[0].astype(bf16)) + b_ref[0]


def _modulation(cond, ada_w, ada_b):
    depth, d, d3 = ada_w.shape
    rows = cond.shape[0]
    return pl.pallas_call(
        _mod_kernel,
        grid=(depth, d3 // MOD_COLS),
        in_specs=[
            pl.BlockSpec((rows, d), lambda l, j: (0, 0)),
            pl.BlockSpec((1, d, MOD_COLS), lambda l, j: (l, 0, j)),
            pl.BlockSpec((1, 1, MOD_COLS), lambda l, j: (l, 0, j)),
        ],
        out_specs=pl.BlockSpec((1, rows, MOD_COLS), lambda l, j: (l, 0, j)),
        out_shape=jax.ShapeDtypeStruct((depth, rows, d3), f32),
        compiler_params=pltpu.CompilerParams(dimension_semantics=("arbitrary", "arbitrary")),
        name="modulation",
    )(cond, ada_w, ada_b.reshape(depth, 1, d3))


def _modulated_norm(x_ref, mod_ref, nw_ref, h_scr):
    def body(i, carry):
        rows = _row_tile(i)
        x = x_ref[0, rows, :]
        y = x * lax.rsqrt(jnp.mean(x * x, axis=-1, keepdims=True) + EPS) * nw_ref[...]
        h_scr[rows, :] = (y * (1.0 + mod_ref[0, 1:2, :]) + mod_ref[0, 0:1, :]).astype(bf16)
        return carry
    lax.fori_loop(0, STEP_TOKENS // ROW_TILE, body, 0)


def _aligned_slice(start, width):
    return pl.ds(start if isinstance(start, int) else pl.multiple_of(start, width), width)


def _pair_lanes(pair):
    return _aligned_slice(pair * (PAIR * HEAD), PAIR * HEAD)


def _project(h, w_ref, pair, first_cols):
    width = PAIR * HEAD
    return [_dot(h, w_ref[:, _aligned_slice(c0 + pair * width, width)]) for c0 in first_cols]


def _add_heads_output(first_head, n_heads, gate_s, o_scr, gnw_ref, wo_ref, out_ref, first, finish):
    width = n_heads * HEAD
    lanes = _aligned_slice(first_head * HEAD, width)
    tiles = []
    for i in range(STEP_TOKENS // ROW_TILE):
        rows = pl.ds(i * ROW_TILE, ROW_TILE)
        parts = []
        for h in range(n_heads):
            o = o_scr[0, h, rows, :].astype(f32) + o_scr[1, h, rows, :].astype(f32)
            parts.append(o * lax.rsqrt(jnp.mean(o * o, axis=-1, keepdims=True) + EPS))
        y = jnp.concatenate(parts, axis=-1) * gnw_ref[:, lanes]
        tiles.append((rows, (y * gate_s[rows, 0:width].astype(f32)).astype(bf16)))
    products = [(rows, _dot(z, wo_ref[lanes, :])) for rows, z in tiles]
    for rows, product in products:
        mixed = product if first else out_ref[0, rows, :] + product
        if finish is not None:
            x_ref, mod_ref, fnw_ref, final = finish
            mixed = x_ref[0, rows, :] + mod_ref[0, 2:3, :] * mixed
            if final:
                mixed = mixed * lax.rsqrt(jnp.mean(mixed * mixed, axis=-1, keepdims=True) + EPS) * fnw_ref[...]
        out_ref[0, rows, :] = mixed


def _scan_tables(plan):
    t = np.arange(CHUNK)[:, None]
    j = np.arange(CHUNK)[None, :]
    fwd = []
    if plan.block_rows:
        fwd.append((j <= t) & (j // plan.block == t // plan.block))
    for m in plan.table_levels:
        mid = (t // (2 * m)) * (2 * m) + m
        right = t >= mid
        fwd.append(np.where(right, (j >= mid) & (j <= t), (j > t) & (j < mid)))
    fwd.append(j <= t)
    fwd.append(np.ones((SUBLANES, CHUNK), bool))
    fwd = np.concatenate(fwd, axis=0).astype(np.float32)
    bwd = fwd.copy()
    n_sym = plan.row_total // CHUNK
    bwd[:plan.row_total] = fwd[:plan.row_total].reshape(n_sym, CHUNK, CHUNK)[:, ::-1, ::-1].reshape(-1, CHUNK)
    table = np.stack([np.tile(fwd, (1, 2)), np.tile(bwd, (1, 2))])
    first = 1 if plan.block else 0
    ids = np.full((CHUNK, CHUNK), -1, np.int32)
    for i, m in enumerate(plan.table_levels + plan.direct_levels):
        ids = np.where((t > j) & ((t ^ j) >= m) & ((t ^ j) < 2 * m), first + i, ids)
    if plan.block:
        ids = np.where((t >= j) & (t // plan.block == j // plan.block), 0, ids)
    else:
        ids = np.where(t == j, plan.diag_id, ids)
    return jnp.asarray(table, bf16), jnp.asarray(np.stack([ids, ids.T]).astype(np.int32))


def _in_chunk_scores(q, k, cum, tabled, lvl, ones, rev, key_masks, plan):
    qb = q.astype(bf16)
    q_heads = [qb if mask is None else qb * mask for mask in key_masks]
    kt = k.astype(bf16).T
    if plan.block:
        inside = tabled[0] if plan.block_rows else cum
        grow, decay = jnp.exp2(-inside).astype(bf16), jnp.exp2(inside).astype(bf16)
        keys = kt * grow.T
        scores = [jnp.where(lvl == 0, _dot(qh * decay, keys), 0.0) for qh in q_heads]
        tabled = tabled[plan.block_rows:]
    else:
        diag_keys = kt * ones
        scores = [jnp.where(lvl == plan.diag_id, _dot(qh, diag_keys), 0.0) for qh in q_heads]
    first = 1 if plan.block else 0
    for i in range(len(plan.table_levels)):
        e = jnp.exp2(tabled[i]).astype(bf16)
        keys = kt * e.T
        scores = [jnp.where(lvl == first + i, _dot(qh * e, keys), sc) for qh, sc in zip(q_heads, scores)]
    for i, m in enumerate(plan.direct_levels, start=first + len(plan.table_levels)):
        blocks = []
        for p0 in range(0, CHUNK, 2 * m):
            left, right = slice(p0, p0 + m), slice(p0 + m, p0 + 2 * m)
            q_side, k_side = (left, right) if rev else (right, left)
            mid_row = p0 + m if rev else p0 + m - 1
            blocks.append((q_side, k_side, cum[mid_row:mid_row + 1, :]))
        q_decay = [jnp.exp2(cum[qs] - mid).astype(bf16) for qs, _, mid in blocks]
        k_decay = []
        for _, ks, mid in blocks:
            decay = jnp.exp2(mid - cum[ks]).astype(bf16)
            zero = jnp.zeros((m, HEAD), bf16)
            k_decay += [zero, decay] if rev else [decay, zero]
        keys = kt * jnp.concatenate(k_decay, axis=0).T
        for h, qh in enumerate(q_heads):
            s = _dot(jnp.concatenate([qh[qs] * e for (qs, _, _), e in zip(blocks, q_decay)], axis=0), keys)
            rows = []
            for b, (qs, ks, _) in enumerate(blocks):
                if len(blocks) == 1:
                    updated = scores[h][qs, :] + s
                else:
                    updated = jnp.where(lvl[qs, :] == i, s[b * m:(b + 1) * m, :], scores[h][qs, :])
                rows += [updated, scores[h][ks, :]] if rev else [scores[h][ks, :], updated]
            scores[h] = jnp.concatenate(rows, axis=0)
    return [sc.astype(bf16) for sc in scores]


def _store_log2_split(g_ref, rows, g, block):
    width = g.shape[-1]
    x = g * LOG2E
    hi = x.astype(bf16)
    g_ref[rows, 0:width] = hi
    g_ref[rows, width:2 * width] = (x - hi.astype(f32)).astype(bf16)
    return jnp.min(jnp.sum(x.reshape(-1, block, width), axis=1))


def _decay_sums(g_split, table):
    width = g_split.shape[-1] // 2
    return _dot(table, jnp.concatenate([g_split[:, :width], g_split[:, width:]], axis=0))


def _key_masks(heads_per_group):
    if heads_per_group == 1:
        return [None]
    lane = lax.broadcasted_iota(jnp.int32, (1, HEAD), 1)
    return [(lane // (HEAD // heads_per_group) == sub).astype(bf16) for sub in range(heads_per_group)]


def _gated_chunks(chains, sums_ref, upcoming, upcoming_ref, heads_per_group, plan):
    masks = _key_masks(heads_per_group)
    groups = [slice(g * HEAD, (g + 1) * HEAD) for g in range(PAIR)]
    n_heads = PAIR * heads_per_group
    group_of = [h // heads_per_group for h in range(n_heads)]
    mask_of = [masks[h % heads_per_group] for h in range(n_heads)]
    flat = [(c, r) for c, (_, units) in enumerate(chains) for r in range(len(units))]
    unit = {(c, r): chains[c][1][r] for c, r in flat}
    index = {key: u for u, key in enumerate(flat)}
    cum_rows = slice(plan.table_blocks * CHUNK, (plan.table_blocks + 1) * CHUNK)
    cum = {key: sums_ref[index[key], cum_rows, :] for key in flat}
    total = {key: sums_ref[index[key], plan.row_total:plan.row_total + 1, :] for key in flat}
    vb = {key: [unit[key][2][:, h * HEAD:(h + 1) * HEAD] for h in range(n_heads)] for key in flat}
    upcoming = list(enumerate(upcoming))
    kv, whole = {}, {}
    for key in flat:
        k = unit[key][1]
        keys = (k * jnp.exp2(total[key] - cum[key])).astype(bf16)
        kv[key] = []
        for h in range(n_heads):
            own = keys[:, groups[group_of[h]]]
            kv[key].append(_dot_tn(own if mask_of[h] is None else own * mask_of[h], vb[key][h]))
        whole[key] = [jnp.broadcast_to(jnp.exp2(total[key][:, lanes]), (HEAD, HEAD)).T for lanes in groups]
    state = [list(states) for states, _ in chains]
    outs, pending, carried = {key: [None] * n_heads for key in flat}, None, {}

    def finish(done):
        pkey, ph, psc = done
        queries, entering = carried[pkey]
        lhs = jnp.concatenate([queries[:, groups[group_of[ph]]], psc], axis=1)
        outs[pkey][ph] = _dot(lhs, jnp.concatenate([entering[ph], vb[pkey][ph]], axis=0))

    for r in range(max(len(units) for _, units in chains)):
        live = [key for key in flat if key[1] == r]
        for key in live:
            queries = (unit[key][0] * jnp.exp2(cum[key])).astype(bf16)
            carried[key] = (queries, [s.astype(bf16) for s in state[key[0]]])
            state[key[0]] = [whole[key][group_of[h]] * state[key[0]][h] + kv[key][h] for h in range(n_heads)]
        for key in live:
            q, k, _, lvl, rev = unit[key]
            ones = (lvl[0:1, :] >= -1).astype(bf16)
            for g, lanes in enumerate(groups):
                tabled = [sums_ref[index[key], i * CHUNK:(i + 1) * CHUNK, lanes] for i in range(plan.table_blocks)]
                group_scores = _in_chunk_scores(q[:, lanes], k[:, lanes], cum[key][:, lanes], tabled, lvl, ones, rev,
                                                masks, plan)
                for sub, sc in enumerate(group_scores):
                    if pending is not None:
                        finish(pending)
                    pending = (key, g * heads_per_group + sub, sc)
            if upcoming:
                u, (g_split, table) = upcoming.pop(0)
                upcoming_ref[u, 0:plan.row_total + SUBLANES, :] = _decay_sums(g_split, table)
    for u, (g_split, table) in upcoming:
        upcoming_ref[u, 0:plan.row_total + SUBLANES, :] = _decay_sums(g_split, table)
    finish(pending)
    return [[outs[(c, r)] for r in range(len(units))] for c, (_, units) in enumerate(chains)], state


def _gated_scans(q_s, k_refs, v_s, g_refs, s_scr, sums_scr, table_ref, lvl_ref, o_scr, seq, heads_per_group, plan):
    n = seq // CHUNK
    n_seq = STEP_TOKENS // seq
    n_heads = PAIR * heads_per_group
    units = GATED_HEADS_PER_ITER // n_heads
    per_iter = min(n, units // 2)
    seqs_per_iter = min(n_seq, units // (2 * per_iter))
    assert n % per_iter == 0 and n_seq % seqs_per_iter == 0
    iters_per_seq = n // per_iter
    n_iters = (n_seq // seqs_per_iter) * iters_per_seq

    def layout(it):
        jj, i = it // iters_per_seq, it % iters_per_seq
        chains = []
        for js in range(seqs_per_iter):
            j = jj * seqs_per_iter + js
            for d in range(2):
                steps = [i * per_iter + r for r in range(per_iter)]
                chains.append((j, d, [_chunk_rows(j * seq, n - 1 - t if d else t) for t in steps]))
        return chains

    def sums_inputs(it):
        return [(g_refs[d][rw, :], table_ref[d]) for _, d, rws in layout(it) for rw in rws]

    for u, (g_split, table) in enumerate(sums_inputs(0)):
        sums_scr[0, u, 0:plan.row_total + SUBLANES, :] = _decay_sums(g_split, table)

    def iteration(it, slot):
        chains = layout(it)
        args = [([s_scr[j, d, h] for h in range(n_heads)],
                 [(q_s[rw, :], k_refs[d][rw, :], v_s[rw, :], lvl_ref[d], bool(d)) for rw in rws])
                for j, d, rws in chains]
        upcoming = sums_inputs(jnp.minimum(it + 1, n_iters - 1))
        outs, new_states = _gated_chunks(args, sums_scr.at[slot], upcoming, sums_scr.at[1 - slot], heads_per_group,
                                         plan)
        for c, (j, d, rws) in enumerate(chains):
            for h in range(n_heads):
                for r, rw in enumerate(rws):
                    o_scr[d, h, rw, :] = outs[c][r][h].astype(o_scr.dtype)
                s_scr[j, d, h] = new_states[c][h]

    assert n_iters % 2 == 0

    def body(it2, carry):
        iteration(2 * it2, 0)
        iteration(2 * it2 + 1, 1)
        return carry
    lax.fori_loop(0, n_iters // 2, body, 0)


def _even_kernel(*refs, seq, has_state, emit_state, final):
    it = iter(refs)
    x_ref, mod_ref, nw_ref = next(it), next(it), next(it)
    win_ref, wlow_ref, wo_ref = next(it), next(it), next(it)
    gkw_ref, gkb_ref, lb_ref, gnw_ref, fnw_ref = next(it), next(it), next(it), next(it), next(it)
    plan_tables = {plan: (next(it), next(it)) for plan in ALL_PLANS}
    s0a_ref, s0b_ref = (next(it), next(it)) if has_state else (None, None)
    out_ref = next(it)
    sta_ref, stb_ref = (next(it), next(it)) if emit_state else (None, None)
    h_scr, o_scr, gate_s, q_s, v_s, kf_s, kb_s, gf_s, gb_s, s_scr, sums_scr, bound_s = it
    n_seq = STEP_TOKENS // seq
    n_tiles = STEP_TOKENS // ROW_TILE

    _modulated_norm(x_ref, mod_ref, nw_ref, h_scr)

    width = PAIR * HEAD

    def run_scans(first_head, s0_ref, st_ref, h0, heads_per_group, k_refs, plans, block_decays, first=False,
                  last=False):
        bounded, robust, limit = plans
        bound_s[0] = (functools.reduce(jnp.minimum, block_decays) >= -limit).astype(jnp.int32)
        n_heads = PAIR * heads_per_group
        key_rows = HEAD // heads_per_group
        own_rows = [pl.ds((h % heads_per_group) * key_rows, key_rows) for h in range(n_heads)]
        for j in range(n_seq):
            for d in range(2):
                for h in range(n_heads):
                    if s0_ref is None or heads_per_group > 1:
                        s_scr[j, d, h] = jnp.zeros((HEAD, HEAD), f32)
                    if s0_ref is not None:
                        s_scr[j, d, h, own_rows[h], :] = s0_ref[j, d, h0 + h]
        def scans(plan):
            table_ref, lvl_ref = plan_tables[plan]
            _gated_scans(q_s, k_refs, v_s, (gf_s, gb_s), s_scr, sums_scr, table_ref, lvl_ref, o_scr, seq,
                         heads_per_group, plan)
        lax.cond(bound_s[0] == 1, functools.partial(scans, bounded), functools.partial(scans, robust))
        if st_ref is not None:
            for j in range(n_seq):
                for d in range(2):
                    for h in range(n_heads):
                        st_ref[j, d, h0 + h] = s_scr[j, d, h, own_rows[h], :]
        _add_heads_output(first_head, n_heads, gate_s, o_scr, gnw_ref, wo_ref, out_ref, first,
                          (x_ref, mod_ref, fnw_ref, final) if last else None)

    def hgrn_pair(pp, carry):
        lb = lb_ref[:, _pair_lanes(pp)]
        log_lb = jnp.log(lb)

        tiles = [pl.ds(i * ROW_TILE, ROW_TILE) for i in range(n_tiles)]
        projected = [_project(h_scr[rows, :], win_ref, pp, HGRN_COLS) for rows in tiles]
        block_decays = []
        for rows, (query, value, forget_f, forget_b, gate) in zip(tiles, projected):
            gate_s[rows, 0:width] = _silu(gate).astype(bf16)
            q_s[rows, :] = _silu(query)
            v_s[rows, 0:width] = value.astype(bf16)
            for a, k_s, g_s in ((forget_f, kf_s, gf_s), (forget_b, kb_s, gb_s)):
                z = log_lb - a
                u = jnp.exp(-jnp.abs(a))
                w = jnp.exp(-jnp.abs(z))
                r = 1.0 / (1.0 + u)
                log_f = jnp.maximum(z, 0.0) + jnp.minimum(a, 0.0) + jnp.log((1.0 + w) * r)
                block_decays.append(_store_log2_split(g_s, rows, log_f, HGRN_PLANS[0].block))
                k_s[rows, :] = (1.0 - lb) * jnp.where(a >= 0.0, u * r, r)
        run_scans(PAIR * pp, s0a_ref, sta_ref, PAIR * pp, 1, (kf_s, kb_s), HGRN_PLANS + (HGRN_DECAY_LIMIT,), block_decays)
        return carry

    def clear(i, carry):
        out_ref[0, _row_tile(i), :] = jnp.zeros((ROW_TILE, out_ref.shape[-1]), f32)
        return carry
    lax.fori_loop(0, n_tiles, clear, 0)
    lax.fori_loop(0, N_HGRN // PAIR, hgrn_pair, 0)

    assert N_GLA == PAIR * GLA_HEADS_PER_GROUP

    def gla_project(rows):
        h = h_scr[rows, :]
        narrow = [_dot(h, win_ref[:, c0:c0 + N_GLA * GLA_DK]) for c0 in GLA_QK_COLS]
        wide = [_dot(h, win_ref[:, c0:c0 + N_GLA * HEAD]) for c0 in GLA_COLS]
        return narrow + wide + [_dot(h, wlow_ref[...])]

    tiles = [pl.ds(i * ROW_TILE, ROW_TILE) for i in range(n_tiles)]
    projected = [gla_project(rows) for rows in tiles]
    block_decays = []
    for rows, (query, key, value, gate, low) in zip(tiles, projected):
        gate_s[rows, :] = _silu(gate).astype(bf16)
        q_s[rows, :] = query * (GLA_DK ** -0.5)
        kf_s[rows, :] = key
        v_s[rows, :] = value.astype(bf16)
        low = low.astype(bf16)
        for d, g_s in enumerate((gf_s, gb_s)):
            logits = _dot(low, gkw_ref[d]) + gkb_ref[d]
            log_gate = jnp.minimum(logits, 0.0) - jnp.log(1.0 + jnp.exp(-jnp.abs(logits)))
            block_decays.append(_store_log2_split(g_s, rows, log_gate * (1.0 / GLA_GATE_NORM), GLA_PLANS[0].block))
    run_scans(N_HGRN, s0b_ref, stb_ref, 0, GLA_HEADS_PER_GROUP, (kf_s, kf_s), GLA_PLANS + (GLA_DECAY_LIMIT,), block_decays, last=True)


def _odd_kernel(*refs, seq, has_state, emit_state, use_rope, final):
    it = iter(refs)
    lg_ref = next(it)
    x_ref, mod_ref, nw_ref = next(it), next(it), next(it)
    wr_ref, wo_ref, gnw_ref, fnw_ref = next(it), next(it), next(it), next(it)
    cos_ref, sin_ref = (next(it), next(it)) if use_rope else (None, None)
    s0_ref = next(it) if has_state else None
    out_ref = next(it)
    st_ref = next(it) if emit_state else None
    h_scr, o_scr, gate_s, q_s, kt_s, v_s, s_scr, dec_scr = it
    n_seq = STEP_TOKENS // seq
    n_tiles = STEP_TOKENS // ROW_TILE
    n_chunks = seq // CHUNK
    per_iter = min(n_chunks, RET_UNITS_PER_ITER // n_seq)
    assert n_chunks % per_iter == 0

    _modulated_norm(x_ref, mod_ref, nw_ref, h_scr)
    t_idx = lax.broadcasted_iota(jnp.int32, (CHUNK, CHUNK), 0)
    s_idx = lax.broadcasted_iota(jnp.int32, (CHUNK, CHUNK), 1)
    row_f = lax.broadcasted_iota(jnp.int32, (CHUNK, HEAD), 0).astype(f32)
    col_f = lax.broadcasted_iota(jnp.int32, (SUBLANES, CHUNK), 1).astype(f32)
    chunk_len = jnp.full((SUBLANES, HEAD), CHUNK, f32)

    def lane_rows(block):
        return slice(block * SUBLANES, (block + 1) * SUBLANES)
    if use_rope:
        lane = lax.broadcasted_iota(jnp.int32, (ROW_TILE, HEAD), 1)
        first_quarter = (lane // (HEAD // 4)) % 2 == 0

    def rope(x, cos, sin_signed):
        xr = jnp.where(first_quarter, pltpu.roll(x, HEAD - HEAD // 4, axis=1), pltpu.roll(x, HEAD // 4, axis=1))
        return x * cos + xr * sin_signed

    width = PAIR * HEAD

    def pair_body(pp):
        tiles = [pl.ds(i * ROW_TILE, ROW_TILE) for i in range(n_tiles)]
        projected = [_project(h_scr[rows, :], wr_ref, pp, RET_COLS) for rows in tiles]
        for rows, (q, k, value, gate) in zip(tiles, projected):
            gate_s[rows, :] = _silu(gate).astype(bf16)
            k = k * (HEAD ** -0.5)
            if use_rope:
                cos, sin_signed = cos_ref[rows, :], sin_ref[rows, :]
                heads = [slice(h * HEAD, (h + 1) * HEAD) for h in range(PAIR)]
                q = jnp.concatenate([rope(q[:, lanes], cos, sin_signed) for lanes in heads], axis=-1)
                k = jnp.concatenate([rope(k[:, lanes], cos, sin_signed) for lanes in heads], axis=-1)
            q_s[rows, :] = q.astype(bf16)
            kt_s[:, rows] = k.T
            v_s[rows, :] = value.astype(bf16)
        for h in range(PAIR):
            scan_head(PAIR * pp + h, h)
        last = pp == N_RET // PAIR - 1
        _add_heads_output(PAIR * pp, PAIR, gate_s, o_scr, gnw_ref, wo_ref, out_ref, pp == 0,
                          (x_ref, mod_ref, fnw_ref, final) if last else None)

    def scan_head(hh, h):
        lanes = slice(h * HEAD, (h + 1) * HEAD)
        lg_f = lg_ref[0, hh]
        lg_b = lg_ref[1, hh]
        for j in range(n_seq):
            for d in range(2):
                s_scr[j, d] = s0_ref[j, d, hh] if has_state else jnp.zeros((HEAD, HEAD), f32)

        dist = (t_idx - s_idx).astype(f32)
        dec_scr[0] = jnp.exp(lg_f * (row_f + 1.0))
        dec_scr[1] = jnp.exp(lg_b * (CHUNK - row_f))
        dec_scr[2] = (jnp.where(t_idx >= s_idx, jnp.exp(lg_f * jnp.maximum(dist, 0.0)), 0.0)
                      + jnp.where(s_idx >= t_idx, jnp.exp(lg_b * jnp.maximum(-dist, 0.0)), 0.0))
        dec_scr[3, lane_rows(0), :] = jnp.exp(lg_f * (CHUNK - 1.0 - col_f))
        dec_scr[3, lane_rows(1), :] = jnp.exp(lg_b * col_f)
        dec_scr[3, lane_rows(2), :] = jnp.exp(lg_f * chunk_len)
        dec_scr[3, lane_rows(3), :] = jnp.exp(lg_b * chunk_len)

        def body(i, c):
            units = [(j, r) for j in range(n_seq) for r in range(per_iter)]
            rows = {(j, r, d): _chunk_rows(j * seq, n_chunks - 1 - (i * per_iter + r) if d else i * per_iter + r)
                    for j, r in units for d in range(2)}
            scores = {u: _dot(q_s[rows[u + (0,)], lanes], kt_s[lanes, rows[u + (0,)]].astype(bf16)) for u in units}
            kv = {}
            for j, r in units:
                for d in range(2):
                    rw = rows[(j, r, d)]
                    keys = (kt_s[lanes, rw] * dec_scr[3, d * SUBLANES:d * SUBLANES + 1, :]).astype(bf16)
                    kv[(j, r, d)] = _dot(keys, v_s[rw, lanes])
            state = {(j, d): s_scr[j, d] for j in range(n_seq) for d in range(2)}
            entering, carried = {}, {}
            for r in range(per_iter):
                for j in range(n_seq):
                    entering[(j, r)] = state[(j, 0)].astype(bf16)
                    carried[(j, r)] = _dot(q_s[rows[(j, r, 1)], lanes], state[(j, 1)].astype(bf16))
                    for d in range(2):
                        whole = dec_scr[3, (2 + d) * SUBLANES:(2 + d) * SUBLANES + 1, :]
                        state[(j, d)] = whole * state[(j, d)] + kv[(j, r, d)]
            seen_f = dec_scr[0].astype(bf16)
            for j, r in units:
                rw = rows[(j, r, 0)]
                lhs = jnp.concatenate([q_s[rw, lanes] * seen_f, (scores[(j, r)] * dec_scr[2]).astype(bf16)], axis=1)
                o_scr[0, h, rw, :] = _dot(lhs, jnp.concatenate([entering[(j, r)], v_s[rw, lanes]], axis=0))
                o_scr[1, h, rows[(j, r, 1)], :] = dec_scr[1] * carried[(j, r)]
            for (j, d), s in state.items():
                s_scr[j, d] = s
            return c
        lax.fori_loop(0, n_chunks // per_iter, body, 0)
        if emit_state:
            for j in range(n_seq):
                for d in range(2):
                    st_ref[j, d, hh] = s_scr[j, d]

    for pp in range(N_RET // PAIR):
        pair_body(pp)


def _const_spec(shape):
    zeros = (0,) * len(shape)
    return pl.BlockSpec(shape, lambda i: zeros, pipeline_mode=pl.Buffered(1))


def _step_spec(shape, per_step, buffers=None):
    zeros = (0,) * (len(shape) - 1)
    mode = {} if buffers is None else {"pipeline_mode": pl.Buffered(buffers)}
    return pl.BlockSpec((per_step,) + tuple(shape[1:]), lambda i: (i,) + zeros, **mode)


def _mod_spec(d, per_sequence):
    if per_sequence:
        return pl.BlockSpec((1, 3, d), lambda i: (i + 1, 0, 0))
    return pl.BlockSpec((1, 3, d), lambda i: (0, 0, 0))


def _layer_call(body, x, mod, consts, states, state_shapes, per_sequence_mod, scratch, name, smem_inputs=(),
                state_buffers=None):
    n_seq, seq, d = x.shape
    per_step = STEP_TOKENS // seq
    assert per_step * seq == STEP_TOKENS and n_seq % per_step == 0 and seq % CHUNK == 0
    assert not per_sequence_mod or per_step == 1
    n_steps = n_seq // per_step
    xs = x.reshape(n_steps, STEP_TOKENS, d)
    inputs = list(smem_inputs) + [xs, mod] + list(consts) + list(states)
    in_specs = [pl.BlockSpec(memory_space=pltpu.SMEM)] * len(smem_inputs)
    in_specs += [_step_spec(xs.shape, 1), _mod_spec(d, per_sequence_mod)]
    in_specs += [_const_spec(a.shape) for a in consts]
    in_specs += [_step_spec(s.shape, per_step) for s in states]
    out_shape = [jax.ShapeDtypeStruct(xs.shape, f32)] + [jax.ShapeDtypeStruct(s, f32) for s in state_shapes]
    out_specs = [_step_spec(xs.shape, 1)] + [_step_spec(s, per_step, buffers=state_buffers) for s in state_shapes]
    outs = pl.pallas_call(
        body,
        grid=(n_steps,),
        in_specs=in_specs,
        out_specs=out_specs,
        out_shape=out_shape,
        scratch_shapes=scratch,
        compiler_params=pltpu.CompilerParams(dimension_semantics=("arbitrary",), vmem_limit_bytes=VMEM_LIMIT_BYTES),
        name=name,
    )(*inputs)
    return [outs[0].reshape(x.shape)] + list(outs[1:])


def _even_layer(x, mod, norm_w, w, final_w, tables, states, emit_state, per_sequence_mod, final):
    n_seq, seq, d = x.shape
    consts = [norm_w, w["win"], w["wlow"], w["wout"], w["gkw"], w["gkb"], w["lb"], w["gnw"],
              final_w] + list(tables)
    state_shapes = [(n_seq, 2, N_HGRN, HEAD, HEAD), (n_seq, 2, N_GLA, GLA_DK, HEAD)] if emit_state else []
    scan_heads = max(PAIR, N_GLA)
    scratch = [
        pltpu.VMEM((STEP_TOKENS, d), bf16),
        pltpu.VMEM((2, scan_heads, STEP_TOKENS, HEAD), bf16),
        pltpu.VMEM((STEP_TOKENS, scan_heads * HEAD), bf16),
        pltpu.VMEM((STEP_TOKENS, PAIR * HEAD), f32),
        pltpu.VMEM((STEP_TOKENS, scan_heads * HEAD), bf16),
        pltpu.VMEM((STEP_TOKENS, PAIR * HEAD), f32),
        pltpu.VMEM((STEP_TOKENS, PAIR * HEAD), f32),
        pltpu.VMEM((STEP_TOKENS, 2 * PAIR * HEAD), bf16),
        pltpu.VMEM((STEP_TOKENS, 2 * PAIR * HEAD), bf16),
        pltpu.VMEM((STEP_TOKENS // seq, 2, scan_heads, HEAD, HEAD), f32),
        pltpu.VMEM((2, GATED_UNITS_MAX, TABLE_ROWS_MAX, PAIR * HEAD), f32),
        pltpu.SMEM((1,), jnp.int32),
    ]
    body = functools.partial(_even_kernel, seq=seq, has_state=states is not None, emit_state=emit_state, final=final)
    return _layer_call(body, x, mod, consts, states or (), state_shapes, per_sequence_mod, scratch,
                       "even_layer_seq%d" % seq)


def _odd_layer(x, mod, norm_w, w, final_w, log_decay, rope, state, emit_state, per_sequence_mod, final):
    n_seq, seq, d = x.shape
    consts = [norm_w, w["wr"], w["wout"], w["gnw"], final_w] + list(rope or ())
    state_shapes = [(n_seq, 2, N_RET, HEAD, HEAD)] if emit_state else []
    scratch = [
        pltpu.VMEM((STEP_TOKENS, d), bf16),
        pltpu.VMEM((2, PAIR, STEP_TOKENS, HEAD), f32),
        pltpu.VMEM((STEP_TOKENS, PAIR * HEAD), bf16),
        pltpu.VMEM((STEP_TOKENS, PAIR * HEAD), bf16),
        pltpu.VMEM((PAIR * HEAD, STEP_TOKENS), f32),
        pltpu.VMEM((STEP_TOKENS, PAIR * HEAD), bf16),
        pltpu.VMEM((STEP_TOKENS // seq, 2, HEAD, HEAD), f32),
        pltpu.VMEM((4, CHUNK, HEAD), f32),
    ]
    body = functools.partial(_odd_kernel, seq=seq, has_state=state is not None, emit_state=emit_state,
                             use_rope=rope is not None, final=final)
    return _layer_call(body, x, mod, consts, () if state is None else (state,), state_shapes, per_sequence_mod,
                       scratch, "odd_layer_seq%d" % seq, smem_inputs=(log_decay,))


def _even_weights(w_in, gk_w, gk_b, lb, gn_w, w_out):
    c0 = GLA_LOW_COLS[0]
    wlow = jnp.pad(w_in[:, c0:c0 + 2 * GLA_RANK], ((0, 0), (0, HEAD - 2 * GLA_RANK)))
    gkw = jnp.stack([jnp.pad(gk_w[d], ((d * GLA_RANK, HEAD - (d + 1) * GLA_RANK), (0, 0))) for d in range(2)])
    return {"win": w_in.astype(bf16), "wlow": wlow.astype(bf16), "gkw": gkw.astype(bf16),
            "gkb": gk_b.reshape(2, 1, -1), "lb": lb.reshape(1, -1), "gnw": gn_w.reshape(1, -1),
            "wout": w_out.astype(bf16)}


def _odd_weights(w_in, gn_w, w_out):
    return {"wr": w_in.astype(bf16), "gnw": gn_w.reshape(1, -1), "wout": w_out.astype(bf16)}


def _rope_tables(seq):
    rows = seq // GRID_W
    t_row = np.repeat(np.arange(rows), GRID_W).astype(np.float32)
    t_col = np.tile(np.arange(GRID_W), rows).astype(np.float32)
    half = HEAD // 2
    inv = (ROPE_BASE ** (-np.arange(0, half, 2, dtype=np.float32) / half)).astype(np.float32)
    ang_r = t_row[:, None] * inv
    ang_c = t_col[:, None] * inv
    ang = np.concatenate([ang_r, ang_r, ang_c, ang_c], axis=-1).astype(np.float32)
    sign = np.where((np.arange(HEAD) // (HEAD // 4)) % 2 == 0, -1.0, 1.0).astype(np.float32)
    return jnp.asarray(np.cos(ang), f32), jnp.asarray(np.sin(ang) * sign, f32)


def kernel(x_prompt, x_sample, state_hgrn, state_gla, state_ret, c, c_ctx, norm_w, ada_w, ada_b, w_in_even, hgrn_lb, gla_gk_w, gla_gk_b, gn_even, w_out_even, w_in_odd, ret_decay, gn_odd, w_out_odd, final_norm_w):
    depth, d = norm_w.shape
    n_lat = x_sample.shape[0]
    n_cond = -(-(1 + n_lat) // SUBLANES) * SUBLANES
    cond = jnp.zeros((n_cond, d), f32).at[0].set(c_ctx).at[1:1 + n_lat].set(c)
    mod = _modulation(cond, ada_w, ada_b).reshape(depth, n_cond, 3, d)
    lbs = jnp.cumsum(jax.nn.softmax(hgrn_lb.astype(f32), axis=0), axis=0)
    final_w = final_norm_w.reshape(1, d)
    rope = _rope_tables(x_sample.shape[1])
    tables = sum((_scan_tables(plan) for plan in ALL_PLANS), ())

    x_c, x_l = x_prompt, x_sample
    new_hgrn, new_gla, new_ret = [], [], []
    for l in range(depth):
        i = l // 2
        final = l == depth - 1
        nw = norm_w[l].reshape(1, d)
        if l % 2 == 0:
            w = _even_weights(w_in_even[i], gla_gk_w[i], gla_gk_b[i], lbs[i], gn_even[i], w_out_even[i])
            x_c, st_a, st_b = _even_layer(x_c, mod[l], nw, w, final_w, tables, None, True, False, final)
            (x_l,) = _even_layer(x_l, mod[l], nw, w, final_w, tables, (state_hgrn[:, i], state_gla[:, i]),
                                 False, True, final)
            new_hgrn.append(st_a)
            new_gla.append(st_b)
        else:
            w = _odd_weights(w_in_odd[i], gn_odd[i], w_out_odd[i])
            log_decay = jax.nn.log_sigmoid(ret_decay[i].astype(f32))
            x_c, st_c = _odd_layer(x_c, mod[l], nw, w, final_w, log_decay, None, None, True, False, final)
            (x_l,) = _odd_layer(x_l, mod[l], nw, w, final_w, log_decay, rope, state_ret[:, i], False, True, final)
            new_ret.append(st_c)
    def stacked(states):
        return states[0][:, None] if len(states) == 1 else jnp.stack(states, axis=1)
    return (x_c, x_l, stacked(new_hgrn), stacked(new_gla), stacked(new_ret))
```

```python
import functools
from typing import NamedTuple

import numpy as np
import jax
import jax.numpy as jnp
from jax import lax
from jax.experimental import pallas as pl
from jax.experimental.pallas import tpu as pltpu

f32 = jnp.float32
bf16 = jnp.bfloat16

EPS = 1e-6
LOG2E = 1.4426950408889634
HEAD = 128
N_HGRN = 4
N_GLA = 4
GLA_DK = 64
N_RET = 8
N_HEADS = 8
GLA_RANK = 16
GLA_GATE_NORM = 16.0
GRID_W = 64
ROPE_BASE = 10000.0

EVEN_SPLITS = (N_HGRN * HEAD,) * 5 + (N_GLA * GLA_DK,) * 2 + (N_GLA * HEAD,) * 2 + (GLA_RANK,) * 2
EVEN_STARTS = tuple(int(c) for c in np.cumsum((0,) + EVEN_SPLITS[:-1]))
HGRN_COLS = EVEN_STARTS[0:5]
GLA_QK_COLS = EVEN_STARTS[5:7]
GLA_COLS = EVEN_STARTS[7:9]
GLA_LOW_COLS = EVEN_STARTS[9:11]
RET_COLS = tuple(i * N_RET * HEAD for i in range(4))

CHUNK = 128
SUBLANES = 8


class _ScanPlan(NamedTuple):
    block: int
    table_levels: tuple
    direct_levels: tuple

    @property
    def block_rows(self):
        return 1 if 0 < self.block < CHUNK else 0

    @property
    def table_blocks(self):
        return self.block_rows + len(self.table_levels)

    @property
    def row_total(self):
        return (self.table_blocks + 1) * CHUNK

    @property
    def diag_id(self):
        return len(self.table_levels) + len(self.direct_levels)


ROBUST_PLAN = _ScanPlan(0, (1, 2, 4), (8, 16, 32, 64))
BLOCK_PLAN = _ScanPlan(64, (), (64,))
CHUNK_PLAN = _ScanPlan(CHUNK, (), ())
HGRN_PLANS = (BLOCK_PLAN, ROBUST_PLAN)
GLA_PLANS = (CHUNK_PLAN, ROBUST_PLAN)
ALL_PLANS = (ROBUST_PLAN, BLOCK_PLAN, CHUNK_PLAN)
HGRN_DECAY_LIMIT = 104.0
GLA_DECAY_LIMIT = 90.0
TABLE_ROWS_MAX = max(p.row_total for p in ALL_PLANS) + SUBLANES
PAIR = 2
GATED_HEADS_PER_ITER = 8
GATED_UNITS_MAX = 4
GLA_HEADS_PER_GROUP = HEAD // GLA_DK
RET_UNITS_PER_ITER = 8
STEP_TOKENS = 1024
ROW_TILE = 512
MOD_COLS = 768
VMEM_LIMIT_BYTES = 61 * 1024 * 1024


def _dot(a, b):
    return jnp.dot(a, b, preferred_element_type=f32)


def _dot_tn(a, b):
    return lax.dot_general(a, b, (((0,), (0,)), ((), ())), preferred_element_type=f32)


def _silu(x):
    return x * jax.nn.sigmoid(x)


def _row_tile(i):
    return pl.ds(pl.multiple_of(i * ROW_TILE, ROW_TILE), ROW_TILE)


def _chunk_rows(seq_start, c):
    return pl.ds(pl.multiple_of(seq_start + c * CHUNK, CHUNK), CHUNK)


def _mod_kernel(cond_ref, w_ref, b_ref, o_ref):
    o_ref[0] = _dot(_silu(cond_ref[...]).astype(bf16), w_ref[0].astype(bf16)) + b_ref[0]


def _modulation(cond, ada_w, ada_b):
    depth, d, d3 = ada_w.shape
    rows = cond.shape[0]
    return pl.pallas_call(
        _mod_kernel,
        grid=(depth, d3 // MOD_COLS),
        in_specs=[
            pl.BlockSpec((rows, d), lambda l, j: (0, 0)),
            pl.BlockSpec((1, d, MOD_COLS), lambda l, j: (l, 0, j)),
            pl.BlockSpec((1, 1, MOD_COLS), lambda l, j: (l, 0, j)),
        ],
        out_specs=pl.BlockSpec((1, rows, MOD_COLS), lambda l, j: (l, 0, j)),
        out_shape=jax.ShapeDtypeStruct((depth, rows, d3), f32),
        compiler_params=pltpu.CompilerParams(dimension_semantics=("arbitrary", "arbitrary")),
        name="modulation",
    )(cond, ada_w, ada_b.reshape(depth, 1, d3))


def _modulated_norm(x_ref, mod_ref, nw_ref, h_scr):
    def body(i, carry):
        rows = _row_tile(i)
        x = x_ref[0, rows, :]
        y = x * lax.rsqrt(jnp.mean(x * x, axis=-1, keepdims=True) + EPS) * nw_ref[...]
        h_scr[rows, :] = (y * (1.0 + mod_ref[0, 1:2, :]) + mod_ref[0, 0:1, :]).astype(bf16)
        return carry
    lax.fori_loop(0, STEP_TOKENS // ROW_TILE, body, 0)


def _aligned_slice(start, width):
    return pl.ds(start if isinstance(start, int) else pl.multiple_of(start, width), width)


def _pair_lanes(pair):
    return _aligned_slice(pair * (PAIR * HEAD), PAIR * HEAD)


def _project(h, w_ref, pair, first_cols):
    width = PAIR * HEAD
    return [_dot(h, w_ref[:, _aligned_slice(c0 + pair * width, width)]) for c0 in first_cols]


def _add_heads_output(first_head, n_heads, gate_s, o_scr, gnw_ref, wo_ref, out_ref, first, finish):
    width = n_heads * HEAD
    lanes = _aligned_slice(first_head * HEAD, width)
    tiles = []
    for i in range(STEP_TOKENS // ROW_TILE):
        rows = pl.ds(i * ROW_TILE, ROW_TILE)
        parts = []
        for h in range(n_heads):
            o = o_scr[0, h, rows, :].astype(f32) + o_scr[1, h, rows, :].astype(f32)
            parts.append(o * lax.rsqrt(jnp.mean(o * o, axis=-1, keepdims=True) + EPS))
        y = jnp.concatenate(parts, axis=-1) * gnw_ref[:, lanes]
        tiles.append((rows, (y * gate_s[rows, 0:width].astype(f32)).astype(bf16)))
    products = [(rows, _dot(z, wo_ref[lanes, :])) for rows, z in tiles]
    for rows, product in products:
        mixed = product if first else out_ref[0, rows, :] + product
        if finish is not None:
            x_ref, mod_ref, fnw_ref, final = finish
            mixed = x_ref[0, rows, :] + mod_ref[0, 2:3, :] * mixed
            if final:
                mixed = mixed * lax.rsqrt(jnp.mean(mixed * mixed, axis=-1, keepdims=True) + EPS) * fnw_ref[...]
        out_ref[0, rows, :] = mixed


def _scan_tables(plan):
    t = np.arange(CHUNK)[:, None]
    j = np.arange(CHUNK)[None, :]
    fwd = []
    if plan.block_rows:
        fwd.append((j <= t) & (j // plan.block == t // plan.block))
    for m in plan.table_levels:
        mid = (t // (2 * m)) * (2 * m) + m
        right = t >= mid
        fwd.append(np.where(right, (j >= mid) & (j <= t), (j > t) & (j < mid)))
    fwd.append(j <= t)
    fwd.append(np.ones((SUBLANES, CHUNK), bool))
    fwd = np.concatenate(fwd, axis=0).astype(np.float32)
    bwd = fwd.copy()
    n_sym = plan.row_total // CHUNK
    bwd[:plan.row_total] = fwd[:plan.row_total].reshape(n_sym, CHUNK, CHUNK)[:, ::-1, ::-1].reshape(-1, CHUNK)
    table = np.stack([np.tile(fwd, (1, 2)), np.tile(bwd, (1, 2))])
    first = 1 if plan.block else 0
    ids = np.full((CHUNK, CHUNK), -1, np.int32)
    for i, m in enumerate(plan.table_levels + plan.direct_levels):
        ids = np.where((t > j) & ((t ^ j) >= m) & ((t ^ j) < 2 * m), first + i, ids)
    if plan.block:
        ids = np.where((t >= j) & (t // plan.block == j // plan.block), 0, ids)
    else:
        ids = np.where(t == j, plan.diag_id, ids)
    return jnp.asarray(table, bf16), jnp.asarray(np.stack([ids, ids.T]).astype(np.int32))


def _in_chunk_scores(q, k, cum, tabled, lvl, ones, rev, key_masks, plan):
    qb = q.astype(bf16)
    q_heads = [qb if mask is None else qb * mask for mask in key_masks]
    kt = k.astype(bf16).T
    if plan.block:
        inside = tabled[0] if plan.block_rows else cum
        grow, decay = jnp.exp2(-inside).astype(bf16), jnp.exp2(inside).astype(bf16)
        keys = kt * grow.T
        scores = [jnp.where(lvl == 0, _dot(qh * decay, keys), 0.0) for qh in q_heads]
        tabled = tabled[plan.block_rows:]
    else:
        diag_keys = kt * ones
        scores = [jnp.where(lvl == plan.diag_id, _dot(qh, diag_keys), 0.0) for qh in q_heads]
    first = 1 if plan.block else 0
    for i in range(len(plan.table_levels)):
        e = jnp.exp2(tabled[i]).astype(bf16)
        keys = kt * e.T
        scores = [jnp.where(lvl == first + i, _dot(qh * e, keys), sc) for qh, sc in zip(q_heads, scores)]
    for i, m in enumerate(plan.direct_levels, start=first + len(plan.table_levels)):
        blocks = []
        for p0 in range(0, CHUNK, 2 * m):
            left, right = slice(p0, p0 + m), slice(p0 + m, p0 + 2 * m)
            q_side, k_side = (left, right) if rev else (right, left)
            mid_row = p0 + m if rev else p0 + m - 1
            blocks.append((q_side, k_side, cum[mid_row:mid_row + 1, :]))
        q_decay = [jnp.exp2(cum[qs] - mid).astype(bf16) for qs, _, mid in blocks]
        k_decay = []
        for _, ks, mid in blocks:
            decay = jnp.exp2(mid - cum[ks]).astype(bf16)
            zero = jnp.zeros((m, HEAD), bf16)
            k_decay += [zero, decay] if rev else [decay, zero]
        keys = kt * jnp.concatenate(k_decay, axis=0).T
        for h, qh in enumerate(q_heads):
            s = _dot(jnp.concatenate([qh[qs] * e for (qs, _, _), e in zip(blocks, q_decay)], axis=0), keys)
            rows = []
            for b, (qs, ks, _) in enumerate(blocks):
                if len(blocks) == 1:
                    updated = scores[h][qs, :] + s
                else:
                    updated = jnp.where(lvl[qs, :] == i, s[b * m:(b + 1) * m, :], scores[h][qs, :])
                rows += [updated, scores[h][ks, :]] if rev else [scores[h][ks, :], updated]
            scores[h] = jnp.concatenate(rows, axis=0)
    return [sc.astype(bf16) for sc in scores]


def _store_log2_split(g_ref, rows, g, block):
    width = g.shape[-1]
    x = g * LOG2E
    hi = x.astype(bf16)
    g_ref[rows, 0:width] = hi
    g_ref[rows, width:2 * width] = (x - hi.astype(f32)).astype(bf16)
    return jnp.min(jnp.sum(x.reshape(-1, block, width), axis=1))


def _decay_sums(g_split, table):
    width = g_split.shape[-1] // 2
    return _dot(table, jnp.concatenate([g_split[:, :width], g_split[:, width:]], axis=0))


def _key_masks(heads_per_group):
    if heads_per_group == 1:
        return [None]
    lane = lax.broadcasted_iota(jnp.int32, (1, HEAD), 1)
    return [(lane // (HEAD // heads_per_group) == sub).astype(bf16) for sub in range(heads_per_group)]


def _gated_chunks(chains, sums_ref, upcoming, upcoming_ref, heads_per_group, plan):
    masks = _key_masks(heads_per_group)
    groups = [slice(g * HEAD, (g + 1) * HEAD) for g in range(PAIR)]
    n_heads = PAIR * heads_per_group
    group_of = [h // heads_per_group for h in range(n_heads)]
    mask_of = [masks[h % heads_per_group] for h in range(n_heads)]
    flat = [(c, r) for c, (_, units) in enumerate(chains) for r in range(len(units))]
    unit = {(c, r): chains[c][1][r] for c, r in flat}
    index = {key: u for u, key in enumerate(flat)}
    cum_rows = slice(plan.table_blocks * CHUNK, (plan.table_blocks + 1) * CHUNK)
    cum = {key: sums_ref[index[key], cum_rows, :] for key in flat}
    total = {key: sums_ref[index[key], plan.row_total:plan.row_total + 1, :] for key in flat}
    vb = {key: [unit[key][2][:, h * HEAD:(h + 1) * HEAD] for h in range(n_heads)] for key in flat}
    upcoming = list(enumerate(upcoming))
    kv, whole = {}, {}
    for key in flat:
        k = unit[key][1]
        keys = (k * jnp.exp2(total[key] - cum[key])).astype(bf16)
        kv[key] = []
        for h in range(n_heads):
            own = keys[:, groups[group_of[h]]]
            kv[key].append(_dot_tn(own if mask_of[h] is None else own * mask_of[h], vb[key][h]))
        whole[key] = [jnp.broadcast_to(jnp.exp2(total[key][:, lanes]), (HEAD, HEAD)).T for lanes in groups]
    state = [list(states) for states, _ in chains]
    outs, pending, carried = {key: [None] * n_heads for key in flat}, None, {}

    def finish(done):
        pkey, ph, psc = done
        queries, entering = carried[pkey]
        lhs = jnp.concatenate([queries[:, groups[group_of[ph]]], psc], axis=1)
        outs[pkey][ph] = _dot(lhs, jnp.concatenate([entering[ph], vb[pkey][ph]], axis=0))

    for r in range(max(len(units) for _, units in chains)):
        live = [key for key in flat if key[1] == r]
        for key in live:
            queries = (unit[key][0] * jnp.exp2(cum[key])).astype(bf16)
            carried[key] = (queries, [s.astype(bf16) for s in state[key[0]]])
            state[key[0]] = [whole[key][group_of[h]] * state[key[0]][h] + kv[key][h] for h in range(n_heads)]
        for key in live:
            q, k, _, lvl, rev = unit[key]
            ones = (lvl[0:1, :] >= -1).astype(bf16)
            for g, lanes in enumerate(groups):
                tabled = [sums_ref[index[key], i * CHUNK:(i + 1) * CHUNK, lanes] for i in range(plan.table_blocks)]
                group_scores = _in_chunk_scores(q[:, lanes], k[:, lanes], cum[key][:, lanes], tabled, lvl, ones, rev,
                                                masks, plan)
                for sub, sc in enumerate(group_scores):
                    if pending is not None:
                        finish(pending)
                    pending = (key, g * heads_per_group + sub, sc)
            if upcoming:
                u, (g_split, table) = upcoming.pop(0)
                upcoming_ref[u, 0:plan.row_total + SUBLANES, :] = _decay_sums(g_split, table)
    for u, (g_split, table) in upcoming:
        upcoming_ref[u, 0:plan.row_total + SUBLANES, :] = _decay_sums(g_split, table)
    finish(pending)
    return [[outs[(c, r)] for r in range(len(units))] for c, (_, units) in enumerate(chains)], state


def _gated_scans(q_s, k_refs, v_s, g_refs, s_scr, sums_scr, table_ref, lvl_ref, o_scr, seq, heads_per_group, plan):
    n = seq // CHUNK
    n_seq = STEP_TOKENS // seq
    n_heads = PAIR * heads_per_group
    units = GATED_HEADS_PER_ITER // n_heads
    per_iter = min(n, units // 2)
    seqs_per_iter = min(n_seq, units // (2 * per_iter))
    assert n % per_iter == 0 and n_seq % seqs_per_iter == 0
    iters_per_seq = n // per_iter
    n_iters = (n_seq // seqs_per_iter) * iters_per_seq

    def layout(it):
        jj, i = it // iters_per_seq, it % iters_per_seq
        chains = []
        for js in range(seqs_per_iter):
            j = jj * seqs_per_iter + js
            for d in range(2):
                steps = [i * per_iter + r for r in range(per_iter)]
                chains.append((j, d, [_chunk_rows(j * seq, n - 1 - t if d else t) for t in steps]))
        return chains

    def sums_inputs(it):
        return [(g_refs[d][rw, :], table_ref[d]) for _, d, rws in layout(it) for rw in rws]

    for u, (g_split, table) in enumerate(sums_inputs(0)):
        sums_scr[0, u, 0:plan.row_total + SUBLANES, :] = _decay_sums(g_split, table)

    def iteration(it, slot):
        chains = layout(it)
        args = [([s_scr[j, d, h] for h in range(n_heads)],
                 [(q_s[rw, :], k_refs[d][rw, :], v_s[rw, :], lvl_ref[d], bool(d)) for rw in rws])
                for j, d, rws in chains]
        upcoming = sums_inputs(jnp.minimum(it + 1, n_iters - 1))
        outs, new_states = _gated_chunks(args, sums_scr.at[slot], upcoming, sums_scr.at[1 - slot], heads_per_group,
                                         plan)
        for c, (j, d, rws) in enumerate(chains):
            for h in range(n_heads):
                for r, rw in enumerate(rws):
                    o_scr[d, h, rw, :] = outs[c][r][h].astype(o_scr.dtype)
                s_scr[j, d, h] = new_states[c][h]

    assert n_iters % 2 == 0

    def body(it2, carry):
        iteration(2 * it2, 0)
        iteration(2 * it2 + 1, 1)
        return carry
    lax.fori_loop(0, n_iters // 2, body, 0)


def _even_kernel(*refs, seq, has_state, emit_state, final):
    it = iter(refs)
    x_ref, mod_ref, nw_ref = next(it), next(it), next(it)
    win_ref, wlow_ref, wo_ref = next(it), next(it), next(it)
    gkw_ref, gkb_ref, lb_ref, gnw_ref, fnw_ref = next(it), next(it), next(it), next(it), next(it)
    plan_tables = {plan: (next(it), next(it)) for plan in ALL_PLANS}
    s0a_ref, s0b_ref = (next(it), next(it)) if has_state else (None, None)
    out_ref = next(it)
    sta_ref, stb_ref = (next(it), next(it)) if emit_state else (None, None)
    h_scr, o_scr, gate_s, q_s, v_s, kf_s, kb_s, gf_s, gb_s, s_scr, sums_scr, bound_s = it
    n_seq = STEP_TOKENS // seq
    n_tiles = STEP_TOKENS // ROW_TILE

    _modulated_norm(x_ref, mod_ref, nw_ref, h_scr)

    width = PAIR * HEAD

    def run_scans(first_head, s0_ref, st_ref, h0, heads_per_group, k_refs, plans, block_decays, first=False,
                  last=False):
        bounded, robust, limit = plans
        bound_s[0] = (functools.reduce(jnp.minimum, block_decays) >= -limit).astype(jnp.int32)
        n_heads = PAIR * heads_per_group
        key_rows = HEAD // heads_per_group
        own_rows = [pl.ds((h % heads_per_group) * key_rows, key_rows) for h in range(n_heads)]
        for j in range(n_seq):
            for d in range(2):
                for h in range(n_heads):
                    if s0_ref is None or heads_per_group > 1:
                        s_scr[j, d, h] = jnp.zeros((HEAD, HEAD), f32)
                    if s0_ref is not None:
                        s_scr[j, d, h, own_rows[h], :] = s0_ref[j, d, h0 + h]
        def scans(plan):
            table_ref, lvl_ref = plan_tables[plan]
            _gated_scans(q_s, k_refs, v_s, (gf_s, gb_s), s_scr, sums_scr, table_ref, lvl_ref, o_scr, seq,
                         heads_per_group, plan)
        lax.cond(bound_s[0] == 1, functools.partial(scans, bounded), functools.partial(scans, robust))
        if st_ref is not None:
            for j in range(n_seq):
                for d in range(2):
                    for h in range(n_heads):
                        st_ref[j, d, h0 + h] = s_scr[j, d, h, own_rows[h], :]
        _add_heads_output(first_head, n_heads, gate_s, o_scr, gnw_ref, wo_ref, out_ref, first,
                          (x_ref, mod_ref, fnw_ref, final) if last else None)

    def hgrn_pair(pp, carry):
        lb = lb_ref[:, _pair_lanes(pp)]
        log_lb = jnp.log(lb)

        tiles = [pl.ds(i * ROW_TILE, ROW_TILE) for i in range(n_tiles)]
        query_c, value_c, forget_f_c, forget_b_c, gate_c = HGRN_COLS
        order = (forget_f_c, forget_b_c, query_c, value_c, gate_c)
        projected = [_project(h_scr[rows, :], win_ref, pp, order) for rows in tiles]
        block_decays = []
        for rows, (forget_f, forget_b, query, value, gate) in zip(tiles, projected):
            for a, k_s, g_s in ((forget_f, kf_s, gf_s), (forget_b, kb_s, gb_s)):
                z = log_lb - a
                u = jnp.exp(-jnp.abs(a))
                w = jnp.exp(-jnp.abs(z))
                r = 1.0 / (1.0 + u)
                log_f = jnp.maximum(z, 0.0) + jnp.minimum(a, 0.0) + jnp.log((1.0 + w) * r)
                block_decays.append(_store_log2_split(g_s, rows, log_f, HGRN_PLANS[0].block))
                k_s[rows, :] = (1.0 - lb) * jnp.where(a >= 0.0, u * r, r)
            q_s[rows, :] = _silu(query)
            v_s[rows, 0:width] = value.astype(bf16)
            gate_s[rows, 0:width] = _silu(gate).astype(bf16)
        run_scans(PAIR * pp, s0a_ref, sta_ref, PAIR * pp, 1, (kf_s, kb_s), HGRN_PLANS + (HGRN_DECAY_LIMIT,), block_decays)
        return carry

    def clear(i, carry):
        out_ref[0, _row_tile(i), :] = jnp.zeros((ROW_TILE, out_ref.shape[-1]), f32)
        return carry
    lax.fori_loop(0, n_tiles, clear, 0)
    lax.fori_loop(0, N_HGRN // PAIR, hgrn_pair, 0)

    assert N_GLA == PAIR * GLA_HEADS_PER_GROUP

    def gla_project(rows):
        h = h_scr[rows, :]
        narrow = [_dot(h, win_ref[:, c0:c0 + N_GLA * GLA_DK]) for c0 in GLA_QK_COLS]
        wide = [_dot(h, win_ref[:, c0:c0 + N_GLA * HEAD]) for c0 in GLA_COLS]
        return narrow + wide + [_dot(h, wlow_ref[...])]

    tiles = [pl.ds(i * ROW_TILE, ROW_TILE) for i in range(n_tiles)]
    projected = [gla_project(rows) for rows in tiles]
    block_decays = []
    for rows, (query, key, value, gate, low) in zip(tiles, projected):
        gate_s[rows, :] = _silu(gate).astype(bf16)
        q_s[rows, :] = query * (GLA_DK ** -0.5)
        kf_s[rows, :] = key
        v_s[rows, :] = value.astype(bf16)
        low = low.astype(bf16)
        for d, g_s in enumerate((gf_s, gb_s)):
            logits = _dot(low, gkw_ref[d]) + gkb_ref[d]
            log_gate = jnp.minimum(logits, 0.0) - jnp.log(1.0 + jnp.exp(-jnp.abs(logits)))
            block_decays.append(_store_log2_split(g_s, rows, log_gate * (1.0 / GLA_GATE_NORM), GLA_PLANS[0].block))
    run_scans(N_HGRN, s0b_ref, stb_ref, 0, GLA_HEADS_PER_GROUP, (kf_s, kf_s), GLA_PLANS + (GLA_DECAY_LIMIT,), block_decays, last=True)


def _odd_kernel(*refs, seq, has_state, emit_state, use_rope, final):
    it = iter(refs)
    lg_ref = next(it)
    x_ref, mod_ref, nw_ref = next(it), next(it), next(it)
    wr_ref, wo_ref, gnw_ref, fnw_ref = next(it), next(it), next(it), next(it)
    cos_ref, sin_ref = (next(it), next(it)) if use_rope else (None, None)
    s0_ref = next(it) if has_state else None
    out_ref = next(it)
    st_ref = next(it) if emit_state else None
    h_scr, o_scr, gate_s, q_s, kt_s, v_s, s_scr, dec_scr = it
    n_seq = STEP_TOKENS // seq
    n_tiles = STEP_TOKENS // ROW_TILE
    n_chunks = seq // CHUNK
    per_iter = min(n_chunks, RET_UNITS_PER_ITER // n_seq)
    assert n_chunks % per_iter == 0

    _modulated_norm(x_ref, mod_ref, nw_ref, h_scr)
    t_idx = lax.broadcasted_iota(jnp.int32, (CHUNK, CHUNK), 0)
    s_idx = lax.broadcasted_iota(jnp.int32, (CHUNK, CHUNK), 1)
    row_f = lax.broadcasted_iota(jnp.int32, (CHUNK, HEAD), 0).astype(f32)
    col_f = lax.broadcasted_iota(jnp.int32, (SUBLANES, CHUNK), 1).astype(f32)
    chunk_len = jnp.full((SUBLANES, HEAD), CHUNK, f32)

    def lane_rows(block):
        return slice(block * SUBLANES, (block + 1) * SUBLANES)
    if use_rope:
        lane = lax.broadcasted_iota(jnp.int32, (ROW_TILE, HEAD), 1)
        first_quarter = (lane // (HEAD // 4)) % 2 == 0

    def rope(x, cos, sin_signed):
        xr = jnp.where(first_quarter, pltpu.roll(x, HEAD - HEAD // 4, axis=1), pltpu.roll(x, HEAD // 4, axis=1))
        return x * cos + xr * sin_signed

    width = PAIR * HEAD

    def pair_body(pp):
        tiles = [pl.ds(i * ROW_TILE, ROW_TILE) for i in range(n_tiles)]
        projected = [_project(h_scr[rows, :], wr_ref, pp, RET_COLS) for rows in tiles]
        for rows, (q, k, value, gate) in zip(tiles, projected):
            gate_s[rows, :] = _silu(gate).astype(bf16)
            k = k * (HEAD ** -0.5)
            if use_rope:
                cos, sin_signed = cos_ref[rows, :], sin_ref[rows, :]
                heads = [slice(h * HEAD, (h + 1) * HEAD) for h in range(PAIR)]
                q = jnp.concatenate([rope(q[:, lanes], cos, sin_signed) for lanes in heads], axis=-1)
                k = jnp.concatenate([rope(k[:, lanes], cos, sin_signed) for lanes in heads], axis=-1)
            q_s[rows, :] = q.astype(bf16)
            kt_s[:, rows] = k.T
            v_s[rows, :] = value.astype(bf16)
        for h in range(PAIR):
            scan_head(PAIR * pp + h, h)
        last = pp == N_RET // PAIR - 1
        _add_heads_output(PAIR * pp, PAIR, gate_s, o_scr, gnw_ref, wo_ref, out_ref, pp == 0,
                          (x_ref, mod_ref, fnw_ref, final) if last else None)

    def scan_head(hh, h):
        lanes = slice(h * HEAD, (h + 1) * HEAD)
        lg_f = lg_ref[0, hh]
        lg_b = lg_ref[1, hh]
        for j in range(n_seq):
            for d in range(2):
                s_scr[j, d] = s0_ref[j, d, hh] if has_state else jnp.zeros((HEAD, HEAD), f32)

        dist = (t_idx - s_idx).astype(f32)
        dec_scr[0] = jnp.exp(lg_f * (row_f + 1.0))
        dec_scr[1] = jnp.exp(lg_b * (CHUNK - row_f))
        dec_scr[2] = (jnp.where(t_idx >= s_idx, jnp.exp(lg_f * jnp.maximum(dist, 0.0)), 0.0)
                      + jnp.where(s_idx >= t_idx, jnp.exp(lg_b * jnp.maximum(-dist, 0.0)), 0.0))
        dec_scr[3, lane_rows(0), :] = jnp.exp(lg_f * (CHUNK - 1.0 - col_f))
        dec_scr[3, lane_rows(1), :] = jnp.exp(lg_b * col_f)
        dec_scr[3, lane_rows(2), :] = jnp.exp(lg_f * chunk_len)
        dec_scr[3, lane_rows(3), :] = jnp.exp(lg_b * chunk_len)

        def body(i, c):
            units = [(j, r) for j in range(n_seq) for r in range(per_iter)]
            rows = {(j, r, d): _chunk_rows(j * seq, n_chunks - 1 - (i * per_iter + r) if d else i * per_iter + r)
                    for j, r in units for d in range(2)}
            scores = {u: _dot(q_s[rows[u + (0,)], lanes], kt_s[lanes, rows[u + (0,)]].astype(bf16)) for u in units}
            kv = {}
            for j, r in units:
                for d in range(2):
                    rw = rows[(j, r, d)]
                    keys = (kt_s[lanes, rw] * dec_scr[3, d * SUBLANES:d * SUBLANES + 1, :]).astype(bf16)
                    kv[(j, r, d)] = _dot(keys, v_s[rw, lanes])
            state = {(j, d): s_scr[j, d] for j in range(n_seq) for d in range(2)}
            entering, carried = {}, {}
            for r in range(per_iter):
                for j in range(n_seq):
                    entering[(j, r)] = state[(j, 0)].astype(bf16)
                    carried[(j, r)] = _dot(q_s[rows[(j, r, 1)], lanes], state[(j, 1)].astype(bf16))
                    for d in range(2):
                        whole = dec_scr[3, (2 + d) * SUBLANES:(2 + d) * SUBLANES + 1, :]
                        state[(j, d)] = whole * state[(j, d)] + kv[(j, r, d)]
            seen_f = dec_scr[0].astype(bf16)
            for j, r in units:
                rw = rows[(j, r, 0)]
                lhs = jnp.concatenate([q_s[rw, lanes] * seen_f, (scores[(j, r)] * dec_scr[2]).astype(bf16)], axis=1)
                o_scr[0, h, rw, :] = _dot(lhs, jnp.concatenate([entering[(j, r)], v_s[rw, lanes]], axis=0))
                o_scr[1, h, rows[(j, r, 1)], :] = dec_scr[1] * carried[(j, r)]
            for (j, d), s in state.items():
                s_scr[j, d] = s
            return c
        lax.fori_loop(0, n_chunks // per_iter, body, 0)
        if emit_state:
            for j in range(n_seq):
                for d in range(2):
                    st_ref[j, d, hh] = s_scr[j, d]

    for pp in range(N_RET // PAIR):
        pair_body(pp)


def _const_spec(shape):
    zeros = (0,) * len(shape)
    return pl.BlockSpec(shape, lambda i: zeros, pipeline_mode=pl.Buffered(1))


def _step_spec(shape, per_step, buffers=None):
    zeros = (0,) * (len(shape) - 1)
    mode = {} if buffers is None else {"pipeline_mode": pl.Buffered(buffers)}
    return pl.BlockSpec((per_step,) + tuple(shape[1:]), lambda i: (i,) + zeros, **mode)


def _mod_spec(d, per_sequence):
    if per_sequence:
        return pl.BlockSpec((1, 3, d), lambda i: (i + 1, 0, 0))
    return pl.BlockSpec((1, 3, d), lambda i: (0, 0, 0))


def _layer_call(body, x, mod, consts, states, state_shapes, per_sequence_mod, scratch, name, smem_inputs=(),
                state_buffers=None):
    n_seq, seq, d = x.shape
    per_step = STEP_TOKENS // seq
    assert per_step * seq == STEP_TOKENS and n_seq % per_step == 0 and seq % CHUNK == 0
    assert not per_sequence_mod or per_step == 1
    n_steps = n_seq // per_step
    xs = x.reshape(n_steps, STEP_TOKENS, d)
    inputs = list(smem_inputs) + [xs, mod] + list(consts) + list(states)
    in_specs = [pl.BlockSpec(memory_space=pltpu.SMEM)] * len(smem_inputs)
    in_specs += [_step_spec(xs.shape, 1), _mod_spec(d, per_sequence_mod)]
    in_specs += [_const_spec(a.shape) for a in consts]
    in_specs += [_step_spec(s.shape, per_step) for s in states]
    out_shape = [jax.ShapeDtypeStruct(xs.shape, f32)] + [jax.ShapeDtypeStruct(s, f32) for s in state_shapes]
    out_specs = [_step_spec(xs.shape, 1)] + [_step_spec(s, per_step, buffers=state_buffers) for s in state_shapes]
    outs = pl.pallas_call(
        body,
        grid=(n_steps,),
        in_specs=in_specs,
        out_specs=out_specs,
        out_shape=out_shape,
        scratch_shapes=scratch,
        compiler_params=pltpu.CompilerParams(dimension_semantics=("arbitrary",), vmem_limit_bytes=VMEM_LIMIT_BYTES),
        name=name,
    )(*inputs)
    return [outs[0].reshape(x.shape)] + list(outs[1:])


def _even_layer(x, mod, norm_w, w, final_w, tables, states, emit_state, per_sequence_mod, final):
    n_seq, seq, d = x.shape
    consts = [norm_w, w["win"], w["wlow"], w["wout"], w["gkw"], w["gkb"], w["lb"], w["gnw"],
              final_w] + list(tables)
    state_shapes = [(n_seq, 2, N_HGRN, HEAD, HEAD), (n_seq, 2, N_GLA, GLA_DK, HEAD)] if emit_state else []
    scan_heads = max(PAIR, N_GLA)
    scratch = [
        pltpu.VMEM((STEP_TOKENS, d), bf16),
        pltpu.VMEM((2, scan_heads, STEP_TOKENS, HEAD), bf16),
        pltpu.VMEM((STEP_TOKENS, scan_heads * HEAD), bf16),
        pltpu.VMEM((STEP_TOKENS, PAIR * HEAD), f32),
        pltpu.VMEM((STEP_TOKENS, scan_heads * HEAD), bf16),
        pltpu.VMEM((STEP_TOKENS, PAIR * HEAD), f32),
        pltpu.VMEM((STEP_TOKENS, PAIR * HEAD), f32),
        pltpu.VMEM((STEP_TOKENS, 2 * PAIR * HEAD), bf16),
        pltpu.VMEM((STEP_TOKENS, 2 * PAIR * HEAD), bf16),
        pltpu.VMEM((STEP_TOKENS // seq, 2, scan_heads, HEAD, HEAD), f32),
        pltpu.VMEM((2, GATED_UNITS_MAX, TABLE_ROWS_MAX, PAIR * HEAD), f32),
        pltpu.SMEM((1,), jnp.int32),
    ]
    body = functools.partial(_even_kernel, seq=seq, has_state=states is not None, emit_state=emit_state, final=final)
    return _layer_call(body, x, mod, consts, states or (), state_shapes, per_sequence_mod, scratch,
                       "even_layer_seq%d" % seq)


def _odd_layer(x, mod, norm_w, w, final_w, log_decay, rope, state, emit_state, per_sequence_mod, final):
    n_seq, seq, d = x.shape
    consts = [norm_w, w["wr"], w["wout"], w["gnw"], final_w] + list(rope or ())
    state_shapes = [(n_seq, 2, N_RET, HEAD, HEAD)] if emit_state else []
    scratch = [
        pltpu.VMEM((STEP_TOKENS, d), bf16),
        pltpu.VMEM((2, PAIR, STEP_TOKENS, HEAD), f32),
        pltpu.VMEM((STEP_TOKENS, PAIR * HEAD), bf16),
        pltpu.VMEM((STEP_TOKENS, PAIR * HEAD), bf16),
        pltpu.VMEM((PAIR * HEAD, STEP_TOKENS), f32),
        pltpu.VMEM((STEP_TOKENS, PAIR * HEAD), bf16),
        pltpu.VMEM((STEP_TOKENS // seq, 2, HEAD, HEAD), f32),
        pltpu.VMEM((4, CHUNK, HEAD), f32),
    ]
    body = functools.partial(_odd_kernel, seq=seq, has_state=state is not None, emit_state=emit_state,
                             use_rope=rope is not None, final=final)
    return _layer_call(body, x, mod, consts, () if state is None else (state,), state_shapes, per_sequence_mod,
                       scratch, "odd_layer_seq%d" % seq, smem_inputs=(log_decay,))


def _even_weights(w_in, gk_w, gk_b, lb, gn_w, w_out):
    c0 = GLA_LOW_COLS[0]
    wlow = jnp.pad(w_in[:, c0:c0 + 2 * GLA_RANK], ((0, 0), (0, HEAD - 2 * GLA_RANK)))
    gkw = jnp.stack([jnp.pad(gk_w[d], ((d * GLA_RANK, HEAD - (d + 1) * GLA_RANK), (0, 0))) for d in range(2)])
    return {"win": w_in.astype(bf16), "wlow": wlow.astype(bf16), "gkw": gkw.astype(bf16),
            "gkb": gk_b.reshape(2, 1, -1), "lb": lb.reshape(1, -1), "gnw": gn_w.reshape(1, -1),
            "wout": w_out.astype(bf16)}


def _odd_weights(w_in, gn_w, w_out):
    return {"wr": w_in.astype(bf16), "gnw": gn_w.reshape(1, -1), "wout": w_out.astype(bf16)}


def _rope_tables(seq):
    rows = seq // GRID_W
    t_row = np.repeat(np.arange(rows), GRID_W).astype(np.float32)
    t_col = np.tile(np.arange(GRID_W), rows).astype(np.float32)
    half = HEAD // 2
    inv = (ROPE_BASE ** (-np.arange(0, half, 2, dtype=np.float32) / half)).astype(np.float32)
    ang_r = t_row[:, None] * inv
    ang_c = t_col[:, None] * inv
    ang = np.concatenate([ang_r, ang_r, ang_c, ang_c], axis=-1).astype(np.float32)
    sign = np.where((np.arange(HEAD) // (HEAD // 4)) % 2 == 0, -1.0, 1.0).astype(np.float32)
    return jnp.asarray(np.cos(ang), f32), jnp.asarray(np.sin(ang) * sign, f32)


def kernel(x_prompt, x_sample, state_hgrn, state_gla, state_ret, c, c_ctx, norm_w, ada_w, ada_b, w_in_even, hgrn_lb, gla_gk_w, gla_gk_b, gn_even, w_out_even, w_in_odd, ret_decay, gn_odd, w_out_odd, final_norm_w):
    depth, d = norm_w.shape
    n_lat = x_sample.shape[0]
    n_cond = -(-(1 + n_lat) // SUBLANES) * SUBLANES
    cond = jnp.zeros((n_cond, d), f32).at[0].set(c_ctx).at[1:1 + n_lat].set(c)
    mod = _modulation(cond, ada_w, ada_b).reshape(depth, n_cond, 3, d)
    lbs = jnp.cumsum(jax.nn.softmax(hgrn_lb.astype(f32), axis=0), axis=0)
    final_w = final_norm_w.reshape(1, d)
    rope = _rope_tables(x_sample.shape[1])
    tables = sum((_scan_tables(plan) for plan in ALL_PLANS), ())

    x_c, x_l = x_prompt, x_sample
    new_hgrn, new_gla, new_ret = [], [], []
    for l in range(depth):
        i = l // 2
        final = l == depth - 1
        nw = norm_w[l].reshape(1, d)
        if l % 2 == 0:
            w = _even_weights(w_in_even[i], gla_gk_w[i], gla_gk_b[i], lbs[i], gn_even[i], w_out_even[i])
            x_c, st_a, st_b = _even_layer(x_c, mod[l], nw, w, final_w, tables, None, True, False, final)
            (x_l,) = _even_layer(x_l, mod[l], nw, w, final_w, tables, (state_hgrn[:, i], state_gla[:, i]),
                                 False, True, final)
            new_hgrn.append(st_a)
            new_gla.append(st_b)
        else:
            w = _odd_weights(w_in_odd[i], gn_odd[i], w_out_odd[i])
            log_decay = jax.nn.log_sigmoid(ret_decay[i].astype(f32))
            x_c, st_c = _odd_layer(x_c, mod[l], nw, w, final_w, log_decay, None, None, True, False, final)
            (x_l,) = _odd_layer(x_l, mod[l], nw, w, final_w, log_decay, rope, state_ret[:, i], False, True, final)
            new_ret.append(st_c)
    def stacked(states):
        return states[0][:, None] if len(states) == 1 else jnp.stack(states, axis=1)
    return (x_c, x_l, stacked(new_hgrn), stacked(new_gla), stacked(new_ret))
```

```python
import functools
from typing import NamedTuple

import numpy as np
import jax
import jax.numpy as jnp
from jax import lax
from jax.experimental import pallas as pl
from jax.experimental.pallas import tpu as pltpu

f32 = jnp.float32
bf16 = jnp.bfloat16

EPS = 1e-6
LOG2E = 1.4426950408889634
HEAD = 128
N_HGRN = 4
N_GLA = 4
GLA_DK = 64
N_RET = 8
N_HEADS = 8
GLA_RANK = 16
GLA_GATE_NORM = 16.0
GRID_W = 64
ROPE_BASE = 10000.0

EVEN_SPLITS = (N_HGRN * HEAD,) * 5 + (N_GLA * GLA_DK,) * 2 + (N_GLA * HEAD,) * 2 + (GLA_RANK,) * 2
EVEN_STARTS = tuple(int(c) for c in np.cumsum((0,) + EVEN_SPLITS[:-1]))
HGRN_COLS = EVEN_STARTS[0:5]
GLA_QK_COLS = EVEN_STARTS[5:7]
GLA_COLS = EVEN_STARTS[7:9]
GLA_LOW_COLS = EVEN_STARTS[9:11]
RET_COLS = tuple(i * N_RET * HEAD for i in range(4))

CHUNK = 128
SUBLANES = 8


class _ScanPlan(NamedTuple):
    block: int
    table_levels: tuple
    direct_levels: tuple

    @property
    def block_rows(self):
        return 1 if 0 < self.block < CHUNK else 0

    @property
    def table_blocks(self):
        return self.block_rows + len(self.table_levels)

    @property
    def row_total(self):
        return (self.table_blocks + 1) * CHUNK

    @property
    def diag_id(self):
        return len(self.table_levels) + len(self.direct_levels)


ROBUST_PLAN = _ScanPlan(0, (1, 2, 4), (8, 16, 32, 64))
BLOCK_PLAN = _ScanPlan(64, (), (64,))
CHUNK_PLAN = _ScanPlan(CHUNK, (), ())
HGRN_PLANS = (BLOCK_PLAN, ROBUST_PLAN)
GLA_PLANS = (CHUNK_PLAN, ROBUST_PLAN)
ALL_PLANS = (ROBUST_PLAN, BLOCK_PLAN, CHUNK_PLAN)
HGRN_DECAY_LIMIT = 104.0
GLA_DECAY_LIMIT = 90.0
TABLE_ROWS_MAX = max(p.row_total for p in ALL_PLANS) + SUBLANES
PAIR = 2
GATED_HEADS_PER_ITER = 8
GATED_UNITS_MAX = 4
GLA_HEADS_PER_GROUP = HEAD // GLA_DK
RET_UNITS_PER_ITER = 8
STEP_TOKENS = 1024
ROW_TILE = 512
MOD_COLS = 768
VMEM_LIMIT_BYTES = 61 * 1024 * 1024


def _dot(a, b):
    return jnp.dot(a, b, preferred_element_type=f32)


def _dot_tn(a, b):
    return lax.dot_general(a, b, (((0,), (0,)), ((), ())), preferred_element_type=f32)


def _silu(x):
    return x * jax.nn.sigmoid(x)


def _row_tile(i):
    return pl.ds(pl.multiple_of(i * ROW_TILE, ROW_TILE), ROW_TILE)


def _chunk_rows(seq_start, c):
    return pl.ds(pl.multiple_of(seq_start + c * CHUNK, CHUNK), CHUNK)


def _mod_kernel(cond_ref, w_ref, b_ref, o_ref):
    o_ref[0] = _dot(_silu(cond_ref[...]).astype(bf16), w_ref[0].astype(bf16)) + b_ref[0]


def _modulation(cond, ada_w, ada_b):
    depth, d, d3 = ada_w.shape
    rows = cond.shape[0]
    return pl.pallas_call(
        _mod_kernel,
        grid=(depth, d3 // MOD_COLS),
        in_specs=[
            pl.BlockSpec((rows, d), lambda l, j: (0, 0)),
            pl.BlockSpec((1, d, MOD_COLS), lambda l, j: (l, 0, j)),
            pl.BlockSpec((1, 1, MOD_COLS), lambda l, j: (l, 0, j)),
        ],
        out_specs=pl.BlockSpec((1, rows, MOD_COLS), lambda l, j: (l, 0, j)),
        out_shape=jax.ShapeDtypeStruct((depth, rows, d3), f32),
        compiler_params=pltpu.CompilerParams(dimension_semantics=("arbitrary", "arbitrary")),
        name="modulation",
    )(cond, ada_w, ada_b.reshape(depth, 1, d3))


def _modulated_norm(x_ref, mod_ref, nw_ref, h_scr):
    def body(i, carry):
        rows = _row_tile(i)
        x = x_ref[0, rows, :]
        y = x * lax.rsqrt(jnp.mean(x * x, axis=-1, keepdims=True) + EPS) * nw_ref[...]
        h_scr[rows, :] = (y * (1.0 + mod_ref[0, 1:2, :]) + mod_ref[0, 0:1, :]).astype(bf16)
        return carry
    lax.fori_loop(0, STEP_TOKENS // ROW_TILE, body, 0)


def _aligned_slice(start, width):
    return pl.ds(start if isinstance(start, int) else pl.multiple_of(start, width), width)


def _pair_lanes(pair):
    return _aligned_slice(pair * (PAIR * HEAD), PAIR * HEAD)


def _project(h, w_ref, pair, first_cols):
    width = PAIR * HEAD
    return [_dot(h, w_ref[:, _aligned_slice(c0 + pair * width, width)]) for c0 in first_cols]


def _add_heads_output(first_head, n_heads, gate_s, o_scr, gnw_ref, wo_ref, out_ref, first, finish):
    width = n_heads * HEAD
    lanes = _aligned_slice(first_head * HEAD, width)
    tiles = []
    for i in range(STEP_TOKENS // ROW_TILE):
        rows = pl.ds(i * ROW_TILE, ROW_TILE)
        parts = []
        for h in range(n_heads):
            o = o_scr[0, h, rows, :].astype(f32) + o_scr[1, h, rows, :].astype(f32)
            parts.append(o * lax.rsqrt(jnp.mean(o * o, axis=-1, keepdims=True) + EPS))
        y = jnp.concatenate(parts, axis=-1) * gnw_ref[:, lanes]
        tiles.append((rows, (y * gate_s[rows, 0:width].astype(f32)).astype(bf16)))
    products = [(rows, _dot(z, wo_ref[lanes, :])) for rows, z in tiles]
    for rows, product in products:
        mixed = product if first else out_ref[0, rows, :] + product
        if finish is not None:
            x_ref, mod_ref, fnw_ref, final = finish
            mixed = x_ref[0, rows, :] + mod_ref[0, 2:3, :] * mixed
            if final:
                mixed = mixed * lax.rsqrt(jnp.mean(mixed * mixed, axis=-1, keepdims=True) + EPS) * fnw_ref[...]
        out_ref[0, rows, :] = mixed


def _scan_tables(plan):
    t = np.arange(CHUNK)[:, None]
    j = np.arange(CHUNK)[None, :]
    fwd = []
    if plan.block_rows:
        fwd.append((j <= t) & (j // plan.block == t // plan.block))
    for m in plan.table_levels:
        mid = (t // (2 * m)) * (2 * m) + m
        right = t >= mid
        fwd.append(np.where(right, (j >= mid) & (j <= t), (j > t) & (j < mid)))
    fwd.append(j <= t)
    fwd.append(np.ones((SUBLANES, CHUNK), bool))
    fwd = np.concatenate(fwd, axis=0).astype(np.float32)
    bwd = fwd.copy()
    n_sym = plan.row_total // CHUNK
    bwd[:plan.row_total] = fwd[:plan.row_total].reshape(n_sym, CHUNK, CHUNK)[:, ::-1, ::-1].reshape(-1, CHUNK)
    table = np.stack([np.tile(fwd, (1, 2)), np.tile(bwd, (1, 2))])
    first = 1 if plan.block else 0
    ids = np.full((CHUNK, CHUNK), -1, np.int32)
    for i, m in enumerate(plan.table_levels + plan.direct_levels):
        ids = np.where((t > j) & ((t ^ j) >= m) & ((t ^ j) < 2 * m), first + i, ids)
    if plan.block:
        ids = np.where((t >= j) & (t // plan.block == j // plan.block), 0, ids)
    else:
        ids = np.where(t == j, plan.diag_id, ids)
    return jnp.asarray(table, bf16), jnp.asarray(np.stack([ids, ids.T]).astype(np.int32))


def _in_chunk_scores(q, k, cum, tabled, lvl, ones, rev, key_masks, plan):
    qb = q.astype(bf16)
    q_heads = [qb if mask is None else qb * mask for mask in key_masks]
    kt = k.astype(bf16).T
    if plan.block:
        inside = tabled[0] if plan.block_rows else cum
        grow, decay = jnp.exp2(-inside).astype(bf16), jnp.exp2(inside).astype(bf16)
        keys = kt * grow.T
        scores = [jnp.where(lvl == 0, _dot(qh * decay, keys), 0.0) for qh in q_heads]
        tabled = tabled[plan.block_rows:]
    else:
        diag_keys = kt * ones
        scores = [jnp.where(lvl == plan.diag_id, _dot(qh, diag_keys), 0.0) for qh in q_heads]
    first = 1 if plan.block else 0
    for i in range(len(plan.table_levels)):
        e = jnp.exp2(tabled[i]).astype(bf16)
        keys = kt * e.T
        scores = [jnp.where(lvl == first + i, _dot(qh * e, keys), sc) for qh, sc in zip(q_heads, scores)]
    for i, m in enumerate(plan.direct_levels, start=first + len(plan.table_levels)):
        blocks = []
        for p0 in range(0, CHUNK, 2 * m):
            left, right = slice(p0, p0 + m), slice(p0 + m, p0 + 2 * m)
            q_side, k_side = (left, right) if rev else (right, left)
            mid_row = p0 + m if rev else p0 + m - 1
            blocks.append((q_side, k_side, cum[mid_row:mid_row + 1, :]))
        q_decay = [jnp.exp2(cum[qs] - mid).astype(bf16) for qs, _, mid in blocks]
        k_decay = []
        for _, ks, mid in blocks:
            decay = jnp.exp2(mid - cum[ks]).astype(bf16)
            zero = jnp.zeros((m, HEAD), bf16)
            k_decay += [zero, decay] if rev else [decay, zero]
        keys = kt * jnp.concatenate(k_decay, axis=0).T
        for h, qh in enumerate(q_heads):
            s = _dot(jnp.concatenate([qh[qs] * e for (qs, _, _), e in zip(blocks, q_decay)], axis=0), keys)
            rows = []
            for b, (qs, ks, _) in enumerate(blocks):
                if len(blocks) == 1:
                    updated = scores[h][qs, :] + s
                else:
                    updated = jnp.where(lvl[qs, :] == i, s[b * m:(b + 1) * m, :], scores[h][qs, :])
                rows += [updated, scores[h][ks, :]] if rev else [scores[h][ks, :], updated]
            scores[h] = jnp.concatenate(rows, axis=0)
    return [sc.astype(bf16) for sc in scores]


def _store_log2_split(g_ref, rows, g, block):
    width = g.shape[-1]
    x = g * LOG2E
    hi = x.astype(bf16)
    g_ref[rows, 0:width] = hi
    g_ref[rows, width:2 * width] = (x - hi.astype(f32)).astype(bf16)
    return jnp.min(jnp.sum(x.reshape(-1, block, width), axis=1))


def _decay_sums(g_split, table):
    width = g_split.shape[-1] // 2
    return _dot(table, jnp.concatenate([g_split[:, :width], g_split[:, width:]], axis=0))


def _key_masks(heads_per_group):
    if heads_per_group == 1:
        return [None]
    lane = lax.broadcasted_iota(jnp.int32, (1, HEAD), 1)
    return [(lane // (HEAD // heads_per_group) == sub).astype(bf16) for sub in range(heads_per_group)]


def _gated_chunks(chains, sums_ref, upcoming, upcoming_ref, heads_per_group, plan):
    masks = _key_masks(heads_per_group)
    groups = [slice(g * HEAD, (g + 1) * HEAD) for g in range(PAIR)]
    n_heads = PAIR * heads_per_group
    group_of = [h // heads_per_group for h in range(n_heads)]
    mask_of = [masks[h % heads_per_group] for h in range(n_heads)]
    flat = [(c, r) for c, (_, units) in enumerate(chains) for r in range(len(units))]
    unit = {(c, r): chains[c][1][r] for c, r in flat}
    index = {key: u for u, key in enumerate(flat)}
    cum_rows = slice(plan.table_blocks * CHUNK, (plan.table_blocks + 1) * CHUNK)
    cum = {key: sums_ref[index[key], cum_rows, :] for key in flat}
    total = {key: sums_ref[index[key], plan.row_total:plan.row_total + 1, :] for key in flat}
    vb = {key: [unit[key][2][:, h * HEAD:(h + 1) * HEAD] for h in range(n_heads)] for key in flat}
    upcoming = list(enumerate(upcoming))
    kv, whole = {}, {}
    for key in flat:
        k = unit[key][1]
        keys = (k * jnp.exp2(total[key] - cum[key])).astype(bf16)
        kv[key] = []
        for h in range(n_heads):
            own = keys[:, groups[group_of[h]]]
            kv[key].append(_dot_tn(own if mask_of[h] is None else own * mask_of[h], vb[key][h]))
        whole[key] = [jnp.broadcast_to(jnp.exp2(total[key][:, lanes]), (HEAD, HEAD)).T for lanes in groups]
    state = [list(states) for states, _ in chains]
    outs, pending, carried = {key: [None] * n_heads for key in flat}, None, {}

    def finish(done):
        pkey, ph, psc = done
        queries, entering = carried[pkey]
        lhs = jnp.concatenate([queries[:, groups[group_of[ph]]], psc], axis=1)
        outs[pkey][ph] = _dot(lhs, jnp.concatenate([entering[ph], vb[pkey][ph]], axis=0))

    for r in range(max(len(units) for _, units in chains)):
        live = [key for key in flat if key[1] == r]
        for key in live:
            queries = (unit[key][0] * jnp.exp2(cum[key])).astype(bf16)
            carried[key] = (queries, [s.astype(bf16) for s in state[key[0]]])
            state[key[0]] = [whole[key][group_of[h]] * state[key[0]][h] + kv[key][h] for h in range(n_heads)]
        for key in live:
            q, k, _, lvl, rev = unit[key]
            ones = (lvl[0:1, :] >= -1).astype(bf16)
            for g, lanes in enumerate(groups):
                tabled = [sums_ref[index[key], i * CHUNK:(i + 1) * CHUNK, lanes] for i in range(plan.table_blocks)]
                group_scores = _in_chunk_scores(q[:, lanes], k[:, lanes], cum[key][:, lanes], tabled, lvl, ones, rev,
                                                masks, plan)
                for sub, sc in enumerate(group_scores):
                    if pending is not None:
                        finish(pending)
                    pending = (key, g * heads_per_group + sub, sc)
            if upcoming:
                u, (g_split, table) = upcoming.pop(0)
                upcoming_ref[u, 0:plan.row_total + SUBLANES, :] = _decay_sums(g_split, table)
    for u, (g_split, table) in upcoming:
        upcoming_ref[u, 0:plan.row_total + SUBLANES, :] = _decay_sums(g_split, table)
    finish(pending)
    return [[outs[(c, r)] for r in range(len(units))] for c, (_, units) in enumerate(chains)], state


def _gated_scans(q_s, k_refs, v_s, g_refs, s_scr, sums_scr, table_ref, lvl_ref, o_scr, seq, heads_per_group, plan):
    n = seq // CHUNK
    n_seq = STEP_TOKENS // seq
    n_heads = PAIR * heads_per_group
    units = GATED_HEADS_PER_ITER // n_heads
    per_iter = min(n, units // 2)
    seqs_per_iter = min(n_seq, units // (2 * per_iter))
    assert n % per_iter == 0 and n_seq % seqs_per_iter == 0
    iters_per_seq = n // per_iter
    n_iters = (n_seq // seqs_per_iter) * iters_per_seq

    def layout(it):
        jj, i = it // iters_per_seq, it % iters_per_seq
        chains = []
        for js in range(seqs_per_iter):
            j = jj * seqs_per_iter + js
            for d in range(2):
                steps = [i * per_iter + r for r in range(per_iter)]
                chains.append((j, d, [_chunk_rows(j * seq, n - 1 - t if d else t) for t in steps]))
        return chains

    def sums_inputs(it):
        return [(g_refs[d][rw, :], table_ref[d]) for _, d, rws in layout(it) for rw in rws]

    for u, (g_split, table) in enumerate(sums_inputs(0)):
        sums_scr[0, u, 0:plan.row_total + SUBLANES, :] = _decay_sums(g_split, table)

    def iteration(it, slot):
        chains = layout(it)
        args = [([s_scr[j, d, h] for h in range(n_heads)],
                 [(q_s[rw, :], k_refs[d][rw, :], v_s[rw, :], lvl_ref[d], bool(d)) for rw in rws])
                for j, d, rws in chains]
        upcoming = sums_inputs(jnp.minimum(it + 1, n_iters - 1))
        outs, new_states = _gated_chunks(args, sums_scr.at[slot], upcoming, sums_scr.at[1 - slot], heads_per_group,
                                         plan)
        for c, (j, d, rws) in enumerate(chains):
            for h in range(n_heads):
                for r, rw in enumerate(rws):
                    o_scr[d, h, rw, :] = outs[c][r][h].astype(o_scr.dtype)
                s_scr[j, d, h] = new_states[c][h]

    assert n_iters % 2 == 0

    def body(it2, carry):
        iteration(2 * it2, 0)
        iteration(2 * it2 + 1, 1)
        return carry
    lax.fori_loop(0, n_iters // 2, body, 0)


def _even_kernel(*refs, seq, has_state, emit_state, final):
    it = iter(refs)
    x_ref, mod_ref, nw_ref = next(it), next(it), next(it)
    win_ref, wlow_ref, wo_ref = next(it), next(it), next(it)
    gkw_ref, gkb_ref, lb_ref, gnw_ref, fnw_ref = next(it), next(it), next(it), next(it), next(it)
    plan_tables = {plan: (next(it), next(it)) for plan in ALL_PLANS}
    s0a_ref, s0b_ref = (next(it), next(it)) if has_state else (None, None)
    out_ref = next(it)
    sta_ref, stb_ref = (next(it), next(it)) if emit_state else (None, None)
    h_scr, o_scr, gate_s, q_s, v_s, kf_s, kb_s, gf_s, gb_s, s_scr, sums_scr, bound_s = it
    n_seq = STEP_TOKENS // seq
    n_tiles = STEP_TOKENS // ROW_TILE

    _modulated_norm(x_ref, mod_ref, nw_ref, h_scr)

    width = PAIR * HEAD

    def run_scans(first_head, s0_ref, st_ref, h0, heads_per_group, k_refs, plans, block_decays, first=False,
                  last=False):
        bounded, robust, limit = plans
        bound_s[0] = (functools.reduce(jnp.minimum, block_decays) >= -limit).astype(jnp.int32)
        n_heads = PAIR * heads_per_group
        key_rows = HEAD // heads_per_group
        own_rows = [pl.ds((h % heads_per_group) * key_rows, key_rows) for h in range(n_heads)]
        for j in range(n_seq):
            for d in range(2):
                for h in range(n_heads):
                    if s0_ref is None or heads_per_group > 1:
                        s_scr[j, d, h] = jnp.zeros((HEAD, HEAD), f32)
                    if s0_ref is not None:
                        s_scr[j, d, h, own_rows[h], :] = s0_ref[j, d, h0 + h]
        def scans(plan):
            table_ref, lvl_ref = plan_tables[plan]
            _gated_scans(q_s, k_refs, v_s, (gf_s, gb_s), s_scr, sums_scr, table_ref, lvl_ref, o_scr, seq,
                         heads_per_group, plan)
        lax.cond(bound_s[0] == 1, functools.partial(scans, bounded), functools.partial(scans, robust))
        if st_ref is not None:
            for j in range(n_seq):
                for d in range(2):
                    for h in range(n_heads):
                        st_ref[j, d, h0 + h] = s_scr[j, d, h, own_rows[h], :]
        _add_heads_output(first_head, n_heads, gate_s, o_scr, gnw_ref, wo_ref, out_ref, first,
                          (x_ref, mod_ref, fnw_ref, final) if last else None)

    def hgrn_pair(pp, carry):
        lb = lb_ref[:, _pair_lanes(pp)]
        log_lb = jnp.log(lb)

        tiles = [pl.ds(i * ROW_TILE, ROW_TILE) for i in range(n_tiles)]
        query_c, value_c, forget_f_c, forget_b_c, gate_c = HGRN_COLS
        order = (forget_f_c, forget_b_c, query_c, value_c, gate_c)
        projected = [_project(h_scr[rows, :], win_ref, pp, order) for rows in tiles]
        block_decays = []
        for rows, (forget_f, forget_b, query, value, gate) in zip(tiles, projected):
            for a, k_s, g_s in ((forget_f, kf_s, gf_s), (forget_b, kb_s, gb_s)):
                z = log_lb - a
                u = jnp.exp(-jnp.abs(a))
                w = jnp.exp(-jnp.abs(z))
                r = 1.0 / (1.0 + u)
                log_f = jnp.maximum(z, 0.0) + jnp.minimum(a, 0.0) + jnp.log((1.0 + w) * r)
                block_decays.append(_store_log2_split(g_s, rows, log_f, HGRN_PLANS[0].block))
                k_s[rows, :] = (1.0 - lb) * jnp.where(a >= 0.0, u * r, r)
            q_s[rows, :] = _silu(query)
            v_s[rows, 0:width] = value.astype(bf16)
            gate_s[rows, 0:width] = _silu(gate).astype(bf16)
        run_scans(PAIR * pp, s0a_ref, sta_ref, PAIR * pp, 1, (kf_s, kb_s), HGRN_PLANS + (HGRN_DECAY_LIMIT,), block_decays)
        return carry

    def clear(i, carry):
        out_ref[0, _row_tile(i), :] = jnp.zeros((ROW_TILE, out_ref.shape[-1]), f32)
        return carry
    lax.fori_loop(0, n_tiles, clear, 0)
    lax.fori_loop(0, N_HGRN // PAIR, hgrn_pair, 0)

    assert N_GLA == PAIR * GLA_HEADS_PER_GROUP

    tiles = [pl.ds(i * ROW_TILE, ROW_TILE) for i in range(n_tiles)]
    lows = [_dot(h_scr[rows, :], wlow_ref[...]).astype(bf16) for rows in tiles]
    projected = []
    for rows, low in zip(tiles, lows):
        h = h_scr[rows, :]
        narrow = [_dot(h, win_ref[:, c0:c0 + N_GLA * GLA_DK]) for c0 in GLA_QK_COLS]
        logits = [_dot(low, gkw_ref[d]) + gkb_ref[d] for d in range(2)]
        wide = [_dot(h, win_ref[:, c0:c0 + N_GLA * HEAD]) for c0 in GLA_COLS]
        projected.append((logits, narrow, wide))
    block_decays = []
    for rows, (logits, (query, key), (value, gate)) in zip(tiles, projected):
        for d, g_s in enumerate((gf_s, gb_s)):
            log_gate = jnp.minimum(logits[d], 0.0) - jnp.log(1.0 + jnp.exp(-jnp.abs(logits[d])))
            block_decays.append(_store_log2_split(g_s, rows, log_gate * (1.0 / GLA_GATE_NORM), GLA_PLANS[0].block))
        q_s[rows, :] = query * (GLA_DK ** -0.5)
        kf_s[rows, :] = key
        v_s[rows, :] = value.astype(bf16)
        gate_s[rows, :] = _silu(gate).astype(bf16)
    run_scans(N_HGRN, s0b_ref, stb_ref, 0, GLA_HEADS_PER_GROUP, (kf_s, kf_s), GLA_PLANS + (GLA_DECAY_LIMIT,), block_decays, last=True)


def _odd_kernel(*refs, seq, has_state, emit_state, use_rope, final):
    it = iter(refs)
    lg_ref = next(it)
    x_ref, mod_ref, nw_ref = next(it), next(it), next(it)
    wr_ref, wo_ref, gnw_ref, fnw_ref = next(it), next(it), next(it), next(it)
    cos_ref, sin_ref = (next(it), next(it)) if use_rope else (None, None)
    s0_ref = next(it) if has_state else None
    out_ref = next(it)
    st_ref = next(it) if emit_state else None
    h_scr, o_scr, gate_s, q_s, kt_s, v_s, s_scr, dec_scr = it
    n_seq = STEP_TOKENS // seq
    n_tiles = STEP_TOKENS // ROW_TILE
    n_chunks = seq // CHUNK
    per_iter = min(n_chunks, RET_UNITS_PER_ITER // n_seq)
    assert n_chunks % per_iter == 0

    _modulated_norm(x_ref, mod_ref, nw_ref, h_scr)
    t_idx = lax.broadcasted_iota(jnp.int32, (CHUNK, CHUNK), 0)
    s_idx = lax.broadcasted_iota(jnp.int32, (CHUNK, CHUNK), 1)
    row_f = lax.broadcasted_iota(jnp.int32, (CHUNK, HEAD), 0).astype(f32)
    col_f = lax.broadcasted_iota(jnp.int32, (SUBLANES, CHUNK), 1).astype(f32)
    chunk_len = jnp.full((SUBLANES, HEAD), CHUNK, f32)

    def lane_rows(block):
        return slice(block * SUBLANES, (block + 1) * SUBLANES)
    if use_rope:
        lane = lax.broadcasted_iota(jnp.int32, (ROW_TILE, HEAD), 1)
        first_quarter = (lane // (HEAD // 4)) % 2 == 0

    def rope(x, cos, sin_signed):
        xr = jnp.where(first_quarter, pltpu.roll(x, HEAD - HEAD // 4, axis=1), pltpu.roll(x, HEAD // 4, axis=1))
        return x * cos + xr * sin_signed

    width = PAIR * HEAD

    def pair_body(pp):
        tiles = [pl.ds(i * ROW_TILE, ROW_TILE) for i in range(n_tiles)]
        projected = [_project(h_scr[rows, :], wr_ref, pp, RET_COLS) for rows in tiles]
        for rows, (q, k, value, gate) in zip(tiles, projected):
            gate_s[rows, :] = _silu(gate).astype(bf16)
            k = k * (HEAD ** -0.5)
            if use_rope:
                cos, sin_signed = cos_ref[rows, :], sin_ref[rows, :]
                heads = [slice(h * HEAD, (h + 1) * HEAD) for h in range(PAIR)]
                q = jnp.concatenate([rope(q[:, lanes], cos, sin_signed) for lanes in heads], axis=-1)
                k = jnp.concatenate([rope(k[:, lanes], cos, sin_signed) for lanes in heads], axis=-1)
            q_s[rows, :] = q.astype(bf16)
            kt_s[:, rows] = k.T
            v_s[rows, :] = value.astype(bf16)
        for h in range(PAIR):
            scan_head(PAIR * pp + h, h)
        last = pp == N_RET // PAIR - 1
        _add_heads_output(PAIR * pp, PAIR, gate_s, o_scr, gnw_ref, wo_ref, out_ref, pp == 0,
                          (x_ref, mod_ref, fnw_ref, final) if last else None)

    def scan_head(hh, h):
        lanes = slice(h * HEAD, (h + 1) * HEAD)
        lg_f = lg_ref[0, hh]
        lg_b = lg_ref[1, hh]
        for j in range(n_seq):
            for d in range(2):
                s_scr[j, d] = s0_ref[j, d, hh] if has_state else jnp.zeros((HEAD, HEAD), f32)

        dist = (t_idx - s_idx).astype(f32)
        dec_scr[0] = jnp.exp(lg_f * (row_f + 1.0))
        dec_scr[1] = jnp.exp(lg_b * (CHUNK - row_f))
        dec_scr[2] = (jnp.where(t_idx >= s_idx, jnp.exp(lg_f * jnp.maximum(dist, 0.0)), 0.0)
                      + jnp.where(s_idx >= t_idx, jnp.exp(lg_b * jnp.maximum(-dist, 0.0)), 0.0))
        dec_scr[3, lane_rows(0), :] = jnp.exp(lg_f * (CHUNK - 1.0 - col_f))
        dec_scr[3, lane_rows(1), :] = jnp.exp(lg_b * col_f)
        dec_scr[3, lane_rows(2), :] = jnp.exp(lg_f * chunk_len)
        dec_scr[3, lane_rows(3), :] = jnp.exp(lg_b * chunk_len)

        def body(i, c):
            units = [(j, r) for j in range(n_seq) for r in range(per_iter)]
            rows = {(j, r, d): _chunk_rows(j * seq, n_chunks - 1 - (i * per_iter + r) if d else i * per_iter + r)
                    for j, r in units for d in range(2)}
            scores = {u: _dot(q_s[rows[u + (0,)], lanes], kt_s[lanes, rows[u + (0,)]].astype(bf16)) for u in units}
            kv = {}
            for j, r in units:
                for d in range(2):
                    rw = rows[(j, r, d)]
                    keys = (kt_s[lanes, rw] * dec_scr[3, d * SUBLANES:d * SUBLANES + 1, :]).astype(bf16)
                    kv[(j, r, d)] = _dot(keys, v_s[rw, lanes])
            state = {(j, d): s_scr[j, d] for j in range(n_seq) for d in range(2)}
            entering, carried = {}, {}
            for r in range(per_iter):
                for j in range(n_seq):
                    entering[(j, r)] = state[(j, 0)].astype(bf16)
                    carried[(j, r)] = _dot(q_s[rows[(j, r, 1)], lanes], state[(j, 1)].astype(bf16))
                    for d in range(2):
                        whole = dec_scr[3, (2 + d) * SUBLANES:(2 + d) * SUBLANES + 1, :]
                        state[(j, d)] = whole * state[(j, d)] + kv[(j, r, d)]
            seen_f = dec_scr[0].astype(bf16)
            for j, r in units:
                rw = rows[(j, r, 0)]
                lhs = jnp.concatenate([q_s[rw, lanes] * seen_f, (scores[(j, r)] * dec_scr[2]).astype(bf16)], axis=1)
                o_scr[0, h, rw, :] = _dot(lhs, jnp.concatenate([entering[(j, r)], v_s[rw, lanes]], axis=0))
                o_scr[1, h, rows[(j, r, 1)], :] = dec_scr[1] * carried[(j, r)]
            for (j, d), s in state.items():
                s_scr[j, d] = s
            return c
        lax.fori_loop(0, n_chunks // per_iter, body, 0)
        if emit_state:
            for j in range(n_seq):
                for d in range(2):
                    st_ref[j, d, hh] = s_scr[j, d]

    for pp in range(N_RET // PAIR):
        pair_body(pp)


def _const_spec(shape):
    zeros = (0,) * len(shape)
    return pl.BlockSpec(shape, lambda i: zeros, pipeline_mode=pl.Buffered(1))


def _step_spec(shape, per_step, buffers=None):
    zeros = (0,) * (len(shape) - 1)
    mode = {} if buffers is None else {"pipeline_mode": pl.Buffered(buffers)}
    return pl.BlockSpec((per_step,) + tuple(shape[1:]), lambda i: (i,) + zeros, **mode)


def _mod_spec(d, per_sequence):
    if per_sequence:
        return pl.BlockSpec((1, 3, d), lambda i: (i + 1, 0, 0))
    return pl.BlockSpec((1, 3, d), lambda i: (0, 0, 0))


def _layer_call(body, x, mod, consts, states, state_shapes, per_sequence_mod, scratch, name, smem_inputs=(),
                state_buffers=None):
    n_seq, seq, d = x.shape
    per_step = STEP_TOKENS // seq
    assert per_step * seq == STEP_TOKENS and n_seq % per_step == 0 and seq % CHUNK == 0
    assert not per_sequence_mod or per_step == 1
    n_steps = n_seq // per_step
    xs = x.reshape(n_steps, STEP_TOKENS, d)
    inputs = list(smem_inputs) + [xs, mod] + list(consts) + list(states)
    in_specs = [pl.BlockSpec(memory_space=pltpu.SMEM)] * len(smem_inputs)
    in_specs += [_step_spec(xs.shape, 1), _mod_spec(d, per_sequence_mod)]
    in_specs += [_const_spec(a.shape) for a in consts]
    in_specs += [_step_spec(s.shape, per_step) for s in states]
    out_shape = [jax.ShapeDtypeStruct(xs.shape, f32)] + [jax.ShapeDtypeStruct(s, f32) for s in state_shapes]
    out_specs = [_step_spec(xs.shape, 1)] + [_step_spec(s, per_step, buffers=state_buffers) for s in state_shapes]
    outs = pl.pallas_call(
        body,
        grid=(n_steps,),
        in_specs=in_specs,
        out_specs=out_specs,
        out_shape=out_shape,
        scratch_shapes=scratch,
        compiler_params=pltpu.CompilerParams(dimension_semantics=("arbitrary",), vmem_limit_bytes=VMEM_LIMIT_BYTES),
        name=name,
    )(*inputs)
    return [outs[0].reshape(x.shape)] + list(outs[1:])


def _even_layer(x, mod, norm_w, w, final_w, tables, states, emit_state, per_sequence_mod, final):
    n_seq, seq, d = x.shape
    consts = [norm_w, w["win"], w["wlow"], w["wout"], w["gkw"], w["gkb"], w["lb"], w["gnw"],
              final_w] + list(tables)
    state_shapes = [(n_seq, 2, N_HGRN, HEAD, HEAD), (n_seq, 2, N_GLA, GLA_DK, HEAD)] if emit_state else []
    scan_heads = max(PAIR, N_GLA)
    scratch = [
        pltpu.VMEM((STEP_TOKENS, d), bf16),
        pltpu.VMEM((2, scan_heads, STEP_TOKENS, HEAD), bf16),
        pltpu.VMEM((STEP_TOKENS, scan_heads * HEAD), bf16),
        pltpu.VMEM((STEP_TOKENS, PAIR * HEAD), f32),
        pltpu.VMEM((STEP_TOKENS, scan_heads * HEAD), bf16),
        pltpu.VMEM((STEP_TOKENS, PAIR * HEAD), f32),
        pltpu.VMEM((STEP_TOKENS, PAIR * HEAD), f32),
        pltpu.VMEM((STEP_TOKENS, 2 * PAIR * HEAD), bf16),
        pltpu.VMEM((STEP_TOKENS, 2 * PAIR * HEAD), bf16),
        pltpu.VMEM((STEP_TOKENS // seq, 2, scan_heads, HEAD, HEAD), f32),
        pltpu.VMEM((2, GATED_UNITS_MAX, TABLE_ROWS_MAX, PAIR * HEAD), f32),
        pltpu.SMEM((1,), jnp.int32),
    ]
    body = functools.partial(_even_kernel, seq=seq, has_state=states is not None, emit_state=emit_state, final=final)
    return _layer_call(body, x, mod, consts, states or (), state_shapes, per_sequence_mod, scratch,
                       "even_layer_seq%d" % seq)


def _odd_layer(x, mod, norm_w, w, final_w, log_decay, rope, state, emit_state, per_sequence_mod, final):
    n_seq, seq, d = x.shape
    consts = [norm_w, w["wr"], w["wout"], w["gnw"], final_w] + list(rope or ())
    state_shapes = [(n_seq, 2, N_RET, HEAD, HEAD)] if emit_state else []
    scratch = [
        pltpu.VMEM((STEP_TOKENS, d), bf16),
        pltpu.VMEM((2, PAIR, STEP_TOKENS, HEAD), f32),
        pltpu.VMEM((STEP_TOKENS, PAIR * HEAD), bf16),
        pltpu.VMEM((STEP_TOKENS, PAIR * HEAD), bf16),
        pltpu.VMEM((PAIR * HEAD, STEP_TOKENS), f32),
        pltpu.VMEM((STEP_TOKENS, PAIR * HEAD), bf16),
        pltpu.VMEM((STEP_TOKENS // seq, 2, HEAD, HEAD), f32),
        pltpu.VMEM((4, CHUNK, HEAD), f32),
    ]
    body = functools.partial(_odd_kernel, seq=seq, has_state=state is not None, emit_state=emit_state,
                             use_rope=rope is not None, final=final)
    return _layer_call(body, x, mod, consts, () if state is None else (state,), state_shapes, per_sequence_mod,
                       scratch, "odd_layer_seq%d" % seq, smem_inputs=(log_decay,))


def _even_weights(w_in, gk_w, gk_b, lb, gn_w, w_out):
    c0 = GLA_LOW_COLS[0]
    wlow = jnp.pad(w_in[:, c0:c0 + 2 * GLA_RANK], ((0, 0), (0, HEAD - 2 * GLA_RANK)))
    gkw = jnp.stack([jnp.pad(gk_w[d], ((d * GLA_RANK, HEAD - (d + 1) * GLA_RANK), (0, 0))) for d in range(2)])
    return {"win": w_in.astype(bf16), "wlow": wlow.astype(bf16), "gkw": gkw.astype(bf16),
            "gkb": gk_b.reshape(2, 1, -1), "lb": lb.reshape(1, -1), "gnw": gn_w.reshape(1, -1),
            "wout": w_out.astype(bf16)}


def _odd_weights(w_in, gn_w, w_out):
    return {"wr": w_in.astype(bf16), "gnw": gn_w.reshape(1, -1), "wout": w_out.astype(bf16)}


def _rope_tables(seq):
    rows = seq // GRID_W
    t_row = np.repeat(np.arange(rows), GRID_W).astype(np.float32)
    t_col = np.tile(np.arange(GRID_W), rows).astype(np.float32)
    half = HEAD // 2
    inv = (ROPE_BASE ** (-np.arange(0, half, 2, dtype=np.float32) / half)).astype(np.float32)
    ang_r = t_row[:, None] * inv
    ang_c = t_col[:, None] * inv
    ang = np.concatenate([ang_r, ang_r, ang_c, ang_c], axis=-1).astype(np.float32)
    sign = np.where((np.arange(HEAD) // (HEAD // 4)) % 2 == 0, -1.0, 1.0).astype(np.float32)
    return jnp.asarray(np.cos(ang), f32), jnp.asarray(np.sin(ang) * sign, f32)


def kernel(x_prompt, x_sample, state_hgrn, state_gla, state_ret, c, c_ctx, norm_w, ada_w, ada_b, w_in_even, hgrn_lb, gla_gk_w, gla_gk_b, gn_even, w_out_even, w_in_odd, ret_decay, gn_odd, w_out_odd, final_norm_w):
    depth, d = norm_w.shape
    n_lat = x_sample.shape[0]
    n_cond = -(-(1 + n_lat) // SUBLANES) * SUBLANES
    cond = jnp.zeros((n_cond, d), f32).at[0].set(c_ctx).at[1:1 + n_lat].set(c)
    mod = _modulation(cond, ada_w, ada_b).reshape(depth, n_cond, 3, d)
    lbs = jnp.cumsum(jax.nn.softmax(hgrn_lb.astype(f32), axis=0), axis=0)
    final_w = final_norm_w.reshape(1, d)
    rope = _rope_tables(x_sample.shape[1])
    tables = sum((_scan_tables(plan) for plan in ALL_PLANS), ())

    x_c, x_l = x_prompt, x_sample
    new_hgrn, new_gla, new_ret = [], [], []
    for l in range(depth):
        i = l // 2
        final = l == depth - 1
        nw = norm_w[l].reshape(1, d)
        if l % 2 == 0:
            w = _even_weights(w_in_even[i], gla_gk_w[i], gla_gk_b[i], lbs[i], gn_even[i], w_out_even[i])
            x_c, st_a, st_b = _even_layer(x_c, mod[l], nw, w, final_w, tables, None, True, False, final)
            (x_l,) = _even_layer(x_l, mod[l], nw, w, final_w, tables, (state_hgrn[:, i], state_gla[:, i]),
                                 False, True, final)
            new_hgrn.append(st_a)
            new_gla.append(st_b)
        else:
            w = _odd_weights(w_in_odd[i], gn_odd[i], w_out_odd[i])
            log_decay = jax.nn.log_sigmoid(ret_decay[i].astype(f32))
            x_c, st_c = _odd_layer(x_c, mod[l], nw, w, final_w, log_decay, None, None, True, False, final)
            (x_l,) = _odd_layer(x_l, mod[l], nw, w, final_w, log_decay, rope, state_ret[:, i], False, True, final)
            new_ret.append(st_c)
    def stacked(states):
        return states[0][:, None] if len(states) == 1 else jnp.stack(states, axis=1)
    return (x_c, x_l, stacked(new_hgrn), stacked(new_gla), stacked(new_ret))
```
